```python
import jax, jax.numpy as jnp
from jax import lax
import numpy as np

D_MODEL = 1024
BATCH = 16
SEQ = 256
DEPTH = 1
DEC_BATCH = 4
DEC_SEQ = 2048
PAST_LEN = 256

GRID_W = 64
RWKV_HEAD_DIM = 64
RWKV_WIDTH = D_MODEL // 2
RWKV_HEADS = RWKV_WIDTH // RWKV_HEAD_DIM
GLA_HEADS = 4
GLA_V_WIDTH = D_MODEL // 2
GLA_VAL_DIM = GLA_V_WIDTH // GLA_HEADS
GLA_QK_WIDTH = GLA_V_WIDTH // 2
GLA_KEY_DIM = GLA_QK_WIDTH // GLA_HEADS
GLA_CHUNK = 32
GLA_GATE_RANK = 16
GLA_GATE_NORMALIZER = 16.0
DECAY_LORA = 64
AAA_LORA = 64
GATE_LORA = 128
D_FF = 4 * D_MODEL
IN_WIDTH = 3 * RWKV_WIDTH + 2 * GLA_QK_WIDTH + 2 * GLA_V_WIDTH
MIX_WIDTH = RWKV_WIDTH + GLA_V_WIDTH
N_MOD = 6
RMS_EPS = 1e-6
LNX_EPS = 64e-5
GLA_NORM_EPS = 1e-5

kernel_name = 'hymba_rwkv7_gla_diffusion_step'


def _rmsnorm(x, g, eps=RMS_EPS):
    xf = x.astype(jnp.float32)
    y = xf * lax.rsqrt(jnp.mean(xf * xf, axis=-1, keepdims=True) + eps)
    return (y * g.astype(jnp.float32)).astype(x.dtype)


def _shift_seq(x):
    xp = jnp.pad(x, ((0, 0), (1, 1), (0, 0)))
    return 0.5 * (xp[:, :-2] + xp[:, 2:])


def _shift_grid(x):
    b, l, ch = x.shape
    rows = l // GRID_W
    g = jnp.pad(x.reshape(b, rows, GRID_W, ch), ((0, 0), (1, 1), (1, 1), (0, 0)))
    nb = g[:, :-2, 1:-1] + g[:, 2:, 1:-1] + g[:, 1:-1, :-2] + g[:, 1:-1, 2:]
    return (0.25 * nb).reshape(b, l, ch)


def _flip(t):
    return jnp.flip(t, axis=1)


def _rwkv7_scan(r, w, k, v, kk, a, s0):
    def step(s, inp):
        r_t, w_t, k_t, v_t, kk_t, a_t = inp
        sa = jnp.einsum('bhij,bhj->bhi', s, -kk_t)
        s = (s * w_t[:, :, None, :] + sa[..., None] * (kk_t * a_t)[:, :, None, :]
             + v_t[..., None] * k_t[:, :, None, :])
        return s, jnp.einsum('bhij,bhj->bhi', s, r_t)
    xs = tuple(jnp.swapaxes(t, 0, 1) for t in (r, w, k, v, kk, a))
    s_fin, ys = lax.scan(step, s0, xs)
    return jnp.swapaxes(ys, 0, 1), s_fin


def _gla_chunked(q, k, v, log_a, s0):
    b, l, h, _ = q.shape
    n = l // GLA_CHUNK

    def chunks(t):
        return t.reshape(b, n, GLA_CHUNK, h, t.shape[-1]).transpose(0, 1, 3, 2, 4)

    q, k, v, log_a = chunks(q), chunks(k), chunks(v), chunks(log_a)
    cum = jnp.cumsum(log_a, axis=3)
    causal = jnp.tril(jnp.ones((GLA_CHUNK, GLA_CHUNK), dtype=bool))[:, :, None]
    diff = cum[..., :, None, :] - cum[..., None, :, :]
    decay = jnp.where(causal, jnp.exp(jnp.where(causal, diff, 0.0)), 0.0)
    att = jnp.sum(q[..., :, None, :] * k[..., None, :, :] * decay, axis=-1)
    o_intra = jnp.einsum('bnhij,bnhjv->bnhiv', att, v)
    last = cum[..., -1:, :]
    q_in = q * jnp.exp(cum)
    k_in = k * jnp.exp(last - cum)
    g_last = jnp.exp(last[..., 0, :])

    def step(s, inp):
        qc, kc, vc, gc = inp
        o = jnp.einsum('bhck,bhkv->bhcv', qc, s)
        s = gc[..., None] * s + jnp.einsum('bhck,bhcv->bhkv', kc, vc)
        return s, o

    xs = tuple(jnp.moveaxis(t, 1, 0) for t in (q_in, k_in, v, g_last))
    s_fin, o_inter = lax.scan(step, s0, xs)
    o = o_intra + jnp.moveaxis(o_inter, 0, 1)
    return o.transpose(0, 1, 3, 2, 4).reshape(b, l, h, v.shape[-1]), s_fin


def _mixer(h, shift, s_rf, s_rb, s_gf, s_gb, p):
    f32 = jnp.float32
    b, l, _ = h.shape
    R = RWKV_WIDTH
    proj = h @ p['w_in']
    rkv, gq, gkey, gv, gg = jnp.split(
        proj, [3 * R, 3 * R + GLA_QK_WIDTH, 3 * R + 2 * GLA_QK_WIDTH,
               3 * R + 2 * GLA_QK_WIDTH + GLA_V_WIDTH], axis=-1)

    rkv = rkv + p['mu_rkv'] * (shift(rkv) - rkv)
    r, k, v = jnp.split(rkv, 3, axis=-1)
    dh = shift(h) - h
    xw = h + p['mu_wag'][0] * dh
    xa = h + p['mu_wag'][1] * dh
    xg = h + p['mu_wag'][2] * dh

    def heads(t):
        return t.astype(f32).reshape(b, l, RWKV_HEADS, RWKV_HEAD_DIM)

    rh, kh, vh = heads(r), heads(k), heads(v)
    kk = kh * p['k_k'].astype(f32).reshape(RWKV_HEADS, RWKV_HEAD_DIM)
    kk = kk * lax.rsqrt(jnp.maximum(jnp.sum(kk * kk, axis=-1, keepdims=True), 1e-12))
    k_a = p['k_a'].astype(f32).reshape(RWKV_HEADS, RWKV_HEAD_DIM)

    def dir_inputs(d):
        z = (p['w0'][d] + jnp.tanh(xw @ p['w1'][d]) @ p['w2'][d]).astype(f32)
        wh = heads(jnp.exp(-jnp.exp(-jax.nn.softplus(-z) - 0.5)))
        ah = heads(jax.nn.sigmoid(p['a0'][d] + (xa @ p['a1'][d]) @ p['a2'][d]))
        kd = kh * (1.0 + (ah - 1.0) * k_a)
        return wh, ah, kd

    w_f, a_f, kd_f = dir_inputs(0)
    w_b, a_b, kd_b = dir_inputs(1)
    y_f, srf = _rwkv7_scan(rh, w_f, kd_f, vh, kk, a_f, s_rf.astype(f32))
    y_b, srb = _rwkv7_scan(_flip(rh), _flip(w_b), _flip(kd_b), _flip(vh), _flip(kk), _flip(a_b),
                           s_rb.astype(f32))
    y = y_f + _flip(y_b)
    mu = jnp.mean(y, axis=-1, keepdims=True)
    var = jnp.mean(jnp.square(y - mu), axis=-1, keepdims=True)
    yn = ((y - mu) * lax.rsqrt(var + LNX_EPS)).reshape(b, l, R)
    yn = yn * p['lnx_g'].astype(f32) + p['lnx_b'].astype(f32)
    bonus = (jnp.sum(rh * (kd_f + kd_b) * p['r_k'].astype(f32), axis=-1, keepdims=True) * vh).reshape(b, l, R)
    gate = (jax.nn.sigmoid(xg @ p['g1']) @ p['g2']).astype(f32)
    rwkv_out = (yn + bonus) * gate

    def gheads(t, dim):
        return t.astype(f32).reshape(b, l, GLA_HEADS, dim)

    q = gheads(gq, GLA_KEY_DIM) * (GLA_KEY_DIM ** -0.5)
    kg = gheads(gkey, GLA_KEY_DIM)
    vg = gheads(gv, GLA_VAL_DIM)

    def log_gate(d):
        logits = ((h @ p['gk1'][d]) @ p['gk2'][d] + p['gk_b'][d]).astype(f32)
        return gheads(jax.nn.log_sigmoid(logits), GLA_KEY_DIM) / GLA_GATE_NORMALIZER

    o_f, sgf = _gla_chunked(q, kg, vg, log_gate(0), s_gf.astype(f32))
    o_b, sgb = _gla_chunked(_flip(q), _flip(kg), _flip(vg), _flip(log_gate(1)), s_gb.astype(f32))
    o = o_f + _flip(o_b)
    o = (o * lax.rsqrt(jnp.mean(o * o, axis=-1, keepdims=True) + GLA_NORM_EPS)
         * p['gla_norm_g'].astype(f32) * jax.nn.silu(gheads(gg, GLA_VAL_DIM)))
    gla_out = o.reshape(b, l, GLA_V_WIDTH)

    out = jnp.concatenate([rwkv_out, gla_out], axis=-1).astype(h.dtype) @ p['w_out']
    return out, (srf, srb, sgf, sgb)


def _block(x, mod, shift, states, p):
    sh1, sc1, gt1, sh2, sc2, gt2 = jnp.split(mod, N_MOD, axis=-1)
    h = _rmsnorm(x, p['norm1_g']) * (1.0 + sc1) + sh1
    o, new_states = _mixer(h, shift, states[0], states[1], states[2], states[3], p)
    x = x + gt1 * o
    h = _rmsnorm(x, p['norm2_g']) * (1.0 + sc2) + sh2
    f = jnp.square(jax.nn.relu(h @ p['mlp_w1'])) @ p['mlp_w2']
    x = x + gt2 * f
    return x, new_states


def setup_inputs(seed: int = 0) -> dict:
    key = jax.random.key(seed)
    ks = iter(jax.random.split(key, 48))

    def nrm(shape, scale):
        return scale * jax.random.normal(next(ks), shape, jnp.float32)

    def unif(shape, lo, hi):
        return jax.random.uniform(next(ks), shape, jnp.float32, lo, hi)

    D = D_MODEL
    R = RWKV_WIDTH
    L = DEPTH
    return {
        'x_prompt': nrm((BATCH, SEQ, D), 1.0),
        'x_sample': nrm((DEC_BATCH, DEC_SEQ, D), 1.0),
        'c': nrm((DEC_BATCH, D), 1.0),
        'state_rwkv_fwd': nrm((DEC_BATCH, L, RWKV_HEADS, RWKV_HEAD_DIM, RWKV_HEAD_DIM), 0.5),
        'state_rwkv_bwd': nrm((DEC_BATCH, L, RWKV_HEADS, RWKV_HEAD_DIM, RWKV_HEAD_DIM), 0.5),
        'state_gla_fwd': nrm((DEC_BATCH, L, GLA_HEADS, GLA_KEY_DIM, GLA_VAL_DIM), 0.5),
        'state_gla_bwd': nrm((DEC_BATCH, L, GLA_HEADS, GLA_KEY_DIM, GLA_VAL_DIM), 0.5),
        'c_ctx': nrm((D,), 1.0),
        'ada_w': nrm((L, D, N_MOD * D), 0.3 * D ** -0.5),
        'ada_b': nrm((L, N_MOD * D), 0.02),
        'norm1_g': 1.0 + nrm((L, D), 0.01),
        'norm2_g': 1.0 + nrm((L, D), 0.01),
        'w_in': nrm((L, D, IN_WIDTH), D ** -0.5),
        'rwkv_mu_rkv': unif((L, 3 * R), 0.2, 0.8),
        'rwkv_mu_wag': unif((L, 3, D), 0.2, 0.8),
        'rwkv_w0': nrm((L, 2, R), 0.5) - 0.5,
        'rwkv_w1': nrm((L, 2, D, DECAY_LORA), D ** -0.5),
        'rwkv_w2': nrm((L, 2, DECAY_LORA, R), 0.3 * DECAY_LORA ** -0.5),
        'rwkv_a0': nrm((L, 2, R), 0.1),
        'rwkv_a1': nrm((L, 2, D, AAA_LORA), D ** -0.5),
        'rwkv_a2': nrm((L, 2, AAA_LORA, R), 0.3 * AAA_LORA ** -0.5),
        'rwkv_g1': nrm((L, D, GATE_LORA), D ** -0.5),
        'rwkv_g2': nrm((L, GATE_LORA, R), GATE_LORA ** -0.5),
        'rwkv_k_k': 0.85 + nrm((L, R), 0.05),
        'rwkv_k_a': 1.0 + nrm((L, R), 0.05),
        'rwkv_r_k': nrm((L, RWKV_HEADS, RWKV_HEAD_DIM), 0.1),
        'rwkv_lnx_g': 1.0 + nrm((L, R), 0.01),
        'rwkv_lnx_b': nrm((L, R), 0.01),
        'gla_gk1': nrm((L, 2, D, GLA_GATE_RANK), D ** -0.5),
        'gla_gk2': nrm((L, 2, GLA_GATE_RANK, GLA_QK_WIDTH), GLA_GATE_RANK ** -0.5),
        'gla_gk_b': nrm((L, 2, GLA_QK_WIDTH), 0.5) + 1.0,
        'gla_norm_g': 1.0 + nrm((L, GLA_VAL_DIM), 0.01),
        'w_out': nrm((L, MIX_WIDTH, D), MIX_WIDTH ** -0.5),
        'mlp_w1': nrm((L, D, D_FF), D ** -0.5),
        'mlp_w2': nrm((L, D_FF, D), D_FF ** -0.5),
        'final_norm_g': 1.0 + nrm((D,), 0.01),
    }


def reference(x_prompt, x_sample, c, state_rwkv_fwd, state_rwkv_bwd, state_gla_fwd, state_gla_bwd,
              c_ctx, ada_w, ada_b, norm1_g, norm2_g, w_in, rwkv_mu_rkv, rwkv_mu_wag,
              rwkv_w0, rwkv_w1, rwkv_w2, rwkv_a0, rwkv_a1, rwkv_a2, rwkv_g1, rwkv_g2,
              rwkv_k_k, rwkv_k_a, rwkv_r_k, rwkv_lnx_g, rwkv_lnx_b,
              gla_gk1, gla_gk2, gla_gk_b, gla_norm_g, w_out, mlp_w1, mlp_w2, final_norm_g):
    f32 = jnp.float32
    nb = x_prompt.shape[0]
    zr = jnp.zeros((nb, RWKV_HEADS, RWKV_HEAD_DIM, RWKV_HEAD_DIM), f32)
    zg = jnp.zeros((nb, GLA_HEADS, GLA_KEY_DIM, GLA_VAL_DIM), f32)
    ctx_cond = jax.nn.silu(c_ctx)
    lat_cond = jax.nn.silu(c)
    xp, xs = x_prompt, x_sample
    new_rf, new_rb, new_gf, new_gb = [], [], [], []
    for layer in range(DEPTH):
        p = {
            'norm1_g': norm1_g[layer], 'norm2_g': norm2_g[layer], 'w_in': w_in[layer],
            'mu_rkv': rwkv_mu_rkv[layer], 'mu_wag': rwkv_mu_wag[layer],
            'w0': rwkv_w0[layer], 'w1': rwkv_w1[layer], 'w2': rwkv_w2[layer],
            'a0': rwkv_a0[layer], 'a1': rwkv_a1[layer], 'a2': rwkv_a2[layer],
            'g1': rwkv_g1[layer], 'g2': rwkv_g2[layer],
            'k_k': rwkv_k_k[layer], 'k_a': rwkv_k_a[layer], 'r_k': rwkv_r_k[layer],
            'lnx_g': rwkv_lnx_g[layer], 'lnx_b': rwkv_lnx_b[layer],
            'gk1': gla_gk1[layer], 'gk2': gla_gk2[layer], 'gk_b': gla_gk_b[layer],
            'gla_norm_g': gla_norm_g[layer], 'w_out': w_out[layer],
            'mlp_w1': mlp_w1[layer], 'mlp_w2': mlp_w2[layer],
        }
        mod_ctx = (ctx_cond @ ada_w[layer] + ada_b[layer])[None, None, :]
        mod_lat = (lat_cond @ ada_w[layer] + ada_b[layer])[:, None, :]
        xp, (srf, srb, sgf, sgb) = _block(xp, mod_ctx, _shift_seq, (zr, zr, zg, zg), p)
        new_rf.append(srf)
        new_rb.append(srb)
        new_gf.append(sgf)
        new_gb.append(sgb)
        xs, _ = _block(xs, mod_lat, _shift_grid,
                       (state_rwkv_fwd[:, layer], state_rwkv_bwd[:, layer],
                        state_gla_fwd[:, layer], state_gla_bwd[:, layer]), p)
    y_prompt = _rmsnorm(xp, final_norm_g)
    y_sample = _rmsnorm(xs, final_norm_g)
    new_state_rwkv_fwd = jnp.stack(new_rf, axis=1).astype(x_prompt.dtype)
    new_state_rwkv_bwd = jnp.stack(new_rb, axis=1).astype(x_prompt.dtype)
    new_state_gla_fwd = jnp.stack(new_gf, axis=1).astype(x_prompt.dtype)
    new_state_gla_bwd = jnp.stack(new_gb, axis=1).astype(x_prompt.dtype)
    return (y_prompt, y_sample, new_state_rwkv_fwd, new_state_rwkv_bwd, new_state_gla_fwd, new_state_gla_bwd)
```

```python
import functools

import jax
import jax.numpy as jnp
from jax import lax
from jax.experimental import pallas as pl
from jax.experimental.pallas import tpu as pltpu

F32 = jnp.float32
BF16 = jnp.bfloat16
HI = lax.Precision.HIGHEST

D_MODEL = 1024
GRID_W = 64
RWKV_WIDTH = 512
RWKV_HEAD_DIM = 64
RWKV_HEADS = 8
GLA_HEADS = 4
GLA_KEY_DIM = 64
GLA_VAL_DIM = 128
GLA_QK_WIDTH = 256
GLA_V_WIDTH = 512
GLA_GATE_NORMALIZER = 16.0
N_MOD = 6
RMS_EPS = 1e-6
LNX_EPS = 64e-5
GLA_NORM_EPS = 1e-5

LANES = 128
RWKV_CHUNK = 64
GLA_CHUNK = 16
SCAN_BLOCK = 64
PRE_BLOCK = 256
POST_BLOCK = 512
VMEM_LIMIT = 56 * 1024 * 1024


def _dot(a, b, prec=None):
    return lax.dot_general(a, b, (((1,), (0,)), ((), ())), precision=prec, preferred_element_type=F32)


def _dot_nt(a, b, prec=None):
    return lax.dot_general(a, b, (((1,), (1,)), ((), ())), precision=prec, preferred_element_type=F32)


def _dot_tn(a, b, prec=None):
    return lax.dot_general(a, b, (((0,), (0,)), ((), ())), precision=prec, preferred_element_type=F32)


def _bdot(a, w_ref):
    return _dot(a.astype(BF16), w_ref[...])


def _sigmoid(x):
    return 1.0 / (1.0 + jnp.exp(-x))


def _softplus(x):
    return jnp.maximum(x, 0.0) + jnp.log(1.0 + jnp.exp(-jnp.abs(x)))


def _seg_sum(x, ind_ref):
    ind = ind_ref[...]
    hi = x.astype(BF16)
    r1 = x - hi.astype(F32)
    mid = r1.astype(BF16)
    lo = (r1 - mid.astype(F32)).astype(BF16)
    return _dot(hi, ind) + _dot(mid, ind) + _dot(lo, ind)


def _rms(x):
    return x * lax.rsqrt(jnp.mean(x * x, axis=-1, keepdims=True) + RMS_EPS)


def _mod_kernel(c_ref, w_ref, b_ref, o_ref):
    c = c_ref[...]
    cond = c * _sigmoid(c)
    o_ref[...] = _dot(cond.astype(BF16), w_ref[...].astype(BF16)) + b_ref[...]


def _modulation(craw, ada_w, ada_b):
    n = ada_w.shape[1]
    bn = 1536
    return pl.pallas_call(
        _mod_kernel,
        grid=(n // bn,),
        in_specs=[
            pl.BlockSpec((8, D_MODEL), lambda j: (0, 0)),
            pl.BlockSpec((D_MODEL, bn), lambda j: (0, j)),
            pl.BlockSpec((1, bn), lambda j: (0, j)),
        ],
        out_specs=pl.BlockSpec((8, bn), lambda j: (0, j)),
        out_shape=jax.ShapeDtypeStruct((8, n), F32),
        compiler_params=pltpu.CompilerParams(dimension_semantics=("arbitrary",), vmem_limit_bytes=VMEM_LIMIT),
        name="modulation",
    )(craw, ada_w, ada_b)


def _pre_kernel(*refs, grid_shift, tb):
    if grid_shift:
        x_ref, xp_ref, xn_ref = refs[:3]
        refs = refs[3:]
    else:
        x_ref = refs[0]
        refs = refs[1:]
    (mod_ref, n1g_ref, wrkv_ref, wrest_ref, gk1_ref, w1_ref, a1_ref, g1_ref, w2_ref, a2_ref, g2_ref,
     gk2_ref, murkv_ref, muwag_ref, w0_ref, a0_ref, gkb_ref, kk_ref, ka_ref, rk_ref, ind_ref) = refs[:21]
    (r_o, v_o, kap_o, kdf_o, kdb_o, bf_o, bb_o, lwf_o, lwb_o, gate_o, bonus_o,
     q_o, kg_o, laf_o, lab_o, vg_o, gsil_o) = refs[21:]
    D = D_MODEL
    R = RWKV_WIDTH
    m = mod_ref[0]
    sh1 = m[:, 0:D]
    sc1 = m[:, D:2 * D]
    n1g = n1g_ref[...]

    def normmod(xx):
        return _rms(xx) * n1g * (1.0 + sc1) + sh1

    h = normmod(x_ref[0])
    row = lax.broadcasted_iota(jnp.int32, (tb, 1), 0)
    if grid_shift:
        i = pl.program_id(1)
        n = pl.num_programs(1)
        hp = normmod(xp_ref[0]) * (i > 0).astype(F32)
        hn = normmod(xn_ref[0]) * (i < n - 1).astype(F32)
        hext = jnp.concatenate([hp, h, hn], axis=0)
        col = row % GRID_W
        m_l = (col != 0).astype(F32)
        m_r = (col != GRID_W - 1).astype(F32)
        ne = tb + 2 * GRID_W

        def shift(ext):
            up = ext[0:tb]
            down = ext[2 * GRID_W:2 * GRID_W + tb]
            left = pltpu.roll(ext, 1, 0)[GRID_W:GRID_W + tb]
            right = pltpu.roll(ext, ne - 1, 0)[GRID_W:GRID_W + tb]
            return 0.25 * (up + down + m_l * left + m_r * right)

        halo = GRID_W
    else:
        hext = h
        m_l = (row != 0).astype(F32)
        m_r = (row != tb - 1).astype(F32)

        def shift(ext):
            return 0.5 * (m_l * pltpu.roll(ext, 1, 0) + m_r * pltpu.roll(ext, tb - 1, 0))

        halo = 0

    rkv_ext = _bdot(hext, wrkv_ref)
    rkv = rkv_ext[halo:halo + tb]
    rkv = rkv + murkv_ref[...] * (shift(rkv_ext) - rkv)
    r = rkv[:, 0:R]
    k = rkv[:, R:2 * R]
    v = rkv[:, 2 * R:3 * R]

    dh = shift(hext) - h
    mu = muwag_ref[...]
    xw = h + mu[0:1] * dh
    xa = h + mu[1:2] * dh
    xg = h + mu[2:3] * dh

    z = w0_ref[...] + _bdot(jnp.tanh(_bdot(xw, w1_ref)), w2_ref)
    lw = -jnp.exp(-_softplus(-z) - 0.5)
    a = _sigmoid(a0_ref[...] + _bdot(_bdot(xa, a1_ref), a2_ref))
    gate = _bdot(_sigmoid(_bdot(xg, g1_ref)), g2_ref)

    kap = k * kk_ref[...]
    kap = kap * lax.rsqrt(jnp.maximum(_seg_sum(kap * kap, ind_ref), 1e-12))
    ka = ka_ref[...]
    a_f = a[:, 0:R]
    a_b = a[:, R:2 * R]
    kd_f = k * (1.0 + (a_f - 1.0) * ka)
    kd_b = k * (1.0 + (a_b - 1.0) * ka)
    bonus = _seg_sum(r * (kd_f + kd_b) * rk_ref[...], ind_ref) * v

    r_o[0] = r
    v_o[0] = v
    kap_o[0] = kap
    kdf_o[0] = kd_f
    kdb_o[0] = kd_b
    bf_o[0] = a_f * kap
    bb_o[0] = a_b * kap
    lwf_o[0] = lw[:, 0:R]
    lwb_o[0] = lw[:, R:2 * R]
    gate_o[0] = gate
    bonus_o[0] = bonus

    rest = _bdot(h, wrest_ref)
    Q = GLA_QK_WIDTH
    q_o[0] = rest[:, 0:Q] * (GLA_KEY_DIM ** -0.5)
    kg_o[0] = rest[:, Q:2 * Q]
    vg_o[0] = rest[:, 2 * Q:2 * Q + GLA_V_WIDTH]
    gg = rest[:, 2 * Q + GLA_V_WIDTH:]
    gsil_o[0] = gg * _sigmoid(gg)
    logits = _bdot(_bdot(h, gk1_ref), gk2_ref) + gkb_ref[...]
    la = -_softplus(-logits) * (1.0 / GLA_GATE_NORMALIZER)
    laf_o[0] = la[:, 0:Q]
    lab_o[0] = la[:, Q:2 * Q]


def _const_spec(shape):
    nd = len(shape)
    return pl.BlockSpec(shape, lambda b, i: (0,) * nd)


def _pre(x, mod3, mod_row, weights, grid_shift):
    B, L, D = x.shape
    tb = PRE_BLOCK
    nblk = L // tb
    if not grid_shift:
        assert nblk == 1
    x_spec = pl.BlockSpec((1, tb, D), lambda b, i: (b, i, 0))
    in_specs = [x_spec]
    args = [x]
    if grid_shift:
        per = tb // GRID_W
        nrow = L // GRID_W
        in_specs += [
            pl.BlockSpec((1, GRID_W, D), lambda b, i: (b, jnp.maximum(i * per - 1, 0), 0)),
            pl.BlockSpec((1, GRID_W, D), lambda b, i: (b, jnp.minimum((i + 1) * per, nrow - 1), 0)),
        ]
        args += [x, x]
    in_specs.append(pl.BlockSpec((1, 1, N_MOD * D), lambda b, i: (mod_row(b), 0, 0)))
    args.append(mod3)
    for w in weights:
        in_specs.append(_const_spec(w.shape))
        args.append(w)
    widths = [512] * 11 + [256, 256, 256, 256, 512, 512]
    out_specs = [pl.BlockSpec((1, tb, w), lambda b, i: (b, i, 0)) for w in widths]
    out_shape = [jax.ShapeDtypeStruct((B, L, w), F32) for w in widths]
    return pl.pallas_call(
        functools.partial(_pre_kernel, grid_shift=grid_shift, tb=tb),
        grid=(B, nblk),
        in_specs=in_specs,
        out_specs=out_specs,
        out_shape=out_shape,
        compiler_params=pltpu.CompilerParams(dimension_semantics=("parallel", "arbitrary"),
                                             vmem_limit_bytes=VMEM_LIMIT),
        name="pre_grid" if grid_shift else "pre_seq",
    )(*args)


def _inv_unit_triangular(lm, eye, blk16, blk32):
    l0 = jnp.where(blk16, lm, 0.0)
    l2 = _dot(l0, l0, HI)
    l4 = _dot(l2, l2, HI)
    l8 = _dot(l4, l4, HI)
    t = _dot(eye - l0, eye + l2, HI)
    t = _dot(t, eye + l4, HI)
    t = _dot(t, eye + l8, HI)
    off1 = jnp.where(jnp.logical_and(blk32, jnp.logical_not(blk16)), lm, 0.0)
    t = t - _dot(_dot(t, off1, HI), t, HI)
    off2 = jnp.where(blk32, 0.0, lm)
    t = t - _dot(_dot(t, off2, HI), t, HI)
    return t


def _rwkv_chunk(r, k, v, kap, b, lw, s, rev, cst):
    C = RWKV_CHUNK
    (incl_f, strict2, incl2, m0, cm0, eye, blk16, blk32, blk64) = cst[rev]
    cum = _dot(incl_f, lw, HI)
    cumx = cum - lw
    if rev:
        mid = cum[C // 2:C // 2 + 1]
        tot = cum[0:1]
    else:
        mid = cum[C // 2 - 1:C // 2]
        tot = cum[C - 1:C]
    e_mid = jnp.exp(mid)
    rt = r * jnp.exp(cum - mid)
    kt = kap * jnp.exp(cumx - mid)
    es = jnp.exp(mid - cum)
    bh = b * es
    kh = k * es
    e_end = jnp.exp(tot - mid)

    def stack(x):
        return jnp.concatenate([jnp.where(m0, x, 0.0), jnp.where(m0, 0.0, x)], axis=0)

    def blockdiag(side):
        return jnp.concatenate([jnp.where(cm0, side, 0.0), jnp.where(cm0, 0.0, side)], axis=0)

    g = _dot_nt(jnp.concatenate([kt, rt], axis=0),
                jnp.concatenate([stack(bh), stack(kh)], axis=0), HI)
    side_ab = jnp.where(strict2, g[0:C, 0:2 * C], 0.0)
    side_ak = jnp.where(strict2, g[0:C, 2 * C:4 * C], 0.0)
    side_rb = jnp.where(incl2, g[C:2 * C, 0:2 * C], 0.0)
    side_rk = jnp.where(incl2, g[C:2 * C, 2 * C:4 * C], 0.0)
    t = _inv_unit_triangular(blockdiag(side_ab), eye, blk16, blk32)
    vs = stack(v)
    akv = _dot(blockdiag(side_ak), vs, HI)
    wu = _dot(t, jnp.concatenate([stack(kt * e_mid), akv], axis=1), HI)
    u = -wu[:, LANES:] - _dot_nt(wu[:, :LANES], s, HI)
    y = _dot_nt(rt * e_mid, s, HI) + _dot(side_rb, u, HI) + _dot(side_rk, vs, HI)
    u_side = u[0:C] + u[C:2 * C]
    upd = _dot_tn(jnp.concatenate([u_side, v], axis=0),
                  jnp.concatenate([bh * e_end, kh * e_end], axis=0), HI)
    s_new = s * jnp.exp(tot) + jnp.where(blk64, upd, 0.0)
    return y, s_new


def _rwkv_consts():
    C = RWKV_CHUNK
    row = lax.broadcasted_iota(jnp.int32, (C, C), 0)
    col = lax.broadcasted_iota(jnp.int32, (C, C), 1)
    row2 = lax.broadcasted_iota(jnp.int32, (C, 2 * C), 0)
    col2 = lax.broadcasted_iota(jnp.int32, (C, 2 * C), 1)
    cs = col2 % C
    lane = lax.broadcasted_iota(jnp.int32, (1, LANES), 1)
    m0 = lane < RWKV_HEAD_DIM
    cm0 = lax.broadcasted_iota(jnp.int32, (1, 2 * C), 1) < C
    rr = lax.broadcasted_iota(jnp.int32, (2 * C, 2 * C), 0)
    cc = lax.broadcasted_iota(jnp.int32, (2 * C, 2 * C), 1)
    eye = (rr == cc).astype(F32)
    blk16 = (rr // 16) == (cc // 16)
    blk32 = (rr // 32) == (cc // 32)
    blk64 = (rr // 64) == (cc // 64)
    out = []
    for rev in (False, True):
        if rev:
            incl_f = (col >= row).astype(F32)
            strict2 = cs > row2
            incl2 = cs >= row2
        else:
            incl_f = (col <= row).astype(F32)
            strict2 = cs < row2
            incl2 = cs <= row2
        out.append((incl_f, strict2, incl2, m0, cm0, eye, blk16, blk32, blk64))
    return out


def _rwkv_kernel(rf_ref, vf_ref, kapf_ref, kdf_ref, bf_ref, lwf_ref,
                 rb_ref, vb_ref, kapb_ref, kdb_ref, bb_ref, lwb_ref, s0f_ref, s0b_ref,
                 yf_ref, yb_ref, sf_ref, sb_ref):
    i = pl.program_id(1)

    @pl.when(i == 0)
    def _():
        sf_ref[...] = s0f_ref[...]
        sb_ref[...] = s0b_ref[...]

    cst = _rwkv_consts()
    dirs = ((False, rf_ref, kdf_ref, vf_ref, kapf_ref, bf_ref, lwf_ref, sf_ref, yf_ref),
            (True, rb_ref, kdb_ref, vb_ref, kapb_ref, bb_ref, lwb_ref, sb_ref, yb_ref))
    for rev, r_ref, k_ref, v_ref, kap_ref, b_ref, lw_ref, s_ref, y_ref in dirs:
        for p in range(RWKV_WIDTH // LANES):
            sl = slice(p * LANES, (p + 1) * LANES)
            y, s_new = _rwkv_chunk(r_ref[0, :, sl], k_ref[0, :, sl], v_ref[0, :, sl], kap_ref[0, :, sl],
                                   b_ref[0, :, sl], lw_ref[0, :, sl], s_ref[0, p], rev, cst)
            y_ref[0, :, sl] = y
            s_ref[0, p] = s_new


def _rwkv_scan(r, v, kap, kd_f, kd_b, b_f, b_b, lw_f, lw_b, s0_f, s0_b):
    B, L, R = r.shape
    C = RWKV_CHUNK
    n = L // C
    fwd = pl.BlockSpec((1, C, R), lambda b, i: (b, i, 0))
    bwd = pl.BlockSpec((1, C, R), lambda b, i: (b, n - 1 - i, 0))
    st = pl.BlockSpec((1, R // LANES, LANES, LANES), lambda b, i: (b, 0, 0, 0))
    return pl.pallas_call(
        _rwkv_kernel,
        grid=(B, n),
        in_specs=[fwd] * 6 + [bwd] * 6 + [st, st],
        out_specs=[fwd, bwd, st, st],
        out_shape=[jax.ShapeDtypeStruct((B, L, R), F32)] * 2
        + [jax.ShapeDtypeStruct((B, R // LANES, LANES, LANES), F32)] * 2,
        compiler_params=pltpu.CompilerParams(dimension_semantics=("parallel", "arbitrary"),
                                             vmem_limit_bytes=VMEM_LIMIT),
        name="rwkv_scan",
    )(r, v, kap, kd_f, b_f, lw_f, r, v, kap, kd_b, b_b, lw_b, s0_f, s0_b)


def _gla_chunk(q, k, v, la, st, rev, cst):
    G = GLA_CHUNK
    incl_f, ind_v, sel, blk = cst[rev]
    cum = _dot(incl_f, la, HI)
    tot = cum[0:1] if rev else cum[G - 1:G]
    o = _dot_nt(q * jnp.exp(cum), st, HI)
    srow = lax.broadcasted_iota(jnp.int32, (G, 1), 0)
    xs = []
    for t in range(G):
        msk = (srow >= t) if rev else (srow <= t)
        e = jnp.exp(jnp.where(msk, cum[t:t + 1] - cum, 0.0))
        p_t = jnp.where(msk, e * (q[t:t + 1] * k), 0.0)
        xs.append(_dot(p_t, ind_v, HI) * v)
    o = o + _dot(sel, jnp.concatenate(xs, axis=0), HI)
    upd = _dot_tn(v, k * jnp.exp(tot - cum), HI)
    st_new = st * jnp.exp(tot) + jnp.where(blk, upd, 0.0)
    return o, st_new


def _gla_consts():
    G = GLA_CHUNK
    row = lax.broadcasted_iota(jnp.int32, (G, G), 0)
    col = lax.broadcasted_iota(jnp.int32, (G, G), 1)
    kc = lax.broadcasted_iota(jnp.int32, (LANES, 2 * LANES), 0)
    vc = lax.broadcasted_iota(jnp.int32, (LANES, 2 * LANES), 1)
    ind_v = ((kc // GLA_KEY_DIM) == (vc // GLA_VAL_DIM)).astype(F32)
    st = lax.broadcasted_iota(jnp.int32, (G, G * G), 0)
    sj = lax.broadcasted_iota(jnp.int32, (G, G * G), 1)
    sel = ((sj // G) == st).astype(F32)
    br = lax.broadcasted_iota(jnp.int32, (2 * LANES, LANES), 0)
    bc = lax.broadcasted_iota(jnp.int32, (2 * LANES, LANES), 1)
    blk = (br // GLA_VAL_DIM) == (bc // GLA_KEY_DIM)
    return [((col <= row).astype(F32), ind_v, sel, blk), ((col >= row).astype(F32), ind_v, sel, blk)]


def _gla_kernel(qf_ref, kf_ref, vf_ref, laf_ref, qb_ref, kb_ref, vb_ref, lab_ref, s0f_ref, s0b_ref,
                of_ref, ob_ref, sf_ref, sb_ref):
    i = pl.program_id(1)

    @pl.when(i == 0)
    def _():
        sf_ref[...] = s0f_ref[...]
        sb_ref[...] = s0b_ref[...]

    cst = _gla_consts()
    G = GLA_CHUNK
    nsub = SCAN_BLOCK // G
    dirs = ((False, qf_ref, kf_ref, vf_ref, laf_ref, sf_ref, of_ref),
            (True, qb_ref, kb_ref, vb_ref, lab_ref, sb_ref, ob_ref))
    for rev, q_ref, k_ref, v_ref, la_ref, s_ref, o_ref in dirs:
        for p in range(GLA_QK_WIDTH // LANES):
            ks = slice(p * LANES, (p + 1) * LANES)
            vs = slice(p * 2 * LANES, (p + 1) * 2 * LANES)
            st = s_ref[0, p]
            for j in range(nsub):
                c = nsub - 1 - j if rev else j
                rows = slice(c * G, (c + 1) * G)
                o, st = _gla_chunk(q_ref[0, rows, ks], k_ref[0, rows, ks], v_ref[0, rows, vs],
                                   la_ref[0, rows, ks], st, rev, cst)
                o_ref[0, rows, vs] = o
            s_ref[0, p] = st


def _gla_scan(q, k, v, la_f, la_b, s0_f, s0_b):
    B, L, Q = q.shape
    V = v.shape[-1]
    T = SCAN_BLOCK
    n = L // T
    npair = Q // LANES

    def fwd(w):
        return pl.BlockSpec((1, T, w), lambda b, i: (b, i, 0))

    def bwd(w):
        return pl.BlockSpec((1, T, w), lambda b, i: (b, n - 1 - i, 0))

    st = pl.BlockSpec((1, npair, 2 * LANES, LANES), lambda b, i: (b, 0, 0, 0))
    return pl.pallas_call(
        _gla_kernel,
        grid=(B, n),
        in_specs=[fwd(Q), fwd(Q), fwd(V), fwd(Q), bwd(Q), bwd(Q), bwd(V), bwd(Q), st, st],
        out_specs=[fwd(V), bwd(V), st, st],
        out_shape=[jax.ShapeDtypeStruct((B, L, V), F32)] * 2
        + [jax.ShapeDtypeStruct((B, npair, 2 * LANES, LANES), F32)] * 2,
        compiler_params=pltpu.CompilerParams(dimension_semantics=("parallel", "arbitrary"),
                                             vmem_limit_bytes=VMEM_LIMIT),
        name="gla_scan",
    )(q, k, v, la_f, q, k, v, la_b, s0_f, s0_b)


def _post_kernel(x_ref, mod_ref, yf_ref, yb_ref, of_ref, ob_ref, bonus_ref, gate_ref, gsil_ref,
                 lng_ref, lnb_ref, gng_ref, n2g_ref, fng_ref, ind_ref, wout_ref, w1_ref, w2_ref, o_ref):
    D = D_MODEL
    m = mod_ref[0]
    gt1 = m[:, 2 * D:3 * D]
    sh2 = m[:, 3 * D:4 * D]
    sc2 = m[:, 4 * D:5 * D]
    gt2 = m[:, 5 * D:6 * D]
    inv_n = 1.0 / RWKV_HEAD_DIM
    y = yf_ref[0] + yb_ref[0]
    mu = _seg_sum(y, ind_ref) * inv_n
    yc = y - mu
    var = _seg_sum(yc * yc, ind_ref) * inv_n
    yn = yc * lax.rsqrt(var + LNX_EPS) * lng_ref[...] + lnb_ref[...]
    rw = (yn + bonus_ref[0]) * gate_ref[0]
    o = of_ref[0] + ob_ref[0]
    gsil = gsil_ref[0]
    gng = gng_ref[...]
    parts = [rw]
    for hh in range(GLA_HEADS):
        sl = slice(hh * GLA_VAL_DIM, (hh + 1) * GLA_VAL_DIM)
        oh = o[:, sl]
        oh = oh * lax.rsqrt(jnp.mean(oh * oh, axis=-1, keepdims=True) + GLA_NORM_EPS)
        parts.append(oh * gng * gsil[:, sl])
    mix = jnp.concatenate(parts, axis=-1)
    x1 = x_ref[0] + gt1 * _bdot(mix, wout_ref)
    h2 = _rms(x1) * n2g_ref[...] * (1.0 + sc2) + sh2
    f = jnp.maximum(_bdot(h2, w1_ref), 0.0)
    x2 = x1 + gt2 * _bdot(f * f, w2_ref)
    o_ref[0] = _rms(x2) * fng_ref[...]


def _post(x, mod3, mod_row, y_f, y_b, o_f, o_b, bonus, gate, gsil, vecs, mats):
    B, L, D = x.shape
    tb = min(POST_BLOCK, L)
    nblk = L // tb

    def tok(w):
        return pl.BlockSpec((1, tb, w), lambda b, i: (b, i, 0))

    in_specs = [tok(D), pl.BlockSpec((1, 1, N_MOD * D), lambda b, i: (mod_row(b), 0, 0))]
    in_specs += [tok(512)] * 7
    in_specs += [_const_spec(w.shape) for w in vecs]
    in_specs += [pl.BlockSpec(w.shape, lambda b, i: (0, 0), pipeline_mode=pl.Buffered(1)) for w in mats]
    return pl.pallas_call(
        _post_kernel,
        grid=(B, nblk),
        in_specs=in_specs,
        out_specs=tok(D),
        out_shape=jax.ShapeDtypeStruct((B, L, D), F32),
        compiler_params=pltpu.CompilerParams(dimension_semantics=("parallel", "arbitrary"),
                                             vmem_limit_bytes=VMEM_LIMIT),
        name="post",
    )(x, mod3, y_f, y_b, o_f, o_b, bonus, gate, gsil, *vecs, *mats)


def _rwkv_state_to_pairs(s):
    B = s.shape[0]
    s = s.reshape(B, 4, 2, 64, 64)
    z = jnp.zeros_like(s[:, :, 0])
    top = jnp.concatenate([s[:, :, 0], z], axis=-1)
    bot = jnp.concatenate([z, s[:, :, 1]], axis=-1)
    return jnp.concatenate([top, bot], axis=-2)


def _rwkv_pairs_to_state(sp):
    B = sp.shape[0]
    a = sp[:, :, 0:64, 0:64]
    b = sp[:, :, 64:128, 64:128]
    return jnp.stack([a, b], axis=2).reshape(B, 8, 64, 64)


def _gla_state_to_pairs(s):
    B = s.shape[0]
    st = jnp.swapaxes(s, -1, -2).reshape(B, 2, 2, 128, 64)
    z = jnp.zeros_like(st[:, :, 0])
    top = jnp.concatenate([st[:, :, 0], z], axis=-1)
    bot = jnp.concatenate([z, st[:, :, 1]], axis=-1)
    return jnp.concatenate([top, bot], axis=-2)


def _gla_pairs_to_state(sp):
    B = sp.shape[0]
    a = sp[:, :, 0:128, 0:64]
    b = sp[:, :, 128:256, 64:128]
    st = jnp.stack([a, b], axis=2).reshape(B, 4, 128, 64)
    return jnp.swapaxes(st, -1, -2)


def _blockdiag2(a, b):
    za = jnp.zeros((a.shape[0], b.shape[1]), a.dtype)
    zb = jnp.zeros((b.shape[0], a.shape[1]), a.dtype)
    return jnp.concatenate([jnp.concatenate([a, za], axis=1), jnp.concatenate([zb, b], axis=1)], axis=0)


def kernel(x_prompt, x_sample, c, state_rwkv_fwd, state_rwkv_bwd, state_gla_fwd, state_gla_bwd, c_ctx, ada_w, ada_b, norm1_g, norm2_g, w_in, rwkv_mu_rkv, rwkv_mu_wag, rwkv_w0, rwkv_w1, rwkv_w2, rwkv_a0, rwkv_a1, rwkv_a2, rwkv_g1, rwkv_g2, rwkv_k_k, rwkv_k_a, rwkv_r_k, rwkv_lnx_g, rwkv_lnx_b, gla_gk1, gla_gk2, gla_gk_b, gla_norm_g, w_out, mlp_w1, mlp_w2, final_norm_g):
    D = D_MODEL
    R = RWKV_WIDTH
    nb = x_prompt.shape[0]
    nd = x_sample.shape[0]
    assert ada_w.shape[0] == 1, "single-layer step"
    layer = 0

    craw = jnp.concatenate([c_ctx[None, :], c, jnp.zeros((8 - 1 - nd, D), F32)], axis=0)
    mod = _modulation(craw, ada_w[layer], ada_b[layer][None, :])
    mod3 = mod.reshape(8, 1, N_MOD * D)

    bf = lambda t: t.astype(BF16)
    row = lambda t: t.reshape(1, -1).astype(F32)
    w_in_l = w_in[layer]
    ind = (jnp.arange(R)[:, None] // RWKV_HEAD_DIM == jnp.arange(R)[None, :] // RWKV_HEAD_DIM).astype(BF16)
    pre_w = [
        row(norm1_g[layer]),
        bf(w_in_l[:, :3 * R]),
        bf(w_in_l[:, 3 * R:]),
        bf(jnp.concatenate([gla_gk1[layer, 0], gla_gk1[layer, 1]], axis=1)),
        bf(jnp.concatenate([rwkv_w1[layer, 0], rwkv_w1[layer, 1]], axis=1)),
        bf(jnp.concatenate([rwkv_a1[layer, 0], rwkv_a1[layer, 1]], axis=1)),
        bf(rwkv_g1[layer]),
        bf(_blockdiag2(rwkv_w2[layer, 0], rwkv_w2[layer, 1])),
        bf(_blockdiag2(rwkv_a2[layer, 0], rwkv_a2[layer, 1])),
        bf(rwkv_g2[layer]),
        bf(_blockdiag2(gla_gk2[layer, 0], gla_gk2[layer, 1])),
        row(rwkv_mu_rkv[layer]),
        rwkv_mu_wag[layer],
        row(rwkv_w0[layer]),
        row(rwkv_a0[layer]),
        row(gla_gk_b[layer]),
        row(rwkv_k_k[layer]),
        row(rwkv_k_a[layer]),
        row(rwkv_r_k[layer]),
        ind,
    ]
    post_vecs = [row(rwkv_lnx_g[layer]), row(rwkv_lnx_b[layer]), row(gla_norm_g[layer]),
                 row(norm2_g[layer]), row(final_norm_g)]
    post_mats = [ind, bf(w_out[layer]), bf(mlp_w1[layer]), bf(mlp_w2[layer])]

    def run_group(x, mod_row, grid_shift, s_rf, s_rb, s_gf, s_gb):
        (r, v, kap, kd_f, kd_b, b_f, b_b, lw_f, lw_b, gate, bonus,
         q, kg, la_f, la_b, vg, gsil) = _pre(x, mod3, mod_row, pre_w, grid_shift)
        y_f, y_b, n_rf, n_rb = _rwkv_scan(r, v, kap, kd_f, kd_b, b_f, b_b, lw_f, lw_b,
                                          _rwkv_state_to_pairs(s_rf), _rwkv_state_to_pairs(s_rb))
        o_f, o_b, n_gf, n_gb = _gla_scan(q, kg, vg, la_f, la_b,
                                         _gla_state_to_pairs(s_gf), _gla_state_to_pairs(s_gb))
        y = _post(x, mod3, mod_row, y_f, y_b, o_f, o_b, bonus, gate, gsil, post_vecs, post_mats)
        return y, (n_rf, n_rb, n_gf, n_gb)

    zr = jnp.zeros((nb, RWKV_HEADS, RWKV_HEAD_DIM, RWKV_HEAD_DIM), F32)
    zg = jnp.zeros((nb, GLA_HEADS, GLA_KEY_DIM, GLA_VAL_DIM), F32)
    y_prompt, (n_rf, n_rb, n_gf, n_gb) = run_group(x_prompt, lambda b: 0, False, zr, zr, zg, zg)
    y_sample, _ = run_group(x_sample, lambda b: b + 1, True,
                            state_rwkv_fwd[:, layer], state_rwkv_bwd[:, layer],
                            state_gla_fwd[:, layer], state_gla_bwd[:, layer])
    return (y_prompt, y_sample,
            _rwkv_pairs_to_state(n_rf)[:, None], _rwkv_pairs_to_state(n_rb)[:, None],
            _gla_pairs_to_state(n_gf)[:, None], _gla_pairs_to_state(n_gb)[:, None])
```

```python
import functools

import jax
import jax.numpy as jnp
from jax import lax
from jax.experimental import pallas as pl
from jax.experimental.pallas import tpu as pltpu

F32 = jnp.float32
BF16 = jnp.bfloat16
HI = lax.Precision.HIGHEST

D_MODEL = 1024
GRID_W = 64
RWKV_WIDTH = 512
RWKV_HEAD_DIM = 64
RWKV_HEADS = 8
GLA_HEADS = 4
GLA_KEY_DIM = 64
GLA_VAL_DIM = 128
GLA_QK_WIDTH = 256
GLA_V_WIDTH = 512
GLA_GATE_NORMALIZER = 16.0
N_MOD = 6
RMS_EPS = 1e-6
LNX_EPS = 64e-5
GLA_NORM_EPS = 1e-5

LANES = 128
RWKV_CHUNK = 64
GLA_CHUNK = 16
SCAN_BLOCK = 64
PRE_BLOCK = 256
POST_BLOCK = 512
VMEM_LIMIT = 56 * 1024 * 1024

P_SCORE = "b1"
P_INV = "b1"
P_SOLVE = "b1"
P_STATE = "b1"
P_OUT = "b1"
P_GSTATE = "b1"
P_GATT = "b1"


def _dot(a, b, prec=None):
    return lax.dot_general(a, b, (((1,), (0,)), ((), ())), precision=prec, preferred_element_type=F32)


def _dot_nt(a, b, prec=None):
    return lax.dot_general(a, b, (((1,), (1,)), ((), ())), precision=prec, preferred_element_type=F32)


def _dot_tn(a, b, prec=None):
    return lax.dot_general(a, b, (((0,), (0,)), ((), ())), precision=prec, preferred_element_type=F32)


_NN = (((1,), (0,)), ((), ()))
_NT = (((1,), (1,)), ((), ()))
_TN = (((0,), (0,)), ((), ()))


def _split2(x):
    hi = x.astype(BF16)
    return hi, (x - hi.astype(F32)).astype(BF16)


def _mm(a, b, dims, mode):
    dg = functools.partial(lax.dot_general, dimension_numbers=dims, preferred_element_type=F32)
    if mode == "hi":
        return dg(a, b, precision=HI)
    if mode == "b1":
        return dg(a.astype(BF16), b.astype(BF16))
    a1, a2 = _split2(a)
    b1, b2 = _split2(b)
    return dg(a1, b1) + dg(a1, b2) + dg(a2, b1)


def _cumsum_mm(tri, x):
    hi = x.astype(BF16)
    r1 = x - hi.astype(F32)
    mid = r1.astype(BF16)
    lo = (r1 - mid.astype(F32)).astype(BF16)
    return _dot(tri, hi) + _dot(tri, mid) + _dot(tri, lo)


def _bdot(a, w_ref):
    return _dot(a.astype(BF16), w_ref[...])


def _sigmoid(x):
    return 1.0 / (1.0 + jnp.exp(-x))


def _softplus(x):
    return jnp.maximum(x, 0.0) + jnp.log(1.0 + jnp.exp(-jnp.abs(x)))


def _seg_sum(x, ind_ref):
    ind = ind_ref[...]
    hi = x.astype(BF16)
    r1 = x - hi.astype(F32)
    mid = r1.astype(BF16)
    lo = (r1 - mid.astype(F32)).astype(BF16)
    return _dot(hi, ind) + _dot(mid, ind) + _dot(lo, ind)


def _rms(x):
    return x * lax.rsqrt(jnp.mean(x * x, axis=-1, keepdims=True) + RMS_EPS)


def _mod_kernel(c_ref, w_ref, b_ref, o_ref):
    c = c_ref[...]
    cond = c * _sigmoid(c)
    o_ref[...] = _dot(cond.astype(BF16), w_ref[...].astype(BF16)) + b_ref[...]


def _modulation(craw, ada_w, ada_b):
    n = ada_w.shape[1]
    bn = 1536
    return pl.pallas_call(
        _mod_kernel,
        grid=(n // bn,),
        in_specs=[
            pl.BlockSpec((8, D_MODEL), lambda j: (0, 0)),
            pl.BlockSpec((D_MODEL, bn), lambda j: (0, j)),
            pl.BlockSpec((1, bn), lambda j: (0, j)),
        ],
        out_specs=pl.BlockSpec((8, bn), lambda j: (0, j)),
        out_shape=jax.ShapeDtypeStruct((8, n), F32),
        compiler_params=pltpu.CompilerParams(dimension_semantics=("arbitrary",), vmem_limit_bytes=VMEM_LIMIT),
        name="modulation",
    )(craw, ada_w, ada_b)


def _pre_kernel(*refs, grid_shift, tb):
    if grid_shift:
        x_ref, xp_ref, xn_ref = refs[:3]
        refs = refs[3:]
    else:
        x_ref = refs[0]
        refs = refs[1:]
    (mod_ref, n1g_ref, wrkv_ref, wrest_ref, gk1_ref, w1_ref, a1_ref, g1_ref, w2_ref, a2_ref, g2_ref,
     gk2_ref, murkv_ref, muwag_ref, w0_ref, a0_ref, gkb_ref, kk_ref, ka_ref, rk_ref, ind_ref) = refs[:21]
    (r_o, v_o, kap_o, kdf_o, kdb_o, bf_o, bb_o, lwf_o, lwb_o, gate_o, bonus_o,
     q_o, kg_o, laf_o, lab_o, vg_o, gsil_o) = refs[21:]
    D = D_MODEL
    R = RWKV_WIDTH
    m = mod_ref[0]
    sh1 = m[:, 0:D]
    sc1 = m[:, D:2 * D]
    n1g = n1g_ref[...]

    def normmod(xx):
        return _rms(xx) * n1g * (1.0 + sc1) + sh1

    h = normmod(x_ref[0])
    row = lax.broadcasted_iota(jnp.int32, (tb, 1), 0)
    if grid_shift:
        i = pl.program_id(1)
        n = pl.num_programs(1)
        hp = normmod(xp_ref[0]) * (i > 0).astype(F32)
        hn = normmod(xn_ref[0]) * (i < n - 1).astype(F32)
        hext = jnp.concatenate([hp, h, hn], axis=0)
        col = row % GRID_W
        m_l = (col != 0).astype(F32)
        m_r = (col != GRID_W - 1).astype(F32)
        ne = tb + 2 * GRID_W

        def shift(ext):
            up = ext[0:tb]
            down = ext[2 * GRID_W:2 * GRID_W + tb]
            left = pltpu.roll(ext, 1, 0)[GRID_W:GRID_W + tb]
            right = pltpu.roll(ext, ne - 1, 0)[GRID_W:GRID_W + tb]
            return 0.25 * (up + down + m_l * left + m_r * right)

        halo = GRID_W
    else:
        hext = h
        m_l = (row != 0).astype(F32)
        m_r = (row != tb - 1).astype(F32)

        def shift(ext):
            return 0.5 * (m_l * pltpu.roll(ext, 1, 0) + m_r * pltpu.roll(ext, tb - 1, 0))

        halo = 0

    rkv_ext = _bdot(hext, wrkv_ref)
    rkv = rkv_ext[halo:halo + tb]
    rkv = rkv + murkv_ref[...] * (shift(rkv_ext) - rkv)
    r = rkv[:, 0:R]
    k = rkv[:, R:2 * R]
    v = rkv[:, 2 * R:3 * R]

    dh = shift(hext) - h
    mu = muwag_ref[...]
    xw = h + mu[0:1] * dh
    xa = h + mu[1:2] * dh
    xg = h + mu[2:3] * dh

    z = w0_ref[...] + _bdot(jnp.tanh(_bdot(xw, w1_ref)), w2_ref)
    lw = -jnp.exp(-_softplus(-z) - 0.5)
    a = _sigmoid(a0_ref[...] + _bdot(_bdot(xa, a1_ref), a2_ref))
    gate = _bdot(_sigmoid(_bdot(xg, g1_ref)), g2_ref)

    kap = k * kk_ref[...]
    kap = kap * lax.rsqrt(jnp.maximum(_seg_sum(kap * kap, ind_ref), 1e-12))
    ka = ka_ref[...]
    a_f = a[:, 0:R]
    a_b = a[:, R:2 * R]
    kd_f = k * (1.0 + (a_f - 1.0) * ka)
    kd_b = k * (1.0 + (a_b - 1.0) * ka)
    bonus = _seg_sum(r * (kd_f + kd_b) * rk_ref[...], ind_ref) * v

    r_o[0] = r
    v_o[0] = v
    kap_o[0] = kap
    kdf_o[0] = kd_f
    kdb_o[0] = kd_b
    bf_o[0] = a_f * kap
    bb_o[0] = a_b * kap
    lwf_o[0] = lw[:, 0:R]
    lwb_o[0] = lw[:, R:2 * R]
    gate_o[0] = gate
    bonus_o[0] = bonus

    rest = _bdot(h, wrest_ref)
    Q = GLA_QK_WIDTH
    q_o[0] = rest[:, 0:Q] * (GLA_KEY_DIM ** -0.5)
    kg_o[0] = rest[:, Q:2 * Q]
    vg_o[0] = rest[:, 2 * Q:2 * Q + GLA_V_WIDTH]
    gg = rest[:, 2 * Q + GLA_V_WIDTH:]
    gsil_o[0] = gg * _sigmoid(gg)
    logits = _bdot(_bdot(h, gk1_ref), gk2_ref) + gkb_ref[...]
    la = -_softplus(-logits) * (1.0 / GLA_GATE_NORMALIZER)
    laf_o[0] = la[:, 0:Q]
    lab_o[0] = la[:, Q:2 * Q]


def _const_spec(shape):
    nd = len(shape)
    return pl.BlockSpec(shape, lambda b, i: (0,) * nd)


def _pre(x, mod3, mod_row, weights, grid_shift):
    B, L, D = x.shape
    tb = PRE_BLOCK
    nblk = L // tb
    if not grid_shift:
        assert nblk == 1
    x_spec = pl.BlockSpec((1, tb, D), lambda b, i: (b, i, 0))
    in_specs = [x_spec]
    args = [x]
    if grid_shift:
        per = tb // GRID_W
        nrow = L // GRID_W
        in_specs += [
            pl.BlockSpec((1, GRID_W, D), lambda b, i: (b, jnp.maximum(i * per - 1, 0), 0)),
            pl.BlockSpec((1, GRID_W, D), lambda b, i: (b, jnp.minimum((i + 1) * per, nrow - 1), 0)),
        ]
        args += [x, x]
    in_specs.append(pl.BlockSpec((1, 1, N_MOD * D), lambda b, i: (mod_row(b), 0, 0)))
    args.append(mod3)
    for w in weights:
        in_specs.append(_const_spec(w.shape))
        args.append(w)
    widths = [512] * 11 + [256, 256, 256, 256, 512, 512]
    out_specs = [pl.BlockSpec((1, tb, w), lambda b, i: (b, i, 0)) for w in widths]
    out_shape = [jax.ShapeDtypeStruct((B, L, w), F32) for w in widths]
    return pl.pallas_call(
        functools.partial(_pre_kernel, grid_shift=grid_shift, tb=tb),
        grid=(B, nblk),
        in_specs=in_specs,
        out_specs=out_specs,
        out_shape=out_shape,
        compiler_params=pltpu.CompilerParams(dimension_semantics=("parallel", "arbitrary"),
                                             vmem_limit_bytes=VMEM_LIMIT),
        name="pre_grid" if grid_shift else "pre_seq",
    )(*args)


def _inv_unit_triangular(lms, eye, blk16, blk32):
    mm = functools.partial(_mm, dims=_NN, mode=P_INV)
    l0 = [jnp.where(blk16, lm, 0.0) for lm in lms]
    l2 = [mm(a, a) for a in l0]
    l4 = [mm(a, a) for a in l2]
    l8 = [mm(a, a) for a in l4]
    t = [mm(eye - a, eye + b) for a, b in zip(l0, l2)]
    t = [mm(a, eye + b) for a, b in zip(t, l4)]
    t = [mm(a, eye + b) for a, b in zip(t, l8)]
    off1_mask = jnp.logical_and(blk32, jnp.logical_not(blk16))
    x = [mm(a, jnp.where(off1_mask, lm, 0.0)) for a, lm in zip(t, lms)]
    t = [a - mm(b, a) for a, b in zip(t, x)]
    x = [mm(a, jnp.where(blk32, 0.0, lm)) for a, lm in zip(t, lms)]
    t = [a - mm(b, a) for a, b in zip(t, x)]
    return t


def _rwkv_chunks(units, cst):
    C = RWKV_CHUNK
    m0, cm0, eye, blk16, blk32, blk64 = cst[2]

    def stack(x):
        return jnp.concatenate([jnp.where(m0, x, 0.0), jnp.where(m0, 0.0, x)], axis=0)

    def blockdiag(side):
        return jnp.concatenate([jnp.where(cm0, side, 0.0), jnp.where(cm0, 0.0, side)], axis=0)

    cums = [_cumsum_mm(cst[rev][0], lw) for (_, _, _, _, _, lw, _, rev) in units]
    prep = []
    for (r, k, v, kap, b, lw, s, rev), cum in zip(units, cums):
        cumx = cum - lw
        if rev:
            mid = cum[C // 2:C // 2 + 1]
            tot = cum[0:1]
        else:
            mid = cum[C // 2 - 1:C // 2]
            tot = cum[C - 1:C]
        e_mid = jnp.exp(mid)
        rt = r * jnp.exp(cum - mid)
        kt = kap * jnp.exp(cumx - mid)
        es = jnp.exp(mid - cum)
        e_end = jnp.exp(tot - mid)
        prep.append(dict(rt=rt, kt=kt, bh=b * es, kh=k * es, r0=rt * e_mid, k0=stack(kt * e_mid),
                         be=b * es * e_end, ke=k * es * e_end, vs=stack(v), v=v, s=s,
                         e_tot=jnp.exp(tot), strict2=cst[rev][1], incl2=cst[rev][2]))
    gs = [_mm(jnp.concatenate([p["kt"], p["rt"]], axis=0),
              jnp.concatenate([stack(p["bh"]), stack(p["kh"])], axis=0), _NT, P_SCORE) for p in prep]
    for p, g in zip(prep, gs):
        p["ab"] = blockdiag(jnp.where(p["strict2"], g[0:C, 0:2 * C], 0.0))
        p["ak"] = blockdiag(jnp.where(p["strict2"], g[0:C, 2 * C:4 * C], 0.0))
        p["rb"] = jnp.where(p["incl2"], g[C:2 * C, 0:2 * C], 0.0)
        p["rk"] = jnp.where(p["incl2"], g[C:2 * C, 2 * C:4 * C], 0.0)
    akv = [_mm(p["ak"], p["vs"], _NN, P_SOLVE) for p in prep]
    ts = _inv_unit_triangular([p["ab"] for p in prep], eye, blk16, blk32)
    wu = [_mm(t, jnp.concatenate([p["k0"], a], axis=1), _NN, P_SOLVE) for t, p, a in zip(ts, prep, akv)]
    ws = [_mm(x[:, :LANES], p["s"], _NT, P_STATE) for x, p in zip(wu, prep)]
    us = [-x[:, LANES:] - w for x, w in zip(wu, ws)]
    upd = [_mm(jnp.concatenate([u[0:C] + u[C:2 * C], p["v"]], axis=0),
               jnp.concatenate([p["be"], p["ke"]], axis=0), _TN, P_STATE) for u, p in zip(us, prep)]
    s_new = [p["s"] * p["e_tot"] + jnp.where(blk64, d, 0.0) for p, d in zip(prep, upd)]
    ys = [_mm(p["r0"], p["s"], _NT, P_STATE) + _mm(p["rb"], u, _NN, P_OUT) + _mm(p["rk"], p["vs"], _NN, P_OUT)
          for p, u in zip(prep, us)]
    return ys, s_new


def _rwkv_consts():
    C = RWKV_CHUNK
    row = lax.broadcasted_iota(jnp.int32, (C, C), 0)
    col = lax.broadcasted_iota(jnp.int32, (C, C), 1)
    row2 = lax.broadcasted_iota(jnp.int32, (C, 2 * C), 0)
    col2 = lax.broadcasted_iota(jnp.int32, (C, 2 * C), 1)
    cs = col2 % C
    lane = lax.broadcasted_iota(jnp.int32, (1, LANES), 1)
    m0 = lane < RWKV_HEAD_DIM
    cm0 = lax.broadcasted_iota(jnp.int32, (1, 2 * C), 1) < C
    rr = lax.broadcasted_iota(jnp.int32, (2 * C, 2 * C), 0)
    cc = lax.broadcasted_iota(jnp.int32, (2 * C, 2 * C), 1)
    eye = (rr == cc).astype(F32)
    blk16 = (rr // 16) == (cc // 16)
    blk32 = (rr // 32) == (cc // 32)
    blk64 = (rr // 64) == (cc // 64)
    fwd = ((col <= row).astype(BF16), cs < row2, cs <= row2)
    bwd = ((col >= row).astype(BF16), cs > row2, cs >= row2)
    return fwd, bwd, (m0, cm0, eye, blk16, blk32, blk64)


def _rwkv_kernel(rf_ref, vf_ref, kapf_ref, kdf_ref, bf_ref, lwf_ref,
                 rb_ref, vb_ref, kapb_ref, kdb_ref, bb_ref, lwb_ref, s0f_ref, s0b_ref,
                 yf_ref, yb_ref, sf_ref, sb_ref):
    i = pl.program_id(1)

    @pl.when(i == 0)
    def _():
        sf_ref[...] = s0f_ref[...]
        sb_ref[...] = s0b_ref[...]

    cst = _rwkv_consts()
    dirs = ((False, rf_ref, kdf_ref, vf_ref, kapf_ref, bf_ref, lwf_ref, sf_ref, yf_ref),
            (True, rb_ref, kdb_ref, vb_ref, kapb_ref, bb_ref, lwb_ref, sb_ref, yb_ref))
    units = []
    dests = []
    for rev, r_ref, k_ref, v_ref, kap_ref, b_ref, lw_ref, s_ref, y_ref in dirs:
        for p in range(RWKV_WIDTH // LANES):
            sl = slice(p * LANES, (p + 1) * LANES)
            units.append((r_ref[0, :, sl], k_ref[0, :, sl], v_ref[0, :, sl], kap_ref[0, :, sl],
                          b_ref[0, :, sl], lw_ref[0, :, sl], s_ref[0, p], rev))
            dests.append((y_ref, s_ref, p, sl))
    ys, s_new = _rwkv_chunks(units, cst)
    for (y_ref, s_ref, p, sl), y, s in zip(dests, ys, s_new):
        y_ref[0, :, sl] = y
        s_ref[0, p] = s


def _rwkv_scan(r, v, kap, kd_f, kd_b, b_f, b_b, lw_f, lw_b, s0_f, s0_b):
    B, L, R = r.shape
    C = RWKV_CHUNK
    n = L // C
    fwd = pl.BlockSpec((1, C, R), lambda b, i: (b, i, 0))
    bwd = pl.BlockSpec((1, C, R), lambda b, i: (b, n - 1 - i, 0))
    st = pl.BlockSpec((1, R // LANES, LANES, LANES), lambda b, i: (b, 0, 0, 0))
    return pl.pallas_call(
        _rwkv_kernel,
        grid=(B, n),
        in_specs=[fwd] * 6 + [bwd] * 6 + [st, st],
        out_specs=[fwd, bwd, st, st],
        out_shape=[jax.ShapeDtypeStruct((B, L, R), F32)] * 2
        + [jax.ShapeDtypeStruct((B, R // LANES, LANES, LANES), F32)] * 2,
        compiler_params=pltpu.CompilerParams(dimension_semantics=("parallel", "arbitrary"),
                                             vmem_limit_bytes=VMEM_LIMIT),
        name="rwkv_scan",
    )(r, v, kap, kd_f, b_f, lw_f, r, v, kap, kd_b, b_b, lw_b, s0_f, s0_b)


def _gla_intra(items, cst):
    G = GLA_CHUNK
    ind_v, sel, blk = cst[2]
    srow = lax.broadcasted_iota(jnp.int32, (G, 1), 0)
    cums = [_cumsum_mm(cst[rev], la) for (_, _, _, la, rev) in items]
    pmats = []
    for (q, k, v, la, rev), cum in zip(items, cums):
        ps = []
        for t in range(G):
            msk = (srow >= t) if rev else (srow <= t)
            e = jnp.exp(jnp.where(msk, cum[t:t + 1] - cum, 0.0))
            ps.append(jnp.where(msk, e * (q[t:t + 1] * k), 0.0))
        pmats.append(jnp.concatenate(ps, axis=0))
    atts = [_mm(pm, ind_v, _NN, P_GATT) for pm in pmats]
    o_intra = [_mm(sel, att * jnp.concatenate([it[2]] * G, axis=0), _NN, P_GATT)
               for att, it in zip(atts, items)]
    out = []
    for (q, k, v, la, rev), cum, oi in zip(items, cums, o_intra):
        tot = cum[0:1] if rev else cum[G - 1:G]
        upd = _mm(v, k * jnp.exp(tot - cum), _TN, P_GSTATE)
        out.append((oi, q * jnp.exp(cum), jnp.where(blk, upd, 0.0), jnp.exp(tot)))
    return out


def _gla_consts():
    G = GLA_CHUNK
    row = lax.broadcasted_iota(jnp.int32, (G, G), 0)
    col = lax.broadcasted_iota(jnp.int32, (G, G), 1)
    kc = lax.broadcasted_iota(jnp.int32, (LANES, 2 * LANES), 0)
    vc = lax.broadcasted_iota(jnp.int32, (LANES, 2 * LANES), 1)
    ind_v = ((kc // GLA_KEY_DIM) == (vc // GLA_VAL_DIM)).astype(F32)
    st = lax.broadcasted_iota(jnp.int32, (G, G * G), 0)
    sj = lax.broadcasted_iota(jnp.int32, (G, G * G), 1)
    sel = ((sj // G) == st).astype(F32)
    br = lax.broadcasted_iota(jnp.int32, (2 * LANES, LANES), 0)
    bc = lax.broadcasted_iota(jnp.int32, (2 * LANES, LANES), 1)
    blk = (br // GLA_VAL_DIM) == (bc // GLA_KEY_DIM)
    return (col <= row).astype(BF16), (col >= row).astype(BF16), (ind_v, sel, blk)


def _gla_kernel(qf_ref, kf_ref, vf_ref, laf_ref, qb_ref, kb_ref, vb_ref, lab_ref, s0f_ref, s0b_ref,
                of_ref, ob_ref, sf_ref, sb_ref):
    i = pl.program_id(1)

    @pl.when(i == 0)
    def _():
        sf_ref[...] = s0f_ref[...]
        sb_ref[...] = s0b_ref[...]

    cst = _gla_consts()
    G = GLA_CHUNK
    nsub = SCAN_BLOCK // G
    dirs = ((False, qf_ref, kf_ref, vf_ref, laf_ref, sf_ref, of_ref),
            (True, qb_ref, kb_ref, vb_ref, lab_ref, sb_ref, ob_ref))
    items = []
    dests = []
    for rev, q_ref, k_ref, v_ref, la_ref, s_ref, o_ref in dirs:
        for p in range(GLA_QK_WIDTH // LANES):
            ks = slice(p * LANES, (p + 1) * LANES)
            vs = slice(p * 2 * LANES, (p + 1) * 2 * LANES)
            for j in range(nsub):
                c = nsub - 1 - j if rev else j
                rows = slice(c * G, (c + 1) * G)
                items.append((q_ref[0, rows, ks], k_ref[0, rows, ks], v_ref[0, rows, vs],
                              la_ref[0, rows, ks], rev))
                dests.append((o_ref, s_ref, p, rows, vs, j))
    parts = _gla_intra(items, cst)
    st = None
    for (o_ref, s_ref, p, rows, vs, j), (o_intra, q_in, upd, e_tot) in zip(dests, parts):
        if j == 0:
            st = s_ref[0, p]
        o_ref[0, rows, vs] = o_intra + _mm(q_in, st, _NT, P_GSTATE)
        st = st * e_tot + upd
        if j == nsub - 1:
            s_ref[0, p] = st


def _gla_scan(q, k, v, la_f, la_b, s0_f, s0_b):
    B, L, Q = q.shape
    V = v.shape[-1]
    T = SCAN_BLOCK
    n = L // T
    npair = Q // LANES

    def fwd(w):
        return pl.BlockSpec((1, T, w), lambda b, i: (b, i, 0))

    def bwd(w):
        return pl.BlockSpec((1, T, w), lambda b, i: (b, n - 1 - i, 0))

    st = pl.BlockSpec((1, npair, 2 * LANES, LANES), lambda b, i: (b, 0, 0, 0))
    return pl.pallas_call(
        _gla_kernel,
        grid=(B, n),
        in_specs=[fwd(Q), fwd(Q), fwd(V), fwd(Q), bwd(Q), bwd(Q), bwd(V), bwd(Q), st, st],
        out_specs=[fwd(V), bwd(V), st, st],
        out_shape=[jax.ShapeDtypeStruct((B, L, V), F32)] * 2
        + [jax.ShapeDtypeStruct((B, npair, 2 * LANES, LANES), F32)] * 2,
        compiler_params=pltpu.CompilerParams(dimension_semantics=("parallel", "arbitrary"),
                                             vmem_limit_bytes=VMEM_LIMIT),
        name="gla_scan",
    )(q, k, v, la_f, q, k, v, la_b, s0_f, s0_b)


def _post_kernel(x_ref, mod_ref, yf_ref, yb_ref, of_ref, ob_ref, bonus_ref, gate_ref, gsil_ref,
                 lng_ref, lnb_ref, gng_ref, n2g_ref, fng_ref, ind_ref, wout_ref, w1_ref, w2_ref, o_ref):
    D = D_MODEL
    m = mod_ref[0]
    gt1 = m[:, 2 * D:3 * D]
    sh2 = m[:, 3 * D:4 * D]
    sc2 = m[:, 4 * D:5 * D]
    gt2 = m[:, 5 * D:6 * D]
    inv_n = 1.0 / RWKV_HEAD_DIM
    y = yf_ref[0] + yb_ref[0]
    mu = _seg_sum(y, ind_ref) * inv_n
    yc = y - mu
    var = _seg_sum(yc * yc, ind_ref) * inv_n
    yn = yc * lax.rsqrt(var + LNX_EPS) * lng_ref[...] + lnb_ref[...]
    rw = (yn + bonus_ref[0]) * gate_ref[0]
    o = of_ref[0] + ob_ref[0]
    gsil = gsil_ref[0]
    gng = gng_ref[...]
    parts = [rw]
    for hh in range(GLA_HEADS):
        sl = slice(hh * GLA_VAL_DIM, (hh + 1) * GLA_VAL_DIM)
        oh = o[:, sl]
        oh = oh * lax.rsqrt(jnp.mean(oh * oh, axis=-1, keepdims=True) + GLA_NORM_EPS)
        parts.append(oh * gng * gsil[:, sl])
    mix = jnp.concatenate(parts, axis=-1)
    x1 = x_ref[0] + gt1 * _bdot(mix, wout_ref)
    h2 = _rms(x1) * n2g_ref[...] * (1.0 + sc2) + sh2
    f = jnp.maximum(_bdot(h2, w1_ref), 0.0)
    x2 = x1 + gt2 * _bdot(f * f, w2_ref)
    o_ref[0] = _rms(x2) * fng_ref[...]


def _post(x, mod3, mod_row, y_f, y_b, o_f, o_b, bonus, gate, gsil, vecs, mats):
    B, L, D = x.shape
    tb = min(POST_BLOCK, L)
    nblk = L // tb

    def tok(w):
        return pl.BlockSpec((1, tb, w), lambda b, i: (b, i, 0))

    in_specs = [tok(D), pl.BlockSpec((1, 1, N_MOD * D), lambda b, i: (mod_row(b), 0, 0))]
    in_specs += [tok(512)] * 7
    in_specs += [_const_spec(w.shape) for w in vecs]
    in_specs += [pl.BlockSpec(w.shape, lambda b, i: (0, 0), pipeline_mode=pl.Buffered(1)) for w in mats]
    return pl.pallas_call(
        _post_kernel,
        grid=(B, nblk),
        in_specs=in_specs,
        out_specs=tok(D),
        out_shape=jax.ShapeDtypeStruct((B, L, D), F32),
        compiler_params=pltpu.CompilerParams(dimension_semantics=("parallel", "arbitrary"),
                                             vmem_limit_bytes=VMEM_LIMIT),
        name="post",
    )(x, mod3, y_f, y_b, o_f, o_b, bonus, gate, gsil, *vecs, *mats)


def _rwkv_state_to_pairs(s):
    B = s.shape[0]
    s = s.reshape(B, 4, 2, 64, 64)
    z = jnp.zeros_like(s[:, :, 0])
    top = jnp.concatenate([s[:, :, 0], z], axis=-1)
    bot = jnp.concatenate([z, s[:, :, 1]], axis=-1)
    return jnp.concatenate([top, bot], axis=-2)


def _rwkv_pairs_to_state(sp):
    B = sp.shape[0]
    a = sp[:, :, 0:64, 0:64]
    b = sp[:, :, 64:128, 64:128]
    return jnp.stack([a, b], axis=2).reshape(B, 8, 64, 64)


def _gla_state_to_pairs(s):
    B = s.shape[0]
    st = jnp.swapaxes(s, -1, -2).reshape(B, 2, 2, 128, 64)
    z = jnp.zeros_like(st[:, :, 0])
    top = jnp.concatenate([st[:, :, 0], z], axis=-1)
    bot = jnp.concatenate([z, st[:, :, 1]], axis=-1)
    return jnp.concatenate([top, bot], axis=-2)


def _gla_pairs_to_state(sp):
    B = sp.shape[0]
    a = sp[:, :, 0:128, 0:64]
    b = sp[:, :, 128:256, 64:128]
    st = jnp.stack([a, b], axis=2).reshape(B, 4, 128, 64)
    return jnp.swapaxes(st, -1, -2)


def _blockdiag2(a, b):
    za = jnp.zeros((a.shape[0], b.shape[1]), a.dtype)
    zb = jnp.zeros((b.shape[0], a.shape[1]), a.dtype)
    return jnp.concatenate([jnp.concatenate([a, za], axis=1), jnp.concatenate([zb, b], axis=1)], axis=0)


def kernel(x_prompt, x_sample, c, state_rwkv_fwd, state_rwkv_bwd, state_gla_fwd, state_gla_bwd, c_ctx, ada_w, ada_b, norm1_g, norm2_g, w_in, rwkv_mu_rkv, rwkv_mu_wag, rwkv_w0, rwkv_w1, rwkv_w2, rwkv_a0, rwkv_a1, rwkv_a2, rwkv_g1, rwkv_g2, rwkv_k_k, rwkv_k_a, rwkv_r_k, rwkv_lnx_g, rwkv_lnx_b, gla_gk1, gla_gk2, gla_gk_b, gla_norm_g, w_out, mlp_w1, mlp_w2, final_norm_g):
    D = D_MODEL
    R = RWKV_WIDTH
    nb = x_prompt.shape[0]
    nd = x_sample.shape[0]
    assert ada_w.shape[0] == 1, "single-layer step"
    layer = 0

    craw = jnp.concatenate([c_ctx[None, :], c, jnp.zeros((8 - 1 - nd, D), F32)], axis=0)
    mod = _modulation(craw, ada_w[layer], ada_b[layer][None, :])
    mod3 = mod.reshape(8, 1, N_MOD * D)

    bf = lambda t: t.astype(BF16)
    row = lambda t: t.reshape(1, -1).astype(F32)
    w_in_l = w_in[layer]
    ind = (jnp.arange(R)[:, None] // RWKV_HEAD_DIM == jnp.arange(R)[None, :] // RWKV_HEAD_DIM).astype(BF16)
    pre_w = [
        row(norm1_g[layer]),
        bf(w_in_l[:, :3 * R]),
        bf(w_in_l[:, 3 * R:]),
        bf(jnp.concatenate([gla_gk1[layer, 0], gla_gk1[layer, 1]], axis=1)),
        bf(jnp.concatenate([rwkv_w1[layer, 0], rwkv_w1[layer, 1]], axis=1)),
        bf(jnp.concatenate([rwkv_a1[layer, 0], rwkv_a1[layer, 1]], axis=1)),
        bf(rwkv_g1[layer]),
        bf(_blockdiag2(rwkv_w2[layer, 0], rwkv_w2[layer, 1])),
        bf(_blockdiag2(rwkv_a2[layer, 0], rwkv_a2[layer, 1])),
        bf(rwkv_g2[layer]),
        bf(_blockdiag2(gla_gk2[layer, 0], gla_gk2[layer, 1])),
        row(rwkv_mu_rkv[layer]),
        rwkv_mu_wag[layer],
        row(rwkv_w0[layer]),
        row(rwkv_a0[layer]),
        row(gla_gk_b[layer]),
        row(rwkv_k_k[layer]),
        row(rwkv_k_a[layer]),
        row(rwkv_r_k[layer]),
        ind,
    ]
    post_vecs = [row(rwkv_lnx_g[layer]), row(rwkv_lnx_b[layer]), row(gla_norm_g[layer]),
                 row(norm2_g[layer]), row(final_norm_g)]
    post_mats = [ind, bf(w_out[layer]), bf(mlp_w1[layer]), bf(mlp_w2[layer])]

    def run_group(x, mod_row, grid_shift, s_rf, s_rb, s_gf, s_gb):
        (r, v, kap, kd_f, kd_b, b_f, b_b, lw_f, lw_b, gate, bonus,
         q, kg, la_f, la_b, vg, gsil) = _pre(x, mod3, mod_row, pre_w, grid_shift)
        y_f, y_b, n_rf, n_rb = _rwkv_scan(r, v, kap, kd_f, kd_b, b_f, b_b, lw_f, lw_b,
                                          _rwkv_state_to_pairs(s_rf), _rwkv_state_to_pairs(s_rb))
        o_f, o_b, n_gf, n_gb = _gla_scan(q, kg, vg, la_f, la_b,
                                         _gla_state_to_pairs(s_gf), _gla_state_to_pairs(s_gb))
        y = _post(x, mod3, mod_row, y_f, y_b, o_f, o_b, bonus, gate, gsil, post_vecs, post_mats)
        return y, (n_rf, n_rb, n_gf, n_gb)

    zr = jnp.zeros((nb, RWKV_HEADS, RWKV_HEAD_DIM, RWKV_HEAD_DIM), F32)
    zg = jnp.zeros((nb, GLA_HEADS, GLA_KEY_DIM, GLA_VAL_DIM), F32)
    y_prompt, (n_rf, n_rb, n_gf, n_gb) = run_group(x_prompt, lambda b: 0, False, zr, zr, zg, zg)
    y_sample, _ = run_group(x_sample, lambda b: b + 1, True,
                            state_rwkv_fwd[:, layer], state_rwkv_bwd[:, layer],
                            state_gla_fwd[:, layer], state_gla_bwd[:, layer])
    return (y_prompt, y_sample,
            _rwkv_pairs_to_state(n_rf)[:, None], _rwkv_pairs_to_state(n_rb)[:, None],
            _gla_pairs_to_state(n_gf)[:, None], _gla_pairs_to_state(n_gb)[:, None])
```

```python
import functools

import jax
import jax.numpy as jnp
from jax import lax
from jax.experimental import pallas as pl
from jax.experimental.pallas import tpu as pltpu

F32 = jnp.float32
BF16 = jnp.bfloat16
HI = lax.Precision.HIGHEST

D_MODEL = 1024
GRID_W = 64
RWKV_WIDTH = 512
RWKV_HEAD_DIM = 64
RWKV_HEADS = 8
GLA_HEADS = 4
GLA_KEY_DIM = 64
GLA_VAL_DIM = 128
GLA_QK_WIDTH = 256
GLA_V_WIDTH = 512
GLA_GATE_NORMALIZER = 16.0
N_MOD = 6
RMS_EPS = 1e-6
LNX_EPS = 64e-5
GLA_NORM_EPS = 1e-5

LANES = 128
RWKV_CHUNK = 64
GLA_CHUNK = 16
SCAN_BLOCK = 128
PRE_BLOCK = 256
POST_BLOCK = 512
VMEM_LIMIT = 56 * 1024 * 1024

P_SCORE = "b1"
P_INV = "b1"
P_SOLVE = "b1"
P_STATE = "b1"
P_OUT = "b1"
P_GSTATE = "b1"
P_GATT = "b1"


def _dot(a, b, prec=None):
    return lax.dot_general(a, b, (((1,), (0,)), ((), ())), precision=prec, preferred_element_type=F32)


def _dot_nt(a, b, prec=None):
    return lax.dot_general(a, b, (((1,), (1,)), ((), ())), precision=prec, preferred_element_type=F32)


def _dot_tn(a, b, prec=None):
    return lax.dot_general(a, b, (((0,), (0,)), ((), ())), precision=prec, preferred_element_type=F32)


_NN = (((1,), (0,)), ((), ()))
_NT = (((1,), (1,)), ((), ()))
_TN = (((0,), (0,)), ((), ()))


def _split2(x):
    hi = x.astype(BF16)
    return hi, (x - hi.astype(F32)).astype(BF16)


def _mm(a, b, dims, mode):
    dg = functools.partial(lax.dot_general, dimension_numbers=dims, preferred_element_type=F32)
    if mode == "hi":
        return dg(a, b, precision=HI)
    if mode == "b1":
        return dg(a.astype(BF16), b.astype(BF16))
    a1, a2 = _split2(a)
    b1, b2 = _split2(b)
    return dg(a1, b1) + dg(a1, b2) + dg(a2, b1)


def _cumsum_mm(tri, x):
    hi = x.astype(BF16)
    r1 = x - hi.astype(F32)
    mid = r1.astype(BF16)
    lo = (r1 - mid.astype(F32)).astype(BF16)
    return _dot(tri, hi) + _dot(tri, mid) + _dot(tri, lo)


def _bdot(a, w_ref):
    return _dot(a.astype(BF16), w_ref[...])


def _sigmoid(x):
    return 1.0 / (1.0 + jnp.exp(-x))


def _softplus(x):
    return jnp.maximum(x, 0.0) + jnp.log(1.0 + jnp.exp(-jnp.abs(x)))


def _seg_sum(x, ind_ref):
    ind = ind_ref[...]
    hi = x.astype(BF16)
    r1 = x - hi.astype(F32)
    mid = r1.astype(BF16)
    lo = (r1 - mid.astype(F32)).astype(BF16)
    return _dot(hi, ind) + _dot(mid, ind) + _dot(lo, ind)


def _rms(x):
    return x * lax.rsqrt(jnp.mean(x * x, axis=-1, keepdims=True) + RMS_EPS)


def _mod_kernel(c_ref, w_ref, b_ref, o_ref):
    c = c_ref[...]
    cond = c * _sigmoid(c)
    o_ref[...] = _dot(cond.astype(BF16), w_ref[...].astype(BF16)) + b_ref[...]


def _modulation(craw, ada_w, ada_b):
    n = ada_w.shape[1]
    bn = 1536
    return pl.pallas_call(
        _mod_kernel,
        grid=(n // bn,),
        in_specs=[
            pl.BlockSpec((8, D_MODEL), lambda j: (0, 0)),
            pl.BlockSpec((D_MODEL, bn), lambda j: (0, j)),
            pl.BlockSpec((1, bn), lambda j: (0, j)),
        ],
        out_specs=pl.BlockSpec((8, bn), lambda j: (0, j)),
        out_shape=jax.ShapeDtypeStruct((8, n), F32),
        compiler_params=pltpu.CompilerParams(dimension_semantics=("arbitrary",), vmem_limit_bytes=VMEM_LIMIT),
        name="modulation",
    )(craw, ada_w, ada_b)


def _pre_kernel(*refs, grid_shift, tb):
    if grid_shift:
        x_ref, xp_ref, xn_ref = refs[:3]
        refs = refs[3:]
    else:
        x_ref = refs[0]
        refs = refs[1:]
    (mod_ref, n1g_ref, wrkv_ref, wrest_ref, gk1_ref, w1_ref, a1_ref, g1_ref, w2_ref, a2_ref, g2_ref,
     gk2_ref, murkv_ref, muwag_ref, w0_ref, a0_ref, gkb_ref, kk_ref, ka_ref, rk_ref, ind_ref) = refs[:21]
    (r_o, v_o, kap_o, kdf_o, kdb_o, bf_o, bb_o, lwf_o, lwb_o, gate_o, bonus_o,
     q_o, kg_o, laf_o, lab_o, vg_o, gsil_o) = refs[21:]
    D = D_MODEL
    R = RWKV_WIDTH
    m = mod_ref[0]
    sh1 = m[:, 0:D]
    sc1 = m[:, D:2 * D]
    n1g = n1g_ref[...]

    def normmod(xx):
        return _rms(xx) * n1g * (1.0 + sc1) + sh1

    h = normmod(x_ref[0])
    row = lax.broadcasted_iota(jnp.int32, (tb, 1), 0)
    if grid_shift:
        i = pl.program_id(1)
        n = pl.num_programs(1)
        hp = normmod(xp_ref[0]) * (i > 0).astype(F32)
        hn = normmod(xn_ref[0]) * (i < n - 1).astype(F32)
        hext = jnp.concatenate([hp, h, hn], axis=0)
        col = row % GRID_W
        m_l = (col != 0).astype(F32)
        m_r = (col != GRID_W - 1).astype(F32)
        ne = tb + 2 * GRID_W

        def shift(ext):
            up = ext[0:tb]
            down = ext[2 * GRID_W:2 * GRID_W + tb]
            left = pltpu.roll(ext, 1, 0)[GRID_W:GRID_W + tb]
            right = pltpu.roll(ext, ne - 1, 0)[GRID_W:GRID_W + tb]
            return 0.25 * (up + down + m_l * left + m_r * right)

        halo = GRID_W
    else:
        hext = h
        m_l = (row != 0).astype(F32)
        m_r = (row != tb - 1).astype(F32)

        def shift(ext):
            return 0.5 * (m_l * pltpu.roll(ext, 1, 0) + m_r * pltpu.roll(ext, tb - 1, 0))

        halo = 0

    rkv_ext = _bdot(hext, wrkv_ref)
    rkv = rkv_ext[halo:halo + tb]
    rkv = rkv + murkv_ref[...] * (shift(rkv_ext) - rkv)
    r = rkv[:, 0:R]
    k = rkv[:, R:2 * R]
    v = rkv[:, 2 * R:3 * R]

    dh = shift(hext) - h
    mu = muwag_ref[...]
    xw = h + mu[0:1] * dh
    xa = h + mu[1:2] * dh
    xg = h + mu[2:3] * dh

    z = w0_ref[...] + _bdot(jnp.tanh(_bdot(xw, w1_ref)), w2_ref)
    lw = -jnp.exp(-_softplus(-z) - 0.5)
    a = _sigmoid(a0_ref[...] + _bdot(_bdot(xa, a1_ref), a2_ref))
    gate = _bdot(_sigmoid(_bdot(xg, g1_ref)), g2_ref)

    kap = k * kk_ref[...]
    kap = kap * lax.rsqrt(jnp.maximum(_seg_sum(kap * kap, ind_ref), 1e-12))
    ka = ka_ref[...]
    a_f = a[:, 0:R]
    a_b = a[:, R:2 * R]
    kd_f = k * (1.0 + (a_f - 1.0) * ka)
    kd_b = k * (1.0 + (a_b - 1.0) * ka)
    bonus = _seg_sum(r * (kd_f + kd_b) * rk_ref[...], ind_ref) * v

    r_o[0] = r
    v_o[0] = v
    kap_o[0] = kap
    kdf_o[0] = kd_f
    kdb_o[0] = kd_b
    bf_o[0] = a_f * kap
    bb_o[0] = a_b * kap
    lwf_o[0] = lw[:, 0:R]
    lwb_o[0] = lw[:, R:2 * R]
    gate_o[0] = gate
    bonus_o[0] = bonus

    rest = _bdot(h, wrest_ref)
    Q = GLA_QK_WIDTH
    q_o[0] = rest[:, 0:Q] * (GLA_KEY_DIM ** -0.5)
    kg_o[0] = rest[:, Q:2 * Q]
    vg_o[0] = rest[:, 2 * Q:2 * Q + GLA_V_WIDTH]
    gg = rest[:, 2 * Q + GLA_V_WIDTH:]
    gsil_o[0] = gg * _sigmoid(gg)
    logits = _bdot(_bdot(h, gk1_ref), gk2_ref) + gkb_ref[...]
    la = -_softplus(-logits) * (1.0 / GLA_GATE_NORMALIZER)
    laf_o[0] = la[:, 0:Q]
    lab_o[0] = la[:, Q:2 * Q]


def _const_spec(shape):
    nd = len(shape)
    return pl.BlockSpec(shape, lambda b, i: (0,) * nd)


def _pre(x, mod3, mod_row, weights, grid_shift):
    B, L, D = x.shape
    tb = PRE_BLOCK
    nblk = L // tb
    if not grid_shift:
        assert nblk == 1
    x_spec = pl.BlockSpec((1, tb, D), lambda b, i: (b, i, 0))
    in_specs = [x_spec]
    args = [x]
    if grid_shift:
        per = tb // GRID_W
        nrow = L // GRID_W
        in_specs += [
            pl.BlockSpec((1, GRID_W, D), lambda b, i: (b, jnp.maximum(i * per - 1, 0), 0)),
            pl.BlockSpec((1, GRID_W, D), lambda b, i: (b, jnp.minimum((i + 1) * per, nrow - 1), 0)),
        ]
        args += [x, x]
    in_specs.append(pl.BlockSpec((1, 1, N_MOD * D), lambda b, i: (mod_row(b), 0, 0)))
    args.append(mod3)
    for w in weights:
        in_specs.append(_const_spec(w.shape))
        args.append(w)
    widths = [512] * 11 + [256, 256, 256, 256, 512, 512]
    out_specs = [pl.BlockSpec((1, tb, w), lambda b, i: (b, i, 0)) for w in widths]
    out_shape = [jax.ShapeDtypeStruct((B, L, w), F32) for w in widths]
    return pl.pallas_call(
        functools.partial(_pre_kernel, grid_shift=grid_shift, tb=tb),
        grid=(B, nblk),
        in_specs=in_specs,
        out_specs=out_specs,
        out_shape=out_shape,
        compiler_params=pltpu.CompilerParams(dimension_semantics=("parallel", "arbitrary"),
                                             vmem_limit_bytes=VMEM_LIMIT),
        name="pre_grid" if grid_shift else "pre_seq",
    )(*args)


def _inv_unit_triangular(lms, eye, blk16, blk32):
    mm = functools.partial(_mm, dims=_NN, mode=P_INV)
    l0 = [jnp.where(blk16, lm, 0.0) for lm in lms]
    l2 = [mm(a, a) for a in l0]
    l4 = [mm(a, a) for a in l2]
    l8 = [mm(a, a) for a in l4]
    t = [mm(eye - a, eye + b) for a, b in zip(l0, l2)]
    t = [mm(a, eye + b) for a, b in zip(t, l4)]
    t = [mm(a, eye + b) for a, b in zip(t, l8)]
    off1_mask = jnp.logical_and(blk32, jnp.logical_not(blk16))
    x = [mm(a, jnp.where(off1_mask, lm, 0.0)) for a, lm in zip(t, lms)]
    t = [a - mm(b, a) for a, b in zip(t, x)]
    x = [mm(a, jnp.where(blk32, 0.0, lm)) for a, lm in zip(t, lms)]
    t = [a - mm(b, a) for a, b in zip(t, x)]
    return t


def _rwkv_prepare(units, cst):
    C = RWKV_CHUNK
    m0, cm0, eye, blk16, blk32, _ = cst[2]

    def stack(x):
        return jnp.concatenate([jnp.where(m0, x, 0.0), jnp.where(m0, 0.0, x)], axis=0)

    def blockdiag(side):
        return jnp.concatenate([jnp.where(cm0, side, 0.0), jnp.where(cm0, 0.0, side)], axis=0)

    cums = [_cumsum_mm(cst[rev][0], lw) for (_, _, _, _, _, lw, rev) in units]
    prep = []
    for (r, k, v, kap, b, lw, rev), cum in zip(units, cums):
        cumx = cum - lw
        if rev:
            mid = cum[C // 2:C // 2 + 1]
            tot = cum[0:1]
        else:
            mid = cum[C // 2 - 1:C // 2]
            tot = cum[C - 1:C]
        e_mid = jnp.exp(mid)
        rt = r * jnp.exp(cum - mid)
        kt = kap * jnp.exp(cumx - mid)
        es = jnp.exp(mid - cum)
        e_end = jnp.exp(tot - mid)
        prep.append(dict(rt=rt, kt=kt, bh=b * es, kh=k * es, r0=rt * e_mid, k0=stack(kt * e_mid),
                         be=b * es * e_end, ke=k * es * e_end, vs=stack(v), v=v,
                         e_tot=jnp.exp(tot), strict2=cst[rev][1], incl2=cst[rev][2]))
    gs = [_mm(jnp.concatenate([p["kt"], p["rt"]], axis=0),
              jnp.concatenate([stack(p["bh"]), stack(p["kh"])], axis=0), _NT, P_SCORE) for p in prep]
    for p, g in zip(prep, gs):
        p["ab"] = blockdiag(jnp.where(p["strict2"], g[0:C, 0:2 * C], 0.0))
        p["ak"] = blockdiag(jnp.where(p["strict2"], g[0:C, 2 * C:4 * C], 0.0))
        p["rb"] = jnp.where(p["incl2"], g[C:2 * C, 0:2 * C], 0.0)
        p["rk"] = jnp.where(p["incl2"], g[C:2 * C, 2 * C:4 * C], 0.0)
    akv = [_mm(p["ak"], p["vs"], _NN, P_SOLVE) for p in prep]
    ts = _inv_unit_triangular([p["ab"] for p in prep], eye, blk16, blk32)
    wu = [_mm(t, jnp.concatenate([p["k0"], a], axis=1), _NN, P_SOLVE) for t, p, a in zip(ts, prep, akv)]
    yv = [_mm(p["rk"], p["vs"], _NN, P_OUT) for p in prep]
    return [dict(w=x[:, :LANES], u0=x[:, LANES:], yv=y, rb=p["rb"], r0=p["r0"], v=p["v"],
                 bke=jnp.concatenate([p["be"], p["ke"]], axis=0), e_tot=p["e_tot"])
            for x, y, p in zip(wu, yv, prep)]


def _rwkv_apply(prep, states, cst):
    C = RWKV_CHUNK
    blk64 = cst[2][5]
    ws = [_mm(p["w"], s, _NT, P_STATE) for p, s in zip(prep, states)]
    us = [-p["u0"] - w for p, w in zip(prep, ws)]
    upd = [_mm(jnp.concatenate([u[0:C] + u[C:2 * C], p["v"]], axis=0), p["bke"], _TN, P_STATE)
           for u, p in zip(us, prep)]
    s_new = [s * p["e_tot"] + jnp.where(blk64, d, 0.0) for p, s, d in zip(prep, states, upd)]
    ys = [_mm(p["r0"], s, _NT, P_STATE) + _mm(p["rb"], u, _NN, P_OUT) + p["yv"]
          for p, s, u in zip(prep, states, us)]
    return ys, s_new


def _rwkv_consts():
    C = RWKV_CHUNK
    row = lax.broadcasted_iota(jnp.int32, (C, C), 0)
    col = lax.broadcasted_iota(jnp.int32, (C, C), 1)
    row2 = lax.broadcasted_iota(jnp.int32, (C, 2 * C), 0)
    col2 = lax.broadcasted_iota(jnp.int32, (C, 2 * C), 1)
    cs = col2 % C
    lane = lax.broadcasted_iota(jnp.int32, (1, LANES), 1)
    m0 = lane < RWKV_HEAD_DIM
    cm0 = lax.broadcasted_iota(jnp.int32, (1, 2 * C), 1) < C
    rr = lax.broadcasted_iota(jnp.int32, (2 * C, 2 * C), 0)
    cc = lax.broadcasted_iota(jnp.int32, (2 * C, 2 * C), 1)
    eye = (rr == cc).astype(F32)
    blk16 = (rr // 16) == (cc // 16)
    blk32 = (rr // 32) == (cc // 32)
    blk64 = (rr // 64) == (cc // 64)
    fwd = ((col <= row).astype(BF16), cs < row2, cs <= row2)
    bwd = ((col >= row).astype(BF16), cs > row2, cs >= row2)
    return fwd, bwd, (m0, cm0, eye, blk16, blk32, blk64)


def _gla_intra(items, cst):
    G = GLA_CHUNK
    ind_v, sel, blk = cst[2]
    srow = lax.broadcasted_iota(jnp.int32, (G, 1), 0)
    cums = [_cumsum_mm(cst[rev], la) for (_, _, _, la, rev) in items]
    pmats = []
    for (q, k, v, la, rev), cum in zip(items, cums):
        ps = []
        for t in range(G):
            msk = (srow >= t) if rev else (srow <= t)
            e = jnp.exp(jnp.where(msk, cum[t:t + 1] - cum, 0.0))
            ps.append(jnp.where(msk, e * (q[t:t + 1] * k), 0.0))
        pmats.append(jnp.concatenate(ps, axis=0))
    atts = [_mm(pm, ind_v, _NN, P_GATT) for pm in pmats]
    o_intra = [_mm(sel, att * jnp.concatenate([it[2]] * G, axis=0), _NN, P_GATT)
               for att, it in zip(atts, items)]
    out = []
    for (q, k, v, la, rev), cum, oi in zip(items, cums, o_intra):
        tot = cum[0:1] if rev else cum[G - 1:G]
        upd = _mm(v, k * jnp.exp(tot - cum), _TN, P_GSTATE)
        out.append((oi, q * jnp.exp(cum), jnp.where(blk, upd, 0.0), jnp.exp(tot)))
    return out


def _gla_consts():
    G = GLA_CHUNK
    row = lax.broadcasted_iota(jnp.int32, (G, G), 0)
    col = lax.broadcasted_iota(jnp.int32, (G, G), 1)
    kc = lax.broadcasted_iota(jnp.int32, (LANES, 2 * LANES), 0)
    vc = lax.broadcasted_iota(jnp.int32, (LANES, 2 * LANES), 1)
    ind_v = ((kc // GLA_KEY_DIM) == (vc // GLA_VAL_DIM)).astype(F32)
    st = lax.broadcasted_iota(jnp.int32, (G, G * G), 0)
    sj = lax.broadcasted_iota(jnp.int32, (G, G * G), 1)
    sel = ((sj // G) == st).astype(F32)
    br = lax.broadcasted_iota(jnp.int32, (2 * LANES, LANES), 0)
    bc = lax.broadcasted_iota(jnp.int32, (2 * LANES, LANES), 1)
    blk = (br // GLA_VAL_DIM) == (bc // GLA_KEY_DIM)
    return (col <= row).astype(BF16), (col >= row).astype(BF16), (ind_v, sel, blk)


def _scan_kernel(rf_ref, rvf_ref, kapf_ref, kdf_ref, bf_ref, lwf_ref,
                 rb_ref, rvb_ref, kapb_ref, kdb_ref, bb_ref, lwb_ref,
                 qf_ref, kf_ref, gvf_ref, laf_ref, qb_ref, kb_ref, gvb_ref, lab_ref,
                 srf0_ref, srb0_ref, sgf0_ref, sgb0_ref,
                 yf_ref, yb_ref, of_ref, ob_ref, srf_ref, srb_ref, sgf_ref, sgb_ref):
    i = pl.program_id(1)

    @pl.when(i == 0)
    def _():
        srf_ref[...] = srf0_ref[...]
        srb_ref[...] = srb0_ref[...]
        sgf_ref[...] = sgf0_ref[...]
        sgb_ref[...] = sgb0_ref[...]

    T = SCAN_BLOCK
    C = RWKV_CHUNK
    G = GLA_CHUNK
    nr = T // C
    ng = T // G
    rcst = _rwkv_consts()
    gcst = _gla_consts()

    r_dirs = ((False, rf_ref, kdf_ref, rvf_ref, kapf_ref, bf_ref, lwf_ref, srf_ref, yf_ref),
              (True, rb_ref, kdb_ref, rvb_ref, kapb_ref, bb_ref, lwb_ref, srb_ref, yb_ref))
    r_units = [[] for _ in range(nr)]
    r_dests = [[] for _ in range(nr)]
    for rev, r_ref, k_ref, v_ref, kap_ref, b_ref, lw_ref, s_ref, y_ref in r_dirs:
        for p in range(RWKV_WIDTH // LANES):
            sl = slice(p * LANES, (p + 1) * LANES)
            for j in range(nr):
                c = nr - 1 - j if rev else j
                rows = slice(c * C, (c + 1) * C)
                r_units[j].append((r_ref[0, rows, sl], k_ref[0, rows, sl], v_ref[0, rows, sl],
                                   kap_ref[0, rows, sl], b_ref[0, rows, sl], lw_ref[0, rows, sl], rev))
                r_dests[j].append((y_ref, s_ref, p, rows, sl))
    r_prep = _rwkv_prepare([u for us in r_units for u in us], rcst)
    nu = len(r_units[0])
    r_prep = [r_prep[j * nu:(j + 1) * nu] for j in range(nr)]

    g_dirs = ((False, qf_ref, kf_ref, gvf_ref, laf_ref, sgf_ref, of_ref),
              (True, qb_ref, kb_ref, gvb_ref, lab_ref, sgb_ref, ob_ref))
    g_items = []
    g_dests = []
    for rev, q_ref, k_ref, v_ref, la_ref, s_ref, o_ref in g_dirs:
        for p in range(GLA_QK_WIDTH // LANES):
            ks = slice(p * LANES, (p + 1) * LANES)
            vs = slice(p * 2 * LANES, (p + 1) * 2 * LANES)
            for j in range(ng):
                c = ng - 1 - j if rev else j
                rows = slice(c * G, (c + 1) * G)
                g_items.append((q_ref[0, rows, ks], k_ref[0, rows, ks], v_ref[0, rows, vs],
                                la_ref[0, rows, ks], rev))
                g_dests.append((o_ref, s_ref, p, rows, vs, j))
    g_parts = _gla_intra(g_items, gcst)

    states = [s_ref[0, p] for (_, s_ref, p, _, _) in r_dests[0]]
    for j in range(nr):
        ys, states = _rwkv_apply(r_prep[j], states, rcst)
        for (y_ref, _, _, rows, sl), y in zip(r_dests[j], ys):
            y_ref[0, rows, sl] = y
    for (_, s_ref, p, _, _), s in zip(r_dests[0], states):
        s_ref[0, p] = s

    st = None
    for (o_ref, s_ref, p, rows, vs, j), (o_intra, q_in, upd, e_tot) in zip(g_dests, g_parts):
        if j == 0:
            st = s_ref[0, p]
        o_ref[0, rows, vs] = o_intra + _mm(q_in, st, _NT, P_GSTATE)
        st = st * e_tot + upd
        if j == ng - 1:
            s_ref[0, p] = st


def _scan(r, v, kap, kd_f, kd_b, b_f, b_b, lw_f, lw_b, q, kg, vg, la_f, la_b, s_rf, s_rb, s_gf, s_gb):
    B, L, R = r.shape
    Q = q.shape[-1]
    V = vg.shape[-1]
    T = SCAN_BLOCK
    n = L // T

    def fwd(w):
        return pl.BlockSpec((1, T, w), lambda b, i: (b, i, 0))

    def bwd(w):
        return pl.BlockSpec((1, T, w), lambda b, i: (b, n - 1 - i, 0))

    rst = pl.BlockSpec((1, R // LANES, LANES, LANES), lambda b, i: (b, 0, 0, 0))
    gst = pl.BlockSpec((1, Q // LANES, 2 * LANES, LANES), lambda b, i: (b, 0, 0, 0))
    return pl.pallas_call(
        _scan_kernel,
        grid=(B, n),
        in_specs=[fwd(R)] * 6 + [bwd(R)] * 6 + [fwd(Q), fwd(Q), fwd(V), fwd(Q), bwd(Q), bwd(Q), bwd(V), bwd(Q)]
        + [rst, rst, gst, gst],
        out_specs=[fwd(R), bwd(R), fwd(V), bwd(V), rst, rst, gst, gst],
        out_shape=[jax.ShapeDtypeStruct((B, L, R), F32)] * 2 + [jax.ShapeDtypeStruct((B, L, V), F32)] * 2
        + [jax.ShapeDtypeStruct((B, R // LANES, LANES, LANES), F32)] * 2
        + [jax.ShapeDtypeStruct((B, Q // LANES, 2 * LANES, LANES), F32)] * 2,
        compiler_params=pltpu.CompilerParams(dimension_semantics=("parallel", "arbitrary"),
                                             vmem_limit_bytes=VMEM_LIMIT),
        name="scan",
    )(r, v, kap, kd_f, b_f, lw_f, r, v, kap, kd_b, b_b, lw_b, q, kg, vg, la_f, q, kg, vg, la_b,
      s_rf, s_rb, s_gf, s_gb)


def _post_kernel(x_ref, mod_ref, yf_ref, yb_ref, of_ref, ob_ref, bonus_ref, gate_ref, gsil_ref,
                 lng_ref, lnb_ref, gng_ref, n2g_ref, fng_ref, ind_ref, wout_ref, w1_ref, w2_ref, o_ref):
    D = D_MODEL
    m = mod_ref[0]
    gt1 = m[:, 2 * D:3 * D]
    sh2 = m[:, 3 * D:4 * D]
    sc2 = m[:, 4 * D:5 * D]
    gt2 = m[:, 5 * D:6 * D]
    inv_n = 1.0 / RWKV_HEAD_DIM
    y = yf_ref[0] + yb_ref[0]
    mu = _seg_sum(y, ind_ref) * inv_n
    yc = y - mu
    var = _seg_sum(yc * yc, ind_ref) * inv_n
    yn = yc * lax.rsqrt(var + LNX_EPS) * lng_ref[...] + lnb_ref[...]
    rw = (yn + bonus_ref[0]) * gate_ref[0]
    o = of_ref[0] + ob_ref[0]
    gsil = gsil_ref[0]
    gng = gng_ref[...]
    parts = [rw]
    for hh in range(GLA_HEADS):
        sl = slice(hh * GLA_VAL_DIM, (hh + 1) * GLA_VAL_DIM)
        oh = o[:, sl]
        oh = oh * lax.rsqrt(jnp.mean(oh * oh, axis=-1, keepdims=True) + GLA_NORM_EPS)
        parts.append(oh * gng * gsil[:, sl])
    mix = jnp.concatenate(parts, axis=-1)
    x1 = x_ref[0] + gt1 * _bdot(mix, wout_ref)
    h2 = _rms(x1) * n2g_ref[...] * (1.0 + sc2) + sh2
    f = jnp.maximum(_bdot(h2, w1_ref), 0.0)
    x2 = x1 + gt2 * _bdot(f * f, w2_ref)
    o_ref[0] = _rms(x2) * fng_ref[...]


def _post(x, mod3, mod_row, y_f, y_b, o_f, o_b, bonus, gate, gsil, vecs, mats):
    B, L, D = x.shape
    tb = min(POST_BLOCK, L)
    nblk = L // tb

    def tok(w):
        return pl.BlockSpec((1, tb, w), lambda b, i: (b, i, 0))

    in_specs = [tok(D), pl.BlockSpec((1, 1, N_MOD * D), lambda b, i: (mod_row(b), 0, 0))]
    in_specs += [tok(512)] * 7
    in_specs += [_const_spec(w.shape) for w in vecs]
    in_specs += [pl.BlockSpec(w.shape, lambda b, i: (0, 0), pipeline_mode=pl.Buffered(1)) for w in mats]
    return pl.pallas_call(
        _post_kernel,
        grid=(B, nblk),
        in_specs=in_specs,
        out_specs=tok(D),
        out_shape=jax.ShapeDtypeStruct((B, L, D), F32),
        compiler_params=pltpu.CompilerParams(dimension_semantics=("parallel", "arbitrary"),
                                             vmem_limit_bytes=VMEM_LIMIT),
        name="post",
    )(x, mod3, y_f, y_b, o_f, o_b, bonus, gate, gsil, *vecs, *mats)


def _rwkv_state_to_pairs(s):
    B = s.shape[0]
    s = s.reshape(B, 4, 2, 64, 64)
    z = jnp.zeros_like(s[:, :, 0])
    top = jnp.concatenate([s[:, :, 0], z], axis=-1)
    bot = jnp.concatenate([z, s[:, :, 1]], axis=-1)
    return jnp.concatenate([top, bot], axis=-2)


def _rwkv_pairs_to_state(sp):
    B = sp.shape[0]
    a = sp[:, :, 0:64, 0:64]
    b = sp[:, :, 64:128, 64:128]
    return jnp.stack([a, b], axis=2).reshape(B, 8, 64, 64)


def _gla_state_to_pairs(s):
    B = s.shape[0]
    st = jnp.swapaxes(s, -1, -2).reshape(B, 2, 2, 128, 64)
    z = jnp.zeros_like(st[:, :, 0])
    top = jnp.concatenate([st[:, :, 0], z], axis=-1)
    bot = jnp.concatenate([z, st[:, :, 1]], axis=-1)
    return jnp.concatenate([top, bot], axis=-2)


def _gla_pairs_to_state(sp):
    B = sp.shape[0]
    a = sp[:, :, 0:128, 0:64]
    b = sp[:, :, 128:256, 64:128]
    st = jnp.stack([a, b], axis=2).reshape(B, 4, 128, 64)
    return jnp.swapaxes(st, -1, -2)


def _blockdiag2(a, b):
    za = jnp.zeros((a.shape[0], b.shape[1]), a.dtype)
    zb = jnp.zeros((b.shape[0], a.shape[1]), a.dtype)
    return jnp.concatenate([jnp.concatenate([a, za], axis=1), jnp.concatenate([zb, b], axis=1)], axis=0)


def kernel(x_prompt, x_sample, c, state_rwkv_fwd, state_rwkv_bwd, state_gla_fwd, state_gla_bwd, c_ctx, ada_w, ada_b, norm1_g, norm2_g, w_in, rwkv_mu_rkv, rwkv_mu_wag, rwkv_w0, rwkv_w1, rwkv_w2, rwkv_a0, rwkv_a1, rwkv_a2, rwkv_g1, rwkv_g2, rwkv_k_k, rwkv_k_a, rwkv_r_k, rwkv_lnx_g, rwkv_lnx_b, gla_gk1, gla_gk2, gla_gk_b, gla_norm_g, w_out, mlp_w1, mlp_w2, final_norm_g):
    D = D_MODEL
    R = RWKV_WIDTH
    nb = x_prompt.shape[0]
    nd = x_sample.shape[0]
    assert ada_w.shape[0] == 1, "single-layer step"
    layer = 0

    craw = jnp.concatenate([c_ctx[None, :], c, jnp.zeros((8 - 1 - nd, D), F32)], axis=0)
    mod = _modulation(craw, ada_w[layer], ada_b[layer][None, :])
    mod3 = mod.reshape(8, 1, N_MOD * D)

    bf = lambda t: t.astype(BF16)
    row = lambda t: t.reshape(1, -1).astype(F32)
    w_in_l = w_in[layer]
    ind = (jnp.arange(R)[:, None] // RWKV_HEAD_DIM == jnp.arange(R)[None, :] // RWKV_HEAD_DIM).astype(BF16)
    pre_w = [
        row(norm1_g[layer]),
        bf(w_in_l[:, :3 * R]),
        bf(w_in_l[:, 3 * R:]),
        bf(jnp.concatenate([gla_gk1[layer, 0], gla_gk1[layer, 1]], axis=1)),
        bf(jnp.concatenate([rwkv_w1[layer, 0], rwkv_w1[layer, 1]], axis=1)),
        bf(jnp.concatenate([rwkv_a1[layer, 0], rwkv_a1[layer, 1]], axis=1)),
        bf(rwkv_g1[layer]),
        bf(_blockdiag2(rwkv_w2[layer, 0], rwkv_w2[layer, 1])),
        bf(_blockdiag2(rwkv_a2[layer, 0], rwkv_a2[layer, 1])),
        bf(rwkv_g2[layer]),
        bf(_blockdiag2(gla_gk2[layer, 0], gla_gk2[layer, 1])),
        row(rwkv_mu_rkv[layer]),
        rwkv_mu_wag[layer],
        row(rwkv_w0[layer]),
        row(rwkv_a0[layer]),
        row(gla_gk_b[layer]),
        row(rwkv_k_k[layer]),
        row(rwkv_k_a[layer]),
        row(rwkv_r_k[layer]),
        ind,
    ]
    post_vecs = [row(rwkv_lnx_g[layer]), row(rwkv_lnx_b[layer]), row(gla_norm_g[layer]),
                 row(norm2_g[layer]), row(final_norm_g)]
    post_mats = [ind, bf(w_out[layer]), bf(mlp_w1[layer]), bf(mlp_w2[layer])]

    def run_group(x, mod_row, grid_shift, s_rf, s_rb, s_gf, s_gb):
        (r, v, kap, kd_f, kd_b, b_f, b_b, lw_f, lw_b, gate, bonus,
         q, kg, la_f, la_b, vg, gsil) = _pre(x, mod3, mod_row, pre_w, grid_shift)
        y_f, y_b, o_f, o_b, n_rf, n_rb, n_gf, n_gb = _scan(
            r, v, kap, kd_f, kd_b, b_f, b_b, lw_f, lw_b, q, kg, vg, la_f, la_b,
            _rwkv_state_to_pairs(s_rf), _rwkv_state_to_pairs(s_rb),
            _gla_state_to_pairs(s_gf), _gla_state_to_pairs(s_gb))
        y = _post(x, mod3, mod_row, y_f, y_b, o_f, o_b, bonus, gate, gsil, post_vecs, post_mats)
        return y, (n_rf, n_rb, n_gf, n_gb)

    zr = jnp.zeros((nb, RWKV_HEADS, RWKV_HEAD_DIM, RWKV_HEAD_DIM), F32)
    zg = jnp.zeros((nb, GLA_HEADS, GLA_KEY_DIM, GLA_VAL_DIM), F32)
    y_prompt, (n_rf, n_rb, n_gf, n_gb) = run_group(x_prompt, lambda b: 0, False, zr, zr, zg, zg)
    y_sample, _ = run_group(x_sample, lambda b: b + 1, True,
                            state_rwkv_fwd[:, layer], state_rwkv_bwd[:, layer],
                            state_gla_fwd[:, layer], state_gla_bwd[:, layer])
    return (y_prompt, y_sample,
            _rwkv_pairs_to_state(n_rf)[:, None], _rwkv_pairs_to_state(n_rb)[:, None],
            _gla_pairs_to_state(n_gf)[:, None], _gla_pairs_to_state(n_gb)[:, None])
```

```python
import functools

import jax
import jax.numpy as jnp
from jax import lax
from jax.experimental import pallas as pl
from jax.experimental.pallas import tpu as pltpu

F32 = jnp.float32
BF16 = jnp.bfloat16
HI = lax.Precision.HIGHEST

D_MODEL = 1024
GRID_W = 64
RWKV_WIDTH = 512
RWKV_HEAD_DIM = 64
RWKV_HEADS = 8
GLA_HEADS = 4
GLA_KEY_DIM = 64
GLA_VAL_DIM = 128
GLA_QK_WIDTH = 256
GLA_V_WIDTH = 512
GLA_GATE_NORMALIZER = 16.0
N_MOD = 6
RMS_EPS = 1e-6
LNX_EPS = 64e-5
GLA_NORM_EPS = 1e-5

LANES = 128
RWKV_CHUNK = 64
GLA_CHUNK = 16
GLA_WIDE_CHUNK = 64
GLA_SAFE_LOG = 60.0
SCAN_BLOCK = 128
PRE_BLOCK = 256
POST_BLOCK = 512
VMEM_LIMIT = 56 * 1024 * 1024

P_SCORE = "b1"
P_INV = "b1"
P_SOLVE = "b1"
P_STATE = "b1"
P_OUT = "b1"
P_GSTATE = "b1"
P_GATT = "b1"


def _dot(a, b, prec=None):
    return lax.dot_general(a, b, (((1,), (0,)), ((), ())), precision=prec, preferred_element_type=F32)


def _dot_nt(a, b, prec=None):
    return lax.dot_general(a, b, (((1,), (1,)), ((), ())), precision=prec, preferred_element_type=F32)


def _dot_tn(a, b, prec=None):
    return lax.dot_general(a, b, (((0,), (0,)), ((), ())), precision=prec, preferred_element_type=F32)


_NN = (((1,), (0,)), ((), ()))
_NT = (((1,), (1,)), ((), ()))
_TN = (((0,), (0,)), ((), ()))


def _split2(x):
    hi = x.astype(BF16)
    return hi, (x - hi.astype(F32)).astype(BF16)


def _mm(a, b, dims, mode):
    dg = functools.partial(lax.dot_general, dimension_numbers=dims, preferred_element_type=F32)
    if mode == "hi":
        return dg(a, b, precision=HI)
    if mode == "b1":
        return dg(a.astype(BF16), b.astype(BF16))
    a1, a2 = _split2(a)
    b1, b2 = _split2(b)
    return dg(a1, b1) + dg(a1, b2) + dg(a2, b1)


def _cumsum_mm(tri, x):
    hi = x.astype(BF16)
    r1 = x - hi.astype(F32)
    mid = r1.astype(BF16)
    lo = (r1 - mid.astype(F32)).astype(BF16)
    return _dot(tri, hi) + _dot(tri, mid) + _dot(tri, lo)


def _bdot(a, w_ref):
    return _dot(a.astype(BF16), w_ref[...])


def _sigmoid(x):
    return 1.0 / (1.0 + jnp.exp(-x))


def _softplus(x):
    return jnp.maximum(x, 0.0) + jnp.log(1.0 + jnp.exp(-jnp.abs(x)))


def _seg_sum(x, ind_ref):
    ind = ind_ref[...]
    hi = x.astype(BF16)
    r1 = x - hi.astype(F32)
    mid = r1.astype(BF16)
    lo = (r1 - mid.astype(F32)).astype(BF16)
    return _dot(hi, ind) + _dot(mid, ind) + _dot(lo, ind)


def _rms(x):
    return x * lax.rsqrt(jnp.mean(x * x, axis=-1, keepdims=True) + RMS_EPS)


def _mod_kernel(c_ref, w_ref, b_ref, o_ref):
    c = c_ref[...]
    cond = c * _sigmoid(c)
    o_ref[...] = _dot(cond.astype(BF16), w_ref[...].astype(BF16)) + b_ref[...]


def _modulation(craw, ada_w, ada_b):
    n = ada_w.shape[1]
    bn = 1536
    return pl.pallas_call(
        _mod_kernel,
        grid=(n // bn,),
        in_specs=[
            pl.BlockSpec((8, D_MODEL), lambda j: (0, 0)),
            pl.BlockSpec((D_MODEL, bn), lambda j: (0, j)),
            pl.BlockSpec((1, bn), lambda j: (0, j)),
        ],
        out_specs=pl.BlockSpec((8, bn), lambda j: (0, j)),
        out_shape=jax.ShapeDtypeStruct((8, n), F32),
        compiler_params=pltpu.CompilerParams(dimension_semantics=("arbitrary",), vmem_limit_bytes=VMEM_LIMIT),
        name="modulation",
    )(craw, ada_w, ada_b)


def _pre_kernel(*refs, grid_shift, tb):
    if grid_shift:
        x_ref, xp_ref, xn_ref = refs[:3]
        refs = refs[3:]
    else:
        x_ref = refs[0]
        refs = refs[1:]
    (mod_ref, n1g_ref, wrkv_ref, wrest_ref, gk1_ref, w1_ref, a1_ref, g1_ref, w2_ref, a2_ref, g2_ref,
     gk2_ref, murkv_ref, muwag_ref, w0_ref, a0_ref, gkb_ref, kk_ref, ka_ref, rk_ref, ind_ref) = refs[:21]
    (r_o, v_o, kap_o, kdf_o, kdb_o, bf_o, bb_o, lwf_o, lwb_o, gate_o, bonus_o,
     q_o, kg_o, laf_o, lab_o, vg_o, gsil_o) = refs[21:]
    D = D_MODEL
    R = RWKV_WIDTH
    m = mod_ref[0]
    sh1 = m[:, 0:D]
    sc1 = m[:, D:2 * D]
    n1g = n1g_ref[...]

    def normmod(xx):
        return _rms(xx) * n1g * (1.0 + sc1) + sh1

    h = normmod(x_ref[0])
    row = lax.broadcasted_iota(jnp.int32, (tb, 1), 0)
    if grid_shift:
        i = pl.program_id(1)
        n = pl.num_programs(1)
        hp = normmod(xp_ref[0]) * (i > 0).astype(F32)
        hn = normmod(xn_ref[0]) * (i < n - 1).astype(F32)
        hext = jnp.concatenate([hp, h, hn], axis=0)
        col = row % GRID_W
        m_l = (col != 0).astype(F32)
        m_r = (col != GRID_W - 1).astype(F32)
        ne = tb + 2 * GRID_W

        def shift(ext):
            up = ext[0:tb]
            down = ext[2 * GRID_W:2 * GRID_W + tb]
            left = pltpu.roll(ext, 1, 0)[GRID_W:GRID_W + tb]
            right = pltpu.roll(ext, ne - 1, 0)[GRID_W:GRID_W + tb]
            return 0.25 * (up + down + m_l * left + m_r * right)

        halo = GRID_W
    else:
        hext = h
        m_l = (row != 0).astype(F32)
        m_r = (row != tb - 1).astype(F32)

        def shift(ext):
            return 0.5 * (m_l * pltpu.roll(ext, 1, 0) + m_r * pltpu.roll(ext, tb - 1, 0))

        halo = 0

    rkv_ext = _bdot(hext, wrkv_ref)
    rkv = rkv_ext[halo:halo + tb]
    rkv = rkv + murkv_ref[...] * (shift(rkv_ext) - rkv)
    r = rkv[:, 0:R]
    k = rkv[:, R:2 * R]
    v = rkv[:, 2 * R:3 * R]

    dh = shift(hext) - h
    mu = muwag_ref[...]
    xw = h + mu[0:1] * dh
    xa = h + mu[1:2] * dh
    xg = h + mu[2:3] * dh

    z = w0_ref[...] + _bdot(jnp.tanh(_bdot(xw, w1_ref)), w2_ref)
    lw = -jnp.exp(-_softplus(-z) - 0.5)
    a = _sigmoid(a0_ref[...] + _bdot(_bdot(xa, a1_ref), a2_ref))
    gate = _bdot(_sigmoid(_bdot(xg, g1_ref)), g2_ref)

    kap = k * kk_ref[...]
    kap = kap * lax.rsqrt(jnp.maximum(_seg_sum(kap * kap, ind_ref), 1e-12))
    ka = ka_ref[...]
    a_f = a[:, 0:R]
    a_b = a[:, R:2 * R]
    kd_f = k * (1.0 + (a_f - 1.0) * ka)
    kd_b = k * (1.0 + (a_b - 1.0) * ka)
    bonus = _seg_sum(r * (kd_f + kd_b) * rk_ref[...], ind_ref) * v

    r_o[0] = r
    v_o[0] = v
    kap_o[0] = kap
    kdf_o[0] = kd_f
    kdb_o[0] = kd_b
    bf_o[0] = a_f * kap
    bb_o[0] = a_b * kap
    lwf_o[0] = lw[:, 0:R]
    lwb_o[0] = lw[:, R:2 * R]
    gate_o[0] = gate
    bonus_o[0] = bonus

    rest = _bdot(h, wrest_ref)
    Q = GLA_QK_WIDTH
    q_o[0] = rest[:, 0:Q] * (GLA_KEY_DIM ** -0.5)
    kg_o[0] = rest[:, Q:2 * Q]
    vg_o[0] = rest[:, 2 * Q:2 * Q + GLA_V_WIDTH]
    gg = rest[:, 2 * Q + GLA_V_WIDTH:]
    gsil_o[0] = gg * _sigmoid(gg)
    logits = _bdot(_bdot(h, gk1_ref), gk2_ref) + gkb_ref[...]
    la = -_softplus(-logits) * (1.0 / GLA_GATE_NORMALIZER)
    laf_o[0] = la[:, 0:Q]
    lab_o[0] = la[:, Q:2 * Q]


def _const_spec(shape):
    nd = len(shape)
    return pl.BlockSpec(shape, lambda b, i: (0,) * nd)


def _pre(x, mod3, mod_row, weights, grid_shift):
    B, L, D = x.shape
    tb = PRE_BLOCK
    nblk = L // tb
    if not grid_shift:
        assert nblk == 1
    x_spec = pl.BlockSpec((1, tb, D), lambda b, i: (b, i, 0))
    in_specs = [x_spec]
    args = [x]
    if grid_shift:
        per = tb // GRID_W
        nrow = L // GRID_W
        in_specs += [
            pl.BlockSpec((1, GRID_W, D), lambda b, i: (b, jnp.maximum(i * per - 1, 0), 0)),
            pl.BlockSpec((1, GRID_W, D), lambda b, i: (b, jnp.minimum((i + 1) * per, nrow - 1), 0)),
        ]
        args += [x, x]
    in_specs.append(pl.BlockSpec((1, 1, N_MOD * D), lambda b, i: (mod_row(b), 0, 0)))
    args.append(mod3)
    for w in weights:
        in_specs.append(_const_spec(w.shape))
        args.append(w)
    widths = [512] * 11 + [256, 256, 256, 256, 512, 512]
    out_specs = [pl.BlockSpec((1, tb, w), lambda b, i: (b, i, 0)) for w in widths]
    out_shape = [jax.ShapeDtypeStruct((B, L, w), F32) for w in widths]
    return pl.pallas_call(
        functools.partial(_pre_kernel, grid_shift=grid_shift, tb=tb),
        grid=(B, nblk),
        in_specs=in_specs,
        out_specs=out_specs,
        out_shape=out_shape,
        compiler_params=pltpu.CompilerParams(dimension_semantics=("parallel", "arbitrary"),
                                             vmem_limit_bytes=VMEM_LIMIT),
        name="pre_grid" if grid_shift else "pre_seq",
    )(*args)


def _inv_unit_triangular(lms, eye, blk16, blk32):
    mm = functools.partial(_mm, dims=_NN, mode=P_INV)
    l0 = [jnp.where(blk16, lm, 0.0) for lm in lms]
    l2 = [mm(a, a) for a in l0]
    l4 = [mm(a, a) for a in l2]
    l8 = [mm(a, a) for a in l4]
    t = [mm(eye - a, eye + b) for a, b in zip(l0, l2)]
    t = [mm(a, eye + b) for a, b in zip(t, l4)]
    t = [mm(a, eye + b) for a, b in zip(t, l8)]
    off1_mask = jnp.logical_and(blk32, jnp.logical_not(blk16))
    x = [mm(a, jnp.where(off1_mask, lm, 0.0)) for a, lm in zip(t, lms)]
    t = [a - mm(b, a) for a, b in zip(t, x)]
    x = [mm(a, jnp.where(blk32, 0.0, lm)) for a, lm in zip(t, lms)]
    t = [a - mm(b, a) for a, b in zip(t, x)]
    return t


def _rwkv_prepare(units, cst):
    C = RWKV_CHUNK
    m0, cm0, eye, blk16, blk32, _ = cst[2]

    def stack(x):
        return jnp.concatenate([jnp.where(m0, x, 0.0), jnp.where(m0, 0.0, x)], axis=0)

    def blockdiag(side):
        return jnp.concatenate([jnp.where(cm0, side, 0.0), jnp.where(cm0, 0.0, side)], axis=0)

    cums = [_cumsum_mm(cst[rev][0], lw) for (_, _, _, _, _, lw, rev) in units]
    prep = []
    for (r, k, v, kap, b, lw, rev), cum in zip(units, cums):
        cumx = cum - lw
        if rev:
            mid = cum[C // 2:C // 2 + 1]
            tot = cum[0:1]
        else:
            mid = cum[C // 2 - 1:C // 2]
            tot = cum[C - 1:C]
        e_mid = jnp.exp(mid)
        rt = r * jnp.exp(cum - mid)
        kt = kap * jnp.exp(cumx - mid)
        es = jnp.exp(mid - cum)
        e_end = jnp.exp(tot - mid)
        prep.append(dict(rt=rt, kt=kt, bh=b * es, kh=k * es, r0=rt * e_mid, k0=stack(kt * e_mid),
                         be=b * es * e_end, ke=k * es * e_end, vs=stack(v), v=v,
                         e_tot=jnp.exp(tot), strict2=cst[rev][1], incl2=cst[rev][2]))
    gs = [_mm(jnp.concatenate([p["kt"], p["rt"]], axis=0),
              jnp.concatenate([stack(p["bh"]), stack(p["kh"])], axis=0), _NT, P_SCORE) for p in prep]
    for p, g in zip(prep, gs):
        p["ab"] = blockdiag(jnp.where(p["strict2"], g[0:C, 0:2 * C], 0.0))
        p["ak"] = blockdiag(jnp.where(p["strict2"], g[0:C, 2 * C:4 * C], 0.0))
        p["rb"] = jnp.where(p["incl2"], g[C:2 * C, 0:2 * C], 0.0)
        p["rk"] = jnp.where(p["incl2"], g[C:2 * C, 2 * C:4 * C], 0.0)
    av = [_mm(jnp.concatenate([p["ak"], p["rk"]], axis=0), p["vs"], _NN, P_SOLVE) for p in prep]
    ts = _inv_unit_triangular([p["ab"] for p in prep], eye, blk16, blk32)
    wu = [_mm(t, jnp.concatenate([p["k0"], a[0:2 * C]], axis=1), _NN, P_SOLVE) for t, p, a in zip(ts, prep, av)]
    return [dict(wr=jnp.concatenate([x[:, :LANES], p["r0"]], axis=0), u0=x[:, LANES:], yv=a[2 * C:3 * C],
                 rb=p["rb"], v=p["v"], bke=jnp.concatenate([p["be"], p["ke"]], axis=0), e_tot=p["e_tot"])
            for x, a, p in zip(wu, av, prep)]


def _rwkv_apply(prep, states, cst):
    C = RWKV_CHUNK
    blk64 = cst[2][5]
    ws = [_mm(p["wr"], s, _NT, P_STATE) for p, s in zip(prep, states)]
    us = [-p["u0"] - w[0:2 * C] for p, w in zip(prep, ws)]
    upd = [_mm(jnp.concatenate([u[0:C] + u[C:2 * C], p["v"]], axis=0), p["bke"], _TN, P_STATE)
           for u, p in zip(us, prep)]
    s_new = [s * p["e_tot"] + jnp.where(blk64, d, 0.0) for p, s, d in zip(prep, states, upd)]
    ys = [w[2 * C:3 * C] + _mm(p["rb"], u, _NN, P_OUT) + p["yv"] for p, w, u in zip(prep, ws, us)]
    return ys, s_new


def _rwkv_consts():
    C = RWKV_CHUNK
    row = lax.broadcasted_iota(jnp.int32, (C, C), 0)
    col = lax.broadcasted_iota(jnp.int32, (C, C), 1)
    row2 = lax.broadcasted_iota(jnp.int32, (C, 2 * C), 0)
    col2 = lax.broadcasted_iota(jnp.int32, (C, 2 * C), 1)
    cs = col2 % C
    lane = lax.broadcasted_iota(jnp.int32, (1, LANES), 1)
    m0 = lane < RWKV_HEAD_DIM
    cm0 = lax.broadcasted_iota(jnp.int32, (1, 2 * C), 1) < C
    rr = lax.broadcasted_iota(jnp.int32, (2 * C, 2 * C), 0)
    cc = lax.broadcasted_iota(jnp.int32, (2 * C, 2 * C), 1)
    eye = (rr == cc).astype(F32)
    blk16 = (rr // 16) == (cc // 16)
    blk32 = (rr // 32) == (cc // 32)
    blk64 = (rr // 64) == (cc // 64)
    fwd = ((col <= row).astype(BF16), cs < row2, cs <= row2)
    bwd = ((col >= row).astype(BF16), cs > row2, cs >= row2)
    return fwd, bwd, (m0, cm0, eye, blk16, blk32, blk64)


def _gla_intra(items, cst):
    G = GLA_CHUNK
    ind_v, sel, blk = cst[2]
    srow = lax.broadcasted_iota(jnp.int32, (G, 1), 0)
    cums = [_cumsum_mm(cst[rev], la) for (_, _, _, la, rev) in items]
    pmats = []
    for (q, k, v, la, rev), cum in zip(items, cums):
        ps = []
        for t in range(G):
            msk = (srow >= t) if rev else (srow <= t)
            e = jnp.exp(jnp.where(msk, cum[t:t + 1] - cum, 0.0))
            ps.append(jnp.where(msk, e * (q[t:t + 1] * k), 0.0))
        pmats.append(jnp.concatenate(ps, axis=0))
    atts = [_mm(pm, ind_v, _NN, P_GATT) for pm in pmats]
    o_intra = [_mm(sel, att * jnp.concatenate([it[2]] * G, axis=0), _NN, P_GATT)
               for att, it in zip(atts, items)]
    out = []
    for (q, k, v, la, rev), cum, oi in zip(items, cums, o_intra):
        tot = cum[0:1] if rev else cum[G - 1:G]
        upd = _mm(v, k * jnp.exp(tot - cum), _TN, P_GSTATE)
        out.append((oi, q * jnp.exp(cum), jnp.where(blk, upd, 0.0), jnp.exp(tot)))
    return out


def _gla_consts():
    G = GLA_CHUNK
    row = lax.broadcasted_iota(jnp.int32, (G, G), 0)
    col = lax.broadcasted_iota(jnp.int32, (G, G), 1)
    kc = lax.broadcasted_iota(jnp.int32, (LANES, 2 * LANES), 0)
    vc = lax.broadcasted_iota(jnp.int32, (LANES, 2 * LANES), 1)
    ind_v = ((kc // GLA_KEY_DIM) == (vc // GLA_VAL_DIM)).astype(F32)
    st = lax.broadcasted_iota(jnp.int32, (G, G * G), 0)
    sj = lax.broadcasted_iota(jnp.int32, (G, G * G), 1)
    sel = ((sj // G) == st).astype(F32)
    br = lax.broadcasted_iota(jnp.int32, (2 * LANES, LANES), 0)
    bc = lax.broadcasted_iota(jnp.int32, (2 * LANES, LANES), 1)
    blk = (br // GLA_VAL_DIM) == (bc // GLA_KEY_DIM)
    return (col <= row).astype(BF16), (col >= row).astype(BF16), (ind_v, sel, blk)


def _gla_factored(items, cst):
    F = GLA_WIDE_CHUNK
    m0, vm0, blk = cst[2]
    cums = [_cumsum_mm(cst[rev][0], la) for (_, _, _, la, rev) in items]
    mids = []
    scores = []
    for (q, k, v, la, rev), cum in zip(items, cums):
        mid = cum[F // 2:F // 2 + 1] if rev else cum[F // 2 - 1:F // 2]
        kt = k * jnp.exp(mid - cum)
        kstk = jnp.concatenate([jnp.where(m0, kt, 0.0), jnp.where(m0, 0.0, kt)], axis=0)
        scores.append(_mm(q * jnp.exp(cum - mid), kstk, _NT, P_GATT))
        mids.append(mid)
    o_intra = []
    for (q, k, v, la, rev), sc in zip(items, scores):
        vstk = jnp.concatenate([jnp.where(vm0, v, 0.0), jnp.where(vm0, 0.0, v)], axis=0)
        o_intra.append(_mm(jnp.where(cst[rev][1], sc, 0.0), vstk, _NN, P_GATT))
    out = []
    for (q, k, v, la, rev), cum, oi in zip(items, cums, o_intra):
        tot = cum[0:1] if rev else cum[F - 1:F]
        upd = _mm(v, k * jnp.exp(tot - cum), _TN, P_GSTATE)
        out.append((oi, q * jnp.exp(cum), jnp.where(blk, upd, 0.0), jnp.exp(tot)))
    return out


def _gla_wide_consts():
    F = GLA_WIDE_CHUNK
    row = lax.broadcasted_iota(jnp.int32, (F, F), 0)
    col = lax.broadcasted_iota(jnp.int32, (F, F), 1)
    row2 = lax.broadcasted_iota(jnp.int32, (F, 2 * F), 0)
    cs = lax.broadcasted_iota(jnp.int32, (F, 2 * F), 1) % F
    m0 = lax.broadcasted_iota(jnp.int32, (1, LANES), 1) < GLA_KEY_DIM
    vm0 = lax.broadcasted_iota(jnp.int32, (1, 2 * LANES), 1) < GLA_VAL_DIM
    br = lax.broadcasted_iota(jnp.int32, (2 * LANES, LANES), 0)
    bc = lax.broadcasted_iota(jnp.int32, (2 * LANES, LANES), 1)
    blk = (br // GLA_VAL_DIM) == (bc // GLA_KEY_DIM)
    return (((col <= row).astype(BF16), cs <= row2), ((col >= row).astype(BF16), cs >= row2), (m0, vm0, blk))


def _scan_kernel(rf_ref, rvf_ref, kapf_ref, kdf_ref, bf_ref, lwf_ref,
                 rb_ref, rvb_ref, kapb_ref, kdb_ref, bb_ref, lwb_ref,
                 qf_ref, kf_ref, gvf_ref, laf_ref, qb_ref, kb_ref, gvb_ref, lab_ref,
                 srf0_ref, srb0_ref, sgf0_ref, sgb0_ref,
                 yf_ref, yb_ref, of_ref, ob_ref, srf_ref, srb_ref, sgf_ref, sgb_ref):
    i = pl.program_id(1)

    @pl.when(i == 0)
    def _():
        srf_ref[...] = srf0_ref[...]
        srb_ref[...] = srb0_ref[...]
        sgf_ref[...] = sgf0_ref[...]
        sgb_ref[...] = sgb0_ref[...]

    T = SCAN_BLOCK
    C = RWKV_CHUNK
    nr = T // C
    rcst = _rwkv_consts()
    g_dirs = ((False, qf_ref, kf_ref, gvf_ref, laf_ref, sgf_ref, of_ref),
              (True, qb_ref, kb_ref, gvb_ref, lab_ref, sgb_ref, ob_ref))
    la_tot = jnp.minimum(jnp.sum(laf_ref[0], axis=0, keepdims=True), jnp.sum(lab_ref[0], axis=0, keepdims=True))
    gla_split_ok = jnp.min(la_tot) >= -GLA_SAFE_LOG

    r_dirs = ((False, rf_ref, kdf_ref, rvf_ref, kapf_ref, bf_ref, lwf_ref, srf_ref, yf_ref),
              (True, rb_ref, kdb_ref, rvb_ref, kapb_ref, bb_ref, lwb_ref, srb_ref, yb_ref))
    r_units = [[] for _ in range(nr)]
    r_dests = [[] for _ in range(nr)]
    for rev, r_ref, k_ref, v_ref, kap_ref, b_ref, lw_ref, s_ref, y_ref in r_dirs:
        for p in range(RWKV_WIDTH // LANES):
            sl = slice(p * LANES, (p + 1) * LANES)
            for j in range(nr):
                c = nr - 1 - j if rev else j
                rows = slice(c * C, (c + 1) * C)
                r_units[j].append((r_ref[0, rows, sl], k_ref[0, rows, sl], v_ref[0, rows, sl],
                                   kap_ref[0, rows, sl], b_ref[0, rows, sl], lw_ref[0, rows, sl], rev))
                r_dests[j].append((y_ref, s_ref, p, rows, sl))
    r_prep = _rwkv_prepare([u for us in r_units for u in us], rcst)
    nu = len(r_units[0])
    r_prep = [r_prep[j * nu:(j + 1) * nu] for j in range(nr)]

    states = [s_ref[0, p] for (_, s_ref, p, _, _) in r_dests[0]]
    for j in range(nr):
        ys, states = _rwkv_apply(r_prep[j], states, rcst)
        for (y_ref, _, _, rows, sl), y in zip(r_dests[j], ys):
            y_ref[0, rows, sl] = y
    for (_, s_ref, p, _, _), s in zip(r_dests[0], states):
        s_ref[0, p] = s

    def gla(chunk, intra, cst):
        ng = T // chunk
        items = []
        dests = []
        for rev, q_ref, k_ref, v_ref, la_ref, s_ref, o_ref in g_dirs:
            for p in range(GLA_QK_WIDTH // LANES):
                ks = slice(p * LANES, (p + 1) * LANES)
                vs = slice(p * 2 * LANES, (p + 1) * 2 * LANES)
                for j in range(ng):
                    c = ng - 1 - j if rev else j
                    rows = slice(c * chunk, (c + 1) * chunk)
                    items.append((q_ref[0, rows, ks], k_ref[0, rows, ks], v_ref[0, rows, vs],
                                  la_ref[0, rows, ks], rev))
                    dests.append((o_ref, s_ref, p, rows, vs, j))
        st = None
        for (o_ref, s_ref, p, rows, vs, j), (o_intra, q_in, upd, e_tot) in zip(dests, intra(items, cst)):
            if j == 0:
                st = s_ref[0, p]
            o_ref[0, rows, vs] = o_intra + _mm(q_in, st, _NT, P_GSTATE)
            st = st * e_tot + upd
            if j == ng - 1:
                s_ref[0, p] = st

    @pl.when(gla_split_ok)
    def _():
        gla(GLA_WIDE_CHUNK, _gla_factored, _gla_wide_consts())

    @pl.when(jnp.logical_not(gla_split_ok))
    def _():
        gla(GLA_CHUNK, _gla_intra, _gla_consts())


def _scan(r, v, kap, kd_f, kd_b, b_f, b_b, lw_f, lw_b, q, kg, vg, la_f, la_b, s_rf, s_rb, s_gf, s_gb):
    B, L, R = r.shape
    Q = q.shape[-1]
    V = vg.shape[-1]
    T = SCAN_BLOCK
    n = L // T

    def fwd(w):
        return pl.BlockSpec((1, T, w), lambda b, i: (b, i, 0))

    def bwd(w):
        return pl.BlockSpec((1, T, w), lambda b, i: (b, n - 1 - i, 0))

    rst = pl.BlockSpec((1, R // LANES, LANES, LANES), lambda b, i: (b, 0, 0, 0))
    gst = pl.BlockSpec((1, Q // LANES, 2 * LANES, LANES), lambda b, i: (b, 0, 0, 0))
    return pl.pallas_call(
        _scan_kernel,
        grid=(B, n),
        in_specs=[fwd(R)] * 6 + [bwd(R)] * 6 + [fwd(Q), fwd(Q), fwd(V), fwd(Q), bwd(Q), bwd(Q), bwd(V), bwd(Q)]
        + [rst, rst, gst, gst],
        out_specs=[fwd(R), bwd(R), fwd(V), bwd(V), rst, rst, gst, gst],
        out_shape=[jax.ShapeDtypeStruct((B, L, R), F32)] * 2 + [jax.ShapeDtypeStruct((B, L, V), F32)] * 2
        + [jax.ShapeDtypeStruct((B, R // LANES, LANES, LANES), F32)] * 2
        + [jax.ShapeDtypeStruct((B, Q // LANES, 2 * LANES, LANES), F32)] * 2,
        compiler_params=pltpu.CompilerParams(dimension_semantics=("parallel", "arbitrary"),
                                             vmem_limit_bytes=VMEM_LIMIT),
        name="scan",
    )(r, v, kap, kd_f, b_f, lw_f, r, v, kap, kd_b, b_b, lw_b, q, kg, vg, la_f, q, kg, vg, la_b,
      s_rf, s_rb, s_gf, s_gb)


def _post_kernel(x_ref, mod_ref, yf_ref, yb_ref, of_ref, ob_ref, bonus_ref, gate_ref, gsil_ref,
                 lng_ref, lnb_ref, gng_ref, n2g_ref, fng_ref, ind_ref, wout_ref, w1_ref, w2_ref, o_ref):
    D = D_MODEL
    m = mod_ref[0]
    gt1 = m[:, 2 * D:3 * D]
    sh2 = m[:, 3 * D:4 * D]
    sc2 = m[:, 4 * D:5 * D]
    gt2 = m[:, 5 * D:6 * D]
    inv_n = 1.0 / RWKV_HEAD_DIM
    y = yf_ref[0] + yb_ref[0]
    mu = _seg_sum(y, ind_ref) * inv_n
    yc = y - mu
    var = _seg_sum(yc * yc, ind_ref) * inv_n
    yn = yc * lax.rsqrt(var + LNX_EPS) * lng_ref[...] + lnb_ref[...]
    rw = (yn + bonus_ref[0]) * gate_ref[0]
    o = of_ref[0] + ob_ref[0]
    gsil = gsil_ref[0]
    gng = gng_ref[...]
    parts = [rw]
    for hh in range(GLA_HEADS):
        sl = slice(hh * GLA_VAL_DIM, (hh + 1) * GLA_VAL_DIM)
        oh = o[:, sl]
        oh = oh * lax.rsqrt(jnp.mean(oh * oh, axis=-1, keepdims=True) + GLA_NORM_EPS)
        parts.append(oh * gng * gsil[:, sl])
    mix = jnp.concatenate(parts, axis=-1)
    x1 = x_ref[0] + gt1 * _bdot(mix, wout_ref)
    h2 = _rms(x1) * n2g_ref[...] * (1.0 + sc2) + sh2
    f = jnp.maximum(_bdot(h2, w1_ref), 0.0)
    x2 = x1 + gt2 * _bdot(f * f, w2_ref)
    o_ref[0] = _rms(x2) * fng_ref[...]


def _post(x, mod3, mod_row, y_f, y_b, o_f, o_b, bonus, gate, gsil, vecs, mats):
    B, L, D = x.shape
    tb = min(POST_BLOCK, L)
    nblk = L // tb

    def tok(w):
        return pl.BlockSpec((1, tb, w), lambda b, i: (b, i, 0))

    in_specs = [tok(D), pl.BlockSpec((1, 1, N_MOD * D), lambda b, i: (mod_row(b), 0, 0))]
    in_specs += [tok(512)] * 7
    in_specs += [_const_spec(w.shape) for w in vecs]
    in_specs += [pl.BlockSpec(w.shape, lambda b, i: (0, 0), pipeline_mode=pl.Buffered(1)) for w in mats]
    return pl.pallas_call(
        _post_kernel,
        grid=(B, nblk),
        in_specs=in_specs,
        out_specs=tok(D),
        out_shape=jax.ShapeDtypeStruct((B, L, D), F32),
        compiler_params=pltpu.CompilerParams(dimension_semantics=("parallel", "arbitrary"),
                                             vmem_limit_bytes=VMEM_LIMIT),
        name="post",
    )(x, mod3, y_f, y_b, o_f, o_b, bonus, gate, gsil, *vecs, *mats)


def _rwkv_state_to_pairs(s):
    B = s.shape[0]
    s = s.reshape(B, 4, 2, 64, 64)
    z = jnp.zeros_like(s[:, :, 0])
    top = jnp.concatenate([s[:, :, 0], z], axis=-1)
    bot = jnp.concatenate([z, s[:, :, 1]], axis=-1)
    return jnp.concatenate([top, bot], axis=-2)


def _rwkv_pairs_to_state(sp):
    B = sp.shape[0]
    a = sp[:, :, 0:64, 0:64]
    b = sp[:, :, 64:128, 64:128]
    return jnp.stack([a, b], axis=2).reshape(B, 8, 64, 64)


def _gla_state_to_pairs(s):
    B = s.shape[0]
    st = jnp.swapaxes(s, -1, -2).reshape(B, 2, 2, 128, 64)
    z = jnp.zeros_like(st[:, :, 0])
    top = jnp.concatenate([st[:, :, 0], z], axis=-1)
    bot = jnp.concatenate([z, st[:, :, 1]], axis=-1)
    return jnp.concatenate([top, bot], axis=-2)


def _gla_pairs_to_state(sp):
    B = sp.shape[0]
    a = sp[:, :, 0:128, 0:64]
    b = sp[:, :, 128:256, 64:128]
    st = jnp.stack([a, b], axis=2).reshape(B, 4, 128, 64)
    return jnp.swapaxes(st, -1, -2)


def _blockdiag2(a, b):
    za = jnp.zeros((a.shape[0], b.shape[1]), a.dtype)
    zb = jnp.zeros((b.shape[0], a.shape[1]), a.dtype)
    return jnp.concatenate([jnp.concatenate([a, za], axis=1), jnp.concatenate([zb, b], axis=1)], axis=0)


def kernel(x_prompt, x_sample, c, state_rwkv_fwd, state_rwkv_bwd, state_gla_fwd, state_gla_bwd, c_ctx, ada_w, ada_b, norm1_g, norm2_g, w_in, rwkv_mu_rkv, rwkv_mu_wag, rwkv_w0, rwkv_w1, rwkv_w2, rwkv_a0, rwkv_a1, rwkv_a2, rwkv_g1, rwkv_g2, rwkv_k_k, rwkv_k_a, rwkv_r_k, rwkv_lnx_g, rwkv_lnx_b, gla_gk1, gla_gk2, gla_gk_b, gla_norm_g, w_out, mlp_w1, mlp_w2, final_norm_g):
    D = D_MODEL
    R = RWKV_WIDTH
    nb = x_prompt.shape[0]
    nd = x_sample.shape[0]
    assert ada_w.shape[0] == 1, "single-layer step"
    layer = 0

    craw = jnp.concatenate([c_ctx[None, :], c, jnp.zeros((8 - 1 - nd, D), F32)], axis=0)
    mod = _modulation(craw, ada_w[layer], ada_b[layer][None, :])
    mod3 = mod.reshape(8, 1, N_MOD * D)

    bf = lambda t: t.astype(BF16)
    row = lambda t: t.reshape(1, -1).astype(F32)
    w_in_l = w_in[layer]
    ind = (jnp.arange(R)[:, None] // RWKV_HEAD_DIM == jnp.arange(R)[None, :] // RWKV_HEAD_DIM).astype(BF16)
    pre_w = [
        row(norm1_g[layer]),
        bf(w_in_l[:, :3 * R]),
        bf(w_in_l[:, 3 * R:]),
        bf(jnp.concatenate([gla_gk1[layer, 0], gla_gk1[layer, 1]], axis=1)),
        bf(jnp.concatenate([rwkv_w1[layer, 0], rwkv_w1[layer, 1]], axis=1)),
        bf(jnp.concatenate([rwkv_a1[layer, 0], rwkv_a1[layer, 1]], axis=1)),
        bf(rwkv_g1[layer]),
        bf(_blockdiag2(rwkv_w2[layer, 0], rwkv_w2[layer, 1])),
        bf(_blockdiag2(rwkv_a2[layer, 0], rwkv_a2[layer, 1])),
        bf(rwkv_g2[layer]),
        bf(_blockdiag2(gla_gk2[layer, 0], gla_gk2[layer, 1])),
        row(rwkv_mu_rkv[layer]),
        rwkv_mu_wag[layer],
        row(rwkv_w0[layer]),
        row(rwkv_a0[layer]),
        row(gla_gk_b[layer]),
        row(rwkv_k_k[layer]),
        row(rwkv_k_a[layer]),
        row(rwkv_r_k[layer]),
        ind,
    ]
    post_vecs = [row(rwkv_lnx_g[layer]), row(rwkv_lnx_b[layer]), row(gla_norm_g[layer]),
                 row(norm2_g[layer]), row(final_norm_g)]
    post_mats = [ind, bf(w_out[layer]), bf(mlp_w1[layer]), bf(mlp_w2[layer])]

    def run_group(x, mod_row, grid_shift, s_rf, s_rb, s_gf, s_gb):
        (r, v, kap, kd_f, kd_b, b_f, b_b, lw_f, lw_b, gate, bonus,
         q, kg, la_f, la_b, vg, gsil) = _pre(x, mod3, mod_row, pre_w, grid_shift)
        y_f, y_b, o_f, o_b, n_rf, n_rb, n_gf, n_gb = _scan(
            r, v, kap, kd_f, kd_b, b_f, b_b, lw_f, lw_b, q, kg, vg, la_f, la_b,
            _rwkv_state_to_pairs(s_rf), _rwkv_state_to_pairs(s_rb),
            _gla_state_to_pairs(s_gf), _gla_state_to_pairs(s_gb))
        y = _post(x, mod3, mod_row, y_f, y_b, o_f, o_b, bonus, gate, gsil, post_vecs, post_mats)
        return y, (n_rf, n_rb, n_gf, n_gb)

    zr = jnp.zeros((nb, RWKV_HEADS, RWKV_HEAD_DIM, RWKV_HEAD_DIM), F32)
    zg = jnp.zeros((nb, GLA_HEADS, GLA_KEY_DIM, GLA_VAL_DIM), F32)
    y_prompt, (n_rf, n_rb, n_gf, n_gb) = run_group(x_prompt, lambda b: 0, False, zr, zr, zg, zg)
    y_sample, _ = run_group(x_sample, lambda b: b + 1, True,
                            state_rwkv_fwd[:, layer], state_rwkv_bwd[:, layer],
                            state_gla_fwd[:, layer], state_gla_bwd[:, layer])
    return (y_prompt, y_sample,
            _rwkv_pairs_to_state(n_rf)[:, None], _rwkv_pairs_to_state(n_rb)[:, None],
            _gla_pairs_to_state(n_gf)[:, None], _gla_pairs_to_state(n_gb)[:, None])
```

```python
import functools

import jax
import jax.numpy as jnp
from jax import lax
from jax.experimental import pallas as pl
from jax.experimental.pallas import tpu as pltpu

F32 = jnp.float32
BF16 = jnp.bfloat16
HI = lax.Precision.HIGHEST

D_MODEL = 1024
GRID_W = 64
RWKV_WIDTH = 512
RWKV_HEAD_DIM = 64
RWKV_HEADS = 8
GLA_HEADS = 4
GLA_KEY_DIM = 64
GLA_VAL_DIM = 128
GLA_QK_WIDTH = 256
GLA_V_WIDTH = 512
GLA_GATE_NORMALIZER = 16.0
N_MOD = 6
RMS_EPS = 1e-6
LNX_EPS = 64e-5
GLA_NORM_EPS = 1e-5

LANES = 128
RWKV_CHUNK = 64
GLA_CHUNK = 16
GLA_WIDE_CHUNK = 64
GLA_SAFE_LOG = 60.0
SCAN_BLOCK = 128
PRE_BLOCK = 256
POST_BLOCK = 512
VMEM_LIMIT = 56 * 1024 * 1024
RWKV_TASK_LAG = 3
GLA_TASK_LAG = 12

P_SCORE = "b1"
P_INV = "b1"
P_SOLVE = "b1"
P_STATE = "b1"
P_OUT = "b1"
P_GSTATE = "b1"
P_GATT = "b1"


def _dot(a, b, prec=None):
    return lax.dot_general(a, b, (((1,), (0,)), ((), ())), precision=prec, preferred_element_type=F32)


def _dot_nt(a, b, prec=None):
    return lax.dot_general(a, b, (((1,), (1,)), ((), ())), precision=prec, preferred_element_type=F32)


def _dot_tn(a, b, prec=None):
    return lax.dot_general(a, b, (((0,), (0,)), ((), ())), precision=prec, preferred_element_type=F32)


_NN = (((1,), (0,)), ((), ()))
_NT = (((1,), (1,)), ((), ()))
_TN = (((0,), (0,)), ((), ()))


def _split2(x):
    hi = x.astype(BF16)
    return hi, (x - hi.astype(F32)).astype(BF16)


def _mm(a, b, dims, mode):
    dg = functools.partial(lax.dot_general, dimension_numbers=dims, preferred_element_type=F32)
    if mode == "hi":
        return dg(a, b, precision=HI)
    if mode == "b1":
        return dg(a.astype(BF16), b.astype(BF16))
    a1, a2 = _split2(a)
    b1, b2 = _split2(b)
    return dg(a1, b1) + dg(a1, b2) + dg(a2, b1)


def _cumsum_mm(tri, x):
    hi = x.astype(BF16)
    r1 = x - hi.astype(F32)
    mid = r1.astype(BF16)
    lo = (r1 - mid.astype(F32)).astype(BF16)
    return _dot(tri, hi) + _dot(tri, mid) + _dot(tri, lo)


def _bdot(a, w_ref):
    return _dot(a.astype(BF16), w_ref[...])


def _sigmoid(x):
    return 1.0 / (1.0 + jnp.exp(-x))


def _softplus(x):
    return jnp.maximum(x, 0.0) + jnp.log(1.0 + jnp.exp(-jnp.abs(x)))


def _seg_sum(x, ind_ref):
    ind = ind_ref[...]
    hi = x.astype(BF16)
    r1 = x - hi.astype(F32)
    mid = r1.astype(BF16)
    lo = (r1 - mid.astype(F32)).astype(BF16)
    return _dot(hi, ind) + _dot(mid, ind) + _dot(lo, ind)


def _rms(x):
    return x * lax.rsqrt(jnp.mean(x * x, axis=-1, keepdims=True) + RMS_EPS)


def _mod_kernel(c_ref, w_ref, b_ref, o_ref):
    c = c_ref[...]
    cond = c * _sigmoid(c)
    o_ref[...] = _dot(cond.astype(BF16), w_ref[...].astype(BF16)) + b_ref[...]


def _modulation(craw, ada_w, ada_b):
    n = ada_w.shape[1]
    bn = 1536
    return pl.pallas_call(
        _mod_kernel,
        grid=(n // bn,),
        in_specs=[
            pl.BlockSpec((8, D_MODEL), lambda j: (0, 0)),
            pl.BlockSpec((D_MODEL, bn), lambda j: (0, j)),
            pl.BlockSpec((1, bn), lambda j: (0, j)),
        ],
        out_specs=pl.BlockSpec((8, bn), lambda j: (0, j)),
        out_shape=jax.ShapeDtypeStruct((8, n), F32),
        compiler_params=pltpu.CompilerParams(dimension_semantics=("arbitrary",), vmem_limit_bytes=VMEM_LIMIT),
        name="modulation",
    )(craw, ada_w, ada_b)


def _pre_kernel(*refs, grid_shift, tb):
    if grid_shift:
        x_ref, xp_ref, xn_ref = refs[:3]
        refs = refs[3:]
    else:
        x_ref = refs[0]
        refs = refs[1:]
    (mod_ref, n1g_ref, wrkv_ref, wrest_ref, gk1_ref, w1_ref, a1_ref, g1_ref, w2_ref, a2_ref, g2_ref,
     gk2_ref, murkv_ref, muwag_ref, w0_ref, a0_ref, gkb_ref, kk_ref, ka_ref, rk_ref, ind_ref) = refs[:21]
    (r_o, v_o, kap_o, kdf_o, kdb_o, bf_o, bb_o, lwf_o, lwb_o, gate_o, bonus_o,
     q_o, kg_o, laf_o, lab_o, vg_o, gsil_o) = refs[21:]
    D = D_MODEL
    R = RWKV_WIDTH
    m = mod_ref[0]
    sh1 = m[:, 0:D]
    sc1 = m[:, D:2 * D]
    n1g = n1g_ref[...]

    def normmod(xx):
        return _rms(xx) * n1g * (1.0 + sc1) + sh1

    h = normmod(x_ref[0])
    row = lax.broadcasted_iota(jnp.int32, (tb, 1), 0)
    if grid_shift:
        i = pl.program_id(1)
        n = pl.num_programs(1)
        hp = normmod(xp_ref[0]) * (i > 0).astype(F32)
        hn = normmod(xn_ref[0]) * (i < n - 1).astype(F32)
        hext = jnp.concatenate([hp, h, hn], axis=0)
        col = row % GRID_W
        m_l = (col != 0).astype(F32)
        m_r = (col != GRID_W - 1).astype(F32)
        ne = tb + 2 * GRID_W

        def shift(ext):
            up = ext[0:tb]
            down = ext[2 * GRID_W:2 * GRID_W + tb]
            left = pltpu.roll(ext, 1, 0)[GRID_W:GRID_W + tb]
            right = pltpu.roll(ext, ne - 1, 0)[GRID_W:GRID_W + tb]
            return 0.25 * (up + down + m_l * left + m_r * right)

        halo = GRID_W
    else:
        hext = h
        m_l = (row != 0).astype(F32)
        m_r = (row != tb - 1).astype(F32)

        def shift(ext):
            return 0.5 * (m_l * pltpu.roll(ext, 1, 0) + m_r * pltpu.roll(ext, tb - 1, 0))

        halo = 0

    rkv_ext = _bdot(hext, wrkv_ref)
    rkv = rkv_ext[halo:halo + tb]
    rkv = rkv + murkv_ref[...] * (shift(rkv_ext) - rkv)
    r = rkv[:, 0:R]
    k = rkv[:, R:2 * R]
    v = rkv[:, 2 * R:3 * R]

    dh = shift(hext) - h
    mu = muwag_ref[...]
    xw = h + mu[0:1] * dh
    xa = h + mu[1:2] * dh
    xg = h + mu[2:3] * dh

    z = w0_ref[...] + _bdot(jnp.tanh(_bdot(xw, w1_ref)), w2_ref)
    lw = -jnp.exp(-_softplus(-z) - 0.5)
    a = _sigmoid(a0_ref[...] + _bdot(_bdot(xa, a1_ref), a2_ref))
    gate = _bdot(_sigmoid(_bdot(xg, g1_ref)), g2_ref)

    kap = k * kk_ref[...]
    kap = kap * lax.rsqrt(jnp.maximum(_seg_sum(kap * kap, ind_ref), 1e-12))
    ka = ka_ref[...]
    a_f = a[:, 0:R]
    a_b = a[:, R:2 * R]
    kd_f = k * (1.0 + (a_f - 1.0) * ka)
    kd_b = k * (1.0 + (a_b - 1.0) * ka)
    bonus = _seg_sum(r * (kd_f + kd_b) * rk_ref[...], ind_ref) * v

    r_o[0] = r
    v_o[0] = v
    kap_o[0] = kap
    kdf_o[0] = kd_f
    kdb_o[0] = kd_b
    bf_o[0] = a_f * kap
    bb_o[0] = a_b * kap
    lwf_o[0] = lw[:, 0:R]
    lwb_o[0] = lw[:, R:2 * R]
    gate_o[0] = gate
    bonus_o[0] = bonus

    rest = _bdot(h, wrest_ref)
    Q = GLA_QK_WIDTH
    q_o[0] = rest[:, 0:Q] * (GLA_KEY_DIM ** -0.5)
    kg_o[0] = rest[:, Q:2 * Q]
    vg_o[0] = rest[:, 2 * Q:2 * Q + GLA_V_WIDTH]
    gg = rest[:, 2 * Q + GLA_V_WIDTH:]
    gsil_o[0] = gg * _sigmoid(gg)
    logits = _bdot(_bdot(h, gk1_ref), gk2_ref) + gkb_ref[...]
    la = -_softplus(-logits) * (1.0 / GLA_GATE_NORMALIZER)
    laf_o[0] = la[:, 0:Q]
    lab_o[0] = la[:, Q:2 * Q]


def _const_spec(shape):
    nd = len(shape)
    return pl.BlockSpec(shape, lambda b, i: (0,) * nd)


def _pre(x, mod3, mod_row, weights, grid_shift):
    B, L, D = x.shape
    tb = PRE_BLOCK
    nblk = L // tb
    if not grid_shift:
        assert nblk == 1
    x_spec = pl.BlockSpec((1, tb, D), lambda b, i: (b, i, 0))
    in_specs = [x_spec]
    args = [x]
    if grid_shift:
        per = tb // GRID_W
        nrow = L // GRID_W
        in_specs += [
            pl.BlockSpec((1, GRID_W, D), lambda b, i: (b, jnp.maximum(i * per - 1, 0), 0)),
            pl.BlockSpec((1, GRID_W, D), lambda b, i: (b, jnp.minimum((i + 1) * per, nrow - 1), 0)),
        ]
        args += [x, x]
    in_specs.append(pl.BlockSpec((1, 1, N_MOD * D), lambda b, i: (mod_row(b), 0, 0)))
    args.append(mod3)
    for w in weights:
        in_specs.append(_const_spec(w.shape))
        args.append(w)
    widths = [512] * 11 + [256, 256, 256, 256, 512, 512]
    out_specs = [pl.BlockSpec((1, tb, w), lambda b, i: (b, i, 0)) for w in widths]
    out_shape = [jax.ShapeDtypeStruct((B, L, w), F32) for w in widths]
    return pl.pallas_call(
        functools.partial(_pre_kernel, grid_shift=grid_shift, tb=tb),
        grid=(B, nblk),
        in_specs=in_specs,
        out_specs=out_specs,
        out_shape=out_shape,
        compiler_params=pltpu.CompilerParams(dimension_semantics=("parallel", "arbitrary"),
                                             vmem_limit_bytes=VMEM_LIMIT),
        name="pre_grid" if grid_shift else "pre_seq",
    )(*args)


def _interleave(tasks):
    live = list(tasks)
    rnd = 0
    while live:
        keep = []
        for first, gen in live:
            if rnd >= first:
                try:
                    next(gen)
                except StopIteration:
                    continue
            keep.append((first, gen))
        live = keep
        rnd += 1


def _inv_unit_triangular(lms, eye, blk16, blk32):
    mm = functools.partial(_mm, dims=_NN, mode=P_INV)
    l0 = [jnp.where(blk16, lm, 0.0) for lm in lms]
    l2 = [mm(a, a) for a in l0]
    yield
    l4 = [mm(a, a) for a in l2]
    yield
    l8 = [mm(a, a) for a in l4]
    yield
    t = [mm(eye - a, eye + b) for a, b in zip(l0, l2)]
    yield
    t = [mm(a, eye + b) for a, b in zip(t, l4)]
    yield
    t = [mm(a, eye + b) for a, b in zip(t, l8)]
    yield
    off1_mask = jnp.logical_and(blk32, jnp.logical_not(blk16))
    x = [mm(a, jnp.where(off1_mask, lm, 0.0)) for a, lm in zip(t, lms)]
    yield
    t = [a - mm(b, a) for a, b in zip(t, x)]
    yield
    x = [mm(a, jnp.where(blk32, 0.0, lm)) for a, lm in zip(t, lms)]
    yield
    t = [a - mm(b, a) for a, b in zip(t, x)]
    yield
    return t


def _rwkv_prepare(units, cst):
    C = RWKV_CHUNK
    m0, cm0, eye, blk16, blk32, _ = cst[2]

    def stack(x):
        return jnp.concatenate([jnp.where(m0, x, 0.0), jnp.where(m0, 0.0, x)], axis=0)

    def blockdiag(side):
        return jnp.concatenate([jnp.where(cm0, side, 0.0), jnp.where(cm0, 0.0, side)], axis=0)

    cums = [_cumsum_mm(cst[rev][0], lw) for (_, _, _, _, _, lw, rev) in units]
    yield
    prep = []
    for (r, k, v, kap, b, lw, rev), cum in zip(units, cums):
        cumx = cum - lw
        if rev:
            mid = cum[C // 2:C // 2 + 1]
            tot = cum[0:1]
        else:
            mid = cum[C // 2 - 1:C // 2]
            tot = cum[C - 1:C]
        e_mid = jnp.exp(mid)
        rt = r * jnp.exp(cum - mid)
        kt = kap * jnp.exp(cumx - mid)
        es = jnp.exp(mid - cum)
        e_end = jnp.exp(tot - mid)
        prep.append(dict(rt=rt, kt=kt, bh=b * es, kh=k * es, r0=rt * e_mid, k0=stack(kt * e_mid),
                         be=b * es * e_end, ke=k * es * e_end, vs=stack(v), v=v,
                         e_tot=jnp.exp(tot), strict2=cst[rev][1], incl2=cst[rev][2]))
    gs = [_mm(jnp.concatenate([p["kt"], p["rt"]], axis=0),
              jnp.concatenate([stack(p["bh"]), stack(p["kh"])], axis=0), _NT, P_SCORE) for p in prep]
    yield
    for p, g in zip(prep, gs):
        p["ab"] = blockdiag(jnp.where(p["strict2"], g[0:C, 0:2 * C], 0.0))
        p["ak"] = blockdiag(jnp.where(p["strict2"], g[0:C, 2 * C:4 * C], 0.0))
        p["rb"] = jnp.where(p["incl2"], g[C:2 * C, 0:2 * C], 0.0)
        p["rk"] = jnp.where(p["incl2"], g[C:2 * C, 2 * C:4 * C], 0.0)
    av = [_mm(jnp.concatenate([p["ak"], p["rk"]], axis=0), p["vs"], _NN, P_SOLVE) for p in prep]
    yield
    ts = yield from _inv_unit_triangular([p["ab"] for p in prep], eye, blk16, blk32)
    wu = [_mm(t, jnp.concatenate([p["k0"], a[0:2 * C]], axis=1), _NN, P_SOLVE) for t, p, a in zip(ts, prep, av)]
    yield
    return [dict(wr=jnp.concatenate([x[:, :LANES], p["r0"]], axis=0), u0=x[:, LANES:], yv=a[2 * C:3 * C],
                 rb=p["rb"], v=p["v"], bke=jnp.concatenate([p["be"], p["ke"]], axis=0), e_tot=p["e_tot"])
            for x, a, p in zip(wu, av, prep)]


def _rwkv_apply(prep, states, cst):
    C = RWKV_CHUNK
    blk64 = cst[2][5]
    ws = [_mm(p["wr"], s, _NT, P_STATE) for p, s in zip(prep, states)]
    yield
    us = [-p["u0"] - w[0:2 * C] for p, w in zip(prep, ws)]
    upd = [_mm(jnp.concatenate([u[0:C] + u[C:2 * C], p["v"]], axis=0), p["bke"], _TN, P_STATE)
           for u, p in zip(us, prep)]
    yield
    s_new = [s * p["e_tot"] + jnp.where(blk64, d, 0.0) for p, s, d in zip(prep, states, upd)]
    ys = [w[2 * C:3 * C] + _mm(p["rb"], u, _NN, P_OUT) + p["yv"] for p, w, u in zip(prep, ws, us)]
    yield
    return ys, s_new


def _rwkv_consts():
    C = RWKV_CHUNK
    row = lax.broadcasted_iota(jnp.int32, (C, C), 0)
    col = lax.broadcasted_iota(jnp.int32, (C, C), 1)
    row2 = lax.broadcasted_iota(jnp.int32, (C, 2 * C), 0)
    col2 = lax.broadcasted_iota(jnp.int32, (C, 2 * C), 1)
    cs = col2 % C
    lane = lax.broadcasted_iota(jnp.int32, (1, LANES), 1)
    m0 = lane < RWKV_HEAD_DIM
    cm0 = lax.broadcasted_iota(jnp.int32, (1, 2 * C), 1) < C
    rr = lax.broadcasted_iota(jnp.int32, (2 * C, 2 * C), 0)
    cc = lax.broadcasted_iota(jnp.int32, (2 * C, 2 * C), 1)
    eye = (rr == cc).astype(F32)
    blk16 = (rr // 16) == (cc // 16)
    blk32 = (rr // 32) == (cc // 32)
    blk64 = (rr // 64) == (cc // 64)
    fwd = ((col <= row).astype(BF16), cs < row2, cs <= row2)
    bwd = ((col >= row).astype(BF16), cs > row2, cs >= row2)
    return fwd, bwd, (m0, cm0, eye, blk16, blk32, blk64)


def _gla_intra(items, cst):
    G = GLA_CHUNK
    ind_v, sel, blk = cst[2]
    srow = lax.broadcasted_iota(jnp.int32, (G, 1), 0)
    cums = [_cumsum_mm(cst[rev], la) for (_, _, _, la, rev) in items]
    pmats = []
    for (q, k, v, la, rev), cum in zip(items, cums):
        ps = []
        for t in range(G):
            msk = (srow >= t) if rev else (srow <= t)
            e = jnp.exp(jnp.where(msk, cum[t:t + 1] - cum, 0.0))
            ps.append(jnp.where(msk, e * (q[t:t + 1] * k), 0.0))
        pmats.append(jnp.concatenate(ps, axis=0))
    atts = [_mm(pm, ind_v, _NN, P_GATT) for pm in pmats]
    o_intra = [_mm(sel, att * jnp.concatenate([it[2]] * G, axis=0), _NN, P_GATT)
               for att, it in zip(atts, items)]
    out = []
    for (q, k, v, la, rev), cum, oi in zip(items, cums, o_intra):
        tot = cum[0:1] if rev else cum[G - 1:G]
        upd = _mm(v, k * jnp.exp(tot - cum), _TN, P_GSTATE)
        out.append((oi, q * jnp.exp(cum), jnp.where(blk, upd, 0.0), jnp.exp(tot)))
    yield
    return out


def _gla_consts():
    G = GLA_CHUNK
    row = lax.broadcasted_iota(jnp.int32, (G, G), 0)
    col = lax.broadcasted_iota(jnp.int32, (G, G), 1)
    kc = lax.broadcasted_iota(jnp.int32, (LANES, 2 * LANES), 0)
    vc = lax.broadcasted_iota(jnp.int32, (LANES, 2 * LANES), 1)
    ind_v = ((kc // GLA_KEY_DIM) == (vc // GLA_VAL_DIM)).astype(F32)
    st = lax.broadcasted_iota(jnp.int32, (G, G * G), 0)
    sj = lax.broadcasted_iota(jnp.int32, (G, G * G), 1)
    sel = ((sj // G) == st).astype(F32)
    br = lax.broadcasted_iota(jnp.int32, (2 * LANES, LANES), 0)
    bc = lax.broadcasted_iota(jnp.int32, (2 * LANES, LANES), 1)
    blk = (br // GLA_VAL_DIM) == (bc // GLA_KEY_DIM)
    return (col <= row).astype(BF16), (col >= row).astype(BF16), (ind_v, sel, blk)


def _gla_factored(items, cst):
    F = GLA_WIDE_CHUNK
    m0, vm0, blk = cst[2]
    cums = [_cumsum_mm(cst[rev][0], la) for (_, _, _, la, rev) in items]
    yield
    scores = []
    for (q, k, v, la, rev), cum in zip(items, cums):
        mid = cum[F // 2:F // 2 + 1] if rev else cum[F // 2 - 1:F // 2]
        kt = k * jnp.exp(mid - cum)
        kstk = jnp.concatenate([jnp.where(m0, kt, 0.0), jnp.where(m0, 0.0, kt)], axis=0)
        scores.append(_mm(q * jnp.exp(cum - mid), kstk, _NT, P_GATT))
    yield
    o_intra = []
    for (q, k, v, la, rev), sc in zip(items, scores):
        vstk = jnp.concatenate([jnp.where(vm0, v, 0.0), jnp.where(vm0, 0.0, v)], axis=0)
        o_intra.append(_mm(jnp.where(cst[rev][1], sc, 0.0), vstk, _NN, P_GATT))
    yield
    out = []
    for (q, k, v, la, rev), cum, oi in zip(items, cums, o_intra):
        tot = cum[0:1] if rev else cum[F - 1:F]
        upd = _mm(v, k * jnp.exp(tot - cum), _TN, P_GSTATE)
        out.append((oi, q * jnp.exp(cum), jnp.where(blk, upd, 0.0), jnp.exp(tot)))
    yield
    return out


def _gla_wide_consts():
    F = GLA_WIDE_CHUNK
    row = lax.broadcasted_iota(jnp.int32, (F, F), 0)
    col = lax.broadcasted_iota(jnp.int32, (F, F), 1)
    row2 = lax.broadcasted_iota(jnp.int32, (F, 2 * F), 0)
    cs = lax.broadcasted_iota(jnp.int32, (F, 2 * F), 1) % F
    m0 = lax.broadcasted_iota(jnp.int32, (1, LANES), 1) < GLA_KEY_DIM
    vm0 = lax.broadcasted_iota(jnp.int32, (1, 2 * LANES), 1) < GLA_VAL_DIM
    br = lax.broadcasted_iota(jnp.int32, (2 * LANES, LANES), 0)
    bc = lax.broadcasted_iota(jnp.int32, (2 * LANES, LANES), 1)
    blk = (br // GLA_VAL_DIM) == (bc // GLA_KEY_DIM)
    return (((col <= row).astype(BF16), cs <= row2), ((col >= row).astype(BF16), cs >= row2), (m0, vm0, blk))


def _scan_kernel(rf_ref, rvf_ref, kapf_ref, kdf_ref, bf_ref, lwf_ref,
                 rb_ref, rvb_ref, kapb_ref, kdb_ref, bb_ref, lwb_ref,
                 qf_ref, kf_ref, gvf_ref, laf_ref, qb_ref, kb_ref, gvb_ref, lab_ref,
                 srf0_ref, srb0_ref, sgf0_ref, sgb0_ref,
                 yf_ref, yb_ref, of_ref, ob_ref, srf_ref, srb_ref, sgf_ref, sgb_ref):
    i = pl.program_id(1)

    @pl.when(i == 0)
    def _():
        srf_ref[...] = srf0_ref[...]
        srb_ref[...] = srb0_ref[...]
        sgf_ref[...] = sgf0_ref[...]
        sgb_ref[...] = sgb0_ref[...]

    T = SCAN_BLOCK
    C = RWKV_CHUNK
    nr = T // C
    rcst = _rwkv_consts()
    g_dirs = ((False, qf_ref, kf_ref, gvf_ref, laf_ref, sgf_ref, of_ref),
              (True, qb_ref, kb_ref, gvb_ref, lab_ref, sgb_ref, ob_ref))
    la_tot = jnp.minimum(jnp.sum(laf_ref[0], axis=0, keepdims=True), jnp.sum(lab_ref[0], axis=0, keepdims=True))
    gla_split_ok = jnp.min(la_tot) >= -GLA_SAFE_LOG

    r_dirs = ((False, rf_ref, kdf_ref, rvf_ref, kapf_ref, bf_ref, lwf_ref, srf_ref, yf_ref),
              (True, rb_ref, kdb_ref, rvb_ref, kapb_ref, bb_ref, lwb_ref, srb_ref, yb_ref))
    r_units = [[] for _ in range(nr)]
    r_dests = [[] for _ in range(nr)]
    for rev, r_ref, k_ref, v_ref, kap_ref, b_ref, lw_ref, s_ref, y_ref in r_dirs:
        for p in range(RWKV_WIDTH // LANES):
            sl = slice(p * LANES, (p + 1) * LANES)
            for j in range(nr):
                c = nr - 1 - j if rev else j
                rows = slice(c * C, (c + 1) * C)
                r_units[j].append((r_ref[0, rows, sl], k_ref[0, rows, sl], v_ref[0, rows, sl],
                                   kap_ref[0, rows, sl], b_ref[0, rows, sl], lw_ref[0, rows, sl], rev))
                r_dests[j].append((y_ref, s_ref, p, rows, sl))
    r_states = [[s_ref[0, p] for (_, s_ref, p, _, _) in r_dests[0]]] + [None] * nr

    def rwkv_task(j):
        prep = yield from _rwkv_prepare(r_units[j], rcst)
        while r_states[j] is None:
            yield
        ys, r_states[j + 1] = yield from _rwkv_apply(prep, r_states[j], rcst)
        for (y_ref, _, _, rows, sl), y in zip(r_dests[j], ys):
            y_ref[0, rows, sl] = y

    npair = GLA_QK_WIDTH // LANES
    g_init = [[s_ref[0, p] for p in range(npair)] for (_, _, _, _, _, s_ref, _) in g_dirs]

    def gla_task(chunk, intra, cst):
        ng = T // chunk
        items = [[] for _ in range(ng)]
        dests = [[] for _ in range(ng)]
        for d, (rev, q_ref, k_ref, v_ref, la_ref, s_ref, o_ref) in enumerate(g_dirs):
            for p in range(npair):
                ks = slice(p * LANES, (p + 1) * LANES)
                vs = slice(p * 2 * LANES, (p + 1) * 2 * LANES)
                for j in range(ng):
                    c = ng - 1 - j if rev else j
                    rows = slice(c * chunk, (c + 1) * chunk)
                    items[j].append((q_ref[0, rows, ks], k_ref[0, rows, ks], v_ref[0, rows, vs],
                                     la_ref[0, rows, ks], rev))
                    dests[j].append((o_ref, s_ref, d, p, rows, vs))
        parts = yield from intra([it for its in items for it in its], cst)
        nu = len(items[0])
        sts = [g_init[d][p] for (_, _, d, p, _, _) in dests[0]]
        for j in range(ng):
            new = []
            for (o_ref, _, _, _, rows, vs), (o_intra, q_in, upd, e_tot), st in zip(
                    dests[j], parts[j * nu:(j + 1) * nu], sts):
                o_ref[0, rows, vs] = o_intra + _mm(q_in, st, _NT, P_GSTATE)
                new.append(st * e_tot + upd)
            sts = new
            yield
        for (_, s_ref, _, p, _, _), st in zip(dests[0], sts):
            s_ref[0, p] = st

    _interleave([(RWKV_TASK_LAG * j, rwkv_task(j)) for j in range(nr)]
                + [(GLA_TASK_LAG, gla_task(GLA_WIDE_CHUNK, _gla_factored, _gla_wide_consts()))])
    for (_, s_ref, p, _, _), s in zip(r_dests[0], r_states[nr]):
        s_ref[0, p] = s

    @pl.when(jnp.logical_not(gla_split_ok))
    def _():
        _interleave([(0, gla_task(GLA_CHUNK, _gla_intra, _gla_consts()))])


def _scan(r, v, kap, kd_f, kd_b, b_f, b_b, lw_f, lw_b, q, kg, vg, la_f, la_b, s_rf, s_rb, s_gf, s_gb):
    B, L, R = r.shape
    Q = q.shape[-1]
    V = vg.shape[-1]
    T = SCAN_BLOCK
    n = L // T

    def fwd(w):
        return pl.BlockSpec((1, T, w), lambda b, i: (b, i, 0))

    def bwd(w):
        return pl.BlockSpec((1, T, w), lambda b, i: (b, n - 1 - i, 0))

    rst = pl.BlockSpec((1, R // LANES, LANES, LANES), lambda b, i: (b, 0, 0, 0))
    gst = pl.BlockSpec((1, Q // LANES, 2 * LANES, LANES), lambda b, i: (b, 0, 0, 0))
    return pl.pallas_call(
        _scan_kernel,
        grid=(B, n),
        in_specs=[fwd(R)] * 6 + [bwd(R)] * 6 + [fwd(Q), fwd(Q), fwd(V), fwd(Q), bwd(Q), bwd(Q), bwd(V), bwd(Q)]
        + [rst, rst, gst, gst],
        out_specs=[fwd(R), bwd(R), fwd(V), bwd(V), rst, rst, gst, gst],
        out_shape=[jax.ShapeDtypeStruct((B, L, R), F32)] * 2 + [jax.ShapeDtypeStruct((B, L, V), F32)] * 2
        + [jax.ShapeDtypeStruct((B, R // LANES, LANES, LANES), F32)] * 2
        + [jax.ShapeDtypeStruct((B, Q // LANES, 2 * LANES, LANES), F32)] * 2,
        compiler_params=pltpu.CompilerParams(dimension_semantics=("parallel", "arbitrary"),
                                             vmem_limit_bytes=VMEM_LIMIT),
        name="scan",
    )(r, v, kap, kd_f, b_f, lw_f, r, v, kap, kd_b, b_b, lw_b, q, kg, vg, la_f, q, kg, vg, la_b,
      s_rf, s_rb, s_gf, s_gb)


def _post_kernel(x_ref, mod_ref, yf_ref, yb_ref, of_ref, ob_ref, bonus_ref, gate_ref, gsil_ref,
                 lng_ref, lnb_ref, gng_ref, n2g_ref, fng_ref, ind_ref, wout_ref, w1_ref, w2_ref, o_ref):
    D = D_MODEL
    m = mod_ref[0]
    gt1 = m[:, 2 * D:3 * D]
    sh2 = m[:, 3 * D:4 * D]
    sc2 = m[:, 4 * D:5 * D]
    gt2 = m[:, 5 * D:6 * D]
    inv_n = 1.0 / RWKV_HEAD_DIM
    y = yf_ref[0] + yb_ref[0]
    mu = _seg_sum(y, ind_ref) * inv_n
    yc = y - mu
    var = _seg_sum(yc * yc, ind_ref) * inv_n
    yn = yc * lax.rsqrt(var + LNX_EPS) * lng_ref[...] + lnb_ref[...]
    rw = (yn + bonus_ref[0]) * gate_ref[0]
    o = of_ref[0] + ob_ref[0]
    gsil = gsil_ref[0]
    gng = gng_ref[...]
    parts = [rw]
    for hh in range(GLA_HEADS):
        sl = slice(hh * GLA_VAL_DIM, (hh + 1) * GLA_VAL_DIM)
        oh = o[:, sl]
        oh = oh * lax.rsqrt(jnp.mean(oh * oh, axis=-1, keepdims=True) + GLA_NORM_EPS)
        parts.append(oh * gng * gsil[:, sl])
    mix = jnp.concatenate(parts, axis=-1)
    x1 = x_ref[0] + gt1 * _bdot(mix, wout_ref)
    h2 = _rms(x1) * n2g_ref[...] * (1.0 + sc2) + sh2
    f = jnp.maximum(_bdot(h2, w1_ref), 0.0)
    x2 = x1 + gt2 * _bdot(f * f, w2_ref)
    o_ref[0] = _rms(x2) * fng_ref[...]


def _post(x, mod3, mod_row, y_f, y_b, o_f, o_b, bonus, gate, gsil, vecs, mats):
    B, L, D = x.shape
    tb = min(POST_BLOCK, L)
    nblk = L // tb

    def tok(w):
        return pl.BlockSpec((1, tb, w), lambda b, i: (b, i, 0))

    in_specs = [tok(D), pl.BlockSpec((1, 1, N_MOD * D), lambda b, i: (mod_row(b), 0, 0))]
    in_specs += [tok(512)] * 7
    in_specs += [_const_spec(w.shape) for w in vecs]
    in_specs += [pl.BlockSpec(w.shape, lambda b, i: (0, 0), pipeline_mode=pl.Buffered(1)) for w in mats]
    return pl.pallas_call(
        _post_kernel,
        grid=(B, nblk),
        in_specs=in_specs,
        out_specs=tok(D),
        out_shape=jax.ShapeDtypeStruct((B, L, D), F32),
        compiler_params=pltpu.CompilerParams(dimension_semantics=("parallel", "arbitrary"),
                                             vmem_limit_bytes=VMEM_LIMIT),
        name="post",
    )(x, mod3, y_f, y_b, o_f, o_b, bonus, gate, gsil, *vecs, *mats)


def _rwkv_state_to_pairs(s):
    B = s.shape[0]
    s = s.reshape(B, 4, 2, 64, 64)
    z = jnp.zeros_like(s[:, :, 0])
    top = jnp.concatenate([s[:, :, 0], z], axis=-1)
    bot = jnp.concatenate([z, s[:, :, 1]], axis=-1)
    return jnp.concatenate([top, bot], axis=-2)


def _rwkv_pairs_to_state(sp):
    B = sp.shape[0]
    a = sp[:, :, 0:64, 0:64]
    b = sp[:, :, 64:128, 64:128]
    return jnp.stack([a, b], axis=2).reshape(B, 8, 64, 64)


def _gla_state_to_pairs(s):
    B = s.shape[0]
    st = jnp.swapaxes(s, -1, -2).reshape(B, 2, 2, 128, 64)
    z = jnp.zeros_like(st[:, :, 0])
    top = jnp.concatenate([st[:, :, 0], z], axis=-1)
    bot = jnp.concatenate([z, st[:, :, 1]], axis=-1)
    return jnp.concatenate([top, bot], axis=-2)


def _gla_pairs_to_state(sp):
    B = sp.shape[0]
    a = sp[:, :, 0:128, 0:64]
    b = sp[:, :, 128:256, 64:128]
    st = jnp.stack([a, b], axis=2).reshape(B, 4, 128, 64)
    return jnp.swapaxes(st, -1, -2)


def _blockdiag2(a, b):
    za = jnp.zeros((a.shape[0], b.shape[1]), a.dtype)
    zb = jnp.zeros((b.shape[0], a.shape[1]), a.dtype)
    return jnp.concatenate([jnp.concatenate([a, za], axis=1), jnp.concatenate([zb, b], axis=1)], axis=0)


def kernel(x_prompt, x_sample, c, state_rwkv_fwd, state_rwkv_bwd, state_gla_fwd, state_gla_bwd, c_ctx, ada_w, ada_b, norm1_g, norm2_g, w_in, rwkv_mu_rkv, rwkv_mu_wag, rwkv_w0, rwkv_w1, rwkv_w2, rwkv_a0, rwkv_a1, rwkv_a2, rwkv_g1, rwkv_g2, rwkv_k_k, rwkv_k_a, rwkv_r_k, rwkv_lnx_g, rwkv_lnx_b, gla_gk1, gla_gk2, gla_gk_b, gla_norm_g, w_out, mlp_w1, mlp_w2, final_norm_g):
    D = D_MODEL
    R = RWKV_WIDTH
    nb = x_prompt.shape[0]
    nd = x_sample.shape[0]
    assert ada_w.shape[0] == 1, "single-layer step"
    layer = 0

    craw = jnp.concatenate([c_ctx[None, :], c, jnp.zeros((8 - 1 - nd, D), F32)], axis=0)
    mod = _modulation(craw, ada_w[layer], ada_b[layer][None, :])
    mod3 = mod.reshape(8, 1, N_MOD * D)

    bf = lambda t: t.astype(BF16)
    row = lambda t: t.reshape(1, -1).astype(F32)
    w_in_l = w_in[layer]
    ind = (jnp.arange(R)[:, None] // RWKV_HEAD_DIM == jnp.arange(R)[None, :] // RWKV_HEAD_DIM).astype(BF16)
    pre_w = [
        row(norm1_g[layer]),
        bf(w_in_l[:, :3 * R]),
        bf(w_in_l[:, 3 * R:]),
        bf(jnp.concatenate([gla_gk1[layer, 0], gla_gk1[layer, 1]], axis=1)),
        bf(jnp.concatenate([rwkv_w1[layer, 0], rwkv_w1[layer, 1]], axis=1)),
        bf(jnp.concatenate([rwkv_a1[layer, 0], rwkv_a1[layer, 1]], axis=1)),
        bf(rwkv_g1[layer]),
        bf(_blockdiag2(rwkv_w2[layer, 0], rwkv_w2[layer, 1])),
        bf(_blockdiag2(rwkv_a2[layer, 0], rwkv_a2[layer, 1])),
        bf(rwkv_g2[layer]),
        bf(_blockdiag2(gla_gk2[layer, 0], gla_gk2[layer, 1])),
        row(rwkv_mu_rkv[layer]),
        rwkv_mu_wag[layer],
        row(rwkv_w0[layer]),
        row(rwkv_a0[layer]),
        row(gla_gk_b[layer]),
        row(rwkv_k_k[layer]),
        row(rwkv_k_a[layer]),
        row(rwkv_r_k[layer]),
        ind,
    ]
    post_vecs = [row(rwkv_lnx_g[layer]), row(rwkv_lnx_b[layer]), row(gla_norm_g[layer]),
                 row(norm2_g[layer]), row(final_norm_g)]
    post_mats = [ind, bf(w_out[layer]), bf(mlp_w1[layer]), bf(mlp_w2[layer])]

    def run_group(x, mod_row, grid_shift, s_rf, s_rb, s_gf, s_gb):
        (r, v, kap, kd_f, kd_b, b_f, b_b, lw_f, lw_b, gate, bonus,
         q, kg, la_f, la_b, vg, gsil) = _pre(x, mod3, mod_row, pre_w, grid_shift)
        y_f, y_b, o_f, o_b, n_rf, n_rb, n_gf, n_gb = _scan(
            r, v, kap, kd_f, kd_b, b_f, b_b, lw_f, lw_b, q, kg, vg, la_f, la_b,
            _rwkv_state_to_pairs(s_rf), _rwkv_state_to_pairs(s_rb),
            _gla_state_to_pairs(s_gf), _gla_state_to_pairs(s_gb))
        y = _post(x, mod3, mod_row, y_f, y_b, o_f, o_b, bonus, gate, gsil, post_vecs, post_mats)
        return y, (n_rf, n_rb, n_gf, n_gb)

    zr = jnp.zeros((nb, RWKV_HEADS, RWKV_HEAD_DIM, RWKV_HEAD_DIM), F32)
    zg = jnp.zeros((nb, GLA_HEADS, GLA_KEY_DIM, GLA_VAL_DIM), F32)
    y_prompt, (n_rf, n_rb, n_gf, n_gb) = run_group(x_prompt, lambda b: 0, False, zr, zr, zg, zg)
    y_sample, _ = run_group(x_sample, lambda b: b + 1, True,
                            state_rwkv_fwd[:, layer], state_rwkv_bwd[:, layer],
                            state_gla_fwd[:, layer], state_gla_bwd[:, layer])
    return (y_prompt, y_sample,
            _rwkv_pairs_to_state(n_rf)[:, None], _rwkv_pairs_to_state(n_rb)[:, None],
            _gla_pairs_to_state(n_gf)[:, None], _gla_pairs_to_state(n_gb)[:, None])
```

```python
import functools

import jax
import jax.numpy as jnp
from jax import lax
from jax.experimental import pallas as pl
from jax.experimental.pallas import tpu as pltpu

F32 = jnp.float32
BF16 = jnp.bfloat16
HI = lax.Precision.HIGHEST

D_MODEL = 1024
GRID_W = 64
RWKV_WIDTH = 512
RWKV_HEAD_DIM = 64
RWKV_HEADS = 8
GLA_HEADS = 4
GLA_KEY_DIM = 64
GLA_VAL_DIM = 128
GLA_QK_WIDTH = 256
GLA_V_WIDTH = 512
GLA_GATE_NORMALIZER = 16.0
N_MOD = 6
RMS_EPS = 1e-6
LNX_EPS = 64e-5
GLA_NORM_EPS = 1e-5

LANES = 128
RWKV_CHUNK = 64
GLA_CHUNK = 16
GLA_WIDE_CHUNK = 64
GLA_SAFE_LOG = 60.0
SCAN_BLOCK = 128
PRE_BLOCK = 256
POST_BLOCK = 512
VMEM_LIMIT = 56 * 1024 * 1024
RWKV_TASK_LAG = 3
GLA_TASK_LAG = 12

P_SCORE = "b1"
P_INV = "b1"
P_SOLVE = "b1"
P_STATE = "b1"
P_OUT = "b1"
P_GSTATE = "b1"
P_GATT = "b1"


def _dot(a, b, prec=None):
    return lax.dot_general(a, b, (((1,), (0,)), ((), ())), precision=prec, preferred_element_type=F32)


def _dot_nt(a, b, prec=None):
    return lax.dot_general(a, b, (((1,), (1,)), ((), ())), precision=prec, preferred_element_type=F32)


def _dot_tn(a, b, prec=None):
    return lax.dot_general(a, b, (((0,), (0,)), ((), ())), precision=prec, preferred_element_type=F32)


_NN = (((1,), (0,)), ((), ()))
_NT = (((1,), (1,)), ((), ()))
_TN = (((0,), (0,)), ((), ()))


def _split2(x):
    hi = x.astype(BF16)
    return hi, (x - hi.astype(F32)).astype(BF16)


def _mm(a, b, dims, mode):
    dg = functools.partial(lax.dot_general, dimension_numbers=dims, preferred_element_type=F32)
    if mode == "hi":
        return dg(a, b, precision=HI)
    if mode == "b1":
        return dg(a.astype(BF16), b.astype(BF16))
    a1, a2 = _split2(a)
    b1, b2 = _split2(b)
    return dg(a1, b1) + dg(a1, b2) + dg(a2, b1)


def _cumsum_mm(tri, x):
    hi = x.astype(BF16)
    r1 = x - hi.astype(F32)
    mid = r1.astype(BF16)
    lo = (r1 - mid.astype(F32)).astype(BF16)
    return _dot(tri, hi) + _dot(tri, mid) + _dot(tri, lo)


def _bdot(a, w_ref):
    return _dot(a.astype(BF16), w_ref[...])


def _sigmoid(x):
    return 1.0 / (1.0 + jnp.exp(-x))


def _softplus(x):
    return jnp.maximum(x, 0.0) + jnp.log(1.0 + jnp.exp(-jnp.abs(x)))


def _seg_sum(x, ind_ref):
    ind = ind_ref[...]
    hi = x.astype(BF16)
    r1 = x - hi.astype(F32)
    mid = r1.astype(BF16)
    lo = (r1 - mid.astype(F32)).astype(BF16)
    return _dot(hi, ind) + _dot(mid, ind) + _dot(lo, ind)


def _rms(x):
    return x * lax.rsqrt(jnp.mean(x * x, axis=-1, keepdims=True) + RMS_EPS)


def _mod_kernel(c_ref, w_ref, b_ref, o_ref):
    c = c_ref[...]
    cond = c * _sigmoid(c)
    o_ref[...] = _dot(cond.astype(BF16), w_ref[...].astype(BF16)) + b_ref[...]


def _modulation(craw, ada_w, ada_b):
    n = ada_w.shape[1]
    bn = 1536
    return pl.pallas_call(
        _mod_kernel,
        grid=(n // bn,),
        in_specs=[
            pl.BlockSpec((8, D_MODEL), lambda j: (0, 0)),
            pl.BlockSpec((D_MODEL, bn), lambda j: (0, j)),
            pl.BlockSpec((1, bn), lambda j: (0, j)),
        ],
        out_specs=pl.BlockSpec((8, bn), lambda j: (0, j)),
        out_shape=jax.ShapeDtypeStruct((8, n), F32),
        compiler_params=pltpu.CompilerParams(dimension_semantics=("arbitrary",), vmem_limit_bytes=VMEM_LIMIT),
        name="modulation",
    )(craw, ada_w, ada_b)


def _pre_kernel(*refs, grid_shift, tb):
    if grid_shift:
        x_ref, xp_ref, xn_ref = refs[:3]
        refs = refs[3:]
    else:
        x_ref = refs[0]
        refs = refs[1:]
    (mod_ref, n1g_ref, wrkv_ref, wrest_ref, gk1_ref, w1_ref, a1_ref, g1_ref, w2_ref, a2_ref, g2_ref,
     gk2_ref, murkv_ref, muwag_ref, w0_ref, a0_ref, gkb_ref, kk_ref, ka_ref, rk_ref, ind_ref) = refs[:21]
    ra_o, rf_o, rb_o, ga_o, gl_o, pg_o = refs[21:]
    D = D_MODEL
    R = RWKV_WIDTH
    m = mod_ref[0]
    sh1 = m[:, 0:D]
    sc1 = m[:, D:2 * D]
    n1g = n1g_ref[...]

    def normmod(xx):
        return _rms(xx) * n1g * (1.0 + sc1) + sh1

    h = normmod(x_ref[0])
    row = lax.broadcasted_iota(jnp.int32, (tb, 1), 0)
    if grid_shift:
        i = pl.program_id(1)
        n = pl.num_programs(1)
        hp = normmod(xp_ref[0]) * (i > 0).astype(F32)
        hn = normmod(xn_ref[0]) * (i < n - 1).astype(F32)
        hext = jnp.concatenate([hp, h, hn], axis=0)
        col = row % GRID_W
        m_l = (col != 0).astype(F32)
        m_r = (col != GRID_W - 1).astype(F32)
        ne = tb + 2 * GRID_W

        def shift(ext):
            up = ext[0:tb]
            down = ext[2 * GRID_W:2 * GRID_W + tb]
            left = pltpu.roll(ext, 1, 0)[GRID_W:GRID_W + tb]
            right = pltpu.roll(ext, ne - 1, 0)[GRID_W:GRID_W + tb]
            return 0.25 * (up + down + m_l * left + m_r * right)

        halo = GRID_W
    else:
        hext = h
        m_l = (row != 0).astype(F32)
        m_r = (row != tb - 1).astype(F32)

        def shift(ext):
            return 0.5 * (m_l * pltpu.roll(ext, 1, 0) + m_r * pltpu.roll(ext, tb - 1, 0))

        halo = 0

    rkv_ext = _bdot(hext, wrkv_ref)
    rkv = rkv_ext[halo:halo + tb]
    rkv = rkv + murkv_ref[...] * (shift(rkv_ext) - rkv)
    r = rkv[:, 0:R]
    k = rkv[:, R:2 * R]
    v = rkv[:, 2 * R:3 * R]

    dh = shift(hext) - h
    mu = muwag_ref[...]
    xw = h + mu[0:1] * dh
    xa = h + mu[1:2] * dh
    xg = h + mu[2:3] * dh

    z = w0_ref[...] + _bdot(jnp.tanh(_bdot(xw, w1_ref)), w2_ref)
    lw = -jnp.exp(-_softplus(-z) - 0.5)
    a = _sigmoid(a0_ref[...] + _bdot(_bdot(xa, a1_ref), a2_ref))
    gate = _bdot(_sigmoid(_bdot(xg, g1_ref)), g2_ref)

    kap = k * kk_ref[...]
    kap = kap * lax.rsqrt(jnp.maximum(_seg_sum(kap * kap, ind_ref), 1e-12))
    ka = ka_ref[...]
    a_f = a[:, 0:R]
    a_b = a[:, R:2 * R]
    kd_f = k * (1.0 + (a_f - 1.0) * ka)
    kd_b = k * (1.0 + (a_b - 1.0) * ka)
    bonus = _seg_sum(r * (kd_f + kd_b) * rk_ref[...], ind_ref) * v

    ra_o[0, :, 0:R] = r
    ra_o[0, :, R:2 * R] = v
    ra_o[0, :, 2 * R:3 * R] = kap
    rf_o[0, :, 0:R] = kd_f
    rf_o[0, :, R:2 * R] = a_f * kap
    rf_o[0, :, 2 * R:3 * R] = lw[:, 0:R]
    rb_o[0, :, 0:R] = kd_b
    rb_o[0, :, R:2 * R] = a_b * kap
    rb_o[0, :, 2 * R:3 * R] = lw[:, R:2 * R]
    pg_o[0, :, 0:R] = gate
    pg_o[0, :, R:2 * R] = bonus

    rest = _bdot(h, wrest_ref)
    Q = GLA_QK_WIDTH
    ga_o[0, :, 0:Q] = rest[:, 0:Q] * (GLA_KEY_DIM ** -0.5)
    ga_o[0, :, Q:] = rest[:, Q:2 * Q + GLA_V_WIDTH]
    gg = rest[:, 2 * Q + GLA_V_WIDTH:]
    pg_o[0, :, 2 * R:] = gg * _sigmoid(gg)
    logits = _bdot(_bdot(h, gk1_ref), gk2_ref) + gkb_ref[...]
    gl_o[0] = -_softplus(-logits) * (1.0 / GLA_GATE_NORMALIZER)


def _const_spec(shape):
    nd = len(shape)
    return pl.BlockSpec(shape, lambda b, i: (0,) * nd)


def _pre(x, mod3, mod_row, weights, grid_shift):
    B, L, D = x.shape
    tb = PRE_BLOCK
    nblk = L // tb
    if not grid_shift:
        assert nblk == 1
    x_spec = pl.BlockSpec((1, tb, D), lambda b, i: (b, i, 0))
    in_specs = [x_spec]
    args = [x]
    if grid_shift:
        per = tb // GRID_W
        nrow = L // GRID_W
        in_specs += [
            pl.BlockSpec((1, GRID_W, D), lambda b, i: (b, jnp.maximum(i * per - 1, 0), 0)),
            pl.BlockSpec((1, GRID_W, D), lambda b, i: (b, jnp.minimum((i + 1) * per, nrow - 1), 0)),
        ]
        args += [x, x]
    in_specs.append(pl.BlockSpec((1, 1, N_MOD * D), lambda b, i: (mod_row(b), 0, 0)))
    args.append(mod3)
    for w in weights:
        in_specs.append(_const_spec(w.shape))
        args.append(w)
    R = RWKV_WIDTH
    widths = [3 * R, 3 * R, 3 * R, 2 * GLA_QK_WIDTH + GLA_V_WIDTH, 2 * GLA_QK_WIDTH, 2 * R + GLA_V_WIDTH]
    out_specs = [pl.BlockSpec((1, tb, w), lambda b, i: (b, i, 0)) for w in widths]
    out_shape = [jax.ShapeDtypeStruct((B, L, w), F32) for w in widths]
    return pl.pallas_call(
        functools.partial(_pre_kernel, grid_shift=grid_shift, tb=tb),
        grid=(B, nblk),
        in_specs=in_specs,
        out_specs=out_specs,
        out_shape=out_shape,
        compiler_params=pltpu.CompilerParams(dimension_semantics=("parallel", "arbitrary"),
                                             vmem_limit_bytes=VMEM_LIMIT),
        name="pre_grid" if grid_shift else "pre_seq",
    )(*args)


def _interleave(tasks):
    live = list(tasks)
    rnd = 0
    while live:
        keep = []
        for first, gen in live:
            if rnd >= first:
                try:
                    next(gen)
                except StopIteration:
                    continue
            keep.append((first, gen))
        live = keep
        rnd += 1


def _inv_unit_triangular(lms, eye, blk16, blk32):
    mm = functools.partial(_mm, dims=_NN, mode=P_INV)
    l0 = [jnp.where(blk16, lm, 0.0) for lm in lms]
    l2 = [mm(a, a) for a in l0]
    yield
    l4 = [mm(a, a) for a in l2]
    yield
    l8 = [mm(a, a) for a in l4]
    yield
    t = [mm(eye - a, eye + b) for a, b in zip(l0, l2)]
    yield
    t = [mm(a, eye + b) for a, b in zip(t, l4)]
    yield
    t = [mm(a, eye + b) for a, b in zip(t, l8)]
    yield
    off1_mask = jnp.logical_and(blk32, jnp.logical_not(blk16))
    x = [mm(a, jnp.where(off1_mask, lm, 0.0)) for a, lm in zip(t, lms)]
    yield
    t = [a - mm(b, a) for a, b in zip(t, x)]
    yield
    x = [mm(a, jnp.where(blk32, 0.0, lm)) for a, lm in zip(t, lms)]
    yield
    t = [a - mm(b, a) for a, b in zip(t, x)]
    yield
    return t


def _rwkv_prepare(units, cst):
    C = RWKV_CHUNK
    m0, cm0, eye, blk16, blk32, _ = cst[2]

    def stack(x):
        return jnp.concatenate([jnp.where(m0, x, 0.0), jnp.where(m0, 0.0, x)], axis=0)

    def blockdiag(side):
        return jnp.concatenate([jnp.where(cm0, side, 0.0), jnp.where(cm0, 0.0, side)], axis=0)

    cums = [_cumsum_mm(cst[rev][0], lw) for (_, _, _, _, _, lw, rev) in units]
    yield
    prep = []
    for (r, k, v, kap, b, lw, rev), cum in zip(units, cums):
        cumx = cum - lw
        if rev:
            mid = cum[C // 2:C // 2 + 1]
            tot = cum[0:1]
        else:
            mid = cum[C // 2 - 1:C // 2]
            tot = cum[C - 1:C]
        e_mid = jnp.exp(mid)
        rt = r * jnp.exp(cum - mid)
        kt = kap * jnp.exp(cumx - mid)
        es = jnp.exp(mid - cum)
        e_end = jnp.exp(tot - mid)
        prep.append(dict(rt=rt, kt=kt, bh=b * es, kh=k * es, r0=rt * e_mid, k0=stack(kt * e_mid),
                         be=b * es * e_end, ke=k * es * e_end, vs=stack(v), v=v,
                         e_tot=jnp.exp(tot), strict2=cst[rev][1], incl2=cst[rev][2]))
    gs = [_mm(jnp.concatenate([p["kt"], p["rt"]], axis=0),
              jnp.concatenate([stack(p["bh"]), stack(p["kh"])], axis=0), _NT, P_SCORE) for p in prep]
    yield
    for p, g in zip(prep, gs):
        p["ab"] = blockdiag(jnp.where(p["strict2"], g[0:C, 0:2 * C], 0.0))
        p["ak"] = blockdiag(jnp.where(p["strict2"], g[0:C, 2 * C:4 * C], 0.0))
        p["rb"] = jnp.where(p["incl2"], g[C:2 * C, 0:2 * C], 0.0)
        p["rk"] = jnp.where(p["incl2"], g[C:2 * C, 2 * C:4 * C], 0.0)
    av = [_mm(jnp.concatenate([p["ak"], p["rk"]], axis=0), p["vs"], _NN, P_SOLVE) for p in prep]
    yield
    ts = yield from _inv_unit_triangular([p["ab"] for p in prep], eye, blk16, blk32)
    wu = [_mm(t, jnp.concatenate([p["k0"], a[0:2 * C]], axis=1), _NN, P_SOLVE) for t, p, a in zip(ts, prep, av)]
    yield
    return [dict(wr=jnp.concatenate([x[:, :LANES], p["r0"]], axis=0), u0=x[:, LANES:], yv=a[2 * C:3 * C],
                 rb=p["rb"], v=p["v"], bke=jnp.concatenate([p["be"], p["ke"]], axis=0), e_tot=p["e_tot"])
            for x, a, p in zip(wu, av, prep)]


def _rwkv_apply(prep, states, cst):
    C = RWKV_CHUNK
    blk64 = cst[2][5]
    ws = [_mm(p["wr"], s, _NT, P_STATE) for p, s in zip(prep, states)]
    yield
    us = [-p["u0"] - w[0:2 * C] for p, w in zip(prep, ws)]
    upd = [_mm(jnp.concatenate([u[0:C] + u[C:2 * C], p["v"]], axis=0), p["bke"], _TN, P_STATE)
           for u, p in zip(us, prep)]
    yield
    s_new = [s * p["e_tot"] + jnp.where(blk64, d, 0.0) for p, s, d in zip(prep, states, upd)]
    ys = [w[2 * C:3 * C] + _mm(p["rb"], u, _NN, P_OUT) + p["yv"] for p, w, u in zip(prep, ws, us)]
    yield
    return ys, s_new


def _rwkv_consts():
    C = RWKV_CHUNK
    row = lax.broadcasted_iota(jnp.int32, (C, C), 0)
    col = lax.broadcasted_iota(jnp.int32, (C, C), 1)
    row2 = lax.broadcasted_iota(jnp.int32, (C, 2 * C), 0)
    col2 = lax.broadcasted_iota(jnp.int32, (C, 2 * C), 1)
    cs = col2 % C
    lane = lax.broadcasted_iota(jnp.int32, (1, LANES), 1)
    m0 = lane < RWKV_HEAD_DIM
    cm0 = lax.broadcasted_iota(jnp.int32, (1, 2 * C), 1) < C
    rr = lax.broadcasted_iota(jnp.int32, (2 * C, 2 * C), 0)
    cc = lax.broadcasted_iota(jnp.int32, (2 * C, 2 * C), 1)
    eye = (rr == cc).astype(F32)
    blk16 = (rr // 16) == (cc // 16)
    blk32 = (rr // 32) == (cc // 32)
    blk64 = (rr // 64) == (cc // 64)
    fwd = ((col <= row).astype(BF16), cs < row2, cs <= row2)
    bwd = ((col >= row).astype(BF16), cs > row2, cs >= row2)
    return fwd, bwd, (m0, cm0, eye, blk16, blk32, blk64)


def _gla_intra(items, cst):
    G = GLA_CHUNK
    ind_v, sel, blk = cst[2]
    srow = lax.broadcasted_iota(jnp.int32, (G, 1), 0)
    cums = [_cumsum_mm(cst[rev], la) for (_, _, _, la, rev) in items]
    pmats = []
    for (q, k, v, la, rev), cum in zip(items, cums):
        ps = []
        for t in range(G):
            msk = (srow >= t) if rev else (srow <= t)
            e = jnp.exp(jnp.where(msk, cum[t:t + 1] - cum, 0.0))
            ps.append(jnp.where(msk, e * (q[t:t + 1] * k), 0.0))
        pmats.append(jnp.concatenate(ps, axis=0))
    atts = [_mm(pm, ind_v, _NN, P_GATT) for pm in pmats]
    o_intra = [_mm(sel, att * jnp.concatenate([it[2]] * G, axis=0), _NN, P_GATT)
               for att, it in zip(atts, items)]
    out = []
    for (q, k, v, la, rev), cum, oi in zip(items, cums, o_intra):
        tot = cum[0:1] if rev else cum[G - 1:G]
        upd = _mm(v, k * jnp.exp(tot - cum), _TN, P_GSTATE)
        out.append((oi, q * jnp.exp(cum), jnp.where(blk, upd, 0.0), jnp.exp(tot)))
    yield
    return out


def _gla_consts():
    G = GLA_CHUNK
    row = lax.broadcasted_iota(jnp.int32, (G, G), 0)
    col = lax.broadcasted_iota(jnp.int32, (G, G), 1)
    kc = lax.broadcasted_iota(jnp.int32, (LANES, 2 * LANES), 0)
    vc = lax.broadcasted_iota(jnp.int32, (LANES, 2 * LANES), 1)
    ind_v = ((kc // GLA_KEY_DIM) == (vc // GLA_VAL_DIM)).astype(F32)
    st = lax.broadcasted_iota(jnp.int32, (G, G * G), 0)
    sj = lax.broadcasted_iota(jnp.int32, (G, G * G), 1)
    sel = ((sj // G) == st).astype(F32)
    br = lax.broadcasted_iota(jnp.int32, (2 * LANES, LANES), 0)
    bc = lax.broadcasted_iota(jnp.int32, (2 * LANES, LANES), 1)
    blk = (br // GLA_VAL_DIM) == (bc // GLA_KEY_DIM)
    return (col <= row).astype(BF16), (col >= row).astype(BF16), (ind_v, sel, blk)


def _gla_factored(items, cst):
    F = GLA_WIDE_CHUNK
    m0, vm0, blk = cst[2]
    cums = [_cumsum_mm(cst[rev][0], la) for (_, _, _, la, rev) in items]
    yield
    scores = []
    for (q, k, v, la, rev), cum in zip(items, cums):
        mid = cum[F // 2:F // 2 + 1] if rev else cum[F // 2 - 1:F // 2]
        kt = k * jnp.exp(mid - cum)
        kstk = jnp.concatenate([jnp.where(m0, kt, 0.0), jnp.where(m0, 0.0, kt)], axis=0)
        scores.append(_mm(q * jnp.exp(cum - mid), kstk, _NT, P_GATT))
    yield
    o_intra = []
    for (q, k, v, la, rev), sc in zip(items, scores):
        vstk = jnp.concatenate([jnp.where(vm0, v, 0.0), jnp.where(vm0, 0.0, v)], axis=0)
        o_intra.append(_mm(jnp.where(cst[rev][1], sc, 0.0), vstk, _NN, P_GATT))
    yield
    out = []
    for (q, k, v, la, rev), cum, oi in zip(items, cums, o_intra):
        tot = cum[0:1] if rev else cum[F - 1:F]
        upd = _mm(v, k * jnp.exp(tot - cum), _TN, P_GSTATE)
        out.append((oi, q * jnp.exp(cum), jnp.where(blk, upd, 0.0), jnp.exp(tot)))
    yield
    return out


def _gla_wide_consts():
    F = GLA_WIDE_CHUNK
    row = lax.broadcasted_iota(jnp.int32, (F, F), 0)
    col = lax.broadcasted_iota(jnp.int32, (F, F), 1)
    row2 = lax.broadcasted_iota(jnp.int32, (F, 2 * F), 0)
    cs = lax.broadcasted_iota(jnp.int32, (F, 2 * F), 1) % F
    m0 = lax.broadcasted_iota(jnp.int32, (1, LANES), 1) < GLA_KEY_DIM
    vm0 = lax.broadcasted_iota(jnp.int32, (1, 2 * LANES), 1) < GLA_VAL_DIM
    br = lax.broadcasted_iota(jnp.int32, (2 * LANES, LANES), 0)
    bc = lax.broadcasted_iota(jnp.int32, (2 * LANES, LANES), 1)
    blk = (br // GLA_VAL_DIM) == (bc // GLA_KEY_DIM)
    return (((col <= row).astype(BF16), cs <= row2), ((col >= row).astype(BF16), cs >= row2), (m0, vm0, blk))


def _scan_kernel(raf_ref, rdf_ref, rab_ref, rdb_ref, gaf_ref, glf_ref, gab_ref, glb_ref,
                 srf0_ref, srb0_ref, sgf0_ref, sgb0_ref,
                 yof_ref, yob_ref, srf_ref, srb_ref, sgf_ref, sgb_ref):
    i = pl.program_id(1)

    @pl.when(i == 0)
    def _():
        srf_ref[...] = srf0_ref[...]
        srb_ref[...] = srb0_ref[...]
        sgf_ref[...] = sgf0_ref[...]
        sgb_ref[...] = sgb0_ref[...]

    T = SCAN_BLOCK
    C = RWKV_CHUNK
    nr = T // C
    rcst = _rwkv_consts()
    R = RWKV_WIDTH
    Q = GLA_QK_WIDTH
    g_dirs = ((False, gaf_ref, glf_ref, 0, sgf_ref, yof_ref),
              (True, gab_ref, glb_ref, Q, sgb_ref, yob_ref))
    la_tot = jnp.minimum(jnp.sum(glf_ref[0, :, 0:Q], axis=0, keepdims=True),
                         jnp.sum(glb_ref[0, :, Q:2 * Q], axis=0, keepdims=True))
    gla_split_ok = jnp.min(la_tot) >= -GLA_SAFE_LOG

    r_dirs = ((False, raf_ref, rdf_ref, srf_ref, yof_ref),
              (True, rab_ref, rdb_ref, srb_ref, yob_ref))
    r_units = [[] for _ in range(nr)]
    r_dests = [[] for _ in range(nr)]
    for rev, ra_ref, rd_ref, s_ref, y_ref in r_dirs:
        for p in range(R // LANES):
            sl, sl1, sl2 = (slice(o + p * LANES, o + (p + 1) * LANES) for o in (0, R, 2 * R))
            for j in range(nr):
                c = nr - 1 - j if rev else j
                rows = slice(c * C, (c + 1) * C)
                r_units[j].append((ra_ref[0, rows, sl], rd_ref[0, rows, sl], ra_ref[0, rows, sl1],
                                   ra_ref[0, rows, sl2], rd_ref[0, rows, sl1], rd_ref[0, rows, sl2], rev))
                r_dests[j].append((y_ref, s_ref, p, rows, sl))
    r_states = [[s_ref[0, p] for (_, s_ref, p, _, _) in r_dests[0]]] + [None] * nr

    def rwkv_task(j):
        prep = yield from _rwkv_prepare(r_units[j], rcst)
        while r_states[j] is None:
            yield
        ys, r_states[j + 1] = yield from _rwkv_apply(prep, r_states[j], rcst)
        for (y_ref, _, _, rows, sl), y in zip(r_dests[j], ys):
            y_ref[0, rows, sl] = y

    npair = GLA_QK_WIDTH // LANES
    g_init = [[s_ref[0, p] for p in range(npair)] for (_, _, _, _, s_ref, _) in g_dirs]

    def gla_task(chunk, intra, cst):
        ng = T // chunk
        items = [[] for _ in range(ng)]
        dests = [[] for _ in range(ng)]
        for d, (rev, ga_ref, gl_ref, la_off, s_ref, o_ref) in enumerate(g_dirs):
            for p in range(npair):
                qs, ks, las = (slice(o + p * LANES, o + (p + 1) * LANES) for o in (0, Q, la_off))
                vs = slice(2 * Q + p * 2 * LANES, 2 * Q + (p + 1) * 2 * LANES)
                os_ = slice(R + p * 2 * LANES, R + (p + 1) * 2 * LANES)
                for j in range(ng):
                    c = ng - 1 - j if rev else j
                    rows = slice(c * chunk, (c + 1) * chunk)
                    items[j].append((ga_ref[0, rows, qs], ga_ref[0, rows, ks], ga_ref[0, rows, vs],
                                     gl_ref[0, rows, las], rev))
                    dests[j].append((o_ref, s_ref, d, p, rows, os_))
        parts = yield from intra([it for its in items for it in its], cst)
        nu = len(items[0])
        sts = [g_init[d][p] for (_, _, d, p, _, _) in dests[0]]
        for j in range(ng):
            new = []
            for (o_ref, _, _, _, rows, vs), (o_intra, q_in, upd, e_tot), st in zip(
                    dests[j], parts[j * nu:(j + 1) * nu], sts):
                o_ref[0, rows, vs] = o_intra + _mm(q_in, st, _NT, P_GSTATE)
                new.append(st * e_tot + upd)
            sts = new
            yield
        for (_, s_ref, _, p, _, _), st in zip(dests[0], sts):
            s_ref[0, p] = st

    _interleave([(RWKV_TASK_LAG * j, rwkv_task(j)) for j in range(nr)]
                + [(GLA_TASK_LAG, gla_task(GLA_WIDE_CHUNK, _gla_factored, _gla_wide_consts()))])
    for (_, s_ref, p, _, _), s in zip(r_dests[0], r_states[nr]):
        s_ref[0, p] = s

    @pl.when(jnp.logical_not(gla_split_ok))
    def _():
        _interleave([(0, gla_task(GLA_CHUNK, _gla_intra, _gla_consts()))])


def _scan(ra, rd_f, rd_b, ga, gl, s_rf, s_rb, s_gf, s_gb):
    B, L, _ = ra.shape
    R = RWKV_WIDTH
    Q = GLA_QK_WIDTH
    V = GLA_V_WIDTH
    T = SCAN_BLOCK
    n = L // T

    def fwd(w):
        return pl.BlockSpec((1, T, w), lambda b, i: (b, i, 0))

    def bwd(w):
        return pl.BlockSpec((1, T, w), lambda b, i: (b, n - 1 - i, 0))

    rst = pl.BlockSpec((1, R // LANES, LANES, LANES), lambda b, i: (b, 0, 0, 0))
    gst = pl.BlockSpec((1, Q // LANES, 2 * LANES, LANES), lambda b, i: (b, 0, 0, 0))
    return pl.pallas_call(
        _scan_kernel,
        grid=(B, n),
        in_specs=[fwd(3 * R), fwd(3 * R), bwd(3 * R), bwd(3 * R),
                  fwd(2 * Q + V), fwd(2 * Q), bwd(2 * Q + V), bwd(2 * Q), rst, rst, gst, gst],
        out_specs=[fwd(R + V), bwd(R + V), rst, rst, gst, gst],
        out_shape=[jax.ShapeDtypeStruct((B, L, R + V), F32)] * 2
        + [jax.ShapeDtypeStruct((B, R // LANES, LANES, LANES), F32)] * 2
        + [jax.ShapeDtypeStruct((B, Q // LANES, 2 * LANES, LANES), F32)] * 2,
        compiler_params=pltpu.CompilerParams(dimension_semantics=("parallel", "arbitrary"),
                                             vmem_limit_bytes=VMEM_LIMIT),
        name="scan",
    )(ra, rd_f, ra, rd_b, ga, gl, ga, gl, s_rf, s_rb, s_gf, s_gb)


def _post_kernel(x_ref, mod_ref, yof_ref, yob_ref, pg_ref,
                 lng_ref, lnb_ref, gng_ref, n2g_ref, fng_ref, ind_ref, wout_ref, w1_ref, w2_ref, o_ref):
    D = D_MODEL
    R = RWKV_WIDTH
    m = mod_ref[0]
    gt1 = m[:, 2 * D:3 * D]
    sh2 = m[:, 3 * D:4 * D]
    sc2 = m[:, 4 * D:5 * D]
    gt2 = m[:, 5 * D:6 * D]
    inv_n = 1.0 / RWKV_HEAD_DIM
    y = yof_ref[0, :, 0:R] + yob_ref[0, :, 0:R]
    mu = _seg_sum(y, ind_ref) * inv_n
    yc = y - mu
    var = _seg_sum(yc * yc, ind_ref) * inv_n
    yn = yc * lax.rsqrt(var + LNX_EPS) * lng_ref[...] + lnb_ref[...]
    rw = (yn + pg_ref[0, :, R:2 * R]) * pg_ref[0, :, 0:R]
    o = yof_ref[0, :, R:] + yob_ref[0, :, R:]
    gsil = pg_ref[0, :, 2 * R:]
    gng = gng_ref[...]
    parts = [rw]
    for hh in range(GLA_HEADS):
        sl = slice(hh * GLA_VAL_DIM, (hh + 1) * GLA_VAL_DIM)
        oh = o[:, sl]
        oh = oh * lax.rsqrt(jnp.mean(oh * oh, axis=-1, keepdims=True) + GLA_NORM_EPS)
        parts.append(oh * gng * gsil[:, sl])
    mix = jnp.concatenate(parts, axis=-1)
    x1 = x_ref[0] + gt1 * _bdot(mix, wout_ref)
    h2 = _rms(x1) * n2g_ref[...] * (1.0 + sc2) + sh2
    f = jnp.maximum(_bdot(h2, w1_ref), 0.0)
    x2 = x1 + gt2 * _bdot(f * f, w2_ref)
    o_ref[0] = _rms(x2) * fng_ref[...]


def _post(x, mod3, mod_row, yo_f, yo_b, pg, vecs, mats):
    B, L, D = x.shape
    tb = min(POST_BLOCK, L)
    nblk = L // tb

    def tok(w):
        return pl.BlockSpec((1, tb, w), lambda b, i: (b, i, 0))

    in_specs = [tok(D), pl.BlockSpec((1, 1, N_MOD * D), lambda b, i: (mod_row(b), 0, 0))]
    in_specs += [tok(yo_f.shape[-1]), tok(yo_b.shape[-1]), tok(pg.shape[-1])]
    in_specs += [_const_spec(w.shape) for w in vecs]
    in_specs += [pl.BlockSpec(w.shape, lambda b, i: (0, 0), pipeline_mode=pl.Buffered(1)) for w in mats]
    return pl.pallas_call(
        _post_kernel,
        grid=(B, nblk),
        in_specs=in_specs,
        out_specs=tok(D),
        out_shape=jax.ShapeDtypeStruct((B, L, D), F32),
        compiler_params=pltpu.CompilerParams(dimension_semantics=("parallel", "arbitrary"),
                                             vmem_limit_bytes=VMEM_LIMIT),
        name="post",
    )(x, mod3, yo_f, yo_b, pg, *vecs, *mats)


def _rwkv_state_to_pairs(s):
    B = s.shape[0]
    s = s.reshape(B, 4, 2, 64, 64)
    z = jnp.zeros_like(s[:, :, 0])
    top = jnp.concatenate([s[:, :, 0], z], axis=-1)
    bot = jnp.concatenate([z, s[:, :, 1]], axis=-1)
    return jnp.concatenate([top, bot], axis=-2)


def _rwkv_pairs_to_state(sp):
    B = sp.shape[0]
    a = sp[:, :, 0:64, 0:64]
    b = sp[:, :, 64:128, 64:128]
    return jnp.stack([a, b], axis=2).reshape(B, 8, 64, 64)


def _gla_state_to_pairs(s):
    B = s.shape[0]
    st = jnp.swapaxes(s, -1, -2).reshape(B, 2, 2, 128, 64)
    z = jnp.zeros_like(st[:, :, 0])
    top = jnp.concatenate([st[:, :, 0], z], axis=-1)
    bot = jnp.concatenate([z, st[:, :, 1]], axis=-1)
    return jnp.concatenate([top, bot], axis=-2)


def _gla_pairs_to_state(sp):
    B = sp.shape[0]
    a = sp[:, :, 0:128, 0:64]
    b = sp[:, :, 128:256, 64:128]
    st = jnp.stack([a, b], axis=2).reshape(B, 4, 128, 64)
    return jnp.swapaxes(st, -1, -2)


def _blockdiag2(a, b):
    za = jnp.zeros((a.shape[0], b.shape[1]), a.dtype)
    zb = jnp.zeros((b.shape[0], a.shape[1]), a.dtype)
    return jnp.concatenate([jnp.concatenate([a, za], axis=1), jnp.concatenate([zb, b], axis=1)], axis=0)


def kernel(x_prompt, x_sample, c, state_rwkv_fwd, state_rwkv_bwd, state_gla_fwd, state_gla_bwd, c_ctx, ada_w, ada_b, norm1_g, norm2_g, w_in, rwkv_mu_rkv, rwkv_mu_wag, rwkv_w0, rwkv_w1, rwkv_w2, rwkv_a0, rwkv_a1, rwkv_a2, rwkv_g1, rwkv_g2, rwkv_k_k, rwkv_k_a, rwkv_r_k, rwkv_lnx_g, rwkv_lnx_b, gla_gk1, gla_gk2, gla_gk_b, gla_norm_g, w_out, mlp_w1, mlp_w2, final_norm_g):
    D = D_MODEL
    R = RWKV_WIDTH
    nb = x_prompt.shape[0]
    nd = x_sample.shape[0]
    assert ada_w.shape[0] == 1, "single-layer step"
    layer = 0

    craw = jnp.concatenate([c_ctx[None, :], c, jnp.zeros((8 - 1 - nd, D), F32)], axis=0)
    mod = _modulation(craw, ada_w[layer], ada_b[layer][None, :])
    mod3 = mod.reshape(8, 1, N_MOD * D)

    bf = lambda t: t.astype(BF16)
    row = lambda t: t.reshape(1, -1).astype(F32)
    w_in_l = w_in[layer]
    ind = (jnp.arange(R)[:, None] // RWKV_HEAD_DIM == jnp.arange(R)[None, :] // RWKV_HEAD_DIM).astype(BF16)
    pre_w = [
        row(norm1_g[layer]),
        bf(w_in_l[:, :3 * R]),
        bf(w_in_l[:, 3 * R:]),
        bf(jnp.concatenate([gla_gk1[layer, 0], gla_gk1[layer, 1]], axis=1)),
        bf(jnp.concatenate([rwkv_w1[layer, 0], rwkv_w1[layer, 1]], axis=1)),
        bf(jnp.concatenate([rwkv_a1[layer, 0], rwkv_a1[layer, 1]], axis=1)),
        bf(rwkv_g1[layer]),
        bf(_blockdiag2(rwkv_w2[layer, 0], rwkv_w2[layer, 1])),
        bf(_blockdiag2(rwkv_a2[layer, 0], rwkv_a2[layer, 1])),
        bf(rwkv_g2[layer]),
        bf(_blockdiag2(gla_gk2[layer, 0], gla_gk2[layer, 1])),
        row(rwkv_mu_rkv[layer]),
        rwkv_mu_wag[layer],
        row(rwkv_w0[layer]),
        row(rwkv_a0[layer]),
        row(gla_gk_b[layer]),
        row(rwkv_k_k[layer]),
        row(rwkv_k_a[layer]),
        row(rwkv_r_k[layer]),
        ind,
    ]
    post_vecs = [row(rwkv_lnx_g[layer]), row(rwkv_lnx_b[layer]), row(gla_norm_g[layer]),
                 row(norm2_g[layer]), row(final_norm_g)]
    post_mats = [ind, bf(w_out[layer]), bf(mlp_w1[layer]), bf(mlp_w2[layer])]

    def run_group(x, mod_row, grid_shift, s_rf, s_rb, s_gf, s_gb):
        ra, rd_f, rd_b, ga, gl, pg = _pre(x, mod3, mod_row, pre_w, grid_shift)
        yo_f, yo_b, n_rf, n_rb, n_gf, n_gb = _scan(ra, rd_f, rd_b, ga, gl, s_rf, s_rb, s_gf, s_gb)
        y = _post(x, mod3, mod_row, yo_f, yo_b, pg, post_vecs, post_mats)
        return y, (n_rf, n_rb, n_gf, n_gb)

    zr = jnp.zeros((nb, R // LANES, LANES, LANES), F32)
    zg = jnp.zeros((nb, GLA_QK_WIDTH // LANES, 2 * LANES, LANES), F32)
    y_prompt, (n_rf, n_rb, n_gf, n_gb) = run_group(x_prompt, lambda b: 0, False, zr, zr, zg, zg)
    y_sample, _ = run_group(x_sample, lambda b: b + 1, True,
                            _rwkv_state_to_pairs(state_rwkv_fwd[:, layer]),
                            _rwkv_state_to_pairs(state_rwkv_bwd[:, layer]),
                            _gla_state_to_pairs(state_gla_fwd[:, layer]),
                            _gla_state_to_pairs(state_gla_bwd[:, layer]))
    return (y_prompt, y_sample,
            _rwkv_pairs_to_state(n_rf)[:, None], _rwkv_pairs_to_state(n_rb)[:, None],
            _gla_pairs_to_state(n_gf)[:, None], _gla_pairs_to_state(n_gb)[:, None])
```

```python
import functools

import jax
import jax.numpy as jnp
from jax import lax
from jax.experimental import pallas as pl
from jax.experimental.pallas import tpu as pltpu

F32 = jnp.float32
BF16 = jnp.bfloat16
HI = lax.Precision.HIGHEST

D_MODEL = 1024
GRID_W = 64
RWKV_WIDTH = 512
RWKV_HEAD_DIM = 64
RWKV_HEADS = 8
GLA_HEADS = 4
GLA_KEY_DIM = 64
GLA_VAL_DIM = 128
GLA_QK_WIDTH = 256
GLA_V_WIDTH = 512
GLA_GATE_NORMALIZER = 16.0
N_MOD = 6
RMS_EPS = 1e-6
LNX_EPS = 64e-5
GLA_NORM_EPS = 1e-5

LANES = 128
RWKV_CHUNK = 64
GLA_CHUNK = 16
GLA_WIDE_CHUNK = 64
GLA_SAFE_LOG = 60.0
SCAN_BLOCK = 128
PRE_BLOCK = 256
POST_BLOCK = 512
VMEM_LIMIT = 56 * 1024 * 1024
RWKV_TASK_LAG = 3
GLA_TASK_LAG = 17

P_SCORE = "b1"
P_INV = "b1"
P_SOLVE = "b1"
P_STATE = "b1"
P_OUT = "b1"
P_GSTATE = "b1"
P_GATT = "b1"


def _dot(a, b, prec=None):
    return lax.dot_general(a, b, (((1,), (0,)), ((), ())), precision=prec, preferred_element_type=F32)


def _dot_nt(a, b, prec=None):
    return lax.dot_general(a, b, (((1,), (1,)), ((), ())), precision=prec, preferred_element_type=F32)


def _dot_tn(a, b, prec=None):
    return lax.dot_general(a, b, (((0,), (0,)), ((), ())), precision=prec, preferred_element_type=F32)


_NN = (((1,), (0,)), ((), ()))
_NT = (((1,), (1,)), ((), ()))
_TN = (((0,), (0,)), ((), ()))


def _split2(x):
    hi = x.astype(BF16)
    return hi, (x - hi.astype(F32)).astype(BF16)


def _mm(a, b, dims, mode):
    dg = functools.partial(lax.dot_general, dimension_numbers=dims, preferred_element_type=F32)
    if mode == "hi":
        return dg(a, b, precision=HI)
    if mode == "b1":
        return dg(a.astype(BF16), b.astype(BF16))
    a1, a2 = _split2(a)
    b1, b2 = _split2(b)
    return dg(a1, b1) + dg(a1, b2) + dg(a2, b1)


def _cumsum_mm(tri, x):
    hi = x.astype(BF16)
    r1 = x - hi.astype(F32)
    mid = r1.astype(BF16)
    lo = (r1 - mid.astype(F32)).astype(BF16)
    return _dot(tri, hi) + _dot(tri, mid) + _dot(tri, lo)


def _bdot(a, w_ref):
    return _dot(a.astype(BF16), w_ref[...])


def _sigmoid(x):
    return 1.0 / (1.0 + jnp.exp(-x))


def _softplus(x):
    return jnp.maximum(x, 0.0) + jnp.log(1.0 + jnp.exp(-jnp.abs(x)))


def _head_sum(x):
    first = lax.broadcasted_iota(jnp.int32, (1, LANES), 1) < RWKV_HEAD_DIM
    outs = []
    for j in range(x.shape[-1] // LANES):
        xb = x[:, j * LANES:(j + 1) * LANES]
        s0 = jnp.sum(jnp.where(first, xb, 0.0), axis=-1, keepdims=True)
        s1 = jnp.sum(jnp.where(first, 0.0, xb), axis=-1, keepdims=True)
        outs.append(jnp.where(first, s0, s1))
    return jnp.concatenate(outs, axis=-1)


def _rms(x):
    return x * lax.rsqrt(jnp.mean(x * x, axis=-1, keepdims=True) + RMS_EPS)


def _mod_kernel(c_ref, w_ref, b_ref, o_ref):
    c = c_ref[...]
    cond = c * _sigmoid(c)
    o_ref[...] = _dot(cond.astype(BF16), w_ref[...].astype(BF16)) + b_ref[...]


def _modulation(craw, ada_w, ada_b):
    n = ada_w.shape[1]
    bn = 1536
    return pl.pallas_call(
        _mod_kernel,
        grid=(n // bn,),
        in_specs=[
            pl.BlockSpec((8, D_MODEL), lambda j: (0, 0)),
            pl.BlockSpec((D_MODEL, bn), lambda j: (0, j)),
            pl.BlockSpec((1, bn), lambda j: (0, j)),
        ],
        out_specs=pl.BlockSpec((8, bn), lambda j: (0, j)),
        out_shape=jax.ShapeDtypeStruct((8, n), F32),
        compiler_params=pltpu.CompilerParams(dimension_semantics=("arbitrary",), vmem_limit_bytes=VMEM_LIMIT),
        name="modulation",
    )(craw, ada_w, ada_b)


def _pre_kernel(*refs, grid_shift, tb):
    if grid_shift:
        x_ref, xp_ref, xn_ref = refs[:3]
        refs = refs[3:]
    else:
        x_ref = refs[0]
        refs = refs[1:]
    (mod_ref, n1g_ref, wrkv_ref, wrest_ref, gk1_ref, w1_ref, a1_ref, g1_ref, w2_ref, a2_ref, g2_ref,
     gk2_ref, murkv_ref, muwag_ref, w0_ref, a0_ref, gkb_ref, kk_ref, ka_ref, rk_ref) = refs[:20]
    ra_o, rf_o, rb_o, ga_o, gl_o, pg_o = refs[20:]
    D = D_MODEL
    R = RWKV_WIDTH
    m = mod_ref[0]
    sh1 = m[:, 0:D]
    sc1 = m[:, D:2 * D]
    n1g = n1g_ref[...]

    def normmod(xx):
        return _rms(xx) * n1g * (1.0 + sc1) + sh1

    h = normmod(x_ref[0])
    row = lax.broadcasted_iota(jnp.int32, (tb, 1), 0)
    if grid_shift:
        i = pl.program_id(1)
        n = pl.num_programs(1)
        hp = normmod(xp_ref[0]) * (i > 0).astype(F32)
        hn = normmod(xn_ref[0]) * (i < n - 1).astype(F32)
        hext = jnp.concatenate([hp, h, hn], axis=0)
        col = row % GRID_W
        m_l = (col != 0).astype(F32)
        m_r = (col != GRID_W - 1).astype(F32)
        ne = tb + 2 * GRID_W

        def shift(ext):
            up = ext[0:tb]
            down = ext[2 * GRID_W:2 * GRID_W + tb]
            left = pltpu.roll(ext, 1, 0)[GRID_W:GRID_W + tb]
            right = pltpu.roll(ext, ne - 1, 0)[GRID_W:GRID_W + tb]
            return 0.25 * (up + down + m_l * left + m_r * right)

        halo = GRID_W
    else:
        hext = h
        m_l = (row != 0).astype(F32)
        m_r = (row != tb - 1).astype(F32)

        def shift(ext):
            return 0.5 * (m_l * pltpu.roll(ext, 1, 0) + m_r * pltpu.roll(ext, tb - 1, 0))

        halo = 0

    rest = _bdot(h, wrest_ref)
    dh = shift(hext) - h
    mu = muwag_ref[...]
    lora_w = _bdot(h + mu[0:1] * dh, w1_ref)
    lora_a = _bdot(h + mu[1:2] * dh, a1_ref)
    lora_g = _bdot(h + mu[2:3] * dh, g1_ref)
    lora_gk = _bdot(h, gk1_ref)
    z = w0_ref[...] + _bdot(jnp.tanh(lora_w), w2_ref)
    a = _sigmoid(a0_ref[...] + _bdot(lora_a, a2_ref))
    gate = _bdot(_sigmoid(lora_g), g2_ref)
    logits = _bdot(lora_gk, gk2_ref) + gkb_ref[...]
    rkv_ext = _bdot(hext, wrkv_ref)

    Q = GLA_QK_WIDTH
    ga_o[0, :, 0:Q] = rest[:, 0:Q] * (GLA_KEY_DIM ** -0.5)
    ga_o[0, :, Q:] = rest[:, Q:2 * Q + GLA_V_WIDTH]
    gg = rest[:, 2 * Q + GLA_V_WIDTH:]
    pg_o[0, :, 2 * R:] = gg * _sigmoid(gg)
    gl_o[0] = -_softplus(-logits) * (1.0 / GLA_GATE_NORMALIZER)

    rkv = rkv_ext[halo:halo + tb]
    rkv = rkv + murkv_ref[...] * (shift(rkv_ext) - rkv)
    r = rkv[:, 0:R]
    k = rkv[:, R:2 * R]
    v = rkv[:, 2 * R:3 * R]
    lw = -jnp.exp(-_softplus(-z) - 0.5)

    kap = k * kk_ref[...]
    kap = kap * lax.rsqrt(jnp.maximum(_head_sum(kap * kap), 1e-12))
    ka = ka_ref[...]
    a_f = a[:, 0:R]
    a_b = a[:, R:2 * R]
    kd_f = k * (1.0 + (a_f - 1.0) * ka)
    kd_b = k * (1.0 + (a_b - 1.0) * ka)
    bonus = _head_sum(r * (kd_f + kd_b) * rk_ref[...]) * v

    ra_o[0, :, 0:R] = r
    ra_o[0, :, R:2 * R] = v
    ra_o[0, :, 2 * R:3 * R] = kap
    rf_o[0, :, 0:R] = kd_f
    rf_o[0, :, R:2 * R] = a_f * kap
    rf_o[0, :, 2 * R:3 * R] = lw[:, 0:R]
    rb_o[0, :, 0:R] = kd_b
    rb_o[0, :, R:2 * R] = a_b * kap
    rb_o[0, :, 2 * R:3 * R] = lw[:, R:2 * R]
    pg_o[0, :, 0:R] = gate
    pg_o[0, :, R:2 * R] = bonus


def _const_spec(shape):
    nd = len(shape)
    return pl.BlockSpec(shape, lambda b, i: (0,) * nd)


def _pre(x, mod3, mod_row, weights, grid_shift):
    B, L, D = x.shape
    tb = PRE_BLOCK
    nblk = L // tb
    if not grid_shift:
        assert nblk == 1
    x_spec = pl.BlockSpec((1, tb, D), lambda b, i: (b, i, 0))
    in_specs = [x_spec]
    args = [x]
    if grid_shift:
        per = tb // GRID_W
        nrow = L // GRID_W
        in_specs += [
            pl.BlockSpec((1, GRID_W, D), lambda b, i: (b, jnp.maximum(i * per - 1, 0), 0)),
            pl.BlockSpec((1, GRID_W, D), lambda b, i: (b, jnp.minimum((i + 1) * per, nrow - 1), 0)),
        ]
        args += [x, x]
    in_specs.append(pl.BlockSpec((1, 1, N_MOD * D), lambda b, i: (mod_row(b), 0, 0)))
    args.append(mod3)
    for w in weights:
        in_specs.append(_const_spec(w.shape))
        args.append(w)
    R = RWKV_WIDTH
    widths = [3 * R, 3 * R, 3 * R, 2 * GLA_QK_WIDTH + GLA_V_WIDTH, 2 * GLA_QK_WIDTH, 2 * R + GLA_V_WIDTH]
    out_specs = [pl.BlockSpec((1, tb, w), lambda b, i: (b, i, 0)) for w in widths]
    out_shape = [jax.ShapeDtypeStruct((B, L, w), F32) for w in widths]
    return pl.pallas_call(
        functools.partial(_pre_kernel, grid_shift=grid_shift, tb=tb),
        grid=(B, nblk),
        in_specs=in_specs,
        out_specs=out_specs,
        out_shape=out_shape,
        compiler_params=pltpu.CompilerParams(dimension_semantics=("parallel", "arbitrary"),
                                             vmem_limit_bytes=VMEM_LIMIT),
        name="pre_grid" if grid_shift else "pre_seq",
    )(*args)


def _interleave(tasks):
    live = list(tasks)
    rnd = 0
    while live:
        keep = []
        for first, gen in live:
            if rnd >= first:
                try:
                    next(gen)
                except StopIteration:
                    continue
            keep.append((first, gen))
        live = keep
        rnd += 1


def _inv_unit_triangular(lms, eye, blk16, blk32):
    mm = functools.partial(_mm, dims=_NN, mode=P_INV)
    l0 = [jnp.where(blk16, lm, 0.0) for lm in lms]
    l2 = [mm(a, a) for a in l0]
    yield
    l4 = [mm(a, a) for a in l2]
    yield
    l8 = [mm(a, a) for a in l4]
    yield
    t = [mm(eye - a, eye + b) for a, b in zip(l0, l2)]
    yield
    t = [mm(a, eye + b) for a, b in zip(t, l4)]
    yield
    t = [mm(a, eye + b) for a, b in zip(t, l8)]
    yield
    off1_mask = jnp.logical_and(blk32, jnp.logical_not(blk16))
    x = [mm(a, jnp.where(off1_mask, lm, 0.0)) for a, lm in zip(t, lms)]
    yield
    t = [a - mm(b, a) for a, b in zip(t, x)]
    yield
    x = [mm(a, jnp.where(blk32, 0.0, lm)) for a, lm in zip(t, lms)]
    yield
    t = [a - mm(b, a) for a, b in zip(t, x)]
    yield
    return t


def _rwkv_prepare(units, cst):
    C = RWKV_CHUNK
    m0, cm0, eye, blk16, blk32, _ = cst[2]

    def stack(x):
        return jnp.concatenate([jnp.where(m0, x, 0.0), jnp.where(m0, 0.0, x)], axis=0)

    def blockdiag(side):
        return jnp.concatenate([jnp.where(cm0, side, 0.0), jnp.where(cm0, 0.0, side)], axis=0)

    cums = [_cumsum_mm(cst[rev][0], lw) for (_, _, _, _, _, lw, rev) in units]
    yield
    prep = []
    for (r, k, v, kap, b, lw, rev), cum in zip(units, cums):
        cumx = cum - lw
        if rev:
            mid = cum[C // 2:C // 2 + 1]
            tot = cum[0:1]
        else:
            mid = cum[C // 2 - 1:C // 2]
            tot = cum[C - 1:C]
        e_mid = jnp.exp(mid)
        rt = r * jnp.exp(cum - mid)
        kt = kap * jnp.exp(cumx - mid)
        es = jnp.exp(mid - cum)
        e_end = jnp.exp(tot - mid)
        prep.append(dict(rt=rt, kt=kt, bh=b * es, kh=k * es, r0=rt * e_mid, k0=stack(kt * e_mid),
                         be=b * es * e_end, ke=k * es * e_end, vs=stack(v), v=v,
                         e_tot=jnp.exp(tot), strict2=cst[rev][1], incl2=cst[rev][2]))
    gs = [_mm(jnp.concatenate([p["kt"], p["rt"]], axis=0),
              jnp.concatenate([stack(p["bh"]), stack(p["kh"])], axis=0), _NT, P_SCORE) for p in prep]
    yield
    for p, g in zip(prep, gs):
        p["ab"] = blockdiag(jnp.where(p["strict2"], g[0:C, 0:2 * C], 0.0))
        p["ak"] = blockdiag(jnp.where(p["strict2"], g[0:C, 2 * C:4 * C], 0.0))
        p["rb"] = jnp.where(p["incl2"], g[C:2 * C, 0:2 * C], 0.0)
        p["rk"] = jnp.where(p["incl2"], g[C:2 * C, 2 * C:4 * C], 0.0)
    av = [_mm(jnp.concatenate([p["ak"], p["rk"]], axis=0), p["vs"], _NN, P_SOLVE) for p in prep]
    yield
    ts = yield from _inv_unit_triangular([p["ab"] for p in prep], eye, blk16, blk32)
    wu = [_mm(t, jnp.concatenate([p["k0"], a[0:2 * C]], axis=1), _NN, P_SOLVE) for t, p, a in zip(ts, prep, av)]
    yield
    return [dict(wr=jnp.concatenate([x[:, :LANES], p["r0"]], axis=0), u0=x[:, LANES:], yv=a[2 * C:3 * C],
                 rb=p["rb"], v=p["v"], bke=jnp.concatenate([p["be"], p["ke"]], axis=0), e_tot=p["e_tot"])
            for x, a, p in zip(wu, av, prep)]


def _rwkv_apply(prep, states, cst):
    C = RWKV_CHUNK
    blk64 = cst[2][5]
    ws = [_mm(p["wr"], s, _NT, P_STATE) for p, s in zip(prep, states)]
    yield
    us = [-p["u0"] - w[0:2 * C] for p, w in zip(prep, ws)]
    upd = [_mm(jnp.concatenate([u[0:C] + u[C:2 * C], p["v"]], axis=0), p["bke"], _TN, P_STATE)
           for u, p in zip(us, prep)]
    yield
    s_new = [s * p["e_tot"] + jnp.where(blk64, d, 0.0) for p, s, d in zip(prep, states, upd)]
    ys = [w[2 * C:3 * C] + _mm(p["rb"], u, _NN, P_OUT) + p["yv"] for p, w, u in zip(prep, ws, us)]
    yield
    return ys, s_new


def _rwkv_consts():
    C = RWKV_CHUNK
    row = lax.broadcasted_iota(jnp.int32, (C, C), 0)
    col = lax.broadcasted_iota(jnp.int32, (C, C), 1)
    row2 = lax.broadcasted_iota(jnp.int32, (C, 2 * C), 0)
    col2 = lax.broadcasted_iota(jnp.int32, (C, 2 * C), 1)
    cs = col2 % C
    lane = lax.broadcasted_iota(jnp.int32, (1, LANES), 1)
    m0 = lane < RWKV_HEAD_DIM
    cm0 = lax.broadcasted_iota(jnp.int32, (1, 2 * C), 1) < C
    rr = lax.broadcasted_iota(jnp.int32, (2 * C, 2 * C), 0)
    cc = lax.broadcasted_iota(jnp.int32, (2 * C, 2 * C), 1)
    eye = (rr == cc).astype(F32)
    blk16 = (rr // 16) == (cc // 16)
    blk32 = (rr // 32) == (cc // 32)
    blk64 = (rr // 64) == (cc // 64)
    fwd = ((col <= row).astype(BF16), cs < row2, cs <= row2)
    bwd = ((col >= row).astype(BF16), cs > row2, cs >= row2)
    return fwd, bwd, (m0, cm0, eye, blk16, blk32, blk64)


def _gla_intra(items, cst):
    G = GLA_CHUNK
    ind_v, sel, blk = cst[2]
    srow = lax.broadcasted_iota(jnp.int32, (G, 1), 0)
    cums = [_cumsum_mm(cst[rev], la) for (_, _, _, la, rev) in items]
    pmats = []
    for (q, k, v, la, rev), cum in zip(items, cums):
        ps = []
        for t in range(G):
            msk = (srow >= t) if rev else (srow <= t)
            e = jnp.exp(jnp.where(msk, cum[t:t + 1] - cum, 0.0))
            ps.append(jnp.where(msk, e * (q[t:t + 1] * k), 0.0))
        pmats.append(jnp.concatenate(ps, axis=0))
    atts = [_mm(pm, ind_v, _NN, P_GATT) for pm in pmats]
    o_intra = [_mm(sel, att * jnp.concatenate([it[2]] * G, axis=0), _NN, P_GATT)
               for att, it in zip(atts, items)]
    out = []
    for (q, k, v, la, rev), cum, oi in zip(items, cums, o_intra):
        tot = cum[0:1] if rev else cum[G - 1:G]
        upd = _mm(v, k * jnp.exp(tot - cum), _TN, P_GSTATE)
        out.append((oi, q * jnp.exp(cum), jnp.where(blk, upd, 0.0), jnp.exp(tot)))
    yield
    return out


def _gla_consts():
    G = GLA_CHUNK
    row = lax.broadcasted_iota(jnp.int32, (G, G), 0)
    col = lax.broadcasted_iota(jnp.int32, (G, G), 1)
    kc = lax.broadcasted_iota(jnp.int32, (LANES, 2 * LANES), 0)
    vc = lax.broadcasted_iota(jnp.int32, (LANES, 2 * LANES), 1)
    ind_v = ((kc // GLA_KEY_DIM) == (vc // GLA_VAL_DIM)).astype(F32)
    st = lax.broadcasted_iota(jnp.int32, (G, G * G), 0)
    sj = lax.broadcasted_iota(jnp.int32, (G, G * G), 1)
    sel = ((sj // G) == st).astype(F32)
    br = lax.broadcasted_iota(jnp.int32, (2 * LANES, LANES), 0)
    bc = lax.broadcasted_iota(jnp.int32, (2 * LANES, LANES), 1)
    blk = (br // GLA_VAL_DIM) == (bc // GLA_KEY_DIM)
    return (col <= row).astype(BF16), (col >= row).astype(BF16), (ind_v, sel, blk)


def _gla_factored(items, cst):
    F = GLA_WIDE_CHUNK
    m0, vm0, blk = cst[2]
    cums = [_cumsum_mm(cst[rev][0], la) for (_, _, _, la, rev) in items]
    yield
    scores = []
    for (q, k, v, la, rev), cum in zip(items, cums):
        mid = cum[F // 2:F // 2 + 1] if rev else cum[F // 2 - 1:F // 2]
        kt = k * jnp.exp(mid - cum)
        kstk = jnp.concatenate([jnp.where(m0, kt, 0.0), jnp.where(m0, 0.0, kt)], axis=0)
        scores.append(_mm(q * jnp.exp(cum - mid), kstk, _NT, P_GATT))
    yield
    o_intra = []
    for (q, k, v, la, rev), sc in zip(items, scores):
        vstk = jnp.concatenate([jnp.where(vm0, v, 0.0), jnp.where(vm0, 0.0, v)], axis=0)
        o_intra.append(_mm(jnp.where(cst[rev][1], sc, 0.0), vstk, _NN, P_GATT))
    yield
    out = []
    for (q, k, v, la, rev), cum, oi in zip(items, cums, o_intra):
        tot = cum[0:1] if rev else cum[F - 1:F]
        upd = _mm(v, k * jnp.exp(tot - cum), _TN, P_GSTATE)
        out.append((oi, q * jnp.exp(cum), jnp.where(blk, upd, 0.0), jnp.exp(tot)))
    yield
    return out


def _gla_wide_consts():
    F = GLA_WIDE_CHUNK
    row = lax.broadcasted_iota(jnp.int32, (F, F), 0)
    col = lax.broadcasted_iota(jnp.int32, (F, F), 1)
    row2 = lax.broadcasted_iota(jnp.int32, (F, 2 * F), 0)
    cs = lax.broadcasted_iota(jnp.int32, (F, 2 * F), 1) % F
    m0 = lax.broadcasted_iota(jnp.int32, (1, LANES), 1) < GLA_KEY_DIM
    vm0 = lax.broadcasted_iota(jnp.int32, (1, 2 * LANES), 1) < GLA_VAL_DIM
    br = lax.broadcasted_iota(jnp.int32, (2 * LANES, LANES), 0)
    bc = lax.broadcasted_iota(jnp.int32, (2 * LANES, LANES), 1)
    blk = (br // GLA_VAL_DIM) == (bc // GLA_KEY_DIM)
    return (((col <= row).astype(BF16), cs <= row2), ((col >= row).astype(BF16), cs >= row2), (m0, vm0, blk))


def _scan_kernel(raf_ref, rdf_ref, rab_ref, rdb_ref, gaf_ref, glf_ref, gab_ref, glb_ref,
                 srf0_ref, srb0_ref, sgf0_ref, sgb0_ref,
                 yof_ref, yob_ref, srf_ref, srb_ref, sgf_ref, sgb_ref):
    i = pl.program_id(1)

    @pl.when(i == 0)
    def _():
        srf_ref[...] = srf0_ref[...]
        srb_ref[...] = srb0_ref[...]
        sgf_ref[...] = sgf0_ref[...]
        sgb_ref[...] = sgb0_ref[...]

    T = SCAN_BLOCK
    C = RWKV_CHUNK
    nr = T // C
    rcst = _rwkv_consts()
    R = RWKV_WIDTH
    Q = GLA_QK_WIDTH
    g_dirs = ((False, gaf_ref, glf_ref, 0, sgf_ref, yof_ref),
              (True, gab_ref, glb_ref, Q, sgb_ref, yob_ref))
    la_tot = jnp.minimum(jnp.sum(glf_ref[0, :, 0:Q], axis=0, keepdims=True),
                         jnp.sum(glb_ref[0, :, Q:2 * Q], axis=0, keepdims=True))
    gla_split_ok = jnp.min(la_tot) >= -GLA_SAFE_LOG

    r_dirs = ((False, raf_ref, rdf_ref, srf_ref, yof_ref),
              (True, rab_ref, rdb_ref, srb_ref, yob_ref))
    r_units = [[] for _ in range(nr)]
    r_dests = [[] for _ in range(nr)]
    for rev, ra_ref, rd_ref, s_ref, y_ref in r_dirs:
        for p in range(R // LANES):
            sl, sl1, sl2 = (slice(o + p * LANES, o + (p + 1) * LANES) for o in (0, R, 2 * R))
            for j in range(nr):
                c = nr - 1 - j if rev else j
                rows = slice(c * C, (c + 1) * C)
                r_units[j].append((ra_ref[0, rows, sl], rd_ref[0, rows, sl], ra_ref[0, rows, sl1],
                                   ra_ref[0, rows, sl2], rd_ref[0, rows, sl1], rd_ref[0, rows, sl2], rev))
                r_dests[j].append((y_ref, s_ref, p, rows, sl))
    r_states = [[s_ref[0, p] for (_, s_ref, p, _, _) in r_dests[0]]] + [None] * nr

    def rwkv_task(j):
        prep = yield from _rwkv_prepare(r_units[j], rcst)
        while r_states[j] is None:
            yield
        ys, r_states[j + 1] = yield from _rwkv_apply(prep, r_states[j], rcst)
        for (y_ref, _, _, rows, sl), y in zip(r_dests[j], ys):
            y_ref[0, rows, sl] = y

    npair = GLA_QK_WIDTH // LANES
    g_init = [[s_ref[0, p] for p in range(npair)] for (_, _, _, _, s_ref, _) in g_dirs]

    def gla_task(chunk, intra, cst):
        ng = T // chunk
        items = [[] for _ in range(ng)]
        dests = [[] for _ in range(ng)]
        for d, (rev, ga_ref, gl_ref, la_off, s_ref, o_ref) in enumerate(g_dirs):
            for p in range(npair):
                qs, ks, las = (slice(o + p * LANES, o + (p + 1) * LANES) for o in (0, Q, la_off))
                vs = slice(2 * Q + p * 2 * LANES, 2 * Q + (p + 1) * 2 * LANES)
                os_ = slice(R + p * 2 * LANES, R + (p + 1) * 2 * LANES)
                for j in range(ng):
                    c = ng - 1 - j if rev else j
                    rows = slice(c * chunk, (c + 1) * chunk)
                    items[j].append((ga_ref[0, rows, qs], ga_ref[0, rows, ks], ga_ref[0, rows, vs],
                                     gl_ref[0, rows, las], rev))
                    dests[j].append((o_ref, s_ref, d, p, rows, os_))
        parts = yield from intra([it for its in items for it in its], cst)
        nu = len(items[0])
        sts = [g_init[d][p] for (_, _, d, p, _, _) in dests[0]]
        for j in range(ng):
            new = []
            for (o_ref, _, _, _, rows, vs), (o_intra, q_in, upd, e_tot), st in zip(
                    dests[j], parts[j * nu:(j + 1) * nu], sts):
                o_ref[0, rows, vs] = o_intra + _mm(q_in, st, _NT, P_GSTATE)
                new.append(st * e_tot + upd)
            sts = new
            yield
        for (_, s_ref, _, p, _, _), st in zip(dests[0], sts):
            s_ref[0, p] = st

    _interleave([(RWKV_TASK_LAG * j, rwkv_task(j)) for j in range(nr)]
                + [(GLA_TASK_LAG, gla_task(GLA_WIDE_CHUNK, _gla_factored, _gla_wide_consts()))])
    for (_, s_ref, p, _, _), s in zip(r_dests[0], r_states[nr]):
        s_ref[0, p] = s

    @pl.when(jnp.logical_not(gla_split_ok))
    def _():
        _interleave([(0, gla_task(GLA_CHUNK, _gla_intra, _gla_consts()))])


def _scan(ra, rd_f, rd_b, ga, gl, s_rf, s_rb, s_gf, s_gb):
    B, L, _ = ra.shape
    R = RWKV_WIDTH
    Q = GLA_QK_WIDTH
    V = GLA_V_WIDTH
    T = SCAN_BLOCK
    n = L // T

    def fwd(w):
        return pl.BlockSpec((1, T, w), lambda b, i: (b, i, 0))

    def bwd(w):
        return pl.BlockSpec((1, T, w), lambda b, i: (b, n - 1 - i, 0))

    rst = pl.BlockSpec((1, R // LANES, LANES, LANES), lambda b, i: (b, 0, 0, 0))
    gst = pl.BlockSpec((1, Q // LANES, 2 * LANES, LANES), lambda b, i: (b, 0, 0, 0))
    return pl.pallas_call(
        _scan_kernel,
        grid=(B, n),
        in_specs=[fwd(3 * R), fwd(3 * R), bwd(3 * R), bwd(3 * R),
                  fwd(2 * Q + V), fwd(2 * Q), bwd(2 * Q + V), bwd(2 * Q), rst, rst, gst, gst],
        out_specs=[fwd(R + V), bwd(R + V), rst, rst, gst, gst],
        out_shape=[jax.ShapeDtypeStruct((B, L, R + V), F32)] * 2
        + [jax.ShapeDtypeStruct((B, R // LANES, LANES, LANES), F32)] * 2
        + [jax.ShapeDtypeStruct((B, Q // LANES, 2 * LANES, LANES), F32)] * 2,
        compiler_params=pltpu.CompilerParams(dimension_semantics=("parallel", "arbitrary"),
                                             vmem_limit_bytes=VMEM_LIMIT),
        name="scan",
    )(ra, rd_f, ra, rd_b, ga, gl, ga, gl, s_rf, s_rb, s_gf, s_gb)


def _post_kernel(x_ref, mod_ref, yof_ref, yob_ref, pg_ref,
                 lng_ref, lnb_ref, gng_ref, n2g_ref, fng_ref, wout_ref, w1_ref, w2_ref, o_ref):
    D = D_MODEL
    R = RWKV_WIDTH
    m = mod_ref[0]
    gt1 = m[:, 2 * D:3 * D]
    sh2 = m[:, 3 * D:4 * D]
    sc2 = m[:, 4 * D:5 * D]
    gt2 = m[:, 5 * D:6 * D]
    inv_n = 1.0 / RWKV_HEAD_DIM
    y = yof_ref[0, :, 0:R] + yob_ref[0, :, 0:R]
    mu = _head_sum(y) * inv_n
    yc = y - mu
    var = _head_sum(yc * yc) * inv_n
    yn = yc * lax.rsqrt(var + LNX_EPS) * lng_ref[...] + lnb_ref[...]
    rw = (yn + pg_ref[0, :, R:2 * R]) * pg_ref[0, :, 0:R]
    o = yof_ref[0, :, R:] + yob_ref[0, :, R:]
    gsil = pg_ref[0, :, 2 * R:]
    gng = gng_ref[...]
    parts = [rw]
    for hh in range(GLA_HEADS):
        sl = slice(hh * GLA_VAL_DIM, (hh + 1) * GLA_VAL_DIM)
        oh = o[:, sl]
        oh = oh * lax.rsqrt(jnp.mean(oh * oh, axis=-1, keepdims=True) + GLA_NORM_EPS)
        parts.append(oh * gng * gsil[:, sl])
    mix = jnp.concatenate(parts, axis=-1)
    x1 = x_ref[0] + gt1 * _bdot(mix, wout_ref)
    h2 = _rms(x1) * n2g_ref[...] * (1.0 + sc2) + sh2
    f = jnp.maximum(_bdot(h2, w1_ref), 0.0)
    x2 = x1 + gt2 * _bdot(f * f, w2_ref)
    o_ref[0] = _rms(x2) * fng_ref[...]


def _post(x, mod3, mod_row, yo_f, yo_b, pg, vecs, mats):
    B, L, D = x.shape
    tb = min(POST_BLOCK, L)
    nblk = L // tb

    def tok(w):
        return pl.BlockSpec((1, tb, w), lambda b, i: (b, i, 0))

    in_specs = [tok(D), pl.BlockSpec((1, 1, N_MOD * D), lambda b, i: (mod_row(b), 0, 0))]
    in_specs += [tok(yo_f.shape[-1]), tok(yo_b.shape[-1]), tok(pg.shape[-1])]
    in_specs += [_const_spec(w.shape) for w in vecs]
    in_specs += [pl.BlockSpec(w.shape, lambda b, i: (0, 0), pipeline_mode=pl.Buffered(1)) for w in mats]
    return pl.pallas_call(
        _post_kernel,
        grid=(B, nblk),
        in_specs=in_specs,
        out_specs=tok(D),
        out_shape=jax.ShapeDtypeStruct((B, L, D), F32),
        compiler_params=pltpu.CompilerParams(dimension_semantics=("parallel", "arbitrary"),
                                             vmem_limit_bytes=VMEM_LIMIT),
        name="post",
    )(x, mod3, yo_f, yo_b, pg, *vecs, *mats)


def _rwkv_state_to_pairs(s):
    B = s.shape[0]
    s = s.reshape(B, 4, 2, 64, 64)
    z = jnp.zeros_like(s[:, :, 0])
    top = jnp.concatenate([s[:, :, 0], z], axis=-1)
    bot = jnp.concatenate([z, s[:, :, 1]], axis=-1)
    return jnp.concatenate([top, bot], axis=-2)


def _rwkv_pairs_to_state(sp):
    B = sp.shape[0]
    a = sp[:, :, 0:64, 0:64]
    b = sp[:, :, 64:128, 64:128]
    return jnp.stack([a, b], axis=2).reshape(B, 8, 64, 64)


def _gla_state_to_pairs(s):
    B = s.shape[0]
    st = jnp.swapaxes(s, -1, -2).reshape(B, 2, 2, 128, 64)
    z = jnp.zeros_like(st[:, :, 0])
    top = jnp.concatenate([st[:, :, 0], z], axis=-1)
    bot = jnp.concatenate([z, st[:, :, 1]], axis=-1)
    return jnp.concatenate([top, bot], axis=-2)


def _gla_pairs_to_state(sp):
    B = sp.shape[0]
    a = sp[:, :, 0:128, 0:64]
    b = sp[:, :, 128:256, 64:128]
    st = jnp.stack([a, b], axis=2).reshape(B, 4, 128, 64)
    return jnp.swapaxes(st, -1, -2)


def _blockdiag2(a, b):
    za = jnp.zeros((a.shape[0], b.shape[1]), a.dtype)
    zb = jnp.zeros((b.shape[0], a.shape[1]), a.dtype)
    return jnp.concatenate([jnp.concatenate([a, za], axis=1), jnp.concatenate([zb, b], axis=1)], axis=0)


def kernel(x_prompt, x_sample, c, state_rwkv_fwd, state_rwkv_bwd, state_gla_fwd, state_gla_bwd, c_ctx, ada_w, ada_b, norm1_g, norm2_g, w_in, rwkv_mu_rkv, rwkv_mu_wag, rwkv_w0, rwkv_w1, rwkv_w2, rwkv_a0, rwkv_a1, rwkv_a2, rwkv_g1, rwkv_g2, rwkv_k_k, rwkv_k_a, rwkv_r_k, rwkv_lnx_g, rwkv_lnx_b, gla_gk1, gla_gk2, gla_gk_b, gla_norm_g, w_out, mlp_w1, mlp_w2, final_norm_g):
    D = D_MODEL
    R = RWKV_WIDTH
    nb = x_prompt.shape[0]
    nd = x_sample.shape[0]
    assert ada_w.shape[0] == 1, "single-layer step"
    layer = 0

    craw = jnp.concatenate([c_ctx[None, :], c, jnp.zeros((8 - 1 - nd, D), F32)], axis=0)
    mod = _modulation(craw, ada_w[layer], ada_b[layer][None, :])
    mod3 = mod.reshape(8, 1, N_MOD * D)

    bf = lambda t: t.astype(BF16)
    row = lambda t: t.reshape(1, -1).astype(F32)
    w_in_l = w_in[layer]
    pre_w = [
        row(norm1_g[layer]),
        bf(w_in_l[:, :3 * R]),
        bf(w_in_l[:, 3 * R:]),
        bf(jnp.concatenate([gla_gk1[layer, 0], gla_gk1[layer, 1]], axis=1)),
        bf(jnp.concatenate([rwkv_w1[layer, 0], rwkv_w1[layer, 1]], axis=1)),
        bf(jnp.concatenate([rwkv_a1[layer, 0], rwkv_a1[layer, 1]], axis=1)),
        bf(rwkv_g1[layer]),
        bf(_blockdiag2(rwkv_w2[layer, 0], rwkv_w2[layer, 1])),
        bf(_blockdiag2(rwkv_a2[layer, 0], rwkv_a2[layer, 1])),
        bf(rwkv_g2[layer]),
        bf(_blockdiag2(gla_gk2[layer, 0], gla_gk2[layer, 1])),
        row(rwkv_mu_rkv[layer]),
        rwkv_mu_wag[layer],
        row(rwkv_w0[layer]),
        row(rwkv_a0[layer]),
        row(gla_gk_b[layer]),
        row(rwkv_k_k[layer]),
        row(rwkv_k_a[layer]),
        row(rwkv_r_k[layer]),
    ]
    post_vecs = [row(rwkv_lnx_g[layer]), row(rwkv_lnx_b[layer]), row(gla_norm_g[layer]),
                 row(norm2_g[layer]), row(final_norm_g)]
    post_mats = [bf(w_out[layer]), bf(mlp_w1[layer]), bf(mlp_w2[layer])]

    def run_group(x, mod_row, grid_shift, s_rf, s_rb, s_gf, s_gb):
        ra, rd_f, rd_b, ga, gl, pg = _pre(x, mod3, mod_row, pre_w, grid_shift)
        yo_f, yo_b, n_rf, n_rb, n_gf, n_gb = _scan(ra, rd_f, rd_b, ga, gl, s_rf, s_rb, s_gf, s_gb)
        y = _post(x, mod3, mod_row, yo_f, yo_b, pg, post_vecs, post_mats)
        return y, (n_rf, n_rb, n_gf, n_gb)

    zr = jnp.zeros((nb, R // LANES, LANES, LANES), F32)
    zg = jnp.zeros((nb, GLA_QK_WIDTH // LANES, 2 * LANES, LANES), F32)
    y_prompt, (n_rf, n_rb, n_gf, n_gb) = run_group(x_prompt, lambda b: 0, False, zr, zr, zg, zg)
    y_sample, _ = run_group(x_sample, lambda b: b + 1, True,
                            _rwkv_state_to_pairs(state_rwkv_fwd[:, layer]),
                            _rwkv_state_to_pairs(state_rwkv_bwd[:, layer]),
                            _gla_state_to_pairs(state_gla_fwd[:, layer]),
                            _gla_state_to_pairs(state_gla_bwd[:, layer]))
    return (y_prompt, y_sample,
            _rwkv_pairs_to_state(n_rf)[:, None], _rwkv_pairs_to_state(n_rb)[:, None],
            _gla_pairs_to_state(n_gf)[:, None], _gla_pairs_to_state(n_gb)[:, None])
```

```python
import functools

import jax
import jax.numpy as jnp
from jax import lax
from jax.experimental import pallas as pl
from jax.experimental.pallas import tpu as pltpu

F32 = jnp.float32
BF16 = jnp.bfloat16
HI = lax.Precision.HIGHEST

D_MODEL = 1024
GRID_W = 64
RWKV_WIDTH = 512
RWKV_HEAD_DIM = 64
RWKV_HEADS = 8
GLA_HEADS = 4
GLA_KEY_DIM = 64
GLA_VAL_DIM = 128
GLA_QK_WIDTH = 256
GLA_V_WIDTH = 512
GLA_GATE_NORMALIZER = 16.0
N_MOD = 6
RMS_EPS = 1e-6
LNX_EPS = 64e-5
GLA_NORM_EPS = 1e-5

LANES = 128
RWKV_CHUNK = 64
GLA_CHUNK = 16
GLA_WIDE_CHUNK = 64
GLA_SAFE_LOG = 60.0
SCAN_BLOCK = 128
PRE_BLOCK = 256
POST_BLOCK = 512
POST_SPLIT = 2
VMEM_LIMIT = 56 * 1024 * 1024
RWKV_TASK_LAG = 2
GLA_TASK_LAG = 14
DECAY_LOG_SCALE = 0.6065306597126334

P_SCORE = "b1"
P_INV = "b1"
P_SOLVE = "b1"
P_STATE = "b1"
P_OUT = "b1"
P_GSTATE = "b1"
P_GATT = "b1"


def _dot(a, b, prec=None):
    return lax.dot_general(a, b, (((1,), (0,)), ((), ())), precision=prec, preferred_element_type=F32)


def _dot_nt(a, b, prec=None):
    return lax.dot_general(a, b, (((1,), (1,)), ((), ())), precision=prec, preferred_element_type=F32)


def _dot_tn(a, b, prec=None):
    return lax.dot_general(a, b, (((0,), (0,)), ((), ())), precision=prec, preferred_element_type=F32)


_NN = (((1,), (0,)), ((), ()))
_NT = (((1,), (1,)), ((), ()))
_TN = (((0,), (0,)), ((), ()))


def _split2(x):
    hi = x.astype(BF16)
    return hi, (x - hi.astype(F32)).astype(BF16)


def _mm(a, b, dims, mode):
    dg = functools.partial(lax.dot_general, dimension_numbers=dims, preferred_element_type=F32)
    if mode == "hi":
        return dg(a, b, precision=HI)
    if mode == "b1":
        return dg(a.astype(BF16), b.astype(BF16))
    a1, a2 = _split2(a)
    b1, b2 = _split2(b)
    return dg(a1, b1) + dg(a1, b2) + dg(a2, b1)


def _cumsum_mm(tri, x):
    hi = x.astype(BF16)
    r1 = x - hi.astype(F32)
    mid = r1.astype(BF16)
    lo = (r1 - mid.astype(F32)).astype(BF16)
    return _dot(tri, hi) + _dot(tri, mid) + _dot(tri, lo)


def _bdot(a, w_ref):
    return _dot(a.astype(BF16), w_ref[...])


def _sigmoid(x):
    return 1.0 / (1.0 + jnp.exp(-x))


def _softplus(x):
    return jnp.maximum(x, 0.0) + jnp.log(1.0 + jnp.exp(-jnp.abs(x)))


def _head_sum(x):
    first = lax.broadcasted_iota(jnp.int32, (1, LANES), 1) < RWKV_HEAD_DIM
    outs = []
    for j in range(x.shape[-1] // LANES):
        xb = x[:, j * LANES:(j + 1) * LANES]
        s0 = jnp.sum(jnp.where(first, xb, 0.0), axis=-1, keepdims=True)
        s1 = jnp.sum(jnp.where(first, 0.0, xb), axis=-1, keepdims=True)
        outs.append(jnp.where(first, s0, s1))
    return jnp.concatenate(outs, axis=-1)


def _rms(x):
    return x * lax.rsqrt(jnp.mean(x * x, axis=-1, keepdims=True) + RMS_EPS)


def _mod_kernel(c_ref, w_ref, b_ref, o_ref):
    c = c_ref[...]
    cond = c * _sigmoid(c)
    o_ref[...] = _dot(cond.astype(BF16), w_ref[...].astype(BF16)) + b_ref[...]


def _modulation(craw, ada_w, ada_b):
    n = ada_w.shape[1]
    bn = 1536
    return pl.pallas_call(
        _mod_kernel,
        grid=(n // bn,),
        in_specs=[
            pl.BlockSpec((8, D_MODEL), lambda j: (0, 0)),
            pl.BlockSpec((D_MODEL, bn), lambda j: (0, j)),
            pl.BlockSpec((1, bn), lambda j: (0, j)),
        ],
        out_specs=pl.BlockSpec((8, bn), lambda j: (0, j)),
        out_shape=jax.ShapeDtypeStruct((8, n), F32),
        compiler_params=pltpu.CompilerParams(dimension_semantics=("arbitrary",), vmem_limit_bytes=VMEM_LIMIT),
        name="modulation",
    )(craw, ada_w, ada_b)


def _pre_kernel(*refs, grid_shift, tb):
    if grid_shift:
        x_ref, xp_ref, xn_ref = refs[:3]
        refs = refs[3:]
    else:
        x_ref = refs[0]
        refs = refs[1:]
    (mod_ref, n1g_ref, wrkv_ref, wrest_ref, gk1_ref, w1_ref, a1_ref, g1_ref, w2_ref, a2_ref, g2_ref,
     gk2_ref, murkv_ref, muwag_ref, w0_ref, a0_ref, gkb_ref, kk_ref, ka_ref, rk_ref) = refs[:20]
    ra_o, rf_o, rb_o, ga_o, gl_o, pg_o = refs[20:]
    D = D_MODEL
    R = RWKV_WIDTH
    m = mod_ref[0]
    sh1 = m[:, 0:D]
    sc1 = m[:, D:2 * D]
    n1g = n1g_ref[...]

    def normmod(xx):
        return _rms(xx) * n1g * (1.0 + sc1) + sh1

    h = normmod(x_ref[0])
    row = lax.broadcasted_iota(jnp.int32, (tb, 1), 0)
    if grid_shift:
        i = pl.program_id(1)
        n = pl.num_programs(1)
        hp = normmod(xp_ref[0]) * (i > 0).astype(F32)
        hn = normmod(xn_ref[0]) * (i < n - 1).astype(F32)
        hext = jnp.concatenate([hp, h, hn], axis=0)
        col = row % GRID_W
        m_l = (col != 0).astype(F32)
        m_r = (col != GRID_W - 1).astype(F32)
        ne = tb + 2 * GRID_W

        def shift(ext):
            up = ext[0:tb]
            down = ext[2 * GRID_W:2 * GRID_W + tb]
            left = pltpu.roll(ext, 1, 0)[GRID_W:GRID_W + tb]
            right = pltpu.roll(ext, ne - 1, 0)[GRID_W:GRID_W + tb]
            return 0.25 * (up + down + m_l * left + m_r * right)

        halo = GRID_W
    else:
        hext = h
        m_l = (row != 0).astype(F32)
        m_r = (row != tb - 1).astype(F32)

        def shift(ext):
            return 0.5 * (m_l * pltpu.roll(ext, 1, 0) + m_r * pltpu.roll(ext, tb - 1, 0))

        halo = 0

    rest = _bdot(h, wrest_ref)
    dh = shift(hext) - h
    mu = muwag_ref[...]
    lora_w = _bdot(h + mu[0:1] * dh, w1_ref)
    lora_a = _bdot(h + mu[1:2] * dh, a1_ref)
    lora_g = _bdot(h + mu[2:3] * dh, g1_ref)
    lora_gk = _bdot(h, gk1_ref)
    z = w0_ref[...] + _bdot(jnp.tanh(lora_w), w2_ref)
    a = _sigmoid(a0_ref[...] + _bdot(lora_a, a2_ref))
    gate = _bdot(_sigmoid(lora_g), g2_ref)
    logits = _bdot(lora_gk, gk2_ref) + gkb_ref[...]
    rkv_ext = _bdot(hext, wrkv_ref)

    Q = GLA_QK_WIDTH
    ga_o[0, :, 0:Q] = rest[:, 0:Q] * (GLA_KEY_DIM ** -0.5)
    ga_o[0, :, Q:] = rest[:, Q:2 * Q + GLA_V_WIDTH]
    gg = rest[:, 2 * Q + GLA_V_WIDTH:]
    pg_o[0, :, 2 * R:] = gg * _sigmoid(gg)
    gl_o[0] = -_softplus(-logits) * (1.0 / GLA_GATE_NORMALIZER)

    rkv = rkv_ext[halo:halo + tb]
    rkv = rkv + murkv_ref[...] * (shift(rkv_ext) - rkv)
    r = rkv[:, 0:R]
    k = rkv[:, R:2 * R]
    v = rkv[:, 2 * R:3 * R]
    lw = -DECAY_LOG_SCALE * _sigmoid(z)

    kap = k * kk_ref[...]
    kap = kap * lax.rsqrt(jnp.maximum(_head_sum(kap * kap), 1e-12))
    ka = ka_ref[...]
    a_f = a[:, 0:R]
    a_b = a[:, R:2 * R]
    kd_f = k * (1.0 + (a_f - 1.0) * ka)
    kd_b = k * (1.0 + (a_b - 1.0) * ka)
    bonus = _head_sum(r * (kd_f + kd_b) * rk_ref[...]) * v

    ra_o[0, :, 0:R] = r
    ra_o[0, :, R:2 * R] = v
    ra_o[0, :, 2 * R:3 * R] = kap
    rf_o[0, :, 0:R] = kd_f
    rf_o[0, :, R:2 * R] = a_f * kap
    rf_o[0, :, 2 * R:3 * R] = lw[:, 0:R]
    rb_o[0, :, 0:R] = kd_b
    rb_o[0, :, R:2 * R] = a_b * kap
    rb_o[0, :, 2 * R:3 * R] = lw[:, R:2 * R]
    pg_o[0, :, 0:R] = gate
    pg_o[0, :, R:2 * R] = bonus


def _const_spec(shape):
    nd = len(shape)
    return pl.BlockSpec(shape, lambda b, i: (0,) * nd)


def _pre(x, mod3, mod_row, weights, grid_shift):
    B, L, D = x.shape
    tb = PRE_BLOCK
    nblk = L // tb
    if not grid_shift:
        assert nblk == 1
    x_spec = pl.BlockSpec((1, tb, D), lambda b, i: (b, i, 0))
    in_specs = [x_spec]
    args = [x]
    if grid_shift:
        per = tb // GRID_W
        nrow = L // GRID_W
        in_specs += [
            pl.BlockSpec((1, GRID_W, D), lambda b, i: (b, jnp.maximum(i * per - 1, 0), 0)),
            pl.BlockSpec((1, GRID_W, D), lambda b, i: (b, jnp.minimum((i + 1) * per, nrow - 1), 0)),
        ]
        args += [x, x]
    in_specs.append(pl.BlockSpec((1, 1, N_MOD * D), lambda b, i: (mod_row(b), 0, 0)))
    args.append(mod3)
    for w in weights:
        in_specs.append(_const_spec(w.shape))
        args.append(w)
    R = RWKV_WIDTH
    widths = [3 * R, 3 * R, 3 * R, 2 * GLA_QK_WIDTH + GLA_V_WIDTH, 2 * GLA_QK_WIDTH, 2 * R + GLA_V_WIDTH]
    out_specs = [pl.BlockSpec((1, tb, w), lambda b, i: (b, i, 0)) for w in widths]
    out_shape = [jax.ShapeDtypeStruct((B, L, w), F32) for w in widths]
    return pl.pallas_call(
        functools.partial(_pre_kernel, grid_shift=grid_shift, tb=tb),
        grid=(B, nblk),
        in_specs=in_specs,
        out_specs=out_specs,
        out_shape=out_shape,
        compiler_params=pltpu.CompilerParams(dimension_semantics=("parallel", "arbitrary"),
                                             vmem_limit_bytes=VMEM_LIMIT),
        name="pre_grid" if grid_shift else "pre_seq",
    )(*args)


def _interleave(tasks):
    live = list(tasks)
    rnd = 0
    while live:
        keep = []
        for first, gen in live:
            if rnd >= first:
                try:
                    next(gen)
                except StopIteration:
                    continue
            keep.append((first, gen))
        live = keep
        rnd += 1


def _inv_unit_triangular(lms, eye, blk16, blk32):
    mm = functools.partial(_mm, dims=_NN, mode=P_INV)
    n = lms[0].shape[0]
    l0 = [jnp.where(blk16, lm, 0.0) for lm in lms]
    l2 = [mm(a, a) for a in l0]
    yield
    t = [eye - a for a in l0]
    s = [mm(jnp.concatenate([a, b], axis=0), b) for a, b in zip(t, l2)]
    yield
    t = [a + x[0:n] for a, x in zip(t, s)]
    l4 = [x[n:2 * n] for x in s]
    s = [mm(jnp.concatenate([a, b], axis=0), b) for a, b in zip(t, l4)]
    yield
    t = [a + x[0:n] for a, x in zip(t, s)]
    l8 = [x[n:2 * n] for x in s]
    t = [a + mm(a, b) for a, b in zip(t, l8)]
    yield
    off1_mask = jnp.logical_and(blk32, jnp.logical_not(blk16))
    x = [mm(a, jnp.where(off1_mask, lm, 0.0)) for a, lm in zip(t, lms)]
    yield
    t = [a - mm(b, a) for a, b in zip(t, x)]
    yield
    x = [mm(a, jnp.where(blk32, 0.0, lm)) for a, lm in zip(t, lms)]
    yield
    t = [a - mm(b, a) for a, b in zip(t, x)]
    yield
    return t


def _rwkv_prepare(units, cst):
    C = RWKV_CHUNK
    m0, cm0, eye, blk16, blk32, _ = cst[2]

    def stack(x):
        return jnp.concatenate([jnp.where(m0, x, 0.0), jnp.where(m0, 0.0, x)], axis=0)

    def blockdiag(side):
        return jnp.concatenate([jnp.where(cm0, side, 0.0), jnp.where(cm0, 0.0, side)], axis=0)

    cums = [_cumsum_mm(cst[rev][0], lw) for (_, _, _, _, _, lw, rev) in units]
    yield
    prep = []
    for (r, k, v, kap, b, lw, rev), cum in zip(units, cums):
        cumx = cum - lw
        if rev:
            mid = cum[C // 2:C // 2 + 1]
            tot = cum[0:1]
        else:
            mid = cum[C // 2 - 1:C // 2]
            tot = cum[C - 1:C]
        e_mid = jnp.exp(mid)
        rt = r * jnp.exp(cum - mid)
        kt = kap * jnp.exp(cumx - mid)
        es = jnp.exp(mid - cum)
        e_end = jnp.exp(tot - mid)
        prep.append(dict(rt=rt, kt=kt, bh=b * es, kh=k * es, r0=rt * e_mid, k0=stack(kt * e_mid),
                         be=b * es * e_end, ke=k * es * e_end, vs=stack(v), v=v,
                         e_tot=jnp.exp(tot), strict2=cst[rev][1], incl2=cst[rev][2]))
    gs = [_mm(jnp.concatenate([p["kt"], p["rt"]], axis=0),
              jnp.concatenate([stack(p["bh"]), stack(p["kh"])], axis=0), _NT, P_SCORE) for p in prep]
    yield
    for p, g in zip(prep, gs):
        p["ab"] = blockdiag(jnp.where(p["strict2"], g[0:C, 0:2 * C], 0.0))
        p["ak"] = blockdiag(jnp.where(p["strict2"], g[0:C, 2 * C:4 * C], 0.0))
        p["rb"] = jnp.where(p["incl2"], g[C:2 * C, 0:2 * C], 0.0)
        p["rk"] = jnp.where(p["incl2"], g[C:2 * C, 2 * C:4 * C], 0.0)
    av = [_mm(jnp.concatenate([p["ak"], p["rk"]], axis=0), p["vs"], _NN, P_SOLVE) for p in prep]
    yield
    ts = yield from _inv_unit_triangular([p["ab"] for p in prep], eye, blk16, blk32)
    wu = [_mm(t, jnp.concatenate([p["k0"], a[0:2 * C]], axis=1), _NN, P_SOLVE) for t, p, a in zip(ts, prep, av)]
    yield
    return [dict(wr=jnp.concatenate([x[:, :LANES], p["r0"]], axis=0), u0=x[:, LANES:], yv=a[2 * C:3 * C],
                 rb=p["rb"], v=p["v"], bke=jnp.concatenate([p["be"], p["ke"]], axis=0), e_tot=p["e_tot"])
            for x, a, p in zip(wu, av, prep)]


def _rwkv_apply(prep, states, cst):
    C = RWKV_CHUNK
    blk64 = cst[2][5]
    ws = [_mm(p["wr"], s, _NT, P_STATE) for p, s in zip(prep, states)]
    yield
    us = [-p["u0"] - w[0:2 * C] for p, w in zip(prep, ws)]
    upd = [_mm(jnp.concatenate([u[0:C] + u[C:2 * C], p["v"]], axis=0), p["bke"], _TN, P_STATE)
           for u, p in zip(us, prep)]
    yield
    s_new = [s * p["e_tot"] + jnp.where(blk64, d, 0.0) for p, s, d in zip(prep, states, upd)]
    ys = [w[2 * C:3 * C] + _mm(p["rb"], u, _NN, P_OUT) + p["yv"] for p, w, u in zip(prep, ws, us)]
    yield
    return ys, s_new


def _rwkv_consts():
    C = RWKV_CHUNK
    row = lax.broadcasted_iota(jnp.int32, (C, C), 0)
    col = lax.broadcasted_iota(jnp.int32, (C, C), 1)
    row2 = lax.broadcasted_iota(jnp.int32, (C, 2 * C), 0)
    col2 = lax.broadcasted_iota(jnp.int32, (C, 2 * C), 1)
    cs = col2 % C
    lane = lax.broadcasted_iota(jnp.int32, (1, LANES), 1)
    m0 = lane < RWKV_HEAD_DIM
    cm0 = lax.broadcasted_iota(jnp.int32, (1, 2 * C), 1) < C
    rr = lax.broadcasted_iota(jnp.int32, (2 * C, 2 * C), 0)
    cc = lax.broadcasted_iota(jnp.int32, (2 * C, 2 * C), 1)
    eye = (rr == cc).astype(F32)
    blk16 = (rr // 16) == (cc // 16)
    blk32 = (rr // 32) == (cc // 32)
    blk64 = (rr // 64) == (cc // 64)
    fwd = ((col <= row).astype(BF16), cs < row2, cs <= row2)
    bwd = ((col >= row).astype(BF16), cs > row2, cs >= row2)
    return fwd, bwd, (m0, cm0, eye, blk16, blk32, blk64)


def _gla_intra(items, cst):
    G = GLA_CHUNK
    ind_v, sel, blk = cst[2]
    srow = lax.broadcasted_iota(jnp.int32, (G, 1), 0)
    cums = [_cumsum_mm(cst[rev], la) for (_, _, _, la, rev) in items]
    pmats = []
    for (q, k, v, la, rev), cum in zip(items, cums):
        ps = []
        for t in range(G):
            msk = (srow >= t) if rev else (srow <= t)
            e = jnp.exp(jnp.where(msk, cum[t:t + 1] - cum, 0.0))
            ps.append(jnp.where(msk, e * (q[t:t + 1] * k), 0.0))
        pmats.append(jnp.concatenate(ps, axis=0))
    atts = [_mm(pm, ind_v, _NN, P_GATT) for pm in pmats]
    o_intra = [_mm(sel, att * jnp.concatenate([it[2]] * G, axis=0), _NN, P_GATT)
               for att, it in zip(atts, items)]
    out = []
    for (q, k, v, la, rev), cum, oi in zip(items, cums, o_intra):
        tot = cum[0:1] if rev else cum[G - 1:G]
        upd = _mm(v, k * jnp.exp(tot - cum), _TN, P_GSTATE)
        out.append((oi, q * jnp.exp(cum), jnp.where(blk, upd, 0.0), jnp.exp(tot)))
    yield
    return out


def _gla_consts():
    G = GLA_CHUNK
    row = lax.broadcasted_iota(jnp.int32, (G, G), 0)
    col = lax.broadcasted_iota(jnp.int32, (G, G), 1)
    kc = lax.broadcasted_iota(jnp.int32, (LANES, 2 * LANES), 0)
    vc = lax.broadcasted_iota(jnp.int32, (LANES, 2 * LANES), 1)
    ind_v = ((kc // GLA_KEY_DIM) == (vc // GLA_VAL_DIM)).astype(F32)
    st = lax.broadcasted_iota(jnp.int32, (G, G * G), 0)
    sj = lax.broadcasted_iota(jnp.int32, (G, G * G), 1)
    sel = ((sj // G) == st).astype(F32)
    br = lax.broadcasted_iota(jnp.int32, (2 * LANES, LANES), 0)
    bc = lax.broadcasted_iota(jnp.int32, (2 * LANES, LANES), 1)
    blk = (br // GLA_VAL_DIM) == (bc // GLA_KEY_DIM)
    return (col <= row).astype(BF16), (col >= row).astype(BF16), (ind_v, sel, blk)


def _gla_factored(items, cst):
    F = GLA_WIDE_CHUNK
    m0, vm0, blk = cst[2]
    cums = [_cumsum_mm(cst[rev][0], la) for (_, _, _, la, rev) in items]
    yield
    scores = []
    for (q, k, v, la, rev), cum in zip(items, cums):
        mid = cum[F // 2:F // 2 + 1] if rev else cum[F // 2 - 1:F // 2]
        kt = k * jnp.exp(mid - cum)
        kstk = jnp.concatenate([jnp.where(m0, kt, 0.0), jnp.where(m0, 0.0, kt)], axis=0)
        scores.append(_mm(q * jnp.exp(cum - mid), kstk, _NT, P_GATT))
    yield
    o_intra = []
    for (q, k, v, la, rev), sc in zip(items, scores):
        vstk = jnp.concatenate([jnp.where(vm0, v, 0.0), jnp.where(vm0, 0.0, v)], axis=0)
        o_intra.append(_mm(jnp.where(cst[rev][1], sc, 0.0), vstk, _NN, P_GATT))
    yield
    out = []
    for (q, k, v, la, rev), cum, oi in zip(items, cums, o_intra):
        tot = cum[0:1] if rev else cum[F - 1:F]
        upd = _mm(v, k * jnp.exp(tot - cum), _TN, P_GSTATE)
        out.append((oi, q * jnp.exp(cum), jnp.where(blk, upd, 0.0), jnp.exp(tot)))
    yield
    return out


def _gla_wide_consts():
    F = GLA_WIDE_CHUNK
    row = lax.broadcasted_iota(jnp.int32, (F, F), 0)
    col = lax.broadcasted_iota(jnp.int32, (F, F), 1)
    row2 = lax.broadcasted_iota(jnp.int32, (F, 2 * F), 0)
    cs = lax.broadcasted_iota(jnp.int32, (F, 2 * F), 1) % F
    m0 = lax.broadcasted_iota(jnp.int32, (1, LANES), 1) < GLA_KEY_DIM
    vm0 = lax.broadcasted_iota(jnp.int32, (1, 2 * LANES), 1) < GLA_VAL_DIM
    br = lax.broadcasted_iota(jnp.int32, (2 * LANES, LANES), 0)
    bc = lax.broadcasted_iota(jnp.int32, (2 * LANES, LANES), 1)
    blk = (br // GLA_VAL_DIM) == (bc // GLA_KEY_DIM)
    return (((col <= row).astype(BF16), cs <= row2), ((col >= row).astype(BF16), cs >= row2), (m0, vm0, blk))


def _scan_kernel(raf_ref, rdf_ref, rab_ref, rdb_ref, gaf_ref, glf_ref, gab_ref, glb_ref, *refs):
    yof_ref, yob_ref, srf_ref, srb_ref, sgf_ref, sgb_ref = refs[-6:]
    state_refs = (srf_ref, srb_ref, sgf_ref, sgb_ref)
    init_refs = refs[:-6]
    i = pl.program_id(1)

    @pl.when(i == 0)
    def _():
        for k, s_ref in enumerate(state_refs):
            s_ref[...] = init_refs[k][...] if init_refs else jnp.zeros(s_ref.shape, F32)

    T = SCAN_BLOCK
    C = RWKV_CHUNK
    nr = T // C
    rcst = _rwkv_consts()
    R = RWKV_WIDTH
    Q = GLA_QK_WIDTH
    g_dirs = ((False, gaf_ref, glf_ref, 0, sgf_ref, yof_ref),
              (True, gab_ref, glb_ref, Q, sgb_ref, yob_ref))
    la_tot = jnp.minimum(jnp.sum(glf_ref[0, :, 0:Q], axis=0, keepdims=True),
                         jnp.sum(glb_ref[0, :, Q:2 * Q], axis=0, keepdims=True))
    gla_split_ok = jnp.min(la_tot) >= -GLA_SAFE_LOG

    r_dirs = ((False, raf_ref, rdf_ref, srf_ref, yof_ref),
              (True, rab_ref, rdb_ref, srb_ref, yob_ref))
    r_units = {}
    r_dests = {}
    for d, (rev, ra_ref, rd_ref, s_ref, y_ref) in enumerate(r_dirs):
        for j in range(nr):
            c = nr - 1 - j if rev else j
            rows = slice(c * C, (c + 1) * C)
            r_units[d, j] = []
            r_dests[d, j] = []
            for p in range(R // LANES):
                sl, sl1, sl2 = (slice(o + p * LANES, o + (p + 1) * LANES) for o in (0, R, 2 * R))
                r_units[d, j].append((ra_ref[0, rows, sl], rd_ref[0, rows, sl], ra_ref[0, rows, sl1],
                                      ra_ref[0, rows, sl2], rd_ref[0, rows, sl1], rd_ref[0, rows, sl2], rev))
                r_dests[d, j].append((y_ref, s_ref, p, rows, sl))
    r_states = {(d, 0): [s_ref[0, p] for (_, s_ref, p, _, _) in r_dests[d, 0]] for d in range(len(r_dirs))}

    def rwkv_task(d, j):
        prep = yield from _rwkv_prepare(r_units[d, j], rcst)
        while (d, j) not in r_states:
            yield
        ys, r_states[d, j + 1] = yield from _rwkv_apply(prep, r_states[d, j], rcst)
        for (y_ref, _, _, rows, sl), y in zip(r_dests[d, j], ys):
            y_ref[0, rows, sl] = y

    npair = GLA_QK_WIDTH // LANES
    g_init = [[s_ref[0, p] for p in range(npair)] for (_, _, _, _, s_ref, _) in g_dirs]

    def gla_task(chunk, intra, cst):
        ng = T // chunk
        items = [[] for _ in range(ng)]
        dests = [[] for _ in range(ng)]
        for d, (rev, ga_ref, gl_ref, la_off, s_ref, o_ref) in enumerate(g_dirs):
            for p in range(npair):
                qs, ks, las = (slice(o + p * LANES, o + (p + 1) * LANES) for o in (0, Q, la_off))
                vs = slice(2 * Q + p * 2 * LANES, 2 * Q + (p + 1) * 2 * LANES)
                os_ = slice(R + p * 2 * LANES, R + (p + 1) * 2 * LANES)
                for j in range(ng):
                    c = ng - 1 - j if rev else j
                    rows = slice(c * chunk, (c + 1) * chunk)
                    items[j].append((ga_ref[0, rows, qs], ga_ref[0, rows, ks], ga_ref[0, rows, vs],
                                     gl_ref[0, rows, las], rev))
                    dests[j].append((o_ref, s_ref, d, p, rows, os_))
        parts = yield from intra([it for its in items for it in its], cst)
        nu = len(items[0])
        sts = [g_init[d][p] for (_, _, d, p, _, _) in dests[0]]
        for j in range(ng):
            new = []
            for (o_ref, _, _, _, rows, vs), (o_intra, q_in, upd, e_tot), st in zip(
                    dests[j], parts[j * nu:(j + 1) * nu], sts):
                o_ref[0, rows, vs] = o_intra + _mm(q_in, st, _NT, P_GSTATE)
                new.append(st * e_tot + upd)
            sts = new
            yield
        for (_, s_ref, _, p, _, _), st in zip(dests[0], sts):
            s_ref[0, p] = st

    nd = len(r_dirs)
    _interleave([(RWKV_TASK_LAG * j, rwkv_task(d, j)) for j in range(nr) for d in range(nd)]
                + [(GLA_TASK_LAG, gla_task(GLA_WIDE_CHUNK, _gla_factored, _gla_wide_consts()))])
    for d in range(nd):
        for (_, s_ref, p, _, _), s in zip(r_dests[d, 0], r_states[d, nr]):
            s_ref[0, p] = s

    @pl.when(jnp.logical_not(gla_split_ok))
    def _():
        _interleave([(0, gla_task(GLA_CHUNK, _gla_intra, _gla_consts()))])


def _scan(ra, rd_f, rd_b, ga, gl, states):
    B, L, _ = ra.shape
    R = RWKV_WIDTH
    Q = GLA_QK_WIDTH
    V = GLA_V_WIDTH
    T = SCAN_BLOCK
    n = L // T

    def fwd(w):
        return pl.BlockSpec((1, T, w), lambda b, i: (b, i, 0))

    def bwd(w):
        return pl.BlockSpec((1, T, w), lambda b, i: (b, n - 1 - i, 0))

    rst = pl.BlockSpec((1, R // LANES, LANES, LANES), lambda b, i: (b, 0, 0, 0))
    gst = pl.BlockSpec((1, Q // LANES, 2 * LANES, LANES), lambda b, i: (b, 0, 0, 0))
    states = () if states is None else tuple(states)
    return pl.pallas_call(
        _scan_kernel,
        grid=(B, n),
        in_specs=[fwd(3 * R), fwd(3 * R), bwd(3 * R), bwd(3 * R),
                  fwd(2 * Q + V), fwd(2 * Q), bwd(2 * Q + V), bwd(2 * Q)] + [rst, rst, gst, gst][:len(states)],
        out_specs=[fwd(R + V), bwd(R + V), rst, rst, gst, gst],
        out_shape=[jax.ShapeDtypeStruct((B, L, R + V), F32)] * 2
        + [jax.ShapeDtypeStruct((B, R // LANES, LANES, LANES), F32)] * 2
        + [jax.ShapeDtypeStruct((B, Q // LANES, 2 * LANES, LANES), F32)] * 2,
        compiler_params=pltpu.CompilerParams(dimension_semantics=("parallel", "arbitrary"),
                                             vmem_limit_bytes=VMEM_LIMIT),
        name="scan",
    )(ra, rd_f, ra, rd_b, ga, gl, ga, gl, *states)


def _post_kernel(x_ref, mod_ref, yof_ref, yob_ref, pg_ref,
                 lng_ref, lnb_ref, gng_ref, n2g_ref, fng_ref, wout_ref, w1_ref, w2_ref, o_ref):
    D = D_MODEL
    R = RWKV_WIDTH
    m = mod_ref[0]
    gt1 = m[:, 2 * D:3 * D]
    sh2 = m[:, 3 * D:4 * D]
    sc2 = m[:, 4 * D:5 * D]
    gt2 = m[:, 5 * D:6 * D]
    inv_n = 1.0 / RWKV_HEAD_DIM
    gng = gng_ref[...]
    tb = x_ref.shape[1]
    nsplit = POST_SPLIT if tb % (8 * POST_SPLIT) == 0 else 1
    rows = [slice(s * (tb // nsplit), (s + 1) * (tb // nsplit)) for s in range(nsplit)]

    def mixed(rs):
        y = yof_ref[0, rs, 0:R] + yob_ref[0, rs, 0:R]
        mu = _head_sum(y) * inv_n
        yc = y - mu
        var = _head_sum(yc * yc) * inv_n
        yn = yc * lax.rsqrt(var + LNX_EPS) * lng_ref[...] + lnb_ref[...]
        parts = [(yn + pg_ref[0, rs, R:2 * R]) * pg_ref[0, rs, 0:R]]
        o = yof_ref[0, rs, R:] + yob_ref[0, rs, R:]
        for hh in range(GLA_HEADS):
            sl = slice(hh * GLA_VAL_DIM, (hh + 1) * GLA_VAL_DIM)
            oh = o[:, sl]
            oh = oh * lax.rsqrt(jnp.mean(oh * oh, axis=-1, keepdims=True) + GLA_NORM_EPS)
            parts.append(oh * gng * pg_ref[0, rs, 2 * R + hh * GLA_VAL_DIM:2 * R + (hh + 1) * GLA_VAL_DIM])
        return jnp.concatenate(parts, axis=-1)

    mix = [mixed(rs) for rs in rows]
    x1 = [x_ref[0, rs] + gt1 * _bdot(mx, wout_ref) for rs, mx in zip(rows, mix)]
    h2 = [_rms(v) * n2g_ref[...] * (1.0 + sc2) + sh2 for v in x1]
    f = [jnp.maximum(_bdot(v, w1_ref), 0.0) for v in h2]
    x2 = [a + gt2 * _bdot(v * v, w2_ref) for a, v in zip(x1, f)]
    for rs, v in zip(rows, x2):
        o_ref[0, rs] = _rms(v) * fng_ref[...]


def _post(x, mod3, mod_row, yo_f, yo_b, pg, vecs, mats):
    B, L, D = x.shape
    tb = min(POST_BLOCK, L)
    nblk = L // tb

    def tok(w):
        return pl.BlockSpec((1, tb, w), lambda b, i: (b, i, 0))

    in_specs = [tok(D), pl.BlockSpec((1, 1, N_MOD * D), lambda b, i: (mod_row(b), 0, 0))]
    in_specs += [tok(yo_f.shape[-1]), tok(yo_b.shape[-1]), tok(pg.shape[-1])]
    in_specs += [_const_spec(w.shape) for w in vecs]
    in_specs += [pl.BlockSpec(w.shape, lambda b, i: (0, 0), pipeline_mode=pl.Buffered(1)) for w in mats]
    return pl.pallas_call(
        _post_kernel,
        grid=(B, nblk),
        in_specs=in_specs,
        out_specs=tok(D),
        out_shape=jax.ShapeDtypeStruct((B, L, D), F32),
        compiler_params=pltpu.CompilerParams(dimension_semantics=("parallel", "arbitrary"),
                                             vmem_limit_bytes=VMEM_LIMIT),
        name="post",
    )(x, mod3, yo_f, yo_b, pg, *vecs, *mats)


def _rwkv_state_to_pairs(s):
    B = s.shape[0]
    s = s.reshape(B, 4, 2, 64, 64)
    z = jnp.zeros_like(s[:, :, 0])
    top = jnp.concatenate([s[:, :, 0], z], axis=-1)
    bot = jnp.concatenate([z, s[:, :, 1]], axis=-1)
    return jnp.concatenate([top, bot], axis=-2)


def _rwkv_pairs_to_state(sp):
    B = sp.shape[0]
    a = sp[:, :, 0:64, 0:64]
    b = sp[:, :, 64:128, 64:128]
    return jnp.stack([a, b], axis=2).reshape(B, 8, 64, 64)


def _gla_state_to_pairs(s):
    B = s.shape[0]
    st = jnp.swapaxes(s, -1, -2).reshape(B, 2, 2, 128, 64)
    z = jnp.zeros_like(st[:, :, 0])
    top = jnp.concatenate([st[:, :, 0], z], axis=-1)
    bot = jnp.concatenate([z, st[:, :, 1]], axis=-1)
    return jnp.concatenate([top, bot], axis=-2)


def _gla_pairs_to_state(sp):
    B = sp.shape[0]
    a = sp[:, :, 0:128, 0:64]
    b = sp[:, :, 128:256, 64:128]
    st = jnp.stack([a, b], axis=2).reshape(B, 4, 128, 64)
    return jnp.swapaxes(st, -1, -2)


def _blockdiag2(a, b):
    za = jnp.zeros((a.shape[0], b.shape[1]), a.dtype)
    zb = jnp.zeros((b.shape[0], a.shape[1]), a.dtype)
    return jnp.concatenate([jnp.concatenate([a, za], axis=1), jnp.concatenate([zb, b], axis=1)], axis=0)


def kernel(x_prompt, x_sample, c, state_rwkv_fwd, state_rwkv_bwd, state_gla_fwd, state_gla_bwd, c_ctx, ada_w, ada_b, norm1_g, norm2_g, w_in, rwkv_mu_rkv, rwkv_mu_wag, rwkv_w0, rwkv_w1, rwkv_w2, rwkv_a0, rwkv_a1, rwkv_a2, rwkv_g1, rwkv_g2, rwkv_k_k, rwkv_k_a, rwkv_r_k, rwkv_lnx_g, rwkv_lnx_b, gla_gk1, gla_gk2, gla_gk_b, gla_norm_g, w_out, mlp_w1, mlp_w2, final_norm_g):
    D = D_MODEL
    R = RWKV_WIDTH
    nb = x_prompt.shape[0]
    nd = x_sample.shape[0]
    assert ada_w.shape[0] == 1, "single-layer step"
    layer = 0

    craw = jnp.concatenate([c_ctx[None, :], c, jnp.zeros((8 - 1 - nd, D), F32)], axis=0)
    mod = _modulation(craw, ada_w[layer], ada_b[layer][None, :])
    mod3 = mod.reshape(8, 1, N_MOD * D)

    bf = lambda t: t.astype(BF16)
    row = lambda t: t.reshape(1, -1).astype(F32)
    w_in_l = w_in[layer]
    pre_w = [
        row(norm1_g[layer]),
        bf(w_in_l[:, :3 * R]),
        bf(w_in_l[:, 3 * R:]),
        bf(jnp.concatenate([gla_gk1[layer, 0], gla_gk1[layer, 1]], axis=1)),
        bf(jnp.concatenate([rwkv_w1[layer, 0], rwkv_w1[layer, 1]], axis=1)),
        bf(jnp.concatenate([rwkv_a1[layer, 0], rwkv_a1[layer, 1]], axis=1)),
        bf(rwkv_g1[layer]),
        bf(_blockdiag2(rwkv_w2[layer, 0], rwkv_w2[layer, 1])),
        bf(_blockdiag2(rwkv_a2[layer, 0], rwkv_a2[layer, 1])),
        bf(rwkv_g2[layer]),
        bf(_blockdiag2(gla_gk2[layer, 0], gla_gk2[layer, 1])),
        row(rwkv_mu_rkv[layer]),
        rwkv_mu_wag[layer],
        row(rwkv_w0[layer]),
        row(rwkv_a0[layer]),
        row(gla_gk_b[layer]),
        row(rwkv_k_k[layer]),
        row(rwkv_k_a[layer]),
        row(rwkv_r_k[layer]),
    ]
    post_vecs = [row(rwkv_lnx_g[layer]), row(rwkv_lnx_b[layer]), row(gla_norm_g[layer]),
                 row(norm2_g[layer]), row(final_norm_g)]
    post_mats = [bf(w_out[layer]), bf(mlp_w1[layer]), bf(mlp_w2[layer])]

    def run_group(x, mod_row, grid_shift, states):
        ra, rd_f, rd_b, ga, gl, pg = _pre(x, mod3, mod_row, pre_w, grid_shift)
        yo_f, yo_b, n_rf, n_rb, n_gf, n_gb = _scan(ra, rd_f, rd_b, ga, gl, states)
        y = _post(x, mod3, mod_row, yo_f, yo_b, pg, post_vecs, post_mats)
        return y, (n_rf, n_rb, n_gf, n_gb)

    y_prompt, (n_rf, n_rb, n_gf, n_gb) = run_group(x_prompt, lambda b: 0, False, None)
    y_sample, _ = run_group(x_sample, lambda b: b + 1, True,
                            (_rwkv_state_to_pairs(state_rwkv_fwd[:, layer]),
                             _rwkv_state_to_pairs(state_rwkv_bwd[:, layer]),
                             _gla_state_to_pairs(state_gla_fwd[:, layer]),
                             _gla_state_to_pairs(state_gla_bwd[:, layer])))
    return (y_prompt, y_sample,
            _rwkv_pairs_to_state(n_rf)[:, None], _rwkv_pairs_to_state(n_rb)[:, None],
            _gla_pairs_to_state(n_gf)[:, None], _gla_pairs_to_state(n_gb)[:, None])
```

```python
import functools

import jax
import jax.numpy as jnp
from jax import lax
from jax.experimental import pallas as pl
from jax.experimental.pallas import tpu as pltpu

F32 = jnp.float32
BF16 = jnp.bfloat16
HI = lax.Precision.HIGHEST

D_MODEL = 1024
GRID_W = 64
RWKV_WIDTH = 512
RWKV_HEAD_DIM = 64
RWKV_HEADS = 8
GLA_HEADS = 4
GLA_KEY_DIM = 64
GLA_VAL_DIM = 128
GLA_QK_WIDTH = 256
GLA_V_WIDTH = 512
GLA_GATE_NORMALIZER = 16.0
N_MOD = 6
RMS_EPS = 1e-6
LNX_EPS = 64e-5
GLA_NORM_EPS = 1e-5

LANES = 128
RWKV_CHUNK = 64
GLA_CHUNK = 16
GLA_WIDE_CHUNK = 64
GLA_SAFE_LOG = 60.0
SCAN_BLOCK = 128
PRE_BLOCK = 256
POST_BLOCK = 512
POST_SPLIT = 2
VMEM_LIMIT = 56 * 1024 * 1024
RWKV_TASK_LAG = 2
GLA_TASK_LAG = 14
DECAY_LOG_SCALE = 0.6065306597126334

P_SCORE = "b1"
P_INV = "b1"
P_SOLVE = "b1"
P_STATE = "b1"
P_OUT = "b1"
P_GSTATE = "b1"
P_GATT = "b1"


def _dot(a, b, prec=None):
    return lax.dot_general(a, b, (((1,), (0,)), ((), ())), precision=prec, preferred_element_type=F32)


def _dot_nt(a, b, prec=None):
    return lax.dot_general(a, b, (((1,), (1,)), ((), ())), precision=prec, preferred_element_type=F32)


def _dot_tn(a, b, prec=None):
    return lax.dot_general(a, b, (((0,), (0,)), ((), ())), precision=prec, preferred_element_type=F32)


_NN = (((1,), (0,)), ((), ()))
_NT = (((1,), (1,)), ((), ()))
_TN = (((0,), (0,)), ((), ()))


def _split2(x):
    hi = x.astype(BF16)
    return hi, (x - hi.astype(F32)).astype(BF16)


def _mm(a, b, dims, mode):
    dg = functools.partial(lax.dot_general, dimension_numbers=dims, preferred_element_type=F32)
    if mode == "hi":
        return dg(a, b, precision=HI)
    if mode == "b1":
        return dg(a.astype(BF16), b.astype(BF16))
    a1, a2 = _split2(a)
    b1, b2 = _split2(b)
    return dg(a1, b1) + dg(a1, b2) + dg(a2, b1)


def _cumsum_mm(tri, x):
    hi = x.astype(BF16)
    r1 = x - hi.astype(F32)
    mid = r1.astype(BF16)
    lo = (r1 - mid.astype(F32)).astype(BF16)
    return _dot(tri, hi) + _dot(tri, mid) + _dot(tri, lo)


def _bdot(a, w_ref):
    return _dot(a.astype(BF16), w_ref[...])


def _sigmoid(x):
    return 1.0 / (1.0 + jnp.exp(-x))


def _softplus(x):
    return jnp.maximum(x, 0.0) + jnp.log(1.0 + jnp.exp(-jnp.abs(x)))


def _head_sum(x):
    first = lax.broadcasted_iota(jnp.int32, (1, LANES), 1) < RWKV_HEAD_DIM
    outs = []
    for j in range(x.shape[-1] // LANES):
        xb = x[:, j * LANES:(j + 1) * LANES]
        s0 = jnp.sum(jnp.where(first, xb, 0.0), axis=-1, keepdims=True)
        s1 = jnp.sum(jnp.where(first, 0.0, xb), axis=-1, keepdims=True)
        outs.append(jnp.where(first, s0, s1))
    return jnp.concatenate(outs, axis=-1)


def _rms(x):
    return x * lax.rsqrt(jnp.mean(x * x, axis=-1, keepdims=True) + RMS_EPS)


def _mod_kernel(c_ref, w_ref, b_ref, o_ref):
    c = c_ref[...]
    cond = c * _sigmoid(c)
    o_ref[...] = _dot(cond.astype(BF16), w_ref[...].astype(BF16)) + b_ref[...]


def _modulation(craw, ada_w, ada_b):
    n = ada_w.shape[1]
    bn = 1536
    return pl.pallas_call(
        _mod_kernel,
        grid=(n // bn,),
        in_specs=[
            pl.BlockSpec((8, D_MODEL), lambda j: (0, 0)),
            pl.BlockSpec((D_MODEL, bn), lambda j: (0, j)),
            pl.BlockSpec((1, bn), lambda j: (0, j)),
        ],
        out_specs=pl.BlockSpec((8, bn), lambda j: (0, j)),
        out_shape=jax.ShapeDtypeStruct((8, n), F32),
        compiler_params=pltpu.CompilerParams(dimension_semantics=("arbitrary",), vmem_limit_bytes=VMEM_LIMIT),
        name="modulation",
    )(craw, ada_w, ada_b)


def _pre_kernel(*refs, grid_shift, tb):
    if grid_shift:
        x_ref, xp_ref, xn_ref = refs[:3]
        refs = refs[3:]
    else:
        x_ref = refs[0]
        refs = refs[1:]
    (mod_ref, n1g_ref, wrkv_ref, wrest_ref, gk1_ref, w1_ref, a1_ref, g1_ref, w2_ref, a2_ref, g2_ref,
     gk2_ref, murkv_ref, muwag_ref, w0_ref, a0_ref, gkb_ref, kk_ref, ka_ref, rk_ref) = refs[:20]
    ra_o, rf_o, rb_o, ga_o, gl_o, pg_o = refs[20:]
    D = D_MODEL
    R = RWKV_WIDTH
    m = mod_ref[0]
    sh1 = m[:, 0:D]
    sc1 = m[:, D:2 * D]
    n1g = n1g_ref[...]

    def normmod(xx):
        return _rms(xx) * n1g * (1.0 + sc1) + sh1

    h = normmod(x_ref[0])
    row = lax.broadcasted_iota(jnp.int32, (tb, 1), 0)
    if grid_shift:
        i = pl.program_id(1)
        n = pl.num_programs(1)
        hp = normmod(xp_ref[0]) * (i > 0).astype(F32)
        hn = normmod(xn_ref[0]) * (i < n - 1).astype(F32)
        hext = jnp.concatenate([hp, h, hn], axis=0)
        col = row % GRID_W
        m_l = (col != 0).astype(F32)
        m_r = (col != GRID_W - 1).astype(F32)
        ne = tb + 2 * GRID_W

        def shift(ext):
            up = ext[0:tb]
            down = ext[2 * GRID_W:2 * GRID_W + tb]
            left = pltpu.roll(ext, 1, 0)[GRID_W:GRID_W + tb]
            right = pltpu.roll(ext, ne - 1, 0)[GRID_W:GRID_W + tb]
            return 0.25 * (up + down + m_l * left + m_r * right)

        halo = GRID_W
    else:
        hext = h
        m_l = (row != 0).astype(F32)
        m_r = (row != tb - 1).astype(F32)

        def shift(ext):
            return 0.5 * (m_l * pltpu.roll(ext, 1, 0) + m_r * pltpu.roll(ext, tb - 1, 0))

        halo = 0

    rest = _bdot(h, wrest_ref)
    dh = shift(hext) - h
    mu = muwag_ref[...]
    lora_w = _bdot(h + mu[0:1] * dh, w1_ref)
    lora_a = _bdot(h + mu[1:2] * dh, a1_ref)
    lora_g = _bdot(h + mu[2:3] * dh, g1_ref)
    lora_gk = _bdot(h, gk1_ref)
    z = w0_ref[...] + _bdot(jnp.tanh(lora_w), w2_ref)
    a = _sigmoid(a0_ref[...] + _bdot(lora_a, a2_ref))
    gate = _bdot(_sigmoid(lora_g), g2_ref)
    logits = _bdot(lora_gk, gk2_ref) + gkb_ref[...]
    rkv_ext = _bdot(hext, wrkv_ref)

    Q = GLA_QK_WIDTH
    ga_o[0, :, 0:Q] = rest[:, 0:Q] * (GLA_KEY_DIM ** -0.5)
    ga_o[0, :, Q:] = rest[:, Q:2 * Q + GLA_V_WIDTH]
    gg = rest[:, 2 * Q + GLA_V_WIDTH:]
    pg_o[0, :, 2 * R:] = gg * _sigmoid(gg)
    gl_o[0] = -_softplus(-logits) * (1.0 / GLA_GATE_NORMALIZER)

    rkv = rkv_ext[halo:halo + tb]
    rkv = rkv + murkv_ref[...] * (shift(rkv_ext) - rkv)
    r = rkv[:, 0:R]
    k = rkv[:, R:2 * R]
    v = rkv[:, 2 * R:3 * R]
    lw = -DECAY_LOG_SCALE * _sigmoid(z)

    kap = k * kk_ref[...]
    kap = kap * lax.rsqrt(jnp.maximum(_head_sum(kap * kap), 1e-12))
    ka = ka_ref[...]
    a_f = a[:, 0:R]
    a_b = a[:, R:2 * R]
    kd_f = k * (1.0 + (a_f - 1.0) * ka)
    kd_b = k * (1.0 + (a_b - 1.0) * ka)
    bonus = _head_sum(r * (kd_f + kd_b) * rk_ref[...]) * v

    ra_o[0, :, 0:R] = r
    ra_o[0, :, R:2 * R] = v
    ra_o[0, :, 2 * R:3 * R] = kap
    rf_o[0, :, 0:R] = kd_f
    rf_o[0, :, R:2 * R] = a_f * kap
    rf_o[0, :, 2 * R:3 * R] = lw[:, 0:R]
    rb_o[0, :, 0:R] = kd_b
    rb_o[0, :, R:2 * R] = a_b * kap
    rb_o[0, :, 2 * R:3 * R] = lw[:, R:2 * R]
    pg_o[0, :, 0:R] = gate
    pg_o[0, :, R:2 * R] = bonus


def _const_spec(shape):
    nd = len(shape)
    return pl.BlockSpec(shape, lambda b, i: (0,) * nd)


def _pre(x, mod3, mod_row, weights, grid_shift):
    B, L, D = x.shape
    tb = PRE_BLOCK
    nblk = L // tb
    if not grid_shift:
        assert nblk == 1
    x_spec = pl.BlockSpec((1, tb, D), lambda b, i: (b, i, 0))
    in_specs = [x_spec]
    args = [x]
    if grid_shift:
        per = tb // GRID_W
        nrow = L // GRID_W
        in_specs += [
            pl.BlockSpec((1, GRID_W, D), lambda b, i: (b, jnp.maximum(i * per - 1, 0), 0)),
            pl.BlockSpec((1, GRID_W, D), lambda b, i: (b, jnp.minimum((i + 1) * per, nrow - 1), 0)),
        ]
        args += [x, x]
    in_specs.append(pl.BlockSpec((1, 1, N_MOD * D), lambda b, i: (mod_row(b), 0, 0)))
    args.append(mod3)
    for w in weights:
        w, spec = w if isinstance(w, tuple) else (w, _const_spec(w.shape))
        in_specs.append(spec)
        args.append(w)
    R = RWKV_WIDTH
    widths = [3 * R, 3 * R, 3 * R, 2 * GLA_QK_WIDTH + GLA_V_WIDTH, 2 * GLA_QK_WIDTH, 2 * R + GLA_V_WIDTH]
    out_specs = [pl.BlockSpec((1, tb, w), lambda b, i: (b, i, 0)) for w in widths]
    out_shape = [jax.ShapeDtypeStruct((B, L, w), F32) for w in widths]
    return pl.pallas_call(
        functools.partial(_pre_kernel, grid_shift=grid_shift, tb=tb),
        grid=(B, nblk),
        in_specs=in_specs,
        out_specs=out_specs,
        out_shape=out_shape,
        compiler_params=pltpu.CompilerParams(dimension_semantics=("parallel", "arbitrary"),
                                             vmem_limit_bytes=VMEM_LIMIT),
        name="pre_grid" if grid_shift else "pre_seq",
    )(*args)


def _interleave(tasks):
    live = list(tasks)
    rnd = 0
    while live:
        keep = []
        for first, gen in live:
            if rnd >= first:
                try:
                    next(gen)
                except StopIteration:
                    continue
            keep.append((first, gen))
        live = keep
        rnd += 1


def _inv_unit_triangular(lms, eye, blk16, blk32):
    mm = functools.partial(_mm, dims=_NN, mode=P_INV)
    n = lms[0].shape[0]
    l0 = [jnp.where(blk16, lm, 0.0) for lm in lms]
    l2 = [mm(a, a) for a in l0]
    yield
    t = [eye - a for a in l0]
    s = [mm(jnp.concatenate([a, b], axis=0), b) for a, b in zip(t, l2)]
    yield
    t = [a + x[0:n] for a, x in zip(t, s)]
    l4 = [x[n:2 * n] for x in s]
    s = [mm(jnp.concatenate([a, b], axis=0), b) for a, b in zip(t, l4)]
    yield
    t = [a + x[0:n] for a, x in zip(t, s)]
    l8 = [x[n:2 * n] for x in s]
    t = [a + mm(a, b) for a, b in zip(t, l8)]
    yield
    off1_mask = jnp.logical_and(blk32, jnp.logical_not(blk16))
    x = [mm(a, jnp.where(off1_mask, lm, 0.0)) for a, lm in zip(t, lms)]
    yield
    t = [a - mm(b, a) for a, b in zip(t, x)]
    yield
    x = [mm(a, jnp.where(blk32, 0.0, lm)) for a, lm in zip(t, lms)]
    yield
    t = [a - mm(b, a) for a, b in zip(t, x)]
    yield
    return t


def _rwkv_prepare(units, cst):
    C = RWKV_CHUNK
    m0, cm0, eye, blk16, blk32, _ = cst[2]

    def stack(x):
        return jnp.concatenate([jnp.where(m0, x, 0.0), jnp.where(m0, 0.0, x)], axis=0)

    def blockdiag(side):
        return jnp.concatenate([jnp.where(cm0, side, 0.0), jnp.where(cm0, 0.0, side)], axis=0)

    cums = [_cumsum_mm(cst[rev][0], lw) for (_, _, _, _, _, lw, rev) in units]
    yield
    prep = []
    for (r, k, v, kap, b, lw, rev), cum in zip(units, cums):
        cumx = cum - lw
        if rev:
            mid = cum[C // 2:C // 2 + 1]
            tot = cum[0:1]
        else:
            mid = cum[C // 2 - 1:C // 2]
            tot = cum[C - 1:C]
        e_mid = jnp.exp(mid)
        rt = r * jnp.exp(cum - mid)
        kt = kap * jnp.exp(cumx - mid)
        es = jnp.exp(mid - cum)
        e_end = jnp.exp(tot - mid)
        prep.append(dict(rt=rt, kt=kt, bh=b * es, kh=k * es, r0=rt * e_mid, k0=stack(kt * e_mid),
                         be=b * es * e_end, ke=k * es * e_end, vs=stack(v), v=v,
                         e_tot=jnp.exp(tot), strict2=cst[rev][1], incl2=cst[rev][2]))
    gs = [_mm(jnp.concatenate([p["kt"], p["rt"]], axis=0),
              jnp.concatenate([stack(p["bh"]), stack(p["kh"])], axis=0), _NT, P_SCORE) for p in prep]
    yield
    for p, g in zip(prep, gs):
        p["ab"] = blockdiag(jnp.where(p["strict2"], g[0:C, 0:2 * C], 0.0))
        p["ak"] = blockdiag(jnp.where(p["strict2"], g[0:C, 2 * C:4 * C], 0.0))
        p["rb"] = jnp.where(p["incl2"], g[C:2 * C, 0:2 * C], 0.0)
        p["rk"] = jnp.where(p["incl2"], g[C:2 * C, 2 * C:4 * C], 0.0)
    av = [_mm(jnp.concatenate([p["ak"], p["rk"]], axis=0), p["vs"], _NN, P_SOLVE) for p in prep]
    yield
    ts = yield from _inv_unit_triangular([p["ab"] for p in prep], eye, blk16, blk32)
    wu = [_mm(t, jnp.concatenate([p["k0"], a[0:2 * C]], axis=1), _NN, P_SOLVE) for t, p, a in zip(ts, prep, av)]
    yield
    return [dict(wr=jnp.concatenate([x[:, :LANES], p["r0"]], axis=0), u0=x[:, LANES:], yv=a[2 * C:3 * C],
                 rb=p["rb"], v=p["v"], bke=jnp.concatenate([p["be"], p["ke"]], axis=0), e_tot=p["e_tot"])
            for x, a, p in zip(wu, av, prep)]


def _rwkv_apply(prep, states, cst):
    C = RWKV_CHUNK
    blk64 = cst[2][5]
    ws = [_mm(p["wr"], s, _NT, P_STATE) for p, s in zip(prep, states)]
    yield
    us = [-p["u0"] - w[0:2 * C] for p, w in zip(prep, ws)]
    upd = [_mm(jnp.concatenate([u[0:C] + u[C:2 * C], p["v"]], axis=0), p["bke"], _TN, P_STATE)
           for u, p in zip(us, prep)]
    yield
    s_new = [s * p["e_tot"] + jnp.where(blk64, d, 0.0) for p, s, d in zip(prep, states, upd)]
    ys = [w[2 * C:3 * C] + _mm(p["rb"], u, _NN, P_OUT) + p["yv"] for p, w, u in zip(prep, ws, us)]
    yield
    return ys, s_new


def _rwkv_consts():
    C = RWKV_CHUNK
    row = lax.broadcasted_iota(jnp.int32, (C, C), 0)
    col = lax.broadcasted_iota(jnp.int32, (C, C), 1)
    row2 = lax.broadcasted_iota(jnp.int32, (C, 2 * C), 0)
    col2 = lax.broadcasted_iota(jnp.int32, (C, 2 * C), 1)
    cs = col2 % C
    lane = lax.broadcasted_iota(jnp.int32, (1, LANES), 1)
    m0 = lane < RWKV_HEAD_DIM
    cm0 = lax.broadcasted_iota(jnp.int32, (1, 2 * C), 1) < C
    rr = lax.broadcasted_iota(jnp.int32, (2 * C, 2 * C), 0)
    cc = lax.broadcasted_iota(jnp.int32, (2 * C, 2 * C), 1)
    eye = (rr == cc).astype(F32)
    blk16 = (rr // 16) == (cc // 16)
    blk32 = (rr // 32) == (cc // 32)
    blk64 = (rr // 64) == (cc // 64)
    fwd = ((col <= row).astype(BF16), cs < row2, cs <= row2)
    bwd = ((col >= row).astype(BF16), cs > row2, cs >= row2)
    return fwd, bwd, (m0, cm0, eye, blk16, blk32, blk64)


def _gla_intra(items, cst):
    G = GLA_CHUNK
    ind_v, sel, blk = cst[2]
    srow = lax.broadcasted_iota(jnp.int32, (G, 1), 0)
    cums = [_cumsum_mm(cst[rev], la) for (_, _, _, la, rev) in items]
    pmats = []
    for (q, k, v, la, rev), cum in zip(items, cums):
        ps = []
        for t in range(G):
            msk = (srow >= t) if rev else (srow <= t)
            e = jnp.exp(jnp.where(msk, cum[t:t + 1] - cum, 0.0))
            ps.append(jnp.where(msk, e * (q[t:t + 1] * k), 0.0))
        pmats.append(jnp.concatenate(ps, axis=0))
    atts = [_mm(pm, ind_v, _NN, P_GATT) for pm in pmats]
    o_intra = [_mm(sel, att * jnp.concatenate([it[2]] * G, axis=0), _NN, P_GATT)
               for att, it in zip(atts, items)]
    out = []
    for (q, k, v, la, rev), cum, oi in zip(items, cums, o_intra):
        tot = cum[0:1] if rev else cum[G - 1:G]
        upd = _mm(v, k * jnp.exp(tot - cum), _TN, P_GSTATE)
        out.append((oi, q * jnp.exp(cum), jnp.where(blk, upd, 0.0), jnp.exp(tot)))
    yield
    return out


def _gla_consts():
    G = GLA_CHUNK
    row = lax.broadcasted_iota(jnp.int32, (G, G), 0)
    col = lax.broadcasted_iota(jnp.int32, (G, G), 1)
    kc = lax.broadcasted_iota(jnp.int32, (LANES, 2 * LANES), 0)
    vc = lax.broadcasted_iota(jnp.int32, (LANES, 2 * LANES), 1)
    ind_v = ((kc // GLA_KEY_DIM) == (vc // GLA_VAL_DIM)).astype(F32)
    st = lax.broadcasted_iota(jnp.int32, (G, G * G), 0)
    sj = lax.broadcasted_iota(jnp.int32, (G, G * G), 1)
    sel = ((sj // G) == st).astype(F32)
    br = lax.broadcasted_iota(jnp.int32, (2 * LANES, LANES), 0)
    bc = lax.broadcasted_iota(jnp.int32, (2 * LANES, LANES), 1)
    blk = (br // GLA_VAL_DIM) == (bc // GLA_KEY_DIM)
    return (col <= row).astype(BF16), (col >= row).astype(BF16), (ind_v, sel, blk)


def _gla_factored(items, cst):
    F = GLA_WIDE_CHUNK
    m0, vm0, blk = cst[2]
    cums = [_cumsum_mm(cst[rev][0], la) for (_, _, _, la, rev) in items]
    yield
    scores = []
    for (q, k, v, la, rev), cum in zip(items, cums):
        mid = cum[F // 2:F // 2 + 1] if rev else cum[F // 2 - 1:F // 2]
        kt = k * jnp.exp(mid - cum)
        kstk = jnp.concatenate([jnp.where(m0, kt, 0.0), jnp.where(m0, 0.0, kt)], axis=0)
        scores.append(_mm(q * jnp.exp(cum - mid), kstk, _NT, P_GATT))
    yield
    o_intra = []
    for (q, k, v, la, rev), sc in zip(items, scores):
        vstk = jnp.concatenate([jnp.where(vm0, v, 0.0), jnp.where(vm0, 0.0, v)], axis=0)
        o_intra.append(_mm(jnp.where(cst[rev][1], sc, 0.0), vstk, _NN, P_GATT))
    yield
    out = []
    for (q, k, v, la, rev), cum, oi in zip(items, cums, o_intra):
        tot = cum[0:1] if rev else cum[F - 1:F]
        upd = _mm(v, k * jnp.exp(tot - cum), _TN, P_GSTATE)
        out.append((oi, q * jnp.exp(cum), jnp.where(blk, upd, 0.0), jnp.exp(tot)))
    yield
    return out


def _gla_wide_consts():
    F = GLA_WIDE_CHUNK
    row = lax.broadcasted_iota(jnp.int32, (F, F), 0)
    col = lax.broadcasted_iota(jnp.int32, (F, F), 1)
    row2 = lax.broadcasted_iota(jnp.int32, (F, 2 * F), 0)
    cs = lax.broadcasted_iota(jnp.int32, (F, 2 * F), 1) % F
    m0 = lax.broadcasted_iota(jnp.int32, (1, LANES), 1) < GLA_KEY_DIM
    vm0 = lax.broadcasted_iota(jnp.int32, (1, 2 * LANES), 1) < GLA_VAL_DIM
    br = lax.broadcasted_iota(jnp.int32, (2 * LANES, LANES), 0)
    bc = lax.broadcasted_iota(jnp.int32, (2 * LANES, LANES), 1)
    blk = (br // GLA_VAL_DIM) == (bc // GLA_KEY_DIM)
    return (((col <= row).astype(BF16), cs <= row2), ((col >= row).astype(BF16), cs >= row2), (m0, vm0, blk))


def _scan_kernel(raf_ref, rdf_ref, rab_ref, rdb_ref, gaf_ref, glf_ref, gab_ref, glb_ref, *refs,
                 has_init, emit_states):
    refs = list(refs)
    init_refs = [refs.pop(0) for _ in range(4)] if has_init else None
    yof_ref, yob_ref = refs.pop(0), refs.pop(0)
    final_refs = [refs.pop(0) for _ in range(4)] if emit_states else None
    srf_ref, srb_ref, sgf_ref, sgb_ref = refs
    i = pl.program_id(1)
    lane = lax.broadcasted_iota(jnp.int32, (1, LANES), 1)
    first_half = lane < LANES // 2
    rr = lax.broadcasted_iota(jnp.int32, (LANES, LANES), 0)
    cc = lax.broadcasted_iota(jnp.int32, (LANES, LANES), 1)
    diag_blocks = (rr // RWKV_HEAD_DIM) == (cc // RWKV_HEAD_DIM)

    @pl.when(i == 0)
    def _():
        if not has_init:
            for s_ref in (srf_ref, srb_ref, sgf_ref, sgb_ref):
                s_ref[...] = jnp.zeros(s_ref.shape, F32)
            return
        for s_ref, init in ((srf_ref, init_refs[0]), (srb_ref, init_refs[1])):
            for p in range(RWKV_WIDTH // LANES):
                x = init[0, p]
                s_ref[0, p] = jnp.where(diag_blocks, jnp.concatenate([x, x], axis=1), 0.0)
        for s_ref, init in ((sgf_ref, init_refs[2]), (sgb_ref, init_refs[3])):
            for p in range(GLA_QK_WIDTH // LANES):
                xt = init[0, p].T
                s_ref[0, p] = jnp.concatenate([jnp.where(first_half, xt, 0.0),
                                               jnp.where(first_half, 0.0, xt)], axis=0)

    T = SCAN_BLOCK
    C = RWKV_CHUNK
    nr = T // C
    rcst = _rwkv_consts()
    R = RWKV_WIDTH
    Q = GLA_QK_WIDTH
    g_dirs = ((False, gaf_ref, glf_ref, 0, sgf_ref, yof_ref),
              (True, gab_ref, glb_ref, Q, sgb_ref, yob_ref))
    la_tot = jnp.minimum(jnp.sum(glf_ref[0, :, 0:Q], axis=0, keepdims=True),
                         jnp.sum(glb_ref[0, :, Q:2 * Q], axis=0, keepdims=True))
    gla_split_ok = jnp.min(la_tot) >= -GLA_SAFE_LOG

    r_dirs = ((False, raf_ref, rdf_ref, srf_ref, yof_ref),
              (True, rab_ref, rdb_ref, srb_ref, yob_ref))
    r_units = {}
    r_dests = {}
    for d, (rev, ra_ref, rd_ref, s_ref, y_ref) in enumerate(r_dirs):
        for j in range(nr):
            c = nr - 1 - j if rev else j
            rows = slice(c * C, (c + 1) * C)
            r_units[d, j] = []
            r_dests[d, j] = []
            for p in range(R // LANES):
                sl, sl1, sl2 = (slice(o + p * LANES, o + (p + 1) * LANES) for o in (0, R, 2 * R))
                r_units[d, j].append((ra_ref[0, rows, sl], rd_ref[0, rows, sl], ra_ref[0, rows, sl1],
                                      ra_ref[0, rows, sl2], rd_ref[0, rows, sl1], rd_ref[0, rows, sl2], rev))
                r_dests[d, j].append((y_ref, s_ref, p, rows, sl))
    r_states = {(d, 0): [s_ref[0, p] for (_, s_ref, p, _, _) in r_dests[d, 0]] for d in range(len(r_dirs))}

    def rwkv_task(d, j):
        prep = yield from _rwkv_prepare(r_units[d, j], rcst)
        while (d, j) not in r_states:
            yield
        ys, r_states[d, j + 1] = yield from _rwkv_apply(prep, r_states[d, j], rcst)
        for (y_ref, _, _, rows, sl), y in zip(r_dests[d, j], ys):
            y_ref[0, rows, sl] = y

    npair = GLA_QK_WIDTH // LANES
    g_init = [[s_ref[0, p] for p in range(npair)] for (_, _, _, _, s_ref, _) in g_dirs]

    def gla_task(chunk, intra, cst):
        ng = T // chunk
        items = [[] for _ in range(ng)]
        dests = [[] for _ in range(ng)]
        for d, (rev, ga_ref, gl_ref, la_off, s_ref, o_ref) in enumerate(g_dirs):
            for p in range(npair):
                qs, ks, las = (slice(o + p * LANES, o + (p + 1) * LANES) for o in (0, Q, la_off))
                vs = slice(2 * Q + p * 2 * LANES, 2 * Q + (p + 1) * 2 * LANES)
                os_ = slice(R + p * 2 * LANES, R + (p + 1) * 2 * LANES)
                for j in range(ng):
                    c = ng - 1 - j if rev else j
                    rows = slice(c * chunk, (c + 1) * chunk)
                    items[j].append((ga_ref[0, rows, qs], ga_ref[0, rows, ks], ga_ref[0, rows, vs],
                                     gl_ref[0, rows, las], rev))
                    dests[j].append((o_ref, s_ref, d, p, rows, os_))
        parts = yield from intra([it for its in items for it in its], cst)
        nu = len(items[0])
        sts = [g_init[d][p] for (_, _, d, p, _, _) in dests[0]]
        for j in range(ng):
            new = []
            for (o_ref, _, _, _, rows, vs), (o_intra, q_in, upd, e_tot), st in zip(
                    dests[j], parts[j * nu:(j + 1) * nu], sts):
                o_ref[0, rows, vs] = o_intra + _mm(q_in, st, _NT, P_GSTATE)
                new.append(st * e_tot + upd)
            sts = new
            yield
        for (_, s_ref, _, p, _, _), st in zip(dests[0], sts):
            s_ref[0, p] = st

    nd = len(r_dirs)
    _interleave([(RWKV_TASK_LAG * j, rwkv_task(d, j)) for j in range(nr) for d in range(nd)]
                + [(GLA_TASK_LAG, gla_task(GLA_WIDE_CHUNK, _gla_factored, _gla_wide_consts()))])
    for d in range(nd):
        for (_, s_ref, p, _, _), s in zip(r_dests[d, 0], r_states[d, nr]):
            s_ref[0, p] = s

    @pl.when(jnp.logical_not(gla_split_ok))
    def _():
        _interleave([(0, gla_task(GLA_CHUNK, _gla_intra, _gla_consts()))])

    if emit_states:
        @pl.when(i == pl.num_programs(1) - 1)
        def _():
            for s_ref, out in ((srf_ref, final_refs[0]), (srb_ref, final_refs[1])):
                for p in range(RWKV_WIDTH // LANES):
                    s = s_ref[0, p]
                    out[0, p] = (s + pltpu.roll(s, LANES // 2, 1))[:, 0:LANES // 2]
            for s_ref, out in ((sgf_ref, final_refs[2]), (sgb_ref, final_refs[3])):
                for p in range(GLA_QK_WIDTH // LANES):
                    st = s_ref[0, p]
                    out[0, p] = st[0:LANES].T + st[LANES:2 * LANES].T


def _scan(ra, rd_f, rd_b, ga, gl, states, emit_states):
    B, L, _ = ra.shape
    R = RWKV_WIDTH
    Q = GLA_QK_WIDTH
    V = GLA_V_WIDTH
    T = SCAN_BLOCK
    n = L // T

    def fwd(w):
        return pl.BlockSpec((1, T, w), lambda b, i: (b, i, 0))

    def bwd(w):
        return pl.BlockSpec((1, T, w), lambda b, i: (b, n - 1 - i, 0))

    rshape = (R // LANES, LANES, LANES // 2)
    gshape = (Q // LANES, LANES, LANES)
    rst = pl.BlockSpec((1,) + rshape, lambda b, i: (b, 0, 0, 0))
    gst = pl.BlockSpec((1,) + gshape, lambda b, i: (b, 0, 0, 0))
    st_specs = [rst, rst, gst, gst]
    st_shapes = [jax.ShapeDtypeStruct((B,) + s, F32) for s in (rshape, rshape, gshape, gshape)]
    states = () if states is None else tuple(states)
    return pl.pallas_call(
        functools.partial(_scan_kernel, has_init=bool(states), emit_states=emit_states),
        grid=(B, n),
        in_specs=[fwd(3 * R), fwd(3 * R), bwd(3 * R), bwd(3 * R),
                  fwd(2 * Q + V), fwd(2 * Q), bwd(2 * Q + V), bwd(2 * Q)] + st_specs[:len(states)],
        out_specs=[fwd(R + V), bwd(R + V)] + (st_specs if emit_states else []),
        out_shape=[jax.ShapeDtypeStruct((B, L, R + V), F32)] * 2 + (st_shapes if emit_states else []),
        scratch_shapes=[pltpu.VMEM((1, R // LANES, LANES, LANES), F32)] * 2
        + [pltpu.VMEM((1, Q // LANES, 2 * LANES, LANES), F32)] * 2,
        compiler_params=pltpu.CompilerParams(dimension_semantics=("parallel", "arbitrary"),
                                             vmem_limit_bytes=VMEM_LIMIT),
        name="scan",
    )(ra, rd_f, ra, rd_b, ga, gl, ga, gl, *states)


def _post_kernel(x_ref, mod_ref, yof_ref, yob_ref, pg_ref,
                 lng_ref, lnb_ref, gng_ref, n2g_ref, fng_ref, wout_ref, w1_ref, w2_ref, o_ref):
    D = D_MODEL
    R = RWKV_WIDTH
    m = mod_ref[0]
    gt1 = m[:, 2 * D:3 * D]
    sh2 = m[:, 3 * D:4 * D]
    sc2 = m[:, 4 * D:5 * D]
    gt2 = m[:, 5 * D:6 * D]
    inv_n = 1.0 / RWKV_HEAD_DIM
    gng = gng_ref[...]
    tb = x_ref.shape[1]
    nsplit = POST_SPLIT if tb % (8 * POST_SPLIT) == 0 else 1
    rows = [slice(s * (tb // nsplit), (s + 1) * (tb // nsplit)) for s in range(nsplit)]

    def mixed(rs):
        y = yof_ref[0, rs, 0:R] + yob_ref[0, rs, 0:R]
        mu = _head_sum(y) * inv_n
        yc = y - mu
        var = _head_sum(yc * yc) * inv_n
        yn = yc * lax.rsqrt(var + LNX_EPS) * lng_ref[...] + lnb_ref[...]
        parts = [(yn + pg_ref[0, rs, R:2 * R]) * pg_ref[0, rs, 0:R]]
        o = yof_ref[0, rs, R:] + yob_ref[0, rs, R:]
        for hh in range(GLA_HEADS):
            sl = slice(hh * GLA_VAL_DIM, (hh + 1) * GLA_VAL_DIM)
            oh = o[:, sl]
            oh = oh * lax.rsqrt(jnp.mean(oh * oh, axis=-1, keepdims=True) + GLA_NORM_EPS)
            parts.append(oh * gng * pg_ref[0, rs, 2 * R + hh * GLA_VAL_DIM:2 * R + (hh + 1) * GLA_VAL_DIM])
        return jnp.concatenate(parts, axis=-1)

    mix = [mixed(rs) for rs in rows]
    x1 = [x_ref[0, rs] + gt1 * _bdot(mx, wout_ref) for rs, mx in zip(rows, mix)]
    h2 = [_rms(v) * n2g_ref[...] * (1.0 + sc2) + sh2 for v in x1]
    f = [jnp.maximum(_bdot(v, w1_ref), 0.0) for v in h2]
    x2 = [a + gt2 * _bdot(v * v, w2_ref) for a, v in zip(x1, f)]
    for rs, v in zip(rows, x2):
        o_ref[0, rs] = _rms(v) * fng_ref[...]


def _post(x, mod3, mod_row, yo_f, yo_b, pg, vecs, mats):
    B, L, D = x.shape
    tb = min(POST_BLOCK, L)
    nblk = L // tb

    def tok(w):
        return pl.BlockSpec((1, tb, w), lambda b, i: (b, i, 0))

    in_specs = [tok(D), pl.BlockSpec((1, 1, N_MOD * D), lambda b, i: (mod_row(b), 0, 0))]
    in_specs += [tok(yo_f.shape[-1]), tok(yo_b.shape[-1]), tok(pg.shape[-1])]
    in_specs += [_const_spec(w.shape) for w in vecs]
    in_specs += [pl.BlockSpec(w.shape, lambda b, i: (0, 0), pipeline_mode=pl.Buffered(1)) for w in mats]
    return pl.pallas_call(
        _post_kernel,
        grid=(B, nblk),
        in_specs=in_specs,
        out_specs=tok(D),
        out_shape=jax.ShapeDtypeStruct((B, L, D), F32),
        compiler_params=pltpu.CompilerParams(dimension_semantics=("parallel", "arbitrary"),
                                             vmem_limit_bytes=VMEM_LIMIT),
        name="post",
    )(x, mod3, yo_f, yo_b, pg, *vecs, *mats)


def _blockdiag2(a, b):
    za = jnp.zeros((a.shape[0], b.shape[1]), a.dtype)
    zb = jnp.zeros((b.shape[0], a.shape[1]), a.dtype)
    return jnp.concatenate([jnp.concatenate([a, za], axis=1), jnp.concatenate([zb, b], axis=1)], axis=0)


def kernel(x_prompt, x_sample, c, state_rwkv_fwd, state_rwkv_bwd, state_gla_fwd, state_gla_bwd, c_ctx, ada_w, ada_b, norm1_g, norm2_g, w_in, rwkv_mu_rkv, rwkv_mu_wag, rwkv_w0, rwkv_w1, rwkv_w2, rwkv_a0, rwkv_a1, rwkv_a2, rwkv_g1, rwkv_g2, rwkv_k_k, rwkv_k_a, rwkv_r_k, rwkv_lnx_g, rwkv_lnx_b, gla_gk1, gla_gk2, gla_gk_b, gla_norm_g, w_out, mlp_w1, mlp_w2, final_norm_g):
    D = D_MODEL
    R = RWKV_WIDTH
    nb = x_prompt.shape[0]
    nd = x_sample.shape[0]
    assert ada_w.shape[0] == 1, "single-layer step"
    layer = 0

    craw = jnp.concatenate([c_ctx[None, :], c, jnp.zeros((8 - 1 - nd, D), F32)], axis=0)
    mod = _modulation(craw, ada_w[layer], ada_b[layer][None, :])
    mod3 = mod.reshape(8, 1, N_MOD * D)

    bf = lambda t: t.astype(BF16)
    row = lambda t: t.reshape(1, -1).astype(F32)
    w_in_bf = bf(w_in[layer])
    half = (D, 3 * R)
    pre_w = [
        row(norm1_g[layer]),
        (w_in_bf, pl.BlockSpec(half, lambda b, i: (0, 0))),
        (w_in_bf, pl.BlockSpec(half, lambda b, i: (0, 1))),
        bf(jnp.concatenate([gla_gk1[layer, 0], gla_gk1[layer, 1]], axis=1)),
        bf(jnp.concatenate([rwkv_w1[layer, 0], rwkv_w1[layer, 1]], axis=1)),
        bf(jnp.concatenate([rwkv_a1[layer, 0], rwkv_a1[layer, 1]], axis=1)),
        bf(rwkv_g1[layer]),
        bf(_blockdiag2(rwkv_w2[layer, 0], rwkv_w2[layer, 1])),
        bf(_blockdiag2(rwkv_a2[layer, 0], rwkv_a2[layer, 1])),
        bf(rwkv_g2[layer]),
        bf(_blockdiag2(gla_gk2[layer, 0], gla_gk2[layer, 1])),
        row(rwkv_mu_rkv[layer]),
        rwkv_mu_wag[layer],
        row(rwkv_w0[layer]),
        row(rwkv_a0[layer]),
        row(gla_gk_b[layer]),
        row(rwkv_k_k[layer]),
        row(rwkv_k_a[layer]),
        row(rwkv_r_k[layer]),
    ]
    post_vecs = [row(rwkv_lnx_g[layer]), row(rwkv_lnx_b[layer]), row(gla_norm_g[layer]),
                 row(norm2_g[layer]), row(final_norm_g)]
    post_mats = [bf(w_out[layer]), bf(mlp_w1[layer]), bf(mlp_w2[layer])]

    def run_group(x, mod_row, grid_shift, states, emit_states):
        ra, rd_f, rd_b, ga, gl, pg = _pre(x, mod3, mod_row, pre_w, grid_shift)
        yo_f, yo_b, *finals = _scan(ra, rd_f, rd_b, ga, gl, states, emit_states)
        y = _post(x, mod3, mod_row, yo_f, yo_b, pg, post_vecs, post_mats)
        return y, finals

    rpair = lambda s: s.reshape(nd, R // LANES, LANES, RWKV_HEAD_DIM)
    gpair = lambda s: s.reshape(nd, GLA_QK_WIDTH // LANES, LANES, GLA_VAL_DIM)
    y_prompt, (n_rf, n_rb, n_gf, n_gb) = run_group(x_prompt, lambda b: 0, False, None, True)
    y_sample, _ = run_group(x_sample, lambda b: b + 1, True,
                            (rpair(state_rwkv_fwd[:, layer]), rpair(state_rwkv_bwd[:, layer]),
                             gpair(state_gla_fwd[:, layer]), gpair(state_gla_bwd[:, layer])), False)
    rshape = (nb, 1, RWKV_HEADS, RWKV_HEAD_DIM, RWKV_HEAD_DIM)
    gshape = (nb, 1, GLA_HEADS, GLA_KEY_DIM, GLA_VAL_DIM)
    return (y_prompt, y_sample, n_rf.reshape(rshape), n_rb.reshape(rshape),
            n_gf.reshape(gshape), n_gb.reshape(gshape))
```

```python
import functools

import jax
import jax.numpy as jnp
from jax import lax
from jax.experimental import pallas as pl
from jax.experimental.pallas import tpu as pltpu

F32 = jnp.float32
BF16 = jnp.bfloat16
HI = lax.Precision.HIGHEST

D_MODEL = 1024
GRID_W = 64
RWKV_WIDTH = 512
RWKV_HEAD_DIM = 64
RWKV_HEADS = 8
GLA_HEADS = 4
GLA_KEY_DIM = 64
GLA_VAL_DIM = 128
GLA_QK_WIDTH = 256
GLA_V_WIDTH = 512
GLA_GATE_NORMALIZER = 16.0
N_MOD = 6
RMS_EPS = 1e-6
LNX_EPS = 64e-5
GLA_NORM_EPS = 1e-5

LANES = 128
RWKV_CHUNK = 64
GLA_CHUNK = 16
GLA_WIDE_CHUNK = 64
GLA_SAFE_LOG = 60.0
SCAN_BLOCK = 256
SCAN_SUB = 128
PRE_BLOCK = 256
POST_BLOCK = 512
POST_SPLIT = 2
VMEM_LIMIT = 56 * 1024 * 1024
RWKV_TASK_LAG = 2
GLA_TASK_LAG = 14
DECAY_LOG_SCALE = 0.6065306597126334

P_SCORE = "b1"
P_INV = "b1"
P_SOLVE = "b1"
P_STATE = "b1"
P_OUT = "b1"
P_GSTATE = "b1"
P_GATT = "b1"


def _dot(a, b, prec=None):
    return lax.dot_general(a, b, (((1,), (0,)), ((), ())), precision=prec, preferred_element_type=F32)


def _dot_nt(a, b, prec=None):
    return lax.dot_general(a, b, (((1,), (1,)), ((), ())), precision=prec, preferred_element_type=F32)


def _dot_tn(a, b, prec=None):
    return lax.dot_general(a, b, (((0,), (0,)), ((), ())), precision=prec, preferred_element_type=F32)


_NN = (((1,), (0,)), ((), ()))
_NT = (((1,), (1,)), ((), ()))
_TN = (((0,), (0,)), ((), ()))


def _split2(x):
    hi = x.astype(BF16)
    return hi, (x - hi.astype(F32)).astype(BF16)


def _mm(a, b, dims, mode):
    dg = functools.partial(lax.dot_general, dimension_numbers=dims, preferred_element_type=F32)
    if mode == "hi":
        return dg(a, b, precision=HI)
    if mode == "b1":
        return dg(a.astype(BF16), b.astype(BF16))
    a1, a2 = _split2(a)
    b1, b2 = _split2(b)
    return dg(a1, b1) + dg(a1, b2) + dg(a2, b1)


def _cumsum_mm(tri, x):
    hi = x.astype(BF16)
    r1 = x - hi.astype(F32)
    mid = r1.astype(BF16)
    lo = (r1 - mid.astype(F32)).astype(BF16)
    return _dot(tri, hi) + _dot(tri, mid) + _dot(tri, lo)


def _bdot(a, w_ref):
    return _dot(a.astype(BF16), w_ref[...])


def _sigmoid(x):
    return 1.0 / (1.0 + jnp.exp(-x))


def _softplus(x):
    return jnp.maximum(x, 0.0) + jnp.log(1.0 + jnp.exp(-jnp.abs(x)))


def _head_sum(x):
    first = lax.broadcasted_iota(jnp.int32, (1, LANES), 1) < RWKV_HEAD_DIM
    outs = []
    for j in range(x.shape[-1] // LANES):
        xb = x[:, j * LANES:(j + 1) * LANES]
        s0 = jnp.sum(jnp.where(first, xb, 0.0), axis=-1, keepdims=True)
        s1 = jnp.sum(jnp.where(first, 0.0, xb), axis=-1, keepdims=True)
        outs.append(jnp.where(first, s0, s1))
    return jnp.concatenate(outs, axis=-1)


def _rms(x):
    return x * lax.rsqrt(jnp.mean(x * x, axis=-1, keepdims=True) + RMS_EPS)


def _mod_kernel(c_ref, w_ref, b_ref, o_ref):
    c = c_ref[...]
    cond = c * _sigmoid(c)
    o_ref[...] = _dot(cond.astype(BF16), w_ref[...].astype(BF16)) + b_ref[...]


def _modulation(craw, ada_w, ada_b):
    n = ada_w.shape[1]
    bn = 1536
    return pl.pallas_call(
        _mod_kernel,
        grid=(n // bn,),
        in_specs=[
            pl.BlockSpec((8, D_MODEL), lambda j: (0, 0)),
            pl.BlockSpec((D_MODEL, bn), lambda j: (0, j)),
            pl.BlockSpec((1, bn), lambda j: (0, j)),
        ],
        out_specs=pl.BlockSpec((8, bn), lambda j: (0, j)),
        out_shape=jax.ShapeDtypeStruct((8, n), F32),
        compiler_params=pltpu.CompilerParams(dimension_semantics=("arbitrary",), vmem_limit_bytes=VMEM_LIMIT),
        name="modulation",
    )(craw, ada_w, ada_b)


def _pre_kernel(*refs, grid_shift, tb):
    if grid_shift:
        x_ref, xp_ref, xn_ref = refs[:3]
        refs = refs[3:]
    else:
        x_ref = refs[0]
        refs = refs[1:]
    (mod_ref, n1g_ref, wrkv_ref, wrest_ref, gk1_ref, w1_ref, a1_ref, g1_ref, w2_ref, a2_ref, g2_ref,
     gk2_ref, murkv_ref, muwag_ref, w0_ref, a0_ref, gkb_ref, kk_ref, ka_ref, rk_ref) = refs[:20]
    ra_o, rf_o, rb_o, ga_o, gl_o, pg_o = refs[20:]
    D = D_MODEL
    R = RWKV_WIDTH
    m = mod_ref[0]
    sh1 = m[:, 0:D]
    sc1 = m[:, D:2 * D]
    n1g = n1g_ref[...]

    def normmod(xx):
        return _rms(xx) * n1g * (1.0 + sc1) + sh1

    h = normmod(x_ref[0])
    row = lax.broadcasted_iota(jnp.int32, (tb, 1), 0)
    if grid_shift:
        i = pl.program_id(1)
        n = pl.num_programs(1)
        hp = normmod(xp_ref[0]) * (i > 0).astype(F32)
        hn = normmod(xn_ref[0]) * (i < n - 1).astype(F32)
        hext = jnp.concatenate([hp, h, hn], axis=0)
        col = row % GRID_W
        m_l = (col != 0).astype(F32)
        m_r = (col != GRID_W - 1).astype(F32)
        ne = tb + 2 * GRID_W

        def shift(ext):
            up = ext[0:tb]
            down = ext[2 * GRID_W:2 * GRID_W + tb]
            left = pltpu.roll(ext, 1, 0)[GRID_W:GRID_W + tb]
            right = pltpu.roll(ext, ne - 1, 0)[GRID_W:GRID_W + tb]
            return 0.25 * (up + down + m_l * left + m_r * right)

        halo = GRID_W
    else:
        hext = h
        m_l = (row != 0).astype(F32)
        m_r = (row != tb - 1).astype(F32)

        def shift(ext):
            return 0.5 * (m_l * pltpu.roll(ext, 1, 0) + m_r * pltpu.roll(ext, tb - 1, 0))

        halo = 0

    rest = _bdot(h, wrest_ref)
    dh = shift(hext) - h
    mu = muwag_ref[...]
    lora_w = _bdot(h + mu[0:1] * dh, w1_ref)
    lora_a = _bdot(h + mu[1:2] * dh, a1_ref)
    lora_g = _bdot(h + mu[2:3] * dh, g1_ref)
    lora_gk = _bdot(h, gk1_ref)
    z = w0_ref[...] + _bdot(jnp.tanh(lora_w), w2_ref)
    a = _sigmoid(a0_ref[...] + _bdot(lora_a, a2_ref))
    gate = _bdot(_sigmoid(lora_g), g2_ref)
    logits = _bdot(lora_gk, gk2_ref) + gkb_ref[...]
    rkv_ext = _bdot(hext, wrkv_ref)

    Q = GLA_QK_WIDTH
    ga_o[0, :, 0:Q] = rest[:, 0:Q] * (GLA_KEY_DIM ** -0.5)
    ga_o[0, :, Q:] = rest[:, Q:2 * Q + GLA_V_WIDTH]
    gg = rest[:, 2 * Q + GLA_V_WIDTH:]
    pg_o[0, :, 2 * R:] = gg * _sigmoid(gg)
    gl_o[0] = -_softplus(-logits) * (1.0 / GLA_GATE_NORMALIZER)

    rkv = rkv_ext[halo:halo + tb]
    rkv = rkv + murkv_ref[...] * (shift(rkv_ext) - rkv)
    r = rkv[:, 0:R]
    k = rkv[:, R:2 * R]
    v = rkv[:, 2 * R:3 * R]
    lw = -DECAY_LOG_SCALE * _sigmoid(z)

    kap = k * kk_ref[...]
    kap = kap * lax.rsqrt(jnp.maximum(_head_sum(kap * kap), 1e-12))
    ka = ka_ref[...]
    a_f = a[:, 0:R]
    a_b = a[:, R:2 * R]
    kd_f = k * (1.0 + (a_f - 1.0) * ka)
    kd_b = k * (1.0 + (a_b - 1.0) * ka)
    bonus = _head_sum(r * (kd_f + kd_b) * rk_ref[...]) * v

    ra_o[0, :, 0:R] = r
    ra_o[0, :, R:2 * R] = v
    ra_o[0, :, 2 * R:3 * R] = kap
    rf_o[0, :, 0:R] = kd_f
    rf_o[0, :, R:2 * R] = a_f * kap
    rf_o[0, :, 2 * R:3 * R] = lw[:, 0:R]
    rb_o[0, :, 0:R] = kd_b
    rb_o[0, :, R:2 * R] = a_b * kap
    rb_o[0, :, 2 * R:3 * R] = lw[:, R:2 * R]
    pg_o[0, :, 0:R] = gate
    pg_o[0, :, R:2 * R] = bonus


def _const_spec(shape):
    nd = len(shape)
    return pl.BlockSpec(shape, lambda b, i: (0,) * nd)


def _pre(x, mod3, mod_row, weights, grid_shift):
    B, L, D = x.shape
    tb = PRE_BLOCK
    nblk = L // tb
    if not grid_shift:
        assert nblk == 1
    x_spec = pl.BlockSpec((1, tb, D), lambda b, i: (b, i, 0))
    in_specs = [x_spec]
    args = [x]
    if grid_shift:
        per = tb // GRID_W
        nrow = L // GRID_W
        in_specs += [
            pl.BlockSpec((1, GRID_W, D), lambda b, i: (b, jnp.maximum(i * per - 1, 0), 0)),
            pl.BlockSpec((1, GRID_W, D), lambda b, i: (b, jnp.minimum((i + 1) * per, nrow - 1), 0)),
        ]
        args += [x, x]
    in_specs.append(pl.BlockSpec((1, 1, N_MOD * D), lambda b, i: (mod_row(b), 0, 0)))
    args.append(mod3)
    for w in weights:
        w, spec = w if isinstance(w, tuple) else (w, _const_spec(w.shape))
        in_specs.append(spec)
        args.append(w)
    R = RWKV_WIDTH
    widths = [3 * R, 3 * R, 3 * R, 2 * GLA_QK_WIDTH + GLA_V_WIDTH, 2 * GLA_QK_WIDTH, 2 * R + GLA_V_WIDTH]
    out_specs = [pl.BlockSpec((1, tb, w), lambda b, i: (b, i, 0)) for w in widths]
    out_shape = [jax.ShapeDtypeStruct((B, L, w), F32) for w in widths]
    return pl.pallas_call(
        functools.partial(_pre_kernel, grid_shift=grid_shift, tb=tb),
        grid=(B, nblk),
        in_specs=in_specs,
        out_specs=out_specs,
        out_shape=out_shape,
        compiler_params=pltpu.CompilerParams(dimension_semantics=("parallel", "arbitrary"),
                                             vmem_limit_bytes=VMEM_LIMIT),
        name="pre_grid" if grid_shift else "pre_seq",
    )(*args)


def _interleave(tasks):
    live = list(tasks)
    rnd = 0
    while live:
        keep = []
        for first, gen in live:
            if rnd >= first:
                try:
                    next(gen)
                except StopIteration:
                    continue
            keep.append((first, gen))
        live = keep
        rnd += 1


def _inv_unit_triangular(lms, eye, blk16, blk32):
    mm = functools.partial(_mm, dims=_NN, mode=P_INV)
    n = lms[0].shape[0]
    l0 = [jnp.where(blk16, lm, 0.0) for lm in lms]
    l2 = [mm(a, a) for a in l0]
    yield
    t = [eye - a for a in l0]
    s = [mm(jnp.concatenate([a, b], axis=0), b) for a, b in zip(t, l2)]
    yield
    t = [a + x[0:n] for a, x in zip(t, s)]
    l4 = [x[n:2 * n] for x in s]
    s = [mm(jnp.concatenate([a, b], axis=0), b) for a, b in zip(t, l4)]
    yield
    t = [a + x[0:n] for a, x in zip(t, s)]
    l8 = [x[n:2 * n] for x in s]
    t = [a + mm(a, b) for a, b in zip(t, l8)]
    yield
    off1_mask = jnp.logical_and(blk32, jnp.logical_not(blk16))
    x = [mm(a, jnp.where(off1_mask, lm, 0.0)) for a, lm in zip(t, lms)]
    yield
    t = [a - mm(b, a) for a, b in zip(t, x)]
    yield
    x = [mm(a, jnp.where(blk32, 0.0, lm)) for a, lm in zip(t, lms)]
    yield
    t = [a - mm(b, a) for a, b in zip(t, x)]
    yield
    return t


def _rwkv_prepare(units, cst):
    C = RWKV_CHUNK
    m0, cm0, eye, blk16, blk32, _ = cst[2]

    def stack(x):
        return jnp.concatenate([jnp.where(m0, x, 0.0), jnp.where(m0, 0.0, x)], axis=0)

    def blockdiag(side):
        return jnp.concatenate([jnp.where(cm0, side, 0.0), jnp.where(cm0, 0.0, side)], axis=0)

    cums = [_cumsum_mm(cst[rev][0], lw) for (_, _, _, _, _, lw, rev) in units]
    yield
    prep = []
    for (r, k, v, kap, b, lw, rev), cum in zip(units, cums):
        cumx = cum - lw
        if rev:
            mid = cum[C // 2:C // 2 + 1]
            tot = cum[0:1]
        else:
            mid = cum[C // 2 - 1:C // 2]
            tot = cum[C - 1:C]
        e_mid = jnp.exp(mid)
        rt = r * jnp.exp(cum - mid)
        kt = kap * jnp.exp(cumx - mid)
        es = jnp.exp(mid - cum)
        e_end = jnp.exp(tot - mid)
        prep.append(dict(rt=rt, kt=kt, bh=b * es, kh=k * es, r0=rt * e_mid, k0=stack(kt * e_mid),
                         be=b * es * e_end, ke=k * es * e_end, vs=stack(v), v=v,
                         e_tot=jnp.exp(tot), strict2=cst[rev][1], incl2=cst[rev][2]))
    gs = [_mm(jnp.concatenate([p["kt"], p["rt"]], axis=0),
              jnp.concatenate([stack(p["bh"]), stack(p["kh"])], axis=0), _NT, P_SCORE) for p in prep]
    yield
    for p, g in zip(prep, gs):
        p["ab"] = blockdiag(jnp.where(p["strict2"], g[0:C, 0:2 * C], 0.0))
        p["ak"] = blockdiag(jnp.where(p["strict2"], g[0:C, 2 * C:4 * C], 0.0))
        p["rb"] = jnp.where(p["incl2"], g[C:2 * C, 0:2 * C], 0.0)
        p["rk"] = jnp.where(p["incl2"], g[C:2 * C, 2 * C:4 * C], 0.0)
    av = [_mm(jnp.concatenate([p["ak"], p["rk"]], axis=0), p["vs"], _NN, P_SOLVE) for p in prep]
    yield
    ts = yield from _inv_unit_triangular([p["ab"] for p in prep], eye, blk16, blk32)
    wu = [_mm(t, jnp.concatenate([p["k0"], a[0:2 * C]], axis=1), _NN, P_SOLVE) for t, p, a in zip(ts, prep, av)]
    yield
    return [dict(wr=jnp.concatenate([x[:, :LANES], p["r0"]], axis=0), u0=x[:, LANES:], yv=a[2 * C:3 * C],
                 rb=p["rb"], v=p["v"], bke=jnp.concatenate([p["be"], p["ke"]], axis=0), e_tot=p["e_tot"])
            for x, a, p in zip(wu, av, prep)]


def _rwkv_apply(prep, states, cst):
    C = RWKV_CHUNK
    blk64 = cst[2][5]
    ws = [_mm(p["wr"], s, _NT, P_STATE) for p, s in zip(prep, states)]
    yield
    us = [-p["u0"] - w[0:2 * C] for p, w in zip(prep, ws)]
    upd = [_mm(jnp.concatenate([u[0:C] + u[C:2 * C], p["v"]], axis=0), p["bke"], _TN, P_STATE)
           for u, p in zip(us, prep)]
    yield
    s_new = [s * p["e_tot"] + jnp.where(blk64, d, 0.0) for p, s, d in zip(prep, states, upd)]
    ys = [w[2 * C:3 * C] + _mm(p["rb"], u, _NN, P_OUT) + p["yv"] for p, w, u in zip(prep, ws, us)]
    yield
    return ys, s_new


def _rwkv_consts():
    C = RWKV_CHUNK
    row = lax.broadcasted_iota(jnp.int32, (C, C), 0)
    col = lax.broadcasted_iota(jnp.int32, (C, C), 1)
    row2 = lax.broadcasted_iota(jnp.int32, (C, 2 * C), 0)
    col2 = lax.broadcasted_iota(jnp.int32, (C, 2 * C), 1)
    cs = col2 % C
    lane = lax.broadcasted_iota(jnp.int32, (1, LANES), 1)
    m0 = lane < RWKV_HEAD_DIM
    cm0 = lax.broadcasted_iota(jnp.int32, (1, 2 * C), 1) < C
    rr = lax.broadcasted_iota(jnp.int32, (2 * C, 2 * C), 0)
    cc = lax.broadcasted_iota(jnp.int32, (2 * C, 2 * C), 1)
    eye = (rr == cc).astype(F32)
    blk16 = (rr // 16) == (cc // 16)
    blk32 = (rr // 32) == (cc // 32)
    blk64 = (rr // 64) == (cc // 64)
    fwd = ((col <= row).astype(BF16), cs < row2, cs <= row2)
    bwd = ((col >= row).astype(BF16), cs > row2, cs >= row2)
    return fwd, bwd, (m0, cm0, eye, blk16, blk32, blk64)


def _gla_intra(items, cst):
    G = GLA_CHUNK
    ind_v, sel, blk = cst[2]
    srow = lax.broadcasted_iota(jnp.int32, (G, 1), 0)
    cums = [_cumsum_mm(cst[rev], la) for (_, _, _, la, rev) in items]
    pmats = []
    for (q, k, v, la, rev), cum in zip(items, cums):
        ps = []
        for t in range(G):
            msk = (srow >= t) if rev else (srow <= t)
            e = jnp.exp(jnp.where(msk, cum[t:t + 1] - cum, 0.0))
            ps.append(jnp.where(msk, e * (q[t:t + 1] * k), 0.0))
        pmats.append(jnp.concatenate(ps, axis=0))
    atts = [_mm(pm, ind_v, _NN, P_GATT) for pm in pmats]
    o_intra = [_mm(sel, att * jnp.concatenate([it[2]] * G, axis=0), _NN, P_GATT)
               for att, it in zip(atts, items)]
    out = []
    for (q, k, v, la, rev), cum, oi in zip(items, cums, o_intra):
        tot = cum[0:1] if rev else cum[G - 1:G]
        upd = _mm(v, k * jnp.exp(tot - cum), _TN, P_GSTATE)
        out.append((oi, q * jnp.exp(cum), jnp.where(blk, upd, 0.0), jnp.exp(tot)))
    yield
    return out


def _gla_consts():
    G = GLA_CHUNK
    row = lax.broadcasted_iota(jnp.int32, (G, G), 0)
    col = lax.broadcasted_iota(jnp.int32, (G, G), 1)
    kc = lax.broadcasted_iota(jnp.int32, (LANES, 2 * LANES), 0)
    vc = lax.broadcasted_iota(jnp.int32, (LANES, 2 * LANES), 1)
    ind_v = ((kc // GLA_KEY_DIM) == (vc // GLA_VAL_DIM)).astype(F32)
    st = lax.broadcasted_iota(jnp.int32, (G, G * G), 0)
    sj = lax.broadcasted_iota(jnp.int32, (G, G * G), 1)
    sel = ((sj // G) == st).astype(F32)
    br = lax.broadcasted_iota(jnp.int32, (2 * LANES, LANES), 0)
    bc = lax.broadcasted_iota(jnp.int32, (2 * LANES, LANES), 1)
    blk = (br // GLA_VAL_DIM) == (bc // GLA_KEY_DIM)
    return (col <= row).astype(BF16), (col >= row).astype(BF16), (ind_v, sel, blk)


def _gla_factored(items, cst):
    F = GLA_WIDE_CHUNK
    m0, vm0, blk = cst[2]
    cums = [_cumsum_mm(cst[rev][0], la) for (_, _, _, la, rev) in items]
    yield
    scores = []
    for (q, k, v, la, rev), cum in zip(items, cums):
        mid = cum[F // 2:F // 2 + 1] if rev else cum[F // 2 - 1:F // 2]
        kt = k * jnp.exp(mid - cum)
        kstk = jnp.concatenate([jnp.where(m0, kt, 0.0), jnp.where(m0, 0.0, kt)], axis=0)
        scores.append(_mm(q * jnp.exp(cum - mid), kstk, _NT, P_GATT))
    yield
    o_intra = []
    for (q, k, v, la, rev), sc in zip(items, scores):
        vstk = jnp.concatenate([jnp.where(vm0, v, 0.0), jnp.where(vm0, 0.0, v)], axis=0)
        o_intra.append(_mm(jnp.where(cst[rev][1], sc, 0.0), vstk, _NN, P_GATT))
    yield
    out = []
    for (q, k, v, la, rev), cum, oi in zip(items, cums, o_intra):
        tot = cum[0:1] if rev else cum[F - 1:F]
        upd = _mm(v, k * jnp.exp(tot - cum), _TN, P_GSTATE)
        out.append((oi, q * jnp.exp(cum), jnp.where(blk, upd, 0.0), jnp.exp(tot)))
    yield
    return out


def _gla_wide_consts():
    F = GLA_WIDE_CHUNK
    row = lax.broadcasted_iota(jnp.int32, (F, F), 0)
    col = lax.broadcasted_iota(jnp.int32, (F, F), 1)
    row2 = lax.broadcasted_iota(jnp.int32, (F, 2 * F), 0)
    cs = lax.broadcasted_iota(jnp.int32, (F, 2 * F), 1) % F
    m0 = lax.broadcasted_iota(jnp.int32, (1, LANES), 1) < GLA_KEY_DIM
    vm0 = lax.broadcasted_iota(jnp.int32, (1, 2 * LANES), 1) < GLA_VAL_DIM
    br = lax.broadcasted_iota(jnp.int32, (2 * LANES, LANES), 0)
    bc = lax.broadcasted_iota(jnp.int32, (2 * LANES, LANES), 1)
    blk = (br // GLA_VAL_DIM) == (bc // GLA_KEY_DIM)
    return (((col <= row).astype(BF16), cs <= row2), ((col >= row).astype(BF16), cs >= row2), (m0, vm0, blk))


def _scan_kernel(raf_ref, rdf_ref, rab_ref, rdb_ref, gaf_ref, glf_ref, gab_ref, glb_ref, *refs,
                 has_init, emit_states):
    refs = list(refs)
    init_refs = [refs.pop(0) for _ in range(4)] if has_init else None
    yof_ref, yob_ref = refs.pop(0), refs.pop(0)
    final_refs = [refs.pop(0) for _ in range(4)] if emit_states else None
    srf_ref, srb_ref, sgf_ref, sgb_ref = refs
    i = pl.program_id(1)
    lane = lax.broadcasted_iota(jnp.int32, (1, LANES), 1)
    first_half = lane < LANES // 2
    rr = lax.broadcasted_iota(jnp.int32, (LANES, LANES), 0)
    cc = lax.broadcasted_iota(jnp.int32, (LANES, LANES), 1)
    diag_blocks = (rr // RWKV_HEAD_DIM) == (cc // RWKV_HEAD_DIM)

    @pl.when(i == 0)
    def _():
        if not has_init:
            for s_ref in (srf_ref, srb_ref, sgf_ref, sgb_ref):
                s_ref[...] = jnp.zeros(s_ref.shape, F32)
            return
        for s_ref, init in ((srf_ref, init_refs[0]), (srb_ref, init_refs[1])):
            for p in range(RWKV_WIDTH // LANES):
                x = init[0, p]
                s_ref[0, p] = jnp.where(diag_blocks, jnp.concatenate([x, x], axis=1), 0.0)
        for s_ref, init in ((sgf_ref, init_refs[2]), (sgb_ref, init_refs[3])):
            for p in range(GLA_QK_WIDTH // LANES):
                xt = init[0, p].T
                s_ref[0, p] = jnp.concatenate([jnp.where(first_half, xt, 0.0),
                                               jnp.where(first_half, 0.0, xt)], axis=0)

    T = SCAN_SUB
    C = RWKV_CHUNK
    nr = T // C
    nsub = SCAN_BLOCK // T
    R = RWKV_WIDTH
    Q = GLA_QK_WIDTH
    npair = Q // LANES

    def sub_block(h, carry):
        base = (0, 0) if nsub == 1 else (pl.multiple_of(h * T, T), pl.multiple_of((nsub - 1 - h) * T, T))

        def rows_of(d, start, size):
            return pl.ds(base[d] + start, size)

        rcst = _rwkv_consts()
        g_dirs = ((False, gaf_ref, glf_ref, 0, sgf_ref, yof_ref),
                  (True, gab_ref, glb_ref, Q, sgb_ref, yob_ref))
        la_tot = jnp.minimum(jnp.sum(glf_ref[0, rows_of(0, 0, T), 0:Q], axis=0, keepdims=True),
                             jnp.sum(glb_ref[0, rows_of(1, 0, T), Q:2 * Q], axis=0, keepdims=True))
        gla_split_ok = jnp.min(la_tot) >= -GLA_SAFE_LOG

        r_dirs = ((False, raf_ref, rdf_ref, srf_ref, yof_ref),
                  (True, rab_ref, rdb_ref, srb_ref, yob_ref))
        r_units = {}
        r_dests = {}
        for d, (rev, ra_ref, rd_ref, s_ref, y_ref) in enumerate(r_dirs):
            for j in range(nr):
                c = nr - 1 - j if rev else j
                rows = rows_of(d, c * C, C)
                r_units[d, j] = []
                r_dests[d, j] = []
                for p in range(R // LANES):
                    sl, sl1, sl2 = (slice(o + p * LANES, o + (p + 1) * LANES) for o in (0, R, 2 * R))
                    r_units[d, j].append((ra_ref[0, rows, sl], rd_ref[0, rows, sl], ra_ref[0, rows, sl1],
                                          ra_ref[0, rows, sl2], rd_ref[0, rows, sl1], rd_ref[0, rows, sl2], rev))
                    r_dests[d, j].append((y_ref, s_ref, p, rows, sl))
        r_states = {(d, 0): [s_ref[0, p] for (_, s_ref, p, _, _) in r_dests[d, 0]] for d in range(len(r_dirs))}

        def rwkv_task(d, j):
            prep = yield from _rwkv_prepare(r_units[d, j], rcst)
            while (d, j) not in r_states:
                yield
            ys, r_states[d, j + 1] = yield from _rwkv_apply(prep, r_states[d, j], rcst)
            for (y_ref, _, _, rows, sl), y in zip(r_dests[d, j], ys):
                y_ref[0, rows, sl] = y

        g_init = [[s_ref[0, p] for p in range(npair)] for (_, _, _, _, s_ref, _) in g_dirs]

        def gla_task(chunk, intra, cst):
            ng = T // chunk
            items = [[] for _ in range(ng)]
            dests = [[] for _ in range(ng)]
            for d, (rev, ga_ref, gl_ref, la_off, s_ref, o_ref) in enumerate(g_dirs):
                for p in range(npair):
                    qs, ks, las = (slice(o + p * LANES, o + (p + 1) * LANES) for o in (0, Q, la_off))
                    vs = slice(2 * Q + p * 2 * LANES, 2 * Q + (p + 1) * 2 * LANES)
                    os_ = slice(R + p * 2 * LANES, R + (p + 1) * 2 * LANES)
                    for j in range(ng):
                        c = ng - 1 - j if rev else j
                        rows = rows_of(d, c * chunk, chunk)
                        items[j].append((ga_ref[0, rows, qs], ga_ref[0, rows, ks], ga_ref[0, rows, vs],
                                         gl_ref[0, rows, las], rev))
                        dests[j].append((o_ref, s_ref, d, p, rows, os_))
            parts = yield from intra([it for its in items for it in its], cst)
            nu = len(items[0])
            sts = [g_init[d][p] for (_, _, d, p, _, _) in dests[0]]
            for j in range(ng):
                new = []
                for (o_ref, _, _, _, rows, vs), (o_intra, q_in, upd, e_tot), st in zip(
                        dests[j], parts[j * nu:(j + 1) * nu], sts):
                    o_ref[0, rows, vs] = o_intra + _mm(q_in, st, _NT, P_GSTATE)
                    new.append(st * e_tot + upd)
                sts = new
                yield
            for (_, s_ref, _, p, _, _), st in zip(dests[0], sts):
                s_ref[0, p] = st

        nd = len(r_dirs)
        _interleave([(RWKV_TASK_LAG * j, rwkv_task(d, j)) for j in range(nr) for d in range(nd)]
                    + [(GLA_TASK_LAG, gla_task(GLA_WIDE_CHUNK, _gla_factored, _gla_wide_consts()))])
        for d in range(nd):
            for (_, s_ref, p, _, _), s in zip(r_dests[d, 0], r_states[d, nr]):
                s_ref[0, p] = s

        @pl.when(jnp.logical_not(gla_split_ok))
        def _():
            _interleave([(0, gla_task(GLA_CHUNK, _gla_intra, _gla_consts()))])

        return carry

    if nsub == 1:
        sub_block(0, 0)
    else:
        lax.fori_loop(0, nsub, sub_block, 0)

    if emit_states:
        @pl.when(i == pl.num_programs(1) - 1)
        def _():
            for s_ref, out in ((srf_ref, final_refs[0]), (srb_ref, final_refs[1])):
                for p in range(RWKV_WIDTH // LANES):
                    s = s_ref[0, p]
                    out[0, p] = (s + pltpu.roll(s, LANES // 2, 1))[:, 0:LANES // 2]
            for s_ref, out in ((sgf_ref, final_refs[2]), (sgb_ref, final_refs[3])):
                for p in range(GLA_QK_WIDTH // LANES):
                    st = s_ref[0, p]
                    out[0, p] = st[0:LANES].T + st[LANES:2 * LANES].T


def _scan(ra, rd_f, rd_b, ga, gl, states, emit_states):
    B, L, _ = ra.shape
    R = RWKV_WIDTH
    Q = GLA_QK_WIDTH
    V = GLA_V_WIDTH
    T = SCAN_BLOCK
    n = L // T

    def fwd(w):
        return pl.BlockSpec((1, T, w), lambda b, i: (b, i, 0))

    def bwd(w):
        return pl.BlockSpec((1, T, w), lambda b, i: (b, n - 1 - i, 0))

    rshape = (R // LANES, LANES, LANES // 2)
    gshape = (Q // LANES, LANES, LANES)
    rst = pl.BlockSpec((1,) + rshape, lambda b, i: (b, 0, 0, 0))
    gst = pl.BlockSpec((1,) + gshape, lambda b, i: (b, 0, 0, 0))
    st_specs = [rst, rst, gst, gst]
    st_shapes = [jax.ShapeDtypeStruct((B,) + s, F32) for s in (rshape, rshape, gshape, gshape)]
    states = () if states is None else tuple(states)
    return pl.pallas_call(
        functools.partial(_scan_kernel, has_init=bool(states), emit_states=emit_states),
        grid=(B, n),
        in_specs=[fwd(3 * R), fwd(3 * R), bwd(3 * R), bwd(3 * R),
                  fwd(2 * Q + V), fwd(2 * Q), bwd(2 * Q + V), bwd(2 * Q)] + st_specs[:len(states)],
        out_specs=[fwd(R + V), bwd(R + V)] + (st_specs if emit_states else []),
        out_shape=[jax.ShapeDtypeStruct((B, L, R + V), F32)] * 2 + (st_shapes if emit_states else []),
        scratch_shapes=[pltpu.VMEM((1, R // LANES, LANES, LANES), F32)] * 2
        + [pltpu.VMEM((1, Q // LANES, 2 * LANES, LANES), F32)] * 2,
        compiler_params=pltpu.CompilerParams(dimension_semantics=("parallel", "arbitrary"),
                                             vmem_limit_bytes=VMEM_LIMIT),
        name="scan",
    )(ra, rd_f, ra, rd_b, ga, gl, ga, gl, *states)


def _post_kernel(x_ref, mod_ref, yof_ref, yob_ref, pg_ref,
                 lng_ref, lnb_ref, gng_ref, n2g_ref, fng_ref, wout_ref, w1_ref, w2_ref, o_ref):
    D = D_MODEL
    R = RWKV_WIDTH
    m = mod_ref[0]
    gt1 = m[:, 2 * D:3 * D]
    sh2 = m[:, 3 * D:4 * D]
    sc2 = m[:, 4 * D:5 * D]
    gt2 = m[:, 5 * D:6 * D]
    inv_n = 1.0 / RWKV_HEAD_DIM
    gng = gng_ref[...]
    tb = x_ref.shape[1]
    nsplit = POST_SPLIT if tb % (8 * POST_SPLIT) == 0 else 1
    rows = [slice(s * (tb // nsplit), (s + 1) * (tb // nsplit)) for s in range(nsplit)]

    def mixed(rs):
        y = yof_ref[0, rs, 0:R] + yob_ref[0, rs, 0:R]
        mu = _head_sum(y) * inv_n
        yc = y - mu
        var = _head_sum(yc * yc) * inv_n
        yn = yc * lax.rsqrt(var + LNX_EPS) * lng_ref[...] + lnb_ref[...]
        parts = [(yn + pg_ref[0, rs, R:2 * R]) * pg_ref[0, rs, 0:R]]
        o = yof_ref[0, rs, R:] + yob_ref[0, rs, R:]
        for hh in range(GLA_HEADS):
            sl = slice(hh * GLA_VAL_DIM, (hh + 1) * GLA_VAL_DIM)
            oh = o[:, sl]
            oh = oh * lax.rsqrt(jnp.mean(oh * oh, axis=-1, keepdims=True) + GLA_NORM_EPS)
            parts.append(oh * gng * pg_ref[0, rs, 2 * R + hh * GLA_VAL_DIM:2 * R + (hh + 1) * GLA_VAL_DIM])
        return jnp.concatenate(parts, axis=-1)

    mix = [mixed(rs) for rs in rows]
    x1 = [x_ref[0, rs] + gt1 * _bdot(mx, wout_ref) for rs, mx in zip(rows, mix)]
    h2 = [_rms(v) * n2g_ref[...] * (1.0 + sc2) + sh2 for v in x1]
    f = [jnp.maximum(_bdot(v, w1_ref), 0.0) for v in h2]
    x2 = [a + gt2 * _bdot(v * v, w2_ref) for a, v in zip(x1, f)]
    for rs, v in zip(rows, x2):
        o_ref[0, rs] = _rms(v) * fng_ref[...]


def _post(x, mod3, mod_row, yo_f, yo_b, pg, vecs, mats):
    B, L, D = x.shape
    tb = min(POST_BLOCK, L)
    nblk = L // tb

    def tok(w):
        return pl.BlockSpec((1, tb, w), lambda b, i: (b, i, 0))

    in_specs = [tok(D), pl.BlockSpec((1, 1, N_MOD * D), lambda b, i: (mod_row(b), 0, 0))]
    in_specs += [tok(yo_f.shape[-1]), tok(yo_b.shape[-1]), tok(pg.shape[-1])]
    in_specs += [_const_spec(w.shape) for w in vecs]
    in_specs += [pl.BlockSpec(w.shape, lambda b, i: (0, 0), pipeline_mode=pl.Buffered(1)) for w in mats]
    return pl.pallas_call(
        _post_kernel,
        grid=(B, nblk),
        in_specs=in_specs,
        out_specs=tok(D),
        out_shape=jax.ShapeDtypeStruct((B, L, D), F32),
        compiler_params=pltpu.CompilerParams(dimension_semantics=("parallel", "arbitrary"),
                                             vmem_limit_bytes=VMEM_LIMIT),
        name="post",
    )(x, mod3, yo_f, yo_b, pg, *vecs, *mats)


def _blockdiag2(a, b):
    za = jnp.zeros((a.shape[0], b.shape[1]), a.dtype)
    zb = jnp.zeros((b.shape[0], a.shape[1]), a.dtype)
    return jnp.concatenate([jnp.concatenate([a, za], axis=1), jnp.concatenate([zb, b], axis=1)], axis=0)


def kernel(x_prompt, x_sample, c, state_rwkv_fwd, state_rwkv_bwd, state_gla_fwd, state_gla_bwd, c_ctx, ada_w, ada_b, norm1_g, norm2_g, w_in, rwkv_mu_rkv, rwkv_mu_wag, rwkv_w0, rwkv_w1, rwkv_w2, rwkv_a0, rwkv_a1, rwkv_a2, rwkv_g1, rwkv_g2, rwkv_k_k, rwkv_k_a, rwkv_r_k, rwkv_lnx_g, rwkv_lnx_b, gla_gk1, gla_gk2, gla_gk_b, gla_norm_g, w_out, mlp_w1, mlp_w2, final_norm_g):
    D = D_MODEL
    R = RWKV_WIDTH
    nb = x_prompt.shape[0]
    nd = x_sample.shape[0]
    assert ada_w.shape[0] == 1, "single-layer step"
    layer = 0

    craw = jnp.concatenate([c_ctx[None, :], c, jnp.zeros((8 - 1 - nd, D), F32)], axis=0)
    mod = _modulation(craw, ada_w[layer], ada_b[layer][None, :])
    mod3 = mod.reshape(8, 1, N_MOD * D)

    bf = lambda t: t.astype(BF16)
    row = lambda t: t.reshape(1, -1).astype(F32)
    w_in_bf = bf(w_in[layer])
    half = (D, 3 * R)
    pre_w = [
        row(norm1_g[layer]),
        (w_in_bf, pl.BlockSpec(half, lambda b, i: (0, 0))),
        (w_in_bf, pl.BlockSpec(half, lambda b, i: (0, 1))),
        bf(jnp.concatenate([gla_gk1[layer, 0], gla_gk1[layer, 1]], axis=1)),
        bf(jnp.concatenate([rwkv_w1[layer, 0], rwkv_w1[layer, 1]], axis=1)),
        bf(jnp.concatenate([rwkv_a1[layer, 0], rwkv_a1[layer, 1]], axis=1)),
        bf(rwkv_g1[layer]),
        bf(_blockdiag2(rwkv_w2[layer, 0], rwkv_w2[layer, 1])),
        bf(_blockdiag2(rwkv_a2[layer, 0], rwkv_a2[layer, 1])),
        bf(rwkv_g2[layer]),
        bf(_blockdiag2(gla_gk2[layer, 0], gla_gk2[layer, 1])),
        row(rwkv_mu_rkv[layer]),
        rwkv_mu_wag[layer],
        row(rwkv_w0[layer]),
        row(rwkv_a0[layer]),
        row(gla_gk_b[layer]),
        row(rwkv_k_k[layer]),
        row(rwkv_k_a[layer]),
        row(rwkv_r_k[layer]),
    ]
    post_vecs = [row(rwkv_lnx_g[layer]), row(rwkv_lnx_b[layer]), row(gla_norm_g[layer]),
                 row(norm2_g[layer]), row(final_norm_g)]
    post_mats = [bf(w_out[layer]), bf(mlp_w1[layer]), bf(mlp_w2[layer])]

    def run_group(x, mod_row, grid_shift, states, emit_states):
        ra, rd_f, rd_b, ga, gl, pg = _pre(x, mod3, mod_row, pre_w, grid_shift)
        yo_f, yo_b, *finals = _scan(ra, rd_f, rd_b, ga, gl, states, emit_states)
        y = _post(x, mod3, mod_row, yo_f, yo_b, pg, post_vecs, post_mats)
        return y, finals

    rpair = lambda s: s.reshape(nd, R // LANES, LANES, RWKV_HEAD_DIM)
    gpair = lambda s: s.reshape(nd, GLA_QK_WIDTH // LANES, LANES, GLA_VAL_DIM)
    y_prompt, (n_rf, n_rb, n_gf, n_gb) = run_group(x_prompt, lambda b: 0, False, None, True)
    y_sample, _ = run_group(x_sample, lambda b: b + 1, True,
                            (rpair(state_rwkv_fwd[:, layer]), rpair(state_rwkv_bwd[:, layer]),
                             gpair(state_gla_fwd[:, layer]), gpair(state_gla_bwd[:, layer])), False)
    rshape = (nb, 1, RWKV_HEADS, RWKV_HEAD_DIM, RWKV_HEAD_DIM)
    gshape = (nb, 1, GLA_HEADS, GLA_KEY_DIM, GLA_VAL_DIM)
    return (y_prompt, y_sample, n_rf.reshape(rshape), n_rb.reshape(rshape),
            n_gf.reshape(gshape), n_gb.reshape(gshape))
```

```python
import functools

import jax
import jax.numpy as jnp
from jax import lax
from jax.experimental import pallas as pl
from jax.experimental.pallas import tpu as pltpu

F32 = jnp.float32
BF16 = jnp.bfloat16

D_MODEL = 1024
GRID_W = 64
RWKV_WIDTH = 512
RWKV_HEAD_DIM = 64
RWKV_HEADS = 8
GLA_HEADS = 4
GLA_KEY_DIM = 64
GLA_VAL_DIM = 128
GLA_QK_WIDTH = 256
GLA_V_WIDTH = 512
GLA_GATE_NORMALIZER = 16.0
N_MOD = 6
RMS_EPS = 1e-6
LNX_EPS = 64e-5
GLA_NORM_EPS = 1e-5

LANES = 128
RWKV_CHUNK = 64
GLA_CHUNK = 16
GLA_WIDE_CHUNK = 64
GLA_SAFE_LOG = 60.0
SCAN_BLOCK = 256
SCAN_SUB = 128
PRE_BLOCK = 256
POST_BLOCK = 512
POST_SPLIT = 2
VMEM_LIMIT = 56 * 1024 * 1024
RWKV_TASK_LAG = 2
GLA_TASK_LAG = 14
DECAY_LOG_SCALE = 0.6065306597126334

_NN = (((1,), (0,)), ((), ()))
_NT = (((1,), (1,)), ((), ()))
_TN = (((0,), (0,)), ((), ()))


def _dot(a, b):
    return lax.dot_general(a, b, _NN, preferred_element_type=F32)


def _mm(a, b, dims):
    return lax.dot_general(a.astype(BF16), b.astype(BF16), dims, preferred_element_type=F32)


def _cumsum_mm(tri, x):
    hi = x.astype(BF16)
    r1 = x - hi.astype(F32)
    mid = r1.astype(BF16)
    lo = (r1 - mid.astype(F32)).astype(BF16)
    return _dot(tri, hi) + _dot(tri, mid) + _dot(tri, lo)


def _bdot(a, w_ref):
    return _dot(a.astype(BF16), w_ref[...])


def _sigmoid(x):
    return 1.0 / (1.0 + jnp.exp(-x))


def _softplus(x):
    return jnp.maximum(x, 0.0) + jnp.log(1.0 + jnp.exp(-jnp.abs(x)))


def _head_sum(x):
    first = lax.broadcasted_iota(jnp.int32, (1, LANES), 1) < RWKV_HEAD_DIM
    outs = []
    for j in range(x.shape[-1] // LANES):
        xb = x[:, j * LANES:(j + 1) * LANES]
        s0 = jnp.sum(jnp.where(first, xb, 0.0), axis=-1, keepdims=True)
        s1 = jnp.sum(jnp.where(first, 0.0, xb), axis=-1, keepdims=True)
        outs.append(jnp.where(first, s0, s1))
    return jnp.concatenate(outs, axis=-1)


def _rms(x):
    return x * lax.rsqrt(jnp.mean(x * x, axis=-1, keepdims=True) + RMS_EPS)


def _mod_kernel(c_ref, w_ref, b_ref, o_ref):
    c = c_ref[...]
    cond = c * _sigmoid(c)
    o_ref[...] = _dot(cond.astype(BF16), w_ref[...].astype(BF16)) + b_ref[...]


def _modulation(craw, ada_w, ada_b):
    n = ada_w.shape[1]
    bn = 1536
    return pl.pallas_call(
        _mod_kernel,
        grid=(n // bn,),
        in_specs=[
            pl.BlockSpec((8, D_MODEL), lambda j: (0, 0)),
            pl.BlockSpec((D_MODEL, bn), lambda j: (0, j)),
            pl.BlockSpec((1, bn), lambda j: (0, j)),
        ],
        out_specs=pl.BlockSpec((8, bn), lambda j: (0, j)),
        out_shape=jax.ShapeDtypeStruct((8, n), F32),
        compiler_params=pltpu.CompilerParams(dimension_semantics=("arbitrary",), vmem_limit_bytes=VMEM_LIMIT),
        name="modulation",
    )(craw, ada_w, ada_b)


def _pre_kernel(*refs, grid_shift, tb):
    if grid_shift:
        x_ref, xp_ref, xn_ref = refs[:3]
        refs = refs[3:]
    else:
        x_ref = refs[0]
        refs = refs[1:]
    (mod_ref, n1g_ref, wrkv_ref, wrest_ref, gk1_ref, w1_ref, a1_ref, g1_ref, w2_ref, a2_ref, g2_ref,
     gk2_ref, murkv_ref, muwag_ref, w0_ref, a0_ref, gkb_ref, kk_ref, ka_ref, rk_ref) = refs[:20]
    ra_o, rf_o, rb_o, ga_o, gl_o, pg_o = refs[20:]
    D = D_MODEL
    R = RWKV_WIDTH
    m = mod_ref[0]
    sh1 = m[:, 0:D]
    sc1 = m[:, D:2 * D]
    n1g = n1g_ref[...]

    def normmod(xx):
        return _rms(xx) * n1g * (1.0 + sc1) + sh1

    h = normmod(x_ref[0])
    row = lax.broadcasted_iota(jnp.int32, (tb, 1), 0)
    if grid_shift:
        i = pl.program_id(1)
        n = pl.num_programs(1)
        hp = normmod(xp_ref[0]) * (i > 0).astype(F32)
        hn = normmod(xn_ref[0]) * (i < n - 1).astype(F32)
        hext = jnp.concatenate([hp, h, hn], axis=0)
        col = row % GRID_W
        m_l = (col != 0).astype(F32)
        m_r = (col != GRID_W - 1).astype(F32)
        ne = tb + 2 * GRID_W

        def shift(ext):
            up = ext[0:tb]
            down = ext[2 * GRID_W:2 * GRID_W + tb]
            left = pltpu.roll(ext, 1, 0)[GRID_W:GRID_W + tb]
            right = pltpu.roll(ext, ne - 1, 0)[GRID_W:GRID_W + tb]
            return 0.25 * (up + down + m_l * left + m_r * right)

        halo = GRID_W
    else:
        hext = h
        m_l = (row != 0).astype(F32)
        m_r = (row != tb - 1).astype(F32)

        def shift(ext):
            return 0.5 * (m_l * pltpu.roll(ext, 1, 0) + m_r * pltpu.roll(ext, tb - 1, 0))

        halo = 0

    rest = _bdot(h, wrest_ref)
    dh = shift(hext) - h
    mu = muwag_ref[...]
    lora_w = _bdot(h + mu[0:1] * dh, w1_ref)
    lora_a = _bdot(h + mu[1:2] * dh, a1_ref)
    lora_g = _bdot(h + mu[2:3] * dh, g1_ref)
    lora_gk = _bdot(h, gk1_ref)
    z = w0_ref[...] + _bdot(jnp.tanh(lora_w), w2_ref)
    a = _sigmoid(a0_ref[...] + _bdot(lora_a, a2_ref))
    gate = _bdot(_sigmoid(lora_g), g2_ref)
    logits = _bdot(lora_gk, gk2_ref) + gkb_ref[...]
    rkv_ext = _bdot(hext, wrkv_ref)

    Q = GLA_QK_WIDTH
    ga_o[0, :, 0:Q] = rest[:, 0:Q] * (GLA_KEY_DIM ** -0.5)
    ga_o[0, :, Q:] = rest[:, Q:2 * Q + GLA_V_WIDTH]
    gg = rest[:, 2 * Q + GLA_V_WIDTH:]
    pg_o[0, :, 2 * R:] = gg * _sigmoid(gg)
    gl_o[0] = -_softplus(-logits) * (1.0 / GLA_GATE_NORMALIZER)

    rkv = rkv_ext[halo:halo + tb]
    rkv = rkv + murkv_ref[...] * (shift(rkv_ext) - rkv)
    r = rkv[:, 0:R]
    k = rkv[:, R:2 * R]
    v = rkv[:, 2 * R:3 * R]
    lw = -DECAY_LOG_SCALE * _sigmoid(z)

    kap = k * kk_ref[...]
    kap = kap * lax.rsqrt(jnp.maximum(_head_sum(kap * kap), 1e-12))
    ka = ka_ref[...]
    a_f = a[:, 0:R]
    a_b = a[:, R:2 * R]
    kd_f = k * (1.0 + (a_f - 1.0) * ka)
    kd_b = k * (1.0 + (a_b - 1.0) * ka)
    bonus = _head_sum(r * (kd_f + kd_b) * rk_ref[...]) * v

    ra_o[0, :, 0:R] = r
    ra_o[0, :, R:2 * R] = v
    ra_o[0, :, 2 * R:3 * R] = kap
    rf_o[0, :, 0:R] = kd_f
    rf_o[0, :, R:2 * R] = a_f * kap
    rf_o[0, :, 2 * R:3 * R] = lw[:, 0:R]
    rb_o[0, :, 0:R] = kd_b
    rb_o[0, :, R:2 * R] = a_b * kap
    rb_o[0, :, 2 * R:3 * R] = lw[:, R:2 * R]
    pg_o[0, :, 0:R] = gate
    pg_o[0, :, R:2 * R] = bonus


def _const_spec(shape):
    nd = len(shape)
    return pl.BlockSpec(shape, lambda b, i: (0,) * nd)


def _pre(x, mod3, mod_row, weights, grid_shift):
    B, L, D = x.shape
    tb = PRE_BLOCK
    nblk = L // tb
    if not grid_shift:
        assert nblk == 1
    x_spec = pl.BlockSpec((1, tb, D), lambda b, i: (b, i, 0))
    in_specs = [x_spec]
    args = [x]
    if grid_shift:
        per = tb // GRID_W
        nrow = L // GRID_W
        in_specs += [
            pl.BlockSpec((1, GRID_W, D), lambda b, i: (b, jnp.maximum(i * per - 1, 0), 0)),
            pl.BlockSpec((1, GRID_W, D), lambda b, i: (b, jnp.minimum((i + 1) * per, nrow - 1), 0)),
        ]
        args += [x, x]
    in_specs.append(pl.BlockSpec((1, 1, N_MOD * D), lambda b, i: (mod_row(b), 0, 0)))
    args.append(mod3)
    for w in weights:
        w, spec = w if isinstance(w, tuple) else (w, _const_spec(w.shape))
        in_specs.append(spec)
        args.append(w)
    R = RWKV_WIDTH
    widths = [3 * R, 3 * R, 3 * R, 2 * GLA_QK_WIDTH + GLA_V_WIDTH, 2 * GLA_QK_WIDTH, 2 * R + GLA_V_WIDTH]
    out_specs = [pl.BlockSpec((1, tb, w), lambda b, i: (b, i, 0)) for w in widths]
    out_shape = [jax.ShapeDtypeStruct((B, L, w), F32) for w in widths]
    return pl.pallas_call(
        functools.partial(_pre_kernel, grid_shift=grid_shift, tb=tb),
        grid=(B, nblk),
        in_specs=in_specs,
        out_specs=out_specs,
        out_shape=out_shape,
        compiler_params=pltpu.CompilerParams(dimension_semantics=("parallel", "arbitrary"),
                                             vmem_limit_bytes=VMEM_LIMIT),
        name="pre_grid" if grid_shift else "pre_seq",
    )(*args)


def _interleave(tasks):
    live = list(tasks)
    rnd = 0
    while live:
        keep = []
        for first, gen in live:
            if rnd >= first:
                try:
                    next(gen)
                except StopIteration:
                    continue
            keep.append((first, gen))
        live = keep
        rnd += 1


def _inv_unit_triangular(lms, eye, blk16, blk32):
    mm = functools.partial(_mm, dims=_NN)
    n = lms[0].shape[0]
    l0 = [jnp.where(blk16, lm, 0.0) for lm in lms]
    l2 = [mm(a, a) for a in l0]
    yield
    t = [eye - a for a in l0]
    s = [mm(jnp.concatenate([a, b], axis=0), b) for a, b in zip(t, l2)]
    yield
    t = [a + x[0:n] for a, x in zip(t, s)]
    l4 = [x[n:2 * n] for x in s]
    s = [mm(jnp.concatenate([a, b], axis=0), b) for a, b in zip(t, l4)]
    yield
    t = [a + x[0:n] for a, x in zip(t, s)]
    l8 = [x[n:2 * n] for x in s]
    t = [a + mm(a, b) for a, b in zip(t, l8)]
    yield
    off1_mask = jnp.logical_and(blk32, jnp.logical_not(blk16))
    x = [mm(a, jnp.where(off1_mask, lm, 0.0)) for a, lm in zip(t, lms)]
    yield
    t = [a - mm(b, a) for a, b in zip(t, x)]
    yield
    x = [mm(a, jnp.where(blk32, 0.0, lm)) for a, lm in zip(t, lms)]
    yield
    t = [a - mm(b, a) for a, b in zip(t, x)]
    yield
    return t


def _rwkv_prepare(units, cst):
    C = RWKV_CHUNK
    m0, cm0, eye, blk16, blk32, _ = cst[2]

    def stack(x):
        return jnp.concatenate([jnp.where(m0, x, 0.0), jnp.where(m0, 0.0, x)], axis=0)

    def blockdiag(side):
        return jnp.concatenate([jnp.where(cm0, side, 0.0), jnp.where(cm0, 0.0, side)], axis=0)

    cums = [_cumsum_mm(cst[rev][0], lw) for (_, _, _, _, _, lw, rev) in units]
    yield
    prep = []
    for (r, k, v, kap, b, lw, rev), cum in zip(units, cums):
        cumx = cum - lw
        if rev:
            mid = cum[C // 2:C // 2 + 1]
            tot = cum[0:1]
        else:
            mid = cum[C // 2 - 1:C // 2]
            tot = cum[C - 1:C]
        e_mid = jnp.exp(mid)
        rt = r * jnp.exp(cum - mid)
        kt = kap * jnp.exp(cumx - mid)
        es = jnp.exp(mid - cum)
        e_end = jnp.exp(tot - mid)
        prep.append(dict(rt=rt, kt=kt, bh=b * es, kh=k * es, r0=rt * e_mid, k0=stack(kt * e_mid),
                         be=b * es * e_end, ke=k * es * e_end, vs=stack(v), v=v,
                         e_tot=jnp.exp(tot), strict2=cst[rev][1], incl2=cst[rev][2]))
    gs = [_mm(jnp.concatenate([p["kt"], p["rt"]], axis=0),
              jnp.concatenate([stack(p["bh"]), stack(p["kh"])], axis=0), _NT) for p in prep]
    yield
    for p, g in zip(prep, gs):
        p["ab"] = blockdiag(jnp.where(p["strict2"], g[0:C, 0:2 * C], 0.0))
        p["ak"] = blockdiag(jnp.where(p["strict2"], g[0:C, 2 * C:4 * C], 0.0))
        p["rb"] = jnp.where(p["incl2"], g[C:2 * C, 0:2 * C], 0.0)
        p["rk"] = jnp.where(p["incl2"], g[C:2 * C, 2 * C:4 * C], 0.0)
    akv = [_mm(p["ak"], p["vs"], _NN) for p in prep]
    yield
    ts = yield from _inv_unit_triangular([p["ab"] for p in prep], eye, blk16, blk32)
    wu = [_mm(t, jnp.concatenate([p["k0"], a], axis=1), _NN) for t, p, a in zip(ts, prep, akv)]
    yield
    return [dict(wr=jnp.concatenate([x[:, :LANES], p["r0"]], axis=0), u0=x[:, LANES:],
                 rbk=jnp.concatenate([p["rb"], p["rk"]], axis=1), vs=p["vs"], v=p["v"],
                 bke=jnp.concatenate([p["be"], p["ke"]], axis=0), e_tot=p["e_tot"])
            for x, p in zip(wu, prep)]


def _rwkv_apply(prep, states, cst):
    C = RWKV_CHUNK
    blk64 = cst[2][5]
    ws = [_mm(p["wr"], s, _NT) for p, s in zip(prep, states)]
    yield
    us = [-p["u0"] - w[0:2 * C] for p, w in zip(prep, ws)]
    upd = [_mm(jnp.concatenate([u[0:C] + u[C:2 * C], p["v"]], axis=0), p["bke"], _TN)
           for u, p in zip(us, prep)]
    yield
    s_new = [s * p["e_tot"] + jnp.where(blk64, d, 0.0) for p, s, d in zip(prep, states, upd)]
    ys = [w[2 * C:3 * C] + _mm(p["rbk"], jnp.concatenate([u, p["vs"]], axis=0), _NN)
          for p, w, u in zip(prep, ws, us)]
    yield
    return ys, s_new


def _rwkv_consts():
    C = RWKV_CHUNK
    row = lax.broadcasted_iota(jnp.int32, (C, C), 0)
    col = lax.broadcasted_iota(jnp.int32, (C, C), 1)
    row2 = lax.broadcasted_iota(jnp.int32, (C, 2 * C), 0)
    col2 = lax.broadcasted_iota(jnp.int32, (C, 2 * C), 1)
    cs = col2 % C
    lane = lax.broadcasted_iota(jnp.int32, (1, LANES), 1)
    m0 = lane < RWKV_HEAD_DIM
    cm0 = lax.broadcasted_iota(jnp.int32, (1, 2 * C), 1) < C
    rr = lax.broadcasted_iota(jnp.int32, (2 * C, 2 * C), 0)
    cc = lax.broadcasted_iota(jnp.int32, (2 * C, 2 * C), 1)
    eye = (rr == cc).astype(F32)
    blk16 = (rr // 16) == (cc // 16)
    blk32 = (rr // 32) == (cc // 32)
    blk64 = (rr // 64) == (cc // 64)
    fwd = ((col <= row).astype(BF16), cs < row2, cs <= row2)
    bwd = ((col >= row).astype(BF16), cs > row2, cs >= row2)
    return fwd, bwd, (m0, cm0, eye, blk16, blk32, blk64)


def _gla_intra(items, cst):
    G = GLA_CHUNK
    ind_v, sel, blk = cst[2]
    srow = lax.broadcasted_iota(jnp.int32, (G, 1), 0)
    cums = [_cumsum_mm(cst[rev], la) for (_, _, _, la, rev) in items]
    pmats = []
    for (q, k, v, la, rev), cum in zip(items, cums):
        ps = []
        for t in range(G):
            msk = (srow >= t) if rev else (srow <= t)
            e = jnp.exp(jnp.where(msk, cum[t:t + 1] - cum, 0.0))
            ps.append(jnp.where(msk, e * (q[t:t + 1] * k), 0.0))
        pmats.append(jnp.concatenate(ps, axis=0))
    atts = [_mm(pm, ind_v, _NN) for pm in pmats]
    o_intra = [_mm(sel, att * jnp.concatenate([it[2]] * G, axis=0), _NN)
               for att, it in zip(atts, items)]
    out = []
    for (q, k, v, la, rev), cum, oi in zip(items, cums, o_intra):
        tot = cum[0:1] if rev else cum[G - 1:G]
        upd = _mm(v, k * jnp.exp(tot - cum), _TN)
        out.append((oi, q * jnp.exp(cum), jnp.where(blk, upd, 0.0), jnp.exp(tot)))
    yield
    return out


def _gla_consts():
    G = GLA_CHUNK
    row = lax.broadcasted_iota(jnp.int32, (G, G), 0)
    col = lax.broadcasted_iota(jnp.int32, (G, G), 1)
    kc = lax.broadcasted_iota(jnp.int32, (LANES, 2 * LANES), 0)
    vc = lax.broadcasted_iota(jnp.int32, (LANES, 2 * LANES), 1)
    ind_v = ((kc // GLA_KEY_DIM) == (vc // GLA_VAL_DIM)).astype(F32)
    st = lax.broadcasted_iota(jnp.int32, (G, G * G), 0)
    sj = lax.broadcasted_iota(jnp.int32, (G, G * G), 1)
    sel = ((sj // G) == st).astype(F32)
    br = lax.broadcasted_iota(jnp.int32, (2 * LANES, LANES), 0)
    bc = lax.broadcasted_iota(jnp.int32, (2 * LANES, LANES), 1)
    blk = (br // GLA_VAL_DIM) == (bc // GLA_KEY_DIM)
    return (col <= row).astype(BF16), (col >= row).astype(BF16), (ind_v, sel, blk)


def _gla_factored(items, cst):
    F = GLA_WIDE_CHUNK
    m0, vm0, blk = cst[2]
    cums = [_cumsum_mm(cst[rev][0], la) for (_, _, _, la, rev) in items]
    yield
    scores = []
    for (q, k, v, la, rev), cum in zip(items, cums):
        mid = cum[F // 2:F // 2 + 1] if rev else cum[F // 2 - 1:F // 2]
        kt = k * jnp.exp(mid - cum)
        kstk = jnp.concatenate([jnp.where(m0, kt, 0.0), jnp.where(m0, 0.0, kt)], axis=0)
        scores.append(_mm(q * jnp.exp(cum - mid), kstk, _NT))
    yield
    o_intra = []
    for (q, k, v, la, rev), sc in zip(items, scores):
        vstk = jnp.concatenate([jnp.where(vm0, v, 0.0), jnp.where(vm0, 0.0, v)], axis=0)
        o_intra.append(_mm(jnp.where(cst[rev][1], sc, 0.0), vstk, _NN))
    yield
    out = []
    for (q, k, v, la, rev), cum, oi in zip(items, cums, o_intra):
        tot = cum[0:1] if rev else cum[F - 1:F]
        upd = _mm(v, k * jnp.exp(tot - cum), _TN)
        out.append((oi, q * jnp.exp(cum), jnp.where(blk, upd, 0.0), jnp.exp(tot)))
    yield
    return out


def _gla_wide_consts():
    F = GLA_WIDE_CHUNK
    row = lax.broadcasted_iota(jnp.int32, (F, F), 0)
    col = lax.broadcasted_iota(jnp.int32, (F, F), 1)
    row2 = lax.broadcasted_iota(jnp.int32, (F, 2 * F), 0)
    cs = lax.broadcasted_iota(jnp.int32, (F, 2 * F), 1) % F
    m0 = lax.broadcasted_iota(jnp.int32, (1, LANES), 1) < GLA_KEY_DIM
    vm0 = lax.broadcasted_iota(jnp.int32, (1, 2 * LANES), 1) < GLA_VAL_DIM
    br = lax.broadcasted_iota(jnp.int32, (2 * LANES, LANES), 0)
    bc = lax.broadcasted_iota(jnp.int32, (2 * LANES, LANES), 1)
    blk = (br // GLA_VAL_DIM) == (bc // GLA_KEY_DIM)
    return (((col <= row).astype(BF16), cs <= row2), ((col >= row).astype(BF16), cs >= row2), (m0, vm0, blk))


def _scan_kernel(raf_ref, rdf_ref, rab_ref, rdb_ref, gaf_ref, glf_ref, gab_ref, glb_ref, *refs,
                 has_init, emit_states):
    refs = list(refs)
    init_refs = [refs.pop(0) for _ in range(4)] if has_init else None
    yof_ref, yob_ref = refs.pop(0), refs.pop(0)
    final_refs = [refs.pop(0) for _ in range(4)] if emit_states else None
    srf_ref, srb_ref, sgf_ref, sgb_ref = refs
    i = pl.program_id(1)
    lane = lax.broadcasted_iota(jnp.int32, (1, LANES), 1)
    first_half = lane < LANES // 2
    rr = lax.broadcasted_iota(jnp.int32, (LANES, LANES), 0)
    cc = lax.broadcasted_iota(jnp.int32, (LANES, LANES), 1)
    diag_blocks = (rr // RWKV_HEAD_DIM) == (cc // RWKV_HEAD_DIM)

    @pl.when(i == 0)
    def _():
        if not has_init:
            for s_ref in (srf_ref, srb_ref, sgf_ref, sgb_ref):
                s_ref[...] = jnp.zeros(s_ref.shape, F32)
            return
        for s_ref, init in ((srf_ref, init_refs[0]), (srb_ref, init_refs[1])):
            for p in range(RWKV_WIDTH // LANES):
                x = init[0, p]
                s_ref[0, p] = jnp.where(diag_blocks, jnp.concatenate([x, x], axis=1), 0.0)
        for s_ref, init in ((sgf_ref, init_refs[2]), (sgb_ref, init_refs[3])):
            for p in range(GLA_QK_WIDTH // LANES):
                xt = init[0, p].T
                s_ref[0, p] = jnp.concatenate([jnp.where(first_half, xt, 0.0),
                                               jnp.where(first_half, 0.0, xt)], axis=0)

    T = SCAN_SUB
    C = RWKV_CHUNK
    nr = T // C
    nsub = SCAN_BLOCK // T
    R = RWKV_WIDTH
    Q = GLA_QK_WIDTH
    npair = Q // LANES

    def sub_block(h, carry):
        base = (0, 0) if nsub == 1 else (pl.multiple_of(h * T, T), pl.multiple_of((nsub - 1 - h) * T, T))

        def rows_of(d, start, size):
            return pl.ds(base[d] + start, size)

        rcst = _rwkv_consts()
        g_dirs = ((False, gaf_ref, glf_ref, 0, sgf_ref, yof_ref),
                  (True, gab_ref, glb_ref, Q, sgb_ref, yob_ref))
        la_tot = jnp.minimum(jnp.sum(glf_ref[0, rows_of(0, 0, T), 0:Q], axis=0, keepdims=True),
                             jnp.sum(glb_ref[0, rows_of(1, 0, T), Q:2 * Q], axis=0, keepdims=True))
        gla_split_ok = jnp.min(la_tot) >= -GLA_SAFE_LOG

        r_dirs = ((False, raf_ref, rdf_ref, srf_ref, yof_ref),
                  (True, rab_ref, rdb_ref, srb_ref, yob_ref))
        r_units = {}
        r_dests = {}
        for d, (rev, ra_ref, rd_ref, s_ref, y_ref) in enumerate(r_dirs):
            for j in range(nr):
                c = nr - 1 - j if rev else j
                rows = rows_of(d, c * C, C)
                r_units[d, j] = []
                r_dests[d, j] = []
                for p in range(R // LANES):
                    sl, sl1, sl2 = (slice(o + p * LANES, o + (p + 1) * LANES) for o in (0, R, 2 * R))
                    r_units[d, j].append((ra_ref[0, rows, sl], rd_ref[0, rows, sl], ra_ref[0, rows, sl1],
                                          ra_ref[0, rows, sl2], rd_ref[0, rows, sl1], rd_ref[0, rows, sl2], rev))
                    r_dests[d, j].append((y_ref, s_ref, p, rows, sl))
        r_states = {(d, 0): [s_ref[0, p] for (_, s_ref, p, _, _) in r_dests[d, 0]] for d in range(len(r_dirs))}

        def rwkv_task(d, j):
            prep = yield from _rwkv_prepare(r_units[d, j], rcst)
            while (d, j) not in r_states:
                yield
            ys, r_states[d, j + 1] = yield from _rwkv_apply(prep, r_states[d, j], rcst)
            for (y_ref, _, _, rows, sl), y in zip(r_dests[d, j], ys):
                y_ref[0, rows, sl] = y

        g_init = [[s_ref[0, p] for p in range(npair)] for (_, _, _, _, s_ref, _) in g_dirs]

        def gla_task(chunk, intra, cst):
            ng = T // chunk
            items = [[] for _ in range(ng)]
            dests = [[] for _ in range(ng)]
            for d, (rev, ga_ref, gl_ref, la_off, s_ref, o_ref) in enumerate(g_dirs):
                for p in range(npair):
                    qs, ks, las = (slice(o + p * LANES, o + (p + 1) * LANES) for o in (0, Q, la_off))
                    vs = slice(2 * Q + p * 2 * LANES, 2 * Q + (p + 1) * 2 * LANES)
                    os_ = slice(R + p * 2 * LANES, R + (p + 1) * 2 * LANES)
                    for j in range(ng):
                        c = ng - 1 - j if rev else j
                        rows = rows_of(d, c * chunk, chunk)
                        items[j].append((ga_ref[0, rows, qs], ga_ref[0, rows, ks], ga_ref[0, rows, vs],
                                         gl_ref[0, rows, las], rev))
                        dests[j].append((o_ref, s_ref, d, p, rows, os_))
            parts = yield from intra([it for its in items for it in its], cst)
            nu = len(items[0])
            sts = [g_init[d][p] for (_, _, d, p, _, _) in dests[0]]
            for j in range(ng):
                new = []
                for (o_ref, _, _, _, rows, vs), (o_intra, q_in, upd, e_tot), st in zip(
                        dests[j], parts[j * nu:(j + 1) * nu], sts):
                    o_ref[0, rows, vs] = o_intra + _mm(q_in, st, _NT)
                    new.append(st * e_tot + upd)
                sts = new
                yield
            for (_, s_ref, _, p, _, _), st in zip(dests[0], sts):
                s_ref[0, p] = st

        nd = len(r_dirs)
        _interleave([(RWKV_TASK_LAG * j, rwkv_task(d, j)) for j in range(nr) for d in range(nd)]
                    + [(GLA_TASK_LAG, gla_task(GLA_WIDE_CHUNK, _gla_factored, _gla_wide_consts()))])
        for d in range(nd):
            for (_, s_ref, p, _, _), s in zip(r_dests[d, 0], r_states[d, nr]):
                s_ref[0, p] = s

        @pl.when(jnp.logical_not(gla_split_ok))
        def _():
            _interleave([(0, gla_task(GLA_CHUNK, _gla_intra, _gla_consts()))])

        return carry

    if nsub == 1:
        sub_block(0, 0)
    else:
        lax.fori_loop(0, nsub, sub_block, 0)

    if emit_states:
        @pl.when(i == pl.num_programs(1) - 1)
        def _():
            for s_ref, out in ((srf_ref, final_refs[0]), (srb_ref, final_refs[1])):
                for p in range(RWKV_WIDTH // LANES):
                    s = s_ref[0, p]
                    out[0, p] = (s + pltpu.roll(s, LANES // 2, 1))[:, 0:LANES // 2]
            for s_ref, out in ((sgf_ref, final_refs[2]), (sgb_ref, final_refs[3])):
                for p in range(GLA_QK_WIDTH // LANES):
                    st = s_ref[0, p]
                    out[0, p] = st[0:LANES].T + st[LANES:2 * LANES].T


def _scan(ra, rd_f, rd_b, ga, gl, states, emit_states):
    B, L, _ = ra.shape
    R = RWKV_WIDTH
    Q = GLA_QK_WIDTH
    V = GLA_V_WIDTH
    T = SCAN_BLOCK
    n = L // T

    def fwd(w):
        return pl.BlockSpec((1, T, w), lambda b, i: (b, i, 0))

    def bwd(w):
        return pl.BlockSpec((1, T, w), lambda b, i: (b, n - 1 - i, 0))

    rshape = (R // LANES, LANES, LANES // 2)
    gshape = (Q // LANES, LANES, LANES)
    rst = pl.BlockSpec((1,) + rshape, lambda b, i: (b, 0, 0, 0))
    gst = pl.BlockSpec((1,) + gshape, lambda b, i: (b, 0, 0, 0))
    st_specs = [rst, rst, gst, gst]
    st_shapes = [jax.ShapeDtypeStruct((B,) + s, F32) for s in (rshape, rshape, gshape, gshape)]
    states = () if states is None else tuple(states)
    return pl.pallas_call(
        functools.partial(_scan_kernel, has_init=bool(states), emit_states=emit_states),
        grid=(B, n),
        in_specs=[fwd(3 * R), fwd(3 * R), bwd(3 * R), bwd(3 * R),
                  fwd(2 * Q + V), fwd(2 * Q), bwd(2 * Q + V), bwd(2 * Q)] + st_specs[:len(states)],
        out_specs=[fwd(R + V), bwd(R + V)] + (st_specs if emit_states else []),
        out_shape=[jax.ShapeDtypeStruct((B, L, R + V), F32)] * 2 + (st_shapes if emit_states else []),
        scratch_shapes=[pltpu.VMEM((1, R // LANES, LANES, LANES), F32)] * 2
        + [pltpu.VMEM((1, Q // LANES, 2 * LANES, LANES), F32)] * 2,
        compiler_params=pltpu.CompilerParams(dimension_semantics=("parallel", "arbitrary"),
                                             vmem_limit_bytes=VMEM_LIMIT),
        name="scan",
    )(ra, rd_f, ra, rd_b, ga, gl, ga, gl, *states)


def _post_kernel(x_ref, mod_ref, yof_ref, yob_ref, pg_ref,
                 lng_ref, lnb_ref, gng_ref, n2g_ref, fng_ref, wout_ref, w1_ref, w2_ref, o_ref):
    D = D_MODEL
    R = RWKV_WIDTH
    m = mod_ref[0]
    gt1 = m[:, 2 * D:3 * D]
    sh2 = m[:, 3 * D:4 * D]
    sc2 = m[:, 4 * D:5 * D]
    gt2 = m[:, 5 * D:6 * D]
    inv_n = 1.0 / RWKV_HEAD_DIM
    gng = gng_ref[...]
    tb = x_ref.shape[1]
    nsplit = POST_SPLIT if tb % (8 * POST_SPLIT) == 0 else 1
    rows = [slice(s * (tb // nsplit), (s + 1) * (tb // nsplit)) for s in range(nsplit)]

    def mixed(rs):
        y = yof_ref[0, rs, 0:R] + yob_ref[0, rs, 0:R]
        mu = _head_sum(y) * inv_n
        yc = y - mu
        var = _head_sum(yc * yc) * inv_n
        yn = yc * lax.rsqrt(var + LNX_EPS) * lng_ref[...] + lnb_ref[...]
        parts = [(yn + pg_ref[0, rs, R:2 * R]) * pg_ref[0, rs, 0:R]]
        o = yof_ref[0, rs, R:] + yob_ref[0, rs, R:]
        for hh in range(GLA_HEADS):
            sl = slice(hh * GLA_VAL_DIM, (hh + 1) * GLA_VAL_DIM)
            oh = o[:, sl]
            oh = oh * lax.rsqrt(jnp.mean(oh * oh, axis=-1, keepdims=True) + GLA_NORM_EPS)
            parts.append(oh * gng * pg_ref[0, rs, 2 * R + hh * GLA_VAL_DIM:2 * R + (hh + 1) * GLA_VAL_DIM])
        return jnp.concatenate(parts, axis=-1)

    mix = [mixed(rs) for rs in rows]
    x1 = [x_ref[0, rs] + gt1 * _bdot(mx, wout_ref) for rs, mx in zip(rows, mix)]
    h2 = [_rms(v) * n2g_ref[...] * (1.0 + sc2) + sh2 for v in x1]
    f = [jnp.maximum(_bdot(v, w1_ref), 0.0) for v in h2]
    x2 = [a + gt2 * _bdot(v * v, w2_ref) for a, v in zip(x1, f)]
    for rs, v in zip(rows, x2):
        o_ref[0, rs] = _rms(v) * fng_ref[...]


def _post(x, mod3, mod_row, yo_f, yo_b, pg, vecs, mats):
    B, L, D = x.shape
    tb = min(POST_BLOCK, L)
    nblk = L // tb

    def tok(w):
        return pl.BlockSpec((1, tb, w), lambda b, i: (b, i, 0))

    in_specs = [tok(D), pl.BlockSpec((1, 1, N_MOD * D), lambda b, i: (mod_row(b), 0, 0))]
    in_specs += [tok(yo_f.shape[-1]), tok(yo_b.shape[-1]), tok(pg.shape[-1])]
    in_specs += [_const_spec(w.shape) for w in vecs]
    in_specs += [pl.BlockSpec(w.shape, lambda b, i: (0, 0), pipeline_mode=pl.Buffered(1)) for w in mats]
    return pl.pallas_call(
        _post_kernel,
        grid=(B, nblk),
        in_specs=in_specs,
        out_specs=tok(D),
        out_shape=jax.ShapeDtypeStruct((B, L, D), F32),
        compiler_params=pltpu.CompilerParams(dimension_semantics=("parallel", "arbitrary"),
                                             vmem_limit_bytes=VMEM_LIMIT),
        name="post",
    )(x, mod3, yo_f, yo_b, pg, *vecs, *mats)


def _blockdiag2(a, b):
    za = jnp.zeros((a.shape[0], b.shape[1]), a.dtype)
    zb = jnp.zeros((b.shape[0], a.shape[1]), a.dtype)
    return jnp.concatenate([jnp.concatenate([a, za], axis=1), jnp.concatenate([zb, b], axis=1)], axis=0)


def kernel(x_prompt, x_sample, c, state_rwkv_fwd, state_rwkv_bwd, state_gla_fwd, state_gla_bwd, c_ctx, ada_w, ada_b, norm1_g, norm2_g, w_in, rwkv_mu_rkv, rwkv_mu_wag, rwkv_w0, rwkv_w1, rwkv_w2, rwkv_a0, rwkv_a1, rwkv_a2, rwkv_g1, rwkv_g2, rwkv_k_k, rwkv_k_a, rwkv_r_k, rwkv_lnx_g, rwkv_lnx_b, gla_gk1, gla_gk2, gla_gk_b, gla_norm_g, w_out, mlp_w1, mlp_w2, final_norm_g):
    D = D_MODEL
    R = RWKV_WIDTH
    nb = x_prompt.shape[0]
    nd = x_sample.shape[0]
    assert ada_w.shape[0] == 1, "single-layer step"
    layer = 0

    craw = jnp.concatenate([c_ctx[None, :], c, jnp.zeros((8 - 1 - nd, D), F32)], axis=0)
    mod = _modulation(craw, ada_w[layer], ada_b[layer][None, :])
    mod3 = mod.reshape(8, 1, N_MOD * D)

    bf = lambda t: t.astype(BF16)
    row = lambda t: t.reshape(1, -1).astype(F32)
    w_in_bf = bf(w_in[layer])
    half = (D, 3 * R)
    pre_w = [
        row(norm1_g[layer]),
        (w_in_bf, pl.BlockSpec(half, lambda b, i: (0, 0))),
        (w_in_bf, pl.BlockSpec(half, lambda b, i: (0, 1))),
        bf(jnp.concatenate([gla_gk1[layer, 0], gla_gk1[layer, 1]], axis=1)),
        bf(jnp.concatenate([rwkv_w1[layer, 0], rwkv_w1[layer, 1]], axis=1)),
        bf(jnp.concatenate([rwkv_a1[layer, 0], rwkv_a1[layer, 1]], axis=1)),
        bf(rwkv_g1[layer]),
        bf(_blockdiag2(rwkv_w2[layer, 0], rwkv_w2[layer, 1])),
        bf(_blockdiag2(rwkv_a2[layer, 0], rwkv_a2[layer, 1])),
        bf(rwkv_g2[layer]),
        bf(_blockdiag2(gla_gk2[layer, 0], gla_gk2[layer, 1])),
        row(rwkv_mu_rkv[layer]),
        rwkv_mu_wag[layer],
        row(rwkv_w0[layer]),
        row(rwkv_a0[layer]),
        row(gla_gk_b[layer]),
        row(rwkv_k_k[layer]),
        row(rwkv_k_a[layer]),
        row(rwkv_r_k[layer]),
    ]
    post_vecs = [row(rwkv_lnx_g[layer]), row(rwkv_lnx_b[layer]), row(gla_norm_g[layer]),
                 row(norm2_g[layer]), row(final_norm_g)]
    post_mats = [bf(w_out[layer]), bf(mlp_w1[layer]), bf(mlp_w2[layer])]

    def run_group(x, mod_row, grid_shift, states, emit_states):
        ra, rd_f, rd_b, ga, gl, pg = _pre(x, mod3, mod_row, pre_w, grid_shift)
        yo_f, yo_b, *finals = _scan(ra, rd_f, rd_b, ga, gl, states, emit_states)
        y = _post(x, mod3, mod_row, yo_f, yo_b, pg, post_vecs, post_mats)
        return y, finals

    rpair = lambda s: s.reshape(nd, R // LANES, LANES, RWKV_HEAD_DIM)
    gpair = lambda s: s.reshape(nd, GLA_QK_WIDTH // LANES, LANES, GLA_VAL_DIM)
    y_prompt, (n_rf, n_rb, n_gf, n_gb) = run_group(x_prompt, lambda b: 0, False, None, True)
    y_sample, _ = run_group(x_sample, lambda b: b + 1, True,
                            (rpair(state_rwkv_fwd[:, layer]), rpair(state_rwkv_bwd[:, layer]),
                             gpair(state_gla_fwd[:, layer]), gpair(state_gla_bwd[:, layer])), False)
    rshape = (nb, 1, RWKV_HEADS, RWKV_HEAD_DIM, RWKV_HEAD_DIM)
    gshape = (nb, 1, GLA_HEADS, GLA_KEY_DIM, GLA_VAL_DIM)
    return (y_prompt, y_sample, n_rf.reshape(rshape), n_rb.reshape(rshape),
            n_gf.reshape(gshape), n_gb.reshape(gshape))
```

```python
import functools

import jax
import jax.numpy as jnp
from jax import lax
from jax.experimental import pallas as pl
from jax.experimental.pallas import tpu as pltpu

F32 = jnp.float32
BF16 = jnp.bfloat16

D_MODEL = 1024
GRID_W = 64
RWKV_WIDTH = 512
RWKV_HEAD_DIM = 64
RWKV_HEADS = 8
GLA_HEADS = 4
GLA_KEY_DIM = 64
GLA_VAL_DIM = 128
GLA_QK_WIDTH = 256
GLA_V_WIDTH = 512
GLA_GATE_NORMALIZER = 16.0
N_MOD = 6
RMS_EPS = 1e-6
LNX_EPS = 64e-5
GLA_NORM_EPS = 1e-5

LANES = 128
RWKV_CHUNK = 64
GLA_CHUNK = 16
GLA_WIDE_CHUNK = 64
GLA_SAFE_LOG = 60.0
SCAN_BLOCK = 256
SCAN_SUB = 128
PRE_BLOCK = 256
POST_BLOCK = 512
POST_SPLIT = 2
VMEM_LIMIT = 56 * 1024 * 1024
RWKV_TASK_LAG = 2
GLA_TASK_LAG = 14
DECAY_LOG_SCALE = 0.6065306597126334

_NN = (((1,), (0,)), ((), ()))
_NT = (((1,), (1,)), ((), ()))
_TN = (((0,), (0,)), ((), ()))


def _dot(a, b):
    return lax.dot_general(a, b, _NN, preferred_element_type=F32)


def _mm(a, b, dims):
    return lax.dot_general(a.astype(BF16), b.astype(BF16), dims, preferred_element_type=F32)


def _cumsum_mm(tri, x):
    hi = x.astype(BF16)
    r1 = x - hi.astype(F32)
    mid = r1.astype(BF16)
    lo = (r1 - mid.astype(F32)).astype(BF16)
    return _dot(tri, hi) + _dot(tri, mid) + _dot(tri, lo)


def _bdot(a, w_ref):
    return _dot(a.astype(BF16), w_ref[...])


def _sigmoid(x):
    return 1.0 / (1.0 + jnp.exp(-x))


def _softplus(x):
    return jnp.maximum(x, 0.0) + jnp.log(1.0 + jnp.exp(-jnp.abs(x)))


def _head_sum(x):
    first = lax.broadcasted_iota(jnp.int32, (1, LANES), 1) < RWKV_HEAD_DIM
    outs = []
    for j in range(x.shape[-1] // LANES):
        xb = x[:, j * LANES:(j + 1) * LANES]
        s0 = jnp.sum(jnp.where(first, xb, 0.0), axis=-1, keepdims=True)
        s1 = jnp.sum(jnp.where(first, 0.0, xb), axis=-1, keepdims=True)
        outs.append(jnp.where(first, s0, s1))
    return jnp.concatenate(outs, axis=-1)


def _rms(x):
    return x * lax.rsqrt(jnp.mean(x * x, axis=-1, keepdims=True) + RMS_EPS)


def _mod_kernel(c_ref, w_ref, b_ref, o_ref):
    c = c_ref[...]
    cond = c * _sigmoid(c)
    o_ref[...] = _dot(cond.astype(BF16), w_ref[...].astype(BF16)) + b_ref[...]


def _modulation(craw, ada_w, ada_b):
    n = ada_w.shape[1]
    bn = 1536
    return pl.pallas_call(
        _mod_kernel,
        grid=(n // bn,),
        in_specs=[
            pl.BlockSpec((8, D_MODEL), lambda j: (0, 0)),
            pl.BlockSpec((D_MODEL, bn), lambda j: (0, j)),
            pl.BlockSpec((1, bn), lambda j: (0, j)),
        ],
        out_specs=pl.BlockSpec((8, bn), lambda j: (0, j)),
        out_shape=jax.ShapeDtypeStruct((8, n), F32),
        compiler_params=pltpu.CompilerParams(dimension_semantics=("arbitrary",), vmem_limit_bytes=VMEM_LIMIT),
        name="modulation",
    )(craw, ada_w, ada_b)


def _pre_kernel(*refs, grid_shift, tb):
    if grid_shift:
        x_ref, xp_ref, xn_ref = refs[:3]
        refs = refs[3:]
    else:
        x_ref = refs[0]
        refs = refs[1:]
    (mod_ref, n1g_ref, wrkv_ref, wrest_ref, gk1_ref, w1_ref, a1_ref, g1_ref, w2_ref, a2_ref, g2_ref,
     gk2_ref, murkv_ref, muwag_ref, w0_ref, a0_ref, gkb_ref, kk_ref, ka_ref, rk_ref) = refs[:20]
    ra_o, rf_o, rb_o, ga_o, gl_o, pg_o = refs[20:]
    D = D_MODEL
    R = RWKV_WIDTH
    m = mod_ref[0]
    sh1 = m[:, 0:D]
    sc1 = m[:, D:2 * D]
    n1g = n1g_ref[...]

    def normmod(xx):
        return _rms(xx) * n1g * (1.0 + sc1) + sh1

    h = normmod(x_ref[0])
    row = lax.broadcasted_iota(jnp.int32, (tb, 1), 0)
    if grid_shift:
        i = pl.program_id(1)
        n = pl.num_programs(1)
        hp = normmod(xp_ref[0]) * (i > 0).astype(F32)
        hn = normmod(xn_ref[0]) * (i < n - 1).astype(F32)
        hext = jnp.concatenate([hp, h, hn], axis=0)
        col = row % GRID_W
        m_l = (col != 0).astype(F32)
        m_r = (col != GRID_W - 1).astype(F32)
        ne = tb + 2 * GRID_W

        def shift(ext):
            up = ext[0:tb]
            down = ext[2 * GRID_W:2 * GRID_W + tb]
            left = pltpu.roll(ext, 1, 0)[GRID_W:GRID_W + tb]
            right = pltpu.roll(ext, ne - 1, 0)[GRID_W:GRID_W + tb]
            return 0.25 * (up + down + m_l * left + m_r * right)

        halo = GRID_W
    else:
        hext = h
        m_l = (row != 0).astype(F32)
        m_r = (row != tb - 1).astype(F32)

        def shift(ext):
            return 0.5 * (m_l * pltpu.roll(ext, 1, 0) + m_r * pltpu.roll(ext, tb - 1, 0))

        halo = 0

    rest = _bdot(h, wrest_ref)
    dh = shift(hext) - h
    mu = muwag_ref[...]
    lora_w = _bdot(h + mu[0:1] * dh, w1_ref)
    lora_a = _bdot(h + mu[1:2] * dh, a1_ref)
    lora_g = _bdot(h + mu[2:3] * dh, g1_ref)
    lora_gk = _bdot(h, gk1_ref)
    z = w0_ref[...] + _bdot(jnp.tanh(lora_w), w2_ref)
    a = _sigmoid(a0_ref[...] + _bdot(lora_a, a2_ref))
    gate = _bdot(_sigmoid(lora_g), g2_ref)
    logits = _bdot(lora_gk, gk2_ref) + gkb_ref[...]
    rkv_ext = _bdot(hext, wrkv_ref)

    Q = GLA_QK_WIDTH
    ga_o[0, :, 0:Q] = rest[:, 0:Q] * (GLA_KEY_DIM ** -0.5)
    ga_o[0, :, Q:] = rest[:, Q:2 * Q + GLA_V_WIDTH]
    gg = rest[:, 2 * Q + GLA_V_WIDTH:]
    pg_o[0, :, 2 * R:] = gg * _sigmoid(gg)
    gl_o[0] = -_softplus(-logits) * (1.0 / GLA_GATE_NORMALIZER)

    rkv = rkv_ext[halo:halo + tb]
    rkv = rkv + murkv_ref[...] * (shift(rkv_ext) - rkv)
    r = rkv[:, 0:R]
    k = rkv[:, R:2 * R]
    v = rkv[:, 2 * R:3 * R]
    lw = -DECAY_LOG_SCALE * _sigmoid(z)

    kap = k * kk_ref[...]
    kap = kap * lax.rsqrt(jnp.maximum(_head_sum(kap * kap), 1e-12))
    ka = ka_ref[...]
    a_f = a[:, 0:R]
    a_b = a[:, R:2 * R]
    kd_f = k * (1.0 + (a_f - 1.0) * ka)
    kd_b = k * (1.0 + (a_b - 1.0) * ka)
    bonus = _head_sum(r * (kd_f + kd_b) * rk_ref[...]) * v

    ra_o[0, :, 0:R] = r
    ra_o[0, :, R:2 * R] = v
    ra_o[0, :, 2 * R:3 * R] = kap
    rf_o[0, :, 0:R] = kd_f
    rf_o[0, :, R:2 * R] = a_f * kap
    rf_o[0, :, 2 * R:3 * R] = lw[:, 0:R]
    rb_o[0, :, 0:R] = kd_b
    rb_o[0, :, R:2 * R] = a_b * kap
    rb_o[0, :, 2 * R:3 * R] = lw[:, R:2 * R]
    pg_o[0, :, 0:R] = gate
    pg_o[0, :, R:2 * R] = bonus


def _const_spec(shape):
    nd = len(shape)
    return pl.BlockSpec(shape, lambda b, i: (0,) * nd)


def _pre(x, mod3, mod_row, weights, grid_shift):
    B, L, D = x.shape
    tb = PRE_BLOCK
    nblk = L // tb
    if not grid_shift:
        assert nblk == 1
    x_spec = pl.BlockSpec((1, tb, D), lambda b, i: (b, i, 0))
    in_specs = [x_spec]
    args = [x]
    if grid_shift:
        per = tb // GRID_W
        nrow = L // GRID_W
        in_specs += [
            pl.BlockSpec((1, GRID_W, D), lambda b, i: (b, jnp.maximum(i * per - 1, 0), 0)),
            pl.BlockSpec((1, GRID_W, D), lambda b, i: (b, jnp.minimum((i + 1) * per, nrow - 1), 0)),
        ]
        args += [x, x]
    in_specs.append(pl.BlockSpec((1, 1, N_MOD * D), lambda b, i: (mod_row(b), 0, 0)))
    args.append(mod3)
    for w in weights:
        w, spec = w if isinstance(w, tuple) else (w, _const_spec(w.shape))
        in_specs.append(spec)
        args.append(w)
    R = RWKV_WIDTH
    widths = [3 * R, 3 * R, 3 * R, 2 * GLA_QK_WIDTH + GLA_V_WIDTH, 2 * GLA_QK_WIDTH, 2 * R + GLA_V_WIDTH]
    out_specs = [pl.BlockSpec((1, tb, w), lambda b, i: (b, i, 0)) for w in widths]
    out_shape = [jax.ShapeDtypeStruct((B, L, w), F32) for w in widths]
    return pl.pallas_call(
        functools.partial(_pre_kernel, grid_shift=grid_shift, tb=tb),
        grid=(B, nblk),
        in_specs=in_specs,
        out_specs=out_specs,
        out_shape=out_shape,
        compiler_params=pltpu.CompilerParams(dimension_semantics=("parallel", "arbitrary"),
                                             vmem_limit_bytes=VMEM_LIMIT),
        name="pre_grid" if grid_shift else "pre_seq",
    )(*args)


def _interleave(tasks):
    live = list(tasks)
    rnd = 0
    while live:
        keep = []
        for first, gen in live:
            if rnd >= first:
                try:
                    next(gen)
                except StopIteration:
                    continue
            keep.append((first, gen))
        live = keep
        rnd += 1


def _inv_unit_triangular(lms, eye, blk16, blk32, blockdiag):
    def mm(a, b):
        return _mm(a, blockdiag(b), _NN)

    n = lms[0].shape[0]
    l0 = [jnp.where(blk16, lm, 0.0) for lm in lms]
    l2 = [mm(a, a) for a in l0]
    yield
    t = [eye - a for a in l0]
    s = [mm(jnp.concatenate([a, b], axis=0), b) for a, b in zip(t, l2)]
    yield
    t = [a + x[0:n] for a, x in zip(t, s)]
    l4 = [x[n:2 * n] for x in s]
    s = [mm(jnp.concatenate([a, b], axis=0), b) for a, b in zip(t, l4)]
    yield
    t = [a + x[0:n] for a, x in zip(t, s)]
    l8 = [x[n:2 * n] for x in s]
    t = [a + mm(a, b) for a, b in zip(t, l8)]
    yield
    off1_mask = jnp.logical_and(blk32, jnp.logical_not(blk16))
    x = [mm(a, jnp.where(off1_mask, lm, 0.0)) for a, lm in zip(t, lms)]
    yield
    t = [a - mm(b, a) for a, b in zip(t, x)]
    yield
    x = [mm(a, jnp.where(blk32, 0.0, lm)) for a, lm in zip(t, lms)]
    yield
    t = [a - mm(b, a) for a, b in zip(t, x)]
    yield
    return t


def _rwkv_prepare(units, cst):
    C = RWKV_CHUNK
    m0, cm0, eye, blk16, blk32, _ = cst[2]

    def stack(x):
        return jnp.concatenate([jnp.where(m0, x, 0.0), jnp.where(m0, 0.0, x)], axis=0)

    def blockdiag(side):
        return jnp.concatenate([jnp.where(cm0, side, 0.0), jnp.where(cm0, 0.0, side)], axis=0)

    cums = [_cumsum_mm(cst[rev][0], lw) for (_, _, _, _, _, lw, rev) in units]
    yield
    prep = []
    for (r, k, v, kap, b, lw, rev), cum in zip(units, cums):
        cumx = cum - lw
        if rev:
            mid = cum[C // 2:C // 2 + 1]
            tot = cum[0:1]
        else:
            mid = cum[C // 2 - 1:C // 2]
            tot = cum[C - 1:C]
        e_mid = jnp.exp(mid)
        rt = r * jnp.exp(cum - mid)
        kt = kap * jnp.exp(cumx - mid)
        es = jnp.exp(mid - cum)
        e_end = jnp.exp(tot - mid)
        prep.append(dict(rt=rt, kt=kt, bh=b * es, kh=k * es, r0=rt * e_mid, k0=stack(kt * e_mid),
                         be=b * es * e_end, ke=k * es * e_end, vs=stack(v), v=v,
                         e_tot=jnp.exp(tot), strict2=cst[rev][1], incl2=cst[rev][2]))
    gs = [_mm(jnp.concatenate([p["kt"], p["rt"]], axis=0),
              jnp.concatenate([stack(p["bh"]), stack(p["kh"])], axis=0), _NT) for p in prep]
    yield
    for p, g in zip(prep, gs):
        p["ab"] = jnp.where(p["strict2"], g[0:C, 0:2 * C], 0.0)
        p["ak"] = jnp.where(p["strict2"], g[0:C, 2 * C:4 * C], 0.0)
        p["rb"] = jnp.where(p["incl2"], g[C:2 * C, 0:2 * C], 0.0)
        p["rk"] = jnp.where(p["incl2"], g[C:2 * C, 2 * C:4 * C], 0.0)
    akv = [_mm(p["ak"], p["vs"], _NN) for p in prep]
    yield
    ts = yield from _inv_unit_triangular([p["ab"] for p in prep], eye, blk16, blk32, blockdiag)
    wu = [_mm(t, jnp.concatenate([p["k0"], stack(a)], axis=1), _NN) for t, p, a in zip(ts, prep, akv)]
    yield
    return [dict(wr=jnp.concatenate([x[:, :LANES], p["r0"]], axis=0), u0=x[:, LANES:],
                 rbk=jnp.concatenate([p["rb"], p["rk"]], axis=1), vs=p["vs"], v=p["v"],
                 bke=jnp.concatenate([p["be"], p["ke"]], axis=0), e_tot=p["e_tot"])
            for x, p in zip(wu, prep)]


def _rwkv_apply(prep, states, cst):
    C = RWKV_CHUNK
    m0 = cst[2][0]
    blk64 = cst[2][5]
    ws = [_mm(p["wr"], s, _NT) for p, s in zip(prep, states)]
    yield
    us = [-p["u0"] - w[0:C] for p, w in zip(prep, ws)]
    upd = [_mm(jnp.concatenate([u, p["v"]], axis=0), p["bke"], _TN) for u, p in zip(us, prep)]
    yield
    s_new = [s * p["e_tot"] + jnp.where(blk64, d, 0.0) for p, s, d in zip(prep, states, upd)]
    ys = [w[C:2 * C] + _mm(p["rbk"], jnp.concatenate([jnp.where(m0, u, 0.0), jnp.where(m0, 0.0, u), p["vs"]],
                                                     axis=0), _NN)
          for p, w, u in zip(prep, ws, us)]
    yield
    return ys, s_new


def _rwkv_consts():
    C = RWKV_CHUNK
    row = lax.broadcasted_iota(jnp.int32, (C, C), 0)
    col = lax.broadcasted_iota(jnp.int32, (C, C), 1)
    row2 = lax.broadcasted_iota(jnp.int32, (C, 2 * C), 0)
    col2 = lax.broadcasted_iota(jnp.int32, (C, 2 * C), 1)
    cs = col2 % C
    lane = lax.broadcasted_iota(jnp.int32, (1, LANES), 1)
    m0 = lane < RWKV_HEAD_DIM
    cm0 = lax.broadcasted_iota(jnp.int32, (1, 2 * C), 1) < C
    rr = lax.broadcasted_iota(jnp.int32, (2 * C, 2 * C), 0)
    cc = lax.broadcasted_iota(jnp.int32, (2 * C, 2 * C), 1)
    eye = (row2 == cs).astype(F32)
    blk16 = (row2 // 16) == (cs // 16)
    blk32 = (row2 // 32) == (cs // 32)
    blk64 = (rr // 64) == (cc // 64)
    fwd = ((col <= row).astype(BF16), cs < row2, cs <= row2)
    bwd = ((col >= row).astype(BF16), cs > row2, cs >= row2)
    return fwd, bwd, (m0, cm0, eye, blk16, blk32, blk64)


def _gla_intra(items, cst):
    G = GLA_CHUNK
    ind_v, sel, blk = cst[2]
    srow = lax.broadcasted_iota(jnp.int32, (G, 1), 0)
    cums = [_cumsum_mm(cst[rev], la) for (_, _, _, la, rev) in items]
    pmats = []
    for (q, k, v, la, rev), cum in zip(items, cums):
        ps = []
        for t in range(G):
            msk = (srow >= t) if rev else (srow <= t)
            e = jnp.exp(jnp.where(msk, cum[t:t + 1] - cum, 0.0))
            ps.append(jnp.where(msk, e * (q[t:t + 1] * k), 0.0))
        pmats.append(jnp.concatenate(ps, axis=0))
    atts = [_mm(pm, ind_v, _NN) for pm in pmats]
    o_intra = [_mm(sel, att * jnp.concatenate([it[2]] * G, axis=0), _NN)
               for att, it in zip(atts, items)]
    out = []
    for (q, k, v, la, rev), cum, oi in zip(items, cums, o_intra):
        tot = cum[0:1] if rev else cum[G - 1:G]
        upd = _mm(v, k * jnp.exp(tot - cum), _TN)
        out.append((oi, q * jnp.exp(cum), jnp.where(blk, upd, 0.0), jnp.exp(tot)))
    yield
    return out


def _gla_consts():
    G = GLA_CHUNK
    row = lax.broadcasted_iota(jnp.int32, (G, G), 0)
    col = lax.broadcasted_iota(jnp.int32, (G, G), 1)
    kc = lax.broadcasted_iota(jnp.int32, (LANES, 2 * LANES), 0)
    vc = lax.broadcasted_iota(jnp.int32, (LANES, 2 * LANES), 1)
    ind_v = ((kc // GLA_KEY_DIM) == (vc // GLA_VAL_DIM)).astype(F32)
    st = lax.broadcasted_iota(jnp.int32, (G, G * G), 0)
    sj = lax.broadcasted_iota(jnp.int32, (G, G * G), 1)
    sel = ((sj // G) == st).astype(F32)
    br = lax.broadcasted_iota(jnp.int32, (2 * LANES, LANES), 0)
    bc = lax.broadcasted_iota(jnp.int32, (2 * LANES, LANES), 1)
    blk = (br // GLA_VAL_DIM) == (bc // GLA_KEY_DIM)
    return (col <= row).astype(BF16), (col >= row).astype(BF16), (ind_v, sel, blk)


def _gla_factored(items, cst):
    F = GLA_WIDE_CHUNK
    m0, vm0, blk = cst[2]
    cums = [_cumsum_mm(cst[rev][0], la) for (_, _, _, la, rev) in items]
    yield
    scores = []
    for (q, k, v, la, rev), cum in zip(items, cums):
        mid = cum[F // 2:F // 2 + 1] if rev else cum[F // 2 - 1:F // 2]
        kt = k * jnp.exp(mid - cum)
        kstk = jnp.concatenate([jnp.where(m0, kt, 0.0), jnp.where(m0, 0.0, kt)], axis=0)
        scores.append(_mm(q * jnp.exp(cum - mid), kstk, _NT))
    yield
    o_intra = []
    for (q, k, v, la, rev), sc in zip(items, scores):
        vstk = jnp.concatenate([jnp.where(vm0, v, 0.0), jnp.where(vm0, 0.0, v)], axis=0)
        o_intra.append(_mm(jnp.where(cst[rev][1], sc, 0.0), vstk, _NN))
    yield
    out = []
    for (q, k, v, la, rev), cum, oi in zip(items, cums, o_intra):
        tot = cum[0:1] if rev else cum[F - 1:F]
        upd = _mm(v, k * jnp.exp(tot - cum), _TN)
        out.append((oi, q * jnp.exp(cum), jnp.where(blk, upd, 0.0), jnp.exp(tot)))
    yield
    return out


def _gla_wide_consts():
    F = GLA_WIDE_CHUNK
    row = lax.broadcasted_iota(jnp.int32, (F, F), 0)
    col = lax.broadcasted_iota(jnp.int32, (F, F), 1)
    row2 = lax.broadcasted_iota(jnp.int32, (F, 2 * F), 0)
    cs = lax.broadcasted_iota(jnp.int32, (F, 2 * F), 1) % F
    m0 = lax.broadcasted_iota(jnp.int32, (1, LANES), 1) < GLA_KEY_DIM
    vm0 = lax.broadcasted_iota(jnp.int32, (1, 2 * LANES), 1) < GLA_VAL_DIM
    br = lax.broadcasted_iota(jnp.int32, (2 * LANES, LANES), 0)
    bc = lax.broadcasted_iota(jnp.int32, (2 * LANES, LANES), 1)
    blk = (br // GLA_VAL_DIM) == (bc // GLA_KEY_DIM)
    return (((col <= row).astype(BF16), cs <= row2), ((col >= row).astype(BF16), cs >= row2), (m0, vm0, blk))


def _scan_kernel(raf_ref, rdf_ref, rab_ref, rdb_ref, gaf_ref, glf_ref, gab_ref, glb_ref, *refs,
                 has_init, emit_states):
    refs = list(refs)
    init_refs = [refs.pop(0) for _ in range(4)] if has_init else None
    yof_ref, yob_ref = refs.pop(0), refs.pop(0)
    final_refs = [refs.pop(0) for _ in range(4)] if emit_states else None
    srf_ref, srb_ref, sgf_ref, sgb_ref = refs
    i = pl.program_id(1)
    lane = lax.broadcasted_iota(jnp.int32, (1, LANES), 1)
    first_half = lane < LANES // 2
    rr = lax.broadcasted_iota(jnp.int32, (LANES, LANES), 0)
    cc = lax.broadcasted_iota(jnp.int32, (LANES, LANES), 1)
    diag_blocks = (rr // RWKV_HEAD_DIM) == (cc // RWKV_HEAD_DIM)

    @pl.when(i == 0)
    def _():
        if not has_init:
            for s_ref in (srf_ref, srb_ref, sgf_ref, sgb_ref):
                s_ref[...] = jnp.zeros(s_ref.shape, F32)
            return
        for s_ref, init in ((srf_ref, init_refs[0]), (srb_ref, init_refs[1])):
            for p in range(RWKV_WIDTH // LANES):
                x = init[0, p]
                s_ref[0, p] = jnp.where(diag_blocks, jnp.concatenate([x, x], axis=1), 0.0)
        for s_ref, init in ((sgf_ref, init_refs[2]), (sgb_ref, init_refs[3])):
            for p in range(GLA_QK_WIDTH // LANES):
                xt = init[0, p].T
                s_ref[0, p] = jnp.concatenate([jnp.where(first_half, xt, 0.0),
                                               jnp.where(first_half, 0.0, xt)], axis=0)

    T = SCAN_SUB
    C = RWKV_CHUNK
    nr = T // C
    nsub = SCAN_BLOCK // T
    R = RWKV_WIDTH
    Q = GLA_QK_WIDTH
    npair = Q // LANES

    def sub_block(h, carry):
        base = (0, 0) if nsub == 1 else (pl.multiple_of(h * T, T), pl.multiple_of((nsub - 1 - h) * T, T))

        def rows_of(d, start, size):
            return pl.ds(base[d] + start, size)

        rcst = _rwkv_consts()
        g_dirs = ((False, gaf_ref, glf_ref, 0, sgf_ref, yof_ref),
                  (True, gab_ref, glb_ref, Q, sgb_ref, yob_ref))
        la_tot = jnp.minimum(jnp.sum(glf_ref[0, rows_of(0, 0, T), 0:Q], axis=0, keepdims=True),
                             jnp.sum(glb_ref[0, rows_of(1, 0, T), Q:2 * Q], axis=0, keepdims=True))
        gla_split_ok = jnp.min(la_tot) >= -GLA_SAFE_LOG

        r_dirs = ((False, raf_ref, rdf_ref, srf_ref, yof_ref),
                  (True, rab_ref, rdb_ref, srb_ref, yob_ref))
        r_units = {}
        r_dests = {}
        for d, (rev, ra_ref, rd_ref, s_ref, y_ref) in enumerate(r_dirs):
            for j in range(nr):
                c = nr - 1 - j if rev else j
                rows = rows_of(d, c * C, C)
                r_units[d, j] = []
                r_dests[d, j] = []
                for p in range(R // LANES):
                    sl, sl1, sl2 = (slice(o + p * LANES, o + (p + 1) * LANES) for o in (0, R, 2 * R))
                    r_units[d, j].append((ra_ref[0, rows, sl], rd_ref[0, rows, sl], ra_ref[0, rows, sl1],
                                          ra_ref[0, rows, sl2], rd_ref[0, rows, sl1], rd_ref[0, rows, sl2], rev))
                    r_dests[d, j].append((y_ref, s_ref, p, rows, sl))
        r_states = {(d, 0): [s_ref[0, p] for (_, s_ref, p, _, _) in r_dests[d, 0]] for d in range(len(r_dirs))}

        def rwkv_task(d, j):
            prep = yield from _rwkv_prepare(r_units[d, j], rcst)
            while (d, j) not in r_states:
                yield
            ys, r_states[d, j + 1] = yield from _rwkv_apply(prep, r_states[d, j], rcst)
            for (y_ref, _, _, rows, sl), y in zip(r_dests[d, j], ys):
                y_ref[0, rows, sl] = y

        g_init = [[s_ref[0, p] for p in range(npair)] for (_, _, _, _, s_ref, _) in g_dirs]

        def gla_task(chunk, intra, cst):
            ng = T // chunk
            items = [[] for _ in range(ng)]
            dests = [[] for _ in range(ng)]
            for d, (rev, ga_ref, gl_ref, la_off, s_ref, o_ref) in enumerate(g_dirs):
                for p in range(npair):
                    qs, ks, las = (slice(o + p * LANES, o + (p + 1) * LANES) for o in (0, Q, la_off))
                    vs = slice(2 * Q + p * 2 * LANES, 2 * Q + (p + 1) * 2 * LANES)
                    os_ = slice(R + p * 2 * LANES, R + (p + 1) * 2 * LANES)
                    for j in range(ng):
                        c = ng - 1 - j if rev else j
                        rows = rows_of(d, c * chunk, chunk)
                        items[j].append((ga_ref[0, rows, qs], ga_ref[0, rows, ks], ga_ref[0, rows, vs],
                                         gl_ref[0, rows, las], rev))
                        dests[j].append((o_ref, s_ref, d, p, rows, os_))
            parts = yield from intra([it for its in items for it in its], cst)
            nu = len(items[0])
            sts = [g_init[d][p] for (_, _, d, p, _, _) in dests[0]]
            for j in range(ng):
                new = []
                for (o_ref, _, _, _, rows, vs), (o_intra, q_in, upd, e_tot), st in zip(
                        dests[j], parts[j * nu:(j + 1) * nu], sts):
                    o_ref[0, rows, vs] = o_intra + _mm(q_in, st, _NT)
                    new.append(st * e_tot + upd)
                sts = new
                yield
            for (_, s_ref, _, p, _, _), st in zip(dests[0], sts):
                s_ref[0, p] = st

        nd = len(r_dirs)
        _interleave([(RWKV_TASK_LAG * j, rwkv_task(d, j)) for j in range(nr) for d in range(nd)]
                    + [(GLA_TASK_LAG, gla_task(GLA_WIDE_CHUNK, _gla_factored, _gla_wide_consts()))])
        for d in range(nd):
            for (_, s_ref, p, _, _), s in zip(r_dests[d, 0], r_states[d, nr]):
                s_ref[0, p] = s

        @pl.when(jnp.logical_not(gla_split_ok))
        def _():
            _interleave([(0, gla_task(GLA_CHUNK, _gla_intra, _gla_consts()))])

        return carry

    if nsub == 1:
        sub_block(0, 0)
    else:
        lax.fori_loop(0, nsub, sub_block, 0)

    if emit_states:
        @pl.when(i == pl.num_programs(1) - 1)
        def _():
            for s_ref, out in ((srf_ref, final_refs[0]), (srb_ref, final_refs[1])):
                for p in range(RWKV_WIDTH // LANES):
                    s = s_ref[0, p]
                    out[0, p] = (s + pltpu.roll(s, LANES // 2, 1))[:, 0:LANES // 2]
            for s_ref, out in ((sgf_ref, final_refs[2]), (sgb_ref, final_refs[3])):
                for p in range(GLA_QK_WIDTH // LANES):
                    st = s_ref[0, p]
                    out[0, p] = st[0:LANES].T + st[LANES:2 * LANES].T


def _scan(ra, rd_f, rd_b, ga, gl, states, emit_states):
    B, L, _ = ra.shape
    R = RWKV_WIDTH
    Q = GLA_QK_WIDTH
    V = GLA_V_WIDTH
    T = SCAN_BLOCK
    n = L // T

    def fwd(w):
        return pl.BlockSpec((1, T, w), lambda b, i: (b, i, 0))

    def bwd(w):
        return pl.BlockSpec((1, T, w), lambda b, i: (b, n - 1 - i, 0))

    rshape = (R // LANES, LANES, LANES // 2)
    gshape = (Q // LANES, LANES, LANES)
    rst = pl.BlockSpec((1,) + rshape, lambda b, i: (b, 0, 0, 0))
    gst = pl.BlockSpec((1,) + gshape, lambda b, i: (b, 0, 0, 0))
    st_specs = [rst, rst, gst, gst]
    st_shapes = [jax.ShapeDtypeStruct((B,) + s, F32) for s in (rshape, rshape, gshape, gshape)]
    states = () if states is None else tuple(states)
    return pl.pallas_call(
        functools.partial(_scan_kernel, has_init=bool(states), emit_states=emit_states),
        grid=(B, n),
        in_specs=[fwd(3 * R), fwd(3 * R), bwd(3 * R), bwd(3 * R),
                  fwd(2 * Q + V), fwd(2 * Q), bwd(2 * Q + V), bwd(2 * Q)] + st_specs[:len(states)],
        out_specs=[fwd(R + V), bwd(R + V)] + (st_specs if emit_states else []),
        out_shape=[jax.ShapeDtypeStruct((B, L, R + V), F32)] * 2 + (st_shapes if emit_states else []),
        scratch_shapes=[pltpu.VMEM((1, R // LANES, LANES, LANES), F32)] * 2
        + [pltpu.VMEM((1, Q // LANES, 2 * LANES, LANES), F32)] * 2,
        compiler_params=pltpu.CompilerParams(dimension_semantics=("parallel", "arbitrary"),
                                             vmem_limit_bytes=VMEM_LIMIT),
        name="scan",
    )(ra, rd_f, ra, rd_b, ga, gl, ga, gl, *states)


def _post_kernel(x_ref, mod_ref, yof_ref, yob_ref, pg_ref,
                 lng_ref, lnb_ref, gng_ref, n2g_ref, fng_ref, wout_ref, w1_ref, w2_ref, o_ref):
    D = D_MODEL
    R = RWKV_WIDTH
    m = mod_ref[0]
    gt1 = m[:, 2 * D:3 * D]
    sh2 = m[:, 3 * D:4 * D]
    sc2 = m[:, 4 * D:5 * D]
    gt2 = m[:, 5 * D:6 * D]
    inv_n = 1.0 / RWKV_HEAD_DIM
    gng = gng_ref[...]
    tb = x_ref.shape[1]
    nsplit = POST_SPLIT if tb % (8 * POST_SPLIT) == 0 else 1
    rows = [slice(s * (tb // nsplit), (s + 1) * (tb // nsplit)) for s in range(nsplit)]

    def mixed(rs):
        y = yof_ref[0, rs, 0:R] + yob_ref[0, rs, 0:R]
        mu = _head_sum(y) * inv_n
        yc = y - mu
        var = _head_sum(yc * yc) * inv_n
        yn = yc * lax.rsqrt(var + LNX_EPS) * lng_ref[...] + lnb_ref[...]
        parts = [(yn + pg_ref[0, rs, R:2 * R]) * pg_ref[0, rs, 0:R]]
        o = yof_ref[0, rs, R:] + yob_ref[0, rs, R:]
        for hh in range(GLA_HEADS):
            sl = slice(hh * GLA_VAL_DIM, (hh + 1) * GLA_VAL_DIM)
            oh = o[:, sl]
            oh = oh * lax.rsqrt(jnp.mean(oh * oh, axis=-1, keepdims=True) + GLA_NORM_EPS)
            parts.append(oh * gng * pg_ref[0, rs, 2 * R + hh * GLA_VAL_DIM:2 * R + (hh + 1) * GLA_VAL_DIM])
        return jnp.concatenate(parts, axis=-1)

    mix = [mixed(rs) for rs in rows]
    x1 = [x_ref[0, rs] + gt1 * _bdot(mx, wout_ref) for rs, mx in zip(rows, mix)]
    h2 = [_rms(v) * n2g_ref[...] * (1.0 + sc2) + sh2 for v in x1]
    f = [jnp.maximum(_bdot(v, w1_ref), 0.0) for v in h2]
    x2 = [a + gt2 * _bdot(v * v, w2_ref) for a, v in zip(x1, f)]
    for rs, v in zip(rows, x2):
        o_ref[0, rs] = _rms(v) * fng_ref[...]


def _post(x, mod3, mod_row, yo_f, yo_b, pg, vecs, mats):
    B, L, D = x.shape
    tb = min(POST_BLOCK, L)
    nblk = L // tb

    def tok(w):
        return pl.BlockSpec((1, tb, w), lambda b, i: (b, i, 0))

    in_specs = [tok(D), pl.BlockSpec((1, 1, N_MOD * D), lambda b, i: (mod_row(b), 0, 0))]
    in_specs += [tok(yo_f.shape[-1]), tok(yo_b.shape[-1]), tok(pg.shape[-1])]
    in_specs += [_const_spec(w.shape) for w in vecs]
    in_specs += [pl.BlockSpec(w.shape, lambda b, i: (0, 0), pipeline_mode=pl.Buffered(1)) for w in mats]
    return pl.pallas_call(
        _post_kernel,
        grid=(B, nblk),
        in_specs=in_specs,
        out_specs=tok(D),
        out_shape=jax.ShapeDtypeStruct((B, L, D), F32),
        compiler_params=pltpu.CompilerParams(dimension_semantics=("parallel", "arbitrary"),
                                             vmem_limit_bytes=VMEM_LIMIT),
        name="post",
    )(x, mod3, yo_f, yo_b, pg, *vecs, *mats)


def _blockdiag2(a, b):
    za = jnp.zeros((a.shape[0], b.shape[1]), a.dtype)
    zb = jnp.zeros((b.shape[0], a.shape[1]), a.dtype)
    return jnp.concatenate([jnp.concatenate([a, za], axis=1), jnp.concatenate([zb, b], axis=1)], axis=0)


def kernel(x_prompt, x_sample, c, state_rwkv_fwd, state_rwkv_bwd, state_gla_fwd, state_gla_bwd, c_ctx, ada_w, ada_b, norm1_g, norm2_g, w_in, rwkv_mu_rkv, rwkv_mu_wag, rwkv_w0, rwkv_w1, rwkv_w2, rwkv_a0, rwkv_a1, rwkv_a2, rwkv_g1, rwkv_g2, rwkv_k_k, rwkv_k_a, rwkv_r_k, rwkv_lnx_g, rwkv_lnx_b, gla_gk1, gla_gk2, gla_gk_b, gla_norm_g, w_out, mlp_w1, mlp_w2, final_norm_g):
    D = D_MODEL
    R = RWKV_WIDTH
    nb = x_prompt.shape[0]
    nd = x_sample.shape[0]
    assert ada_w.shape[0] == 1, "single-layer step"
    layer = 0

    craw = jnp.concatenate([c_ctx[None, :], c, jnp.zeros((8 - 1 - nd, D), F32)], axis=0)
    mod = _modulation(craw, ada_w[layer], ada_b[layer][None, :])
    mod3 = mod.reshape(8, 1, N_MOD * D)

    bf = lambda t: t.astype(BF16)
    row = lambda t: t.reshape(1, -1).astype(F32)
    w_in_bf = bf(w_in[layer])
    half = (D, 3 * R)
    pre_w = [
        row(norm1_g[layer]),
        (w_in_bf, pl.BlockSpec(half, lambda b, i: (0, 0))),
        (w_in_bf, pl.BlockSpec(half, lambda b, i: (0, 1))),
        bf(jnp.concatenate([gla_gk1[layer, 0], gla_gk1[layer, 1]], axis=1)),
        bf(jnp.concatenate([rwkv_w1[layer, 0], rwkv_w1[layer, 1]], axis=1)),
        bf(jnp.concatenate([rwkv_a1[layer, 0], rwkv_a1[layer, 1]], axis=1)),
        bf(rwkv_g1[layer]),
        bf(_blockdiag2(rwkv_w2[layer, 0], rwkv_w2[layer, 1])),
        bf(_blockdiag2(rwkv_a2[layer, 0], rwkv_a2[layer, 1])),
        bf(rwkv_g2[layer]),
        bf(_blockdiag2(gla_gk2[layer, 0], gla_gk2[layer, 1])),
        row(rwkv_mu_rkv[layer]),
        rwkv_mu_wag[layer],
        row(rwkv_w0[layer]),
        row(rwkv_a0[layer]),
        row(gla_gk_b[layer]),
        row(rwkv_k_k[layer]),
        row(rwkv_k_a[layer]),
        row(rwkv_r_k[layer]),
    ]
    post_vecs = [row(rwkv_lnx_g[layer]), row(rwkv_lnx_b[layer]), row(gla_norm_g[layer]),
                 row(norm2_g[layer]), row(final_norm_g)]
    post_mats = [bf(w_out[layer]), bf(mlp_w1[layer]), bf(mlp_w2[layer])]

    def run_group(x, mod_row, grid_shift, states, emit_states):
        ra, rd_f, rd_b, ga, gl, pg = _pre(x, mod3, mod_row, pre_w, grid_shift)
        yo_f, yo_b, *finals = _scan(ra, rd_f, rd_b, ga, gl, states, emit_states)
        y = _post(x, mod3, mod_row, yo_f, yo_b, pg, post_vecs, post_mats)
        return y, finals

    rpair = lambda s: s.reshape(nd, R // LANES, LANES, RWKV_HEAD_DIM)
    gpair = lambda s: s.reshape(nd, GLA_QK_WIDTH // LANES, LANES, GLA_VAL_DIM)
    y_prompt, (n_rf, n_rb, n_gf, n_gb) = run_group(x_prompt, lambda b: 0, False, None, True)
    y_sample, _ = run_group(x_sample, lambda b: b + 1, True,
                            (rpair(state_rwkv_fwd[:, layer]), rpair(state_rwkv_bwd[:, layer]),
                             gpair(state_gla_fwd[:, layer]), gpair(state_gla_bwd[:, layer])), False)
    rshape = (nb, 1, RWKV_HEADS, RWKV_HEAD_DIM, RWKV_HEAD_DIM)
    gshape = (nb, 1, GLA_HEADS, GLA_KEY_DIM, GLA_VAL_DIM)
    return (y_prompt, y_sample, n_rf.reshape(rshape), n_rb.reshape(rshape),
            n_gf.reshape(gshape), n_gb.reshape(gshape))
```

```python
import functools

import jax
import jax.numpy as jnp
from jax import lax
from jax.experimental import pallas as pl
from jax.experimental.pallas import tpu as pltpu

F32 = jnp.float32
BF16 = jnp.bfloat16

D_MODEL = 1024
GRID_W = 64
RWKV_WIDTH = 512
RWKV_HEAD_DIM = 64
RWKV_HEADS = 8
GLA_HEADS = 4
GLA_KEY_DIM = 64
GLA_VAL_DIM = 128
GLA_QK_WIDTH = 256
GLA_V_WIDTH = 512
GLA_GATE_NORMALIZER = 16.0
N_MOD = 6
RMS_EPS = 1e-6
LNX_EPS = 64e-5
GLA_NORM_EPS = 1e-5

LANES = 128
RWKV_CHUNK = 64
GLA_CHUNK = 16
GLA_WIDE_CHUNK = 64
GLA_SAFE_LOG = 60.0
SCAN_BLOCK = 256
SCAN_SUB = 128
PRE_BLOCK = 256
POST_BLOCK = 512
POST_SPLIT = 2
VMEM_LIMIT = 56 * 1024 * 1024
RWKV_TASK_LAG = 2
GLA_TASK_LAG = 14
DECAY_LOG_SCALE = 0.6065306597126334

_NN = (((1,), (0,)), ((), ()))
_NT = (((1,), (1,)), ((), ()))
_TN = (((0,), (0,)), ((), ()))


def _dot(a, b):
    return lax.dot_general(a, b, _NN, preferred_element_type=F32)


def _mm(a, b, dims):
    return lax.dot_general(a.astype(BF16), b.astype(BF16), dims, preferred_element_type=F32)


def _cumsum_mm(tri, x):
    hi = x.astype(BF16)
    r1 = x - hi.astype(F32)
    mid = r1.astype(BF16)
    lo = (r1 - mid.astype(F32)).astype(BF16)
    return _dot(tri, hi) + _dot(tri, mid) + _dot(tri, lo)


def _bdot(a, w_ref):
    return _dot(a.astype(BF16), w_ref[...])


def _sigmoid(x):
    return 1.0 / (1.0 + jnp.exp(-x))


def _softplus(x):
    return jnp.maximum(x, 0.0) + jnp.log(1.0 + jnp.exp(-jnp.abs(x)))


def _head_sum(x):
    first = lax.broadcasted_iota(jnp.int32, (1, LANES), 1) < RWKV_HEAD_DIM
    outs = []
    for j in range(x.shape[-1] // LANES):
        xb = x[:, j * LANES:(j + 1) * LANES]
        s0 = jnp.sum(jnp.where(first, xb, 0.0), axis=-1, keepdims=True)
        s1 = jnp.sum(jnp.where(first, 0.0, xb), axis=-1, keepdims=True)
        outs.append(jnp.where(first, s0, s1))
    return jnp.concatenate(outs, axis=-1)


def _rms(x):
    return x * lax.rsqrt(jnp.mean(x * x, axis=-1, keepdims=True) + RMS_EPS)


def _mod_kernel(c_ref, w_ref, b_ref, o_ref):
    c = c_ref[...]
    cond = c * _sigmoid(c)
    o_ref[...] = _dot(cond.astype(BF16), w_ref[...].astype(BF16)) + b_ref[...]


def _modulation(craw, ada_w, ada_b):
    n = ada_w.shape[1]
    bn = 1536
    return pl.pallas_call(
        _mod_kernel,
        grid=(n // bn,),
        in_specs=[
            pl.BlockSpec((8, D_MODEL), lambda j: (0, 0)),
            pl.BlockSpec((D_MODEL, bn), lambda j: (0, j)),
            pl.BlockSpec((1, bn), lambda j: (0, j)),
        ],
        out_specs=pl.BlockSpec((8, bn), lambda j: (0, j)),
        out_shape=jax.ShapeDtypeStruct((8, n), F32),
        compiler_params=pltpu.CompilerParams(dimension_semantics=("arbitrary",), vmem_limit_bytes=VMEM_LIMIT),
        name="modulation",
    )(craw, ada_w, ada_b)


def _pre_kernel(*refs, grid_shift, tb):
    if grid_shift:
        x_ref, xp_ref, xn_ref = refs[:3]
        refs = refs[3:]
    else:
        x_ref = refs[0]
        refs = refs[1:]
    (mod_ref, n1g_ref, wrkv_ref, wrest_ref, gk1_ref, w1_ref, a1_ref, g1_ref, w2_ref, a2_ref, g2_ref,
     gk2_ref, murkv_ref, muwag_ref, w0_ref, a0_ref, gkb_ref, kk_ref, ka_ref, rk_ref) = refs[:20]
    ra_o, rf_o, rb_o, ga_o, gl_o, pg_o = refs[20:]
    D = D_MODEL
    R = RWKV_WIDTH
    m = mod_ref[0]
    sh1 = m[:, 0:D]
    sc1 = m[:, D:2 * D]
    n1g = n1g_ref[...]

    def normmod(xx):
        return _rms(xx) * n1g * (1.0 + sc1) + sh1

    h = normmod(x_ref[0])
    row = lax.broadcasted_iota(jnp.int32, (tb, 1), 0)
    if grid_shift:
        i = pl.program_id(1)
        n = pl.num_programs(1)
        hp = normmod(xp_ref[0]) * (i > 0).astype(F32)
        hn = normmod(xn_ref[0]) * (i < n - 1).astype(F32)
        hext = jnp.concatenate([hp, h, hn], axis=0)
        col = row % GRID_W
        m_l = (col != 0).astype(F32)
        m_r = (col != GRID_W - 1).astype(F32)
        ne = tb + 2 * GRID_W

        def shift(ext):
            up = ext[0:tb]
            down = ext[2 * GRID_W:2 * GRID_W + tb]
            left = pltpu.roll(ext, 1, 0)[GRID_W:GRID_W + tb]
            right = pltpu.roll(ext, ne - 1, 0)[GRID_W:GRID_W + tb]
            return 0.25 * (up + down + m_l * left + m_r * right)

        halo = GRID_W
    else:
        hext = h
        m_l = (row != 0).astype(F32)
        m_r = (row != tb - 1).astype(F32)

        def shift(ext):
            return 0.5 * (m_l * pltpu.roll(ext, 1, 0) + m_r * pltpu.roll(ext, tb - 1, 0))

        halo = 0

    rest = _bdot(h, wrest_ref)
    dh = shift(hext) - h
    mu = muwag_ref[...]
    lora_w = _bdot(h + mu[0:1] * dh, w1_ref)
    lora_a = _bdot(h + mu[1:2] * dh, a1_ref)
    lora_g = _bdot(h + mu[2:3] * dh, g1_ref)
    lora_gk = _bdot(h, gk1_ref)
    z = w0_ref[...] + _bdot(jnp.tanh(lora_w), w2_ref)
    a = _sigmoid(a0_ref[...] + _bdot(lora_a, a2_ref))
    gate = _bdot(_sigmoid(lora_g), g2_ref)
    logits = _bdot(lora_gk, gk2_ref) + gkb_ref[...]
    rkv_ext = _bdot(hext, wrkv_ref)

    Q = GLA_QK_WIDTH
    ga_o[0, :, 0:Q] = rest[:, 0:Q] * (GLA_KEY_DIM ** -0.5)
    ga_o[0, :, Q:] = rest[:, Q:2 * Q + GLA_V_WIDTH]
    gg = rest[:, 2 * Q + GLA_V_WIDTH:]
    pg_o[0, :, 2 * R:] = gg * _sigmoid(gg)
    gl_o[0] = -_softplus(-logits) * (1.0 / GLA_GATE_NORMALIZER)

    rkv = rkv_ext[halo:halo + tb]
    rkv = rkv + murkv_ref[...] * (shift(rkv_ext) - rkv)
    r = rkv[:, 0:R]
    k = rkv[:, R:2 * R]
    v = rkv[:, 2 * R:3 * R]
    lw = -DECAY_LOG_SCALE * _sigmoid(z)

    kap = k * kk_ref[...]
    kap = kap * lax.rsqrt(jnp.maximum(_head_sum(kap * kap), 1e-12))
    ka = ka_ref[...]
    a_f = a[:, 0:R]
    a_b = a[:, R:2 * R]
    kd_f = k * (1.0 + (a_f - 1.0) * ka)
    kd_b = k * (1.0 + (a_b - 1.0) * ka)
    bonus = _head_sum(r * (kd_f + kd_b) * rk_ref[...]) * v

    ra_o[0, :, 0:R] = r
    ra_o[0, :, R:2 * R] = v
    ra_o[0, :, 2 * R:3 * R] = kap
    rf_o[0, :, 0:R] = kd_f
    rf_o[0, :, R:2 * R] = a_f * kap
    rf_o[0, :, 2 * R:3 * R] = lw[:, 0:R]
    rb_o[0, :, 0:R] = kd_b
    rb_o[0, :, R:2 * R] = a_b * kap
    rb_o[0, :, 2 * R:3 * R] = lw[:, R:2 * R]
    pg_o[0, :, 0:R] = gate
    pg_o[0, :, R:2 * R] = bonus


def _const_spec(shape):
    nd = len(shape)
    return pl.BlockSpec(shape, lambda b, i: (0,) * nd)


def _pre(x, mod3, mod_row, weights, grid_shift):
    B, L, D = x.shape
    tb = PRE_BLOCK
    nblk = L // tb
    if not grid_shift:
        assert nblk == 1
    x_spec = pl.BlockSpec((1, tb, D), lambda b, i: (b, i, 0))
    in_specs = [x_spec]
    args = [x]
    if grid_shift:
        per = tb // GRID_W
        nrow = L // GRID_W
        in_specs += [
            pl.BlockSpec((1, GRID_W, D), lambda b, i: (b, jnp.maximum(i * per - 1, 0), 0)),
            pl.BlockSpec((1, GRID_W, D), lambda b, i: (b, jnp.minimum((i + 1) * per, nrow - 1), 0)),
        ]
        args += [x, x]
    in_specs.append(pl.BlockSpec((1, 1, N_MOD * D), lambda b, i: (mod_row(b), 0, 0)))
    args.append(mod3)
    for w in weights:
        w, spec = w if isinstance(w, tuple) else (w, _const_spec(w.shape))
        in_specs.append(spec)
        args.append(w)
    R = RWKV_WIDTH
    widths = [3 * R, 3 * R, 3 * R, 2 * GLA_QK_WIDTH + GLA_V_WIDTH, 2 * GLA_QK_WIDTH, 2 * R + GLA_V_WIDTH]
    out_specs = [pl.BlockSpec((1, tb, w), lambda b, i: (b, i, 0)) for w in widths]
    out_shape = [jax.ShapeDtypeStruct((B, L, w), F32) for w in widths]
    return pl.pallas_call(
        functools.partial(_pre_kernel, grid_shift=grid_shift, tb=tb),
        grid=(B, nblk),
        in_specs=in_specs,
        out_specs=out_specs,
        out_shape=out_shape,
        compiler_params=pltpu.CompilerParams(dimension_semantics=("parallel", "arbitrary"),
                                             vmem_limit_bytes=VMEM_LIMIT),
        name="pre_grid" if grid_shift else "pre_seq",
    )(*args)


def _interleave(tasks):
    live = list(tasks)
    rnd = 0
    while live:
        keep = []
        for first, gen in live:
            if rnd >= first:
                try:
                    next(gen)
                except StopIteration:
                    continue
            keep.append((first, gen))
        live = keep
        rnd += 1


def _inv_unit_triangular(lms, eye, blk16, blk32, blockdiag):
    def mm(a, b):
        return _mm(a, blockdiag(b), _NN)

    n = lms[0].shape[0]
    l0 = [jnp.where(blk16, lm, 0.0) for lm in lms]
    l2 = [mm(a, a) for a in l0]
    yield
    t = [eye - a for a in l0]
    s = [mm(jnp.concatenate([a, b], axis=0), b) for a, b in zip(t, l2)]
    yield
    t = [a + x[0:n] for a, x in zip(t, s)]
    l4 = [x[n:2 * n] for x in s]
    s = [mm(jnp.concatenate([a, b], axis=0), b) for a, b in zip(t, l4)]
    yield
    t = [a + x[0:n] for a, x in zip(t, s)]
    l8 = [x[n:2 * n] for x in s]
    t = [a + mm(a, b) for a, b in zip(t, l8)]
    yield
    off1_mask = jnp.logical_and(blk32, jnp.logical_not(blk16))
    x = [mm(a, jnp.where(off1_mask, lm, 0.0)) for a, lm in zip(t, lms)]
    yield
    t = [a - mm(b, a) for a, b in zip(t, x)]
    yield
    x = [mm(a, jnp.where(blk32, 0.0, lm)) for a, lm in zip(t, lms)]
    yield
    t = [a - mm(b, a) for a, b in zip(t, x)]
    yield
    return t


def _rwkv_prepare(units, cst):
    C = RWKV_CHUNK
    m0, cm0, eye, blk16, blk32, _ = cst[2]

    def stack(x):
        return jnp.concatenate([jnp.where(m0, x, 0.0), jnp.where(m0, 0.0, x)], axis=0)

    def blockdiag(side):
        return jnp.concatenate([jnp.where(cm0, side, 0.0), jnp.where(cm0, 0.0, side)], axis=0)

    cums = [_cumsum_mm(cst[rev][0], lw) for (_, _, _, _, _, lw, rev) in units]
    yield
    prep = []
    for (r, k, v, kap, b, lw, rev), cum in zip(units, cums):
        cumx = cum - lw
        if rev:
            mid = cum[C // 2:C // 2 + 1]
            tot = cum[0:1]
        else:
            mid = cum[C // 2 - 1:C // 2]
            tot = cum[C - 1:C]
        e_mid = jnp.exp(mid)
        rt = r * jnp.exp(cum - mid)
        kt = kap * jnp.exp(cumx - mid)
        es = jnp.exp(mid - cum)
        e_end = jnp.exp(tot - mid)
        prep.append(dict(rt=rt, kt=kt, bh=b * es, kh=k * es, r0=rt * e_mid, k0=stack(kt * e_mid),
                         be=b * es * e_end, ke=k * es * e_end, vs=stack(v), v=v,
                         e_tot=jnp.exp(tot), strict2=cst[rev][1], incl2=cst[rev][2]))
    gs = [_mm(jnp.concatenate([p["kt"], p["rt"]], axis=0),
              jnp.concatenate([stack(p["bh"]), stack(p["kh"])], axis=0), _NT) for p in prep]
    yield
    for p, g in zip(prep, gs):
        p["ab"] = jnp.where(p["strict2"], g[0:C, 0:2 * C], 0.0)
        p["ak"] = jnp.where(p["strict2"], g[0:C, 2 * C:4 * C], 0.0)
        p["rb"] = jnp.where(p["incl2"], g[C:2 * C, 0:2 * C], 0.0)
        p["rk"] = jnp.where(p["incl2"], g[C:2 * C, 2 * C:4 * C], 0.0)
    akv = [_mm(p["ak"], p["vs"], _NN) for p in prep]
    yield
    ts = yield from _inv_unit_triangular([p["ab"] for p in prep], eye, blk16, blk32, blockdiag)
    wu = [_mm(t, jnp.concatenate([p["k0"], stack(a)], axis=1), _NN) for t, p, a in zip(ts, prep, akv)]
    yield
    return [dict(wr=jnp.concatenate([x[:, :LANES], p["r0"]], axis=0), u0=x[:, LANES:],
                 rbk=jnp.concatenate([p["rb"], p["rk"]], axis=1), vs=p["vs"], v=p["v"],
                 bke=jnp.concatenate([p["be"], p["ke"]], axis=0), e_tot=p["e_tot"])
            for x, p in zip(wu, prep)]


def _rwkv_apply(prep, states, cst):
    C = RWKV_CHUNK
    m0 = cst[2][0]
    blk64 = cst[2][5]
    ws = [_mm(p["wr"], s, _NT) for p, s in zip(prep, states)]
    yield
    us = [-p["u0"] - w[0:C] for p, w in zip(prep, ws)]
    upd = [_mm(jnp.concatenate([u, p["v"]], axis=0), p["bke"], _TN) for u, p in zip(us, prep)]
    yield
    s_new = [s * p["e_tot"] + jnp.where(blk64, d, 0.0) for p, s, d in zip(prep, states, upd)]
    ys = [w[C:2 * C] + _mm(p["rbk"], jnp.concatenate([jnp.where(m0, u, 0.0), jnp.where(m0, 0.0, u), p["vs"]],
                                                     axis=0), _NN)
          for p, w, u in zip(prep, ws, us)]
    yield
    return ys, s_new


def _rwkv_consts():
    C = RWKV_CHUNK
    row = lax.broadcasted_iota(jnp.int32, (C, C), 0)
    col = lax.broadcasted_iota(jnp.int32, (C, C), 1)
    row2 = lax.broadcasted_iota(jnp.int32, (C, 2 * C), 0)
    col2 = lax.broadcasted_iota(jnp.int32, (C, 2 * C), 1)
    cs = col2 % C
    lane = lax.broadcasted_iota(jnp.int32, (1, LANES), 1)
    m0 = lane < RWKV_HEAD_DIM
    cm0 = lax.broadcasted_iota(jnp.int32, (1, 2 * C), 1) < C
    rr = lax.broadcasted_iota(jnp.int32, (2 * C, 2 * C), 0)
    cc = lax.broadcasted_iota(jnp.int32, (2 * C, 2 * C), 1)
    eye = (row2 == cs).astype(F32)
    blk16 = (row2 // 16) == (cs // 16)
    blk32 = (row2 // 32) == (cs // 32)
    blk64 = (rr // 64) == (cc // 64)
    fwd = ((col <= row).astype(BF16), cs < row2, cs <= row2)
    bwd = ((col >= row).astype(BF16), cs > row2, cs >= row2)
    return fwd, bwd, (m0, cm0, eye, blk16, blk32, blk64)


def _gla_intra(items, cst):
    G = GLA_CHUNK
    ind_v, sel, blk = cst[2]
    srow = lax.broadcasted_iota(jnp.int32, (G, 1), 0)
    cums = [_cumsum_mm(cst[rev], la) for (_, _, _, la, rev) in items]
    pmats = []
    for (q, k, v, la, rev), cum in zip(items, cums):
        ps = []
        for t in range(G):
            msk = (srow >= t) if rev else (srow <= t)
            e = jnp.exp(jnp.where(msk, cum[t:t + 1] - cum, 0.0))
            ps.append(jnp.where(msk, e * (q[t:t + 1] * k), 0.0))
        pmats.append(jnp.concatenate(ps, axis=0))
    atts = [_mm(pm, ind_v, _NN) for pm in pmats]
    o_intra = [_mm(sel, att * jnp.concatenate([it[2]] * G, axis=0), _NN)
               for att, it in zip(atts, items)]
    out = []
    for (q, k, v, la, rev), cum, oi in zip(items, cums, o_intra):
        tot = cum[0:1] if rev else cum[G - 1:G]
        upd = _mm(v, k * jnp.exp(tot - cum), _TN)
        out.append((oi, q * jnp.exp(cum), jnp.where(blk, upd, 0.0), jnp.exp(tot)))
    yield
    return out


def _gla_consts():
    G = GLA_CHUNK
    row = lax.broadcasted_iota(jnp.int32, (G, G), 0)
    col = lax.broadcasted_iota(jnp.int32, (G, G), 1)
    kc = lax.broadcasted_iota(jnp.int32, (LANES, 2 * LANES), 0)
    vc = lax.broadcasted_iota(jnp.int32, (LANES, 2 * LANES), 1)
    ind_v = ((kc // GLA_KEY_DIM) == (vc // GLA_VAL_DIM)).astype(F32)
    st = lax.broadcasted_iota(jnp.int32, (G, G * G), 0)
    sj = lax.broadcasted_iota(jnp.int32, (G, G * G), 1)
    sel = ((sj // G) == st).astype(F32)
    br = lax.broadcasted_iota(jnp.int32, (2 * LANES, LANES), 0)
    bc = lax.broadcasted_iota(jnp.int32, (2 * LANES, LANES), 1)
    blk = (br // GLA_VAL_DIM) == (bc // GLA_KEY_DIM)
    return (col <= row).astype(BF16), (col >= row).astype(BF16), (ind_v, sel, blk)


def _gla_factored(items, cst):
    F = GLA_WIDE_CHUNK
    m0, vm0, blk = cst[2]
    cums = [_cumsum_mm(cst[rev][0], la) for (_, _, _, la, rev) in items]
    yield
    scores = []
    for (q, k, v, la, rev), cum in zip(items, cums):
        mid = cum[F // 2:F // 2 + 1] if rev else cum[F // 2 - 1:F // 2]
        kt = k * jnp.exp(mid - cum)
        kstk = jnp.concatenate([jnp.where(m0, kt, 0.0), jnp.where(m0, 0.0, kt)], axis=0)
        scores.append(_mm(q * jnp.exp(cum - mid), kstk, _NT))
    yield
    o_intra = []
    for (q, k, v, la, rev), sc in zip(items, scores):
        vstk = jnp.concatenate([jnp.where(vm0, v, 0.0), jnp.where(vm0, 0.0, v)], axis=0)
        o_intra.append(_mm(jnp.where(cst[rev][1], sc, 0.0), vstk, _NN))
    yield
    out = []
    for (q, k, v, la, rev), cum, oi in zip(items, cums, o_intra):
        tot = cum[0:1] if rev else cum[F - 1:F]
        upd = _mm(v, k * jnp.exp(tot - cum), _TN)
        out.append((oi, q * jnp.exp(cum), jnp.where(blk, upd, 0.0), jnp.exp(tot)))
    yield
    return out


def _gla_wide_consts():
    F = GLA_WIDE_CHUNK
    row = lax.broadcasted_iota(jnp.int32, (F, F), 0)
    col = lax.broadcasted_iota(jnp.int32, (F, F), 1)
    row2 = lax.broadcasted_iota(jnp.int32, (F, 2 * F), 0)
    cs = lax.broadcasted_iota(jnp.int32, (F, 2 * F), 1) % F
    m0 = lax.broadcasted_iota(jnp.int32, (1, LANES), 1) < GLA_KEY_DIM
    vm0 = lax.broadcasted_iota(jnp.int32, (1, 2 * LANES), 1) < GLA_VAL_DIM
    br = lax.broadcasted_iota(jnp.int32, (2 * LANES, LANES), 0)
    bc = lax.broadcasted_iota(jnp.int32, (2 * LANES, LANES), 1)
    blk = (br // GLA_VAL_DIM) == (bc // GLA_KEY_DIM)
    return (((col <= row).astype(BF16), cs <= row2), ((col >= row).astype(BF16), cs >= row2), (m0, vm0, blk))


def _scan_kernel(raf_ref, rdf_ref, rab_ref, rdb_ref, gaf_ref, glf_ref, gab_ref, glb_ref, *refs,
                 has_init, emit_states):
    refs = list(refs)
    init_refs = [refs.pop(0) for _ in range(4)] if has_init else None
    yof_ref, yob_ref = refs.pop(0), refs.pop(0)
    final_refs = [refs.pop(0) for _ in range(4)] if emit_states else None
    srf_ref, srb_ref, sgf_ref, sgb_ref = refs
    i = pl.program_id(1)
    lane = lax.broadcasted_iota(jnp.int32, (1, LANES), 1)
    first_half = lane < LANES // 2
    rr = lax.broadcasted_iota(jnp.int32, (LANES, LANES), 0)
    cc = lax.broadcasted_iota(jnp.int32, (LANES, LANES), 1)
    diag_blocks = (rr // RWKV_HEAD_DIM) == (cc // RWKV_HEAD_DIM)

    @pl.when(i == 0)
    def _():
        if not has_init:
            for s_ref in (srf_ref, srb_ref, sgf_ref, sgb_ref):
                s_ref[...] = jnp.zeros(s_ref.shape, F32)
            return
        for s_ref, init in ((srf_ref, init_refs[0]), (srb_ref, init_refs[1])):
            for p in range(RWKV_WIDTH // LANES):
                x = init[0, p]
                s_ref[0, p] = jnp.where(diag_blocks, jnp.concatenate([x, x], axis=1), 0.0)
        for s_ref, init in ((sgf_ref, init_refs[2]), (sgb_ref, init_refs[3])):
            for p in range(GLA_QK_WIDTH // LANES):
                xt = init[0, p].T
                s_ref[0, p] = jnp.concatenate([jnp.where(first_half, xt, 0.0),
                                               jnp.where(first_half, 0.0, xt)], axis=0)

    T = SCAN_SUB
    C = RWKV_CHUNK
    nr = T // C
    nsub = SCAN_BLOCK // T
    R = RWKV_WIDTH
    Q = GLA_QK_WIDTH
    npair = Q // LANES

    def sub_block(h, carry):
        base = (0, 0) if nsub == 1 else (pl.multiple_of(h * T, T), pl.multiple_of((nsub - 1 - h) * T, T))

        def rows_of(d, start, size):
            return pl.ds(base[d] + start, size)

        rcst = _rwkv_consts()
        g_dirs = ((False, gaf_ref, glf_ref, 0, sgf_ref, yof_ref),
                  (True, gab_ref, glb_ref, Q, sgb_ref, yob_ref))
        la_tot = jnp.minimum(jnp.sum(glf_ref[0, rows_of(0, 0, T), 0:Q], axis=0, keepdims=True),
                             jnp.sum(glb_ref[0, rows_of(1, 0, T), Q:2 * Q], axis=0, keepdims=True))
        gla_split_ok = jnp.min(la_tot) >= -GLA_SAFE_LOG

        r_dirs = ((False, raf_ref, rdf_ref, srf_ref, yof_ref),
                  (True, rab_ref, rdb_ref, srb_ref, yob_ref))
        r_units = {}
        r_dests = {}
        for d, (rev, ra_ref, rd_ref, s_ref, y_ref) in enumerate(r_dirs):
            for j in range(nr):
                c = nr - 1 - j if rev else j
                rows = rows_of(d, c * C, C)
                r_units[d, j] = []
                r_dests[d, j] = []
                for p in range(R // LANES):
                    sl, sl1, sl2 = (slice(o + p * LANES, o + (p + 1) * LANES) for o in (0, R, 2 * R))
                    r_units[d, j].append((ra_ref[0, rows, sl], rd_ref[0, rows, sl], ra_ref[0, rows, sl1],
                                          ra_ref[0, rows, sl2], rd_ref[0, rows, sl1], rd_ref[0, rows, sl2], rev))
                    r_dests[d, j].append((y_ref, s_ref, p, rows, sl))
        r_states = {(d, 0): [s_ref[0, p] for (_, s_ref, p, _, _) in r_dests[d, 0]] for d in range(len(r_dirs))}

        def rwkv_task(d, j):
            prep = yield from _rwkv_prepare(r_units[d, j], rcst)
            while (d, j) not in r_states:
                yield
            ys, r_states[d, j + 1] = yield from _rwkv_apply(prep, r_states[d, j], rcst)
            for (y_ref, _, _, rows, sl), y in zip(r_dests[d, j], ys):
                y_ref[0, rows, sl] = y

        g_init = [[s_ref[0, p] for p in range(npair)] for (_, _, _, _, s_ref, _) in g_dirs]

        def gla_task(chunk, intra, cst):
            ng = T // chunk
            items = [[] for _ in range(ng)]
            dests = [[] for _ in range(ng)]
            for d, (rev, ga_ref, gl_ref, la_off, s_ref, o_ref) in enumerate(g_dirs):
                for p in range(npair):
                    qs, ks, las = (slice(o + p * LANES, o + (p + 1) * LANES) for o in (0, Q, la_off))
                    vs = slice(2 * Q + p * 2 * LANES, 2 * Q + (p + 1) * 2 * LANES)
                    os_ = slice(R + p * 2 * LANES, R + (p + 1) * 2 * LANES)
                    for j in range(ng):
                        c = ng - 1 - j if rev else j
                        rows = rows_of(d, c * chunk, chunk)
                        items[j].append((ga_ref[0, rows, qs], ga_ref[0, rows, ks], ga_ref[0, rows, vs],
                                         gl_ref[0, rows, las], rev))
                        dests[j].append((o_ref, s_ref, d, p, rows, os_))
            parts = yield from intra([it for its in items for it in its], cst)
            nu = len(items[0])
            sts = [g_init[d][p] for (_, _, d, p, _, _) in dests[0]]
            for j in range(ng):
                new = []
                for (o_ref, _, _, _, rows, vs), (o_intra, q_in, upd, e_tot), st in zip(
                        dests[j], parts[j * nu:(j + 1) * nu], sts):
                    o_ref[0, rows, vs] = o_intra + _mm(q_in, st, _NT)
                    new.append(st * e_tot + upd)
                sts = new
                yield
            for (_, s_ref, _, p, _, _), st in zip(dests[0], sts):
                s_ref[0, p] = st

        nd = len(r_dirs)
        _interleave([(RWKV_TASK_LAG * j, rwkv_task(d, j)) for j in range(nr) for d in range(nd)]
                    + [(GLA_TASK_LAG, gla_task(GLA_WIDE_CHUNK, _gla_factored, _gla_wide_consts()))])
        for d in range(nd):
            for (_, s_ref, p, _, _), s in zip(r_dests[d, 0], r_states[d, nr]):
                s_ref[0, p] = s

        @pl.when(jnp.logical_not(gla_split_ok))
        def _():
            _interleave([(0, gla_task(GLA_CHUNK, _gla_intra, _gla_consts()))])

        return carry

    if nsub == 1:
        sub_block(0, 0)
    else:
        lax.fori_loop(0, nsub, sub_block, 0)

    if emit_states:
        @pl.when(i == pl.num_programs(1) - 1)
        def _():
            for s_ref, out in ((srf_ref, final_refs[0]), (srb_ref, final_refs[1])):
                for p in range(RWKV_WIDTH // LANES):
                    s = s_ref[0, p]
                    out[0, p] = (s + pltpu.roll(s, LANES // 2, 1))[:, 0:LANES // 2]
            for s_ref, out in ((sgf_ref, final_refs[2]), (sgb_ref, final_refs[3])):
                for p in range(GLA_QK_WIDTH // LANES):
                    st = s_ref[0, p]
                    out[0, p] = st[0:LANES].T + st[LANES:2 * LANES].T


def _scan(ra, rd_f, rd_b, ga, gl, states, emit_states):
    B, L, _ = ra.shape
    R = RWKV_WIDTH
    Q = GLA_QK_WIDTH
    V = GLA_V_WIDTH
    T = SCAN_BLOCK
    n = L // T

    def fwd(w):
        return pl.BlockSpec((1, T, w), lambda b, i: (b, i, 0))

    def bwd(w):
        return pl.BlockSpec((1, T, w), lambda b, i: (b, n - 1 - i, 0))

    rshape = (R // LANES, LANES, LANES // 2)
    gshape = (Q // LANES, LANES, LANES)
    rst = pl.BlockSpec((1,) + rshape, lambda b, i: (b, 0, 0, 0))
    gst = pl.BlockSpec((1,) + gshape, lambda b, i: (b, 0, 0, 0))
    st_specs = [rst, rst, gst, gst]
    st_shapes = [jax.ShapeDtypeStruct((B,) + s, F32) for s in (rshape, rshape, gshape, gshape)]
    states = () if states is None else tuple(states)
    return pl.pallas_call(
        functools.partial(_scan_kernel, has_init=bool(states), emit_states=emit_states),
        grid=(B, n),
        in_specs=[fwd(3 * R), fwd(3 * R), bwd(3 * R), bwd(3 * R),
                  fwd(2 * Q + V), fwd(2 * Q), bwd(2 * Q + V), bwd(2 * Q)] + st_specs[:len(states)],
        out_specs=[fwd(R + V), bwd(R + V)] + (st_specs if emit_states else []),
        out_shape=[jax.ShapeDtypeStruct((B, L, R + V), F32)] * 2 + (st_shapes if emit_states else []),
        scratch_shapes=[pltpu.VMEM((1, R // LANES, LANES, LANES), F32)] * 2
        + [pltpu.VMEM((1, Q // LANES, 2 * LANES, LANES), F32)] * 2,
        compiler_params=pltpu.CompilerParams(dimension_semantics=("parallel", "arbitrary"),
                                             vmem_limit_bytes=VMEM_LIMIT),
        name="scan",
    )(ra, rd_f, ra, rd_b, ga, gl, ga, gl, *states)


def _post_kernel(x_ref, mod_ref, yof_ref, yob_ref, pg_ref,
                 lng_ref, lnb_ref, gng_ref, n2g_ref, fng_ref, wout_ref, w1_ref, w2_ref, o_ref):
    D = D_MODEL
    R = RWKV_WIDTH
    m = mod_ref[0]
    gt1 = m[:, 2 * D:3 * D]
    sh2 = m[:, 3 * D:4 * D]
    sc2 = m[:, 4 * D:5 * D]
    gt2 = m[:, 5 * D:6 * D]
    inv_n = 1.0 / RWKV_HEAD_DIM
    gng = gng_ref[...]
    nseq, tb = x_ref.shape[0], x_ref.shape[1]
    if nseq > 1:
        rows = [(s, slice(0, tb)) for s in range(nseq)]
    else:
        nsplit = POST_SPLIT if tb % (8 * POST_SPLIT) == 0 else 1
        rows = [(0, slice(s * (tb // nsplit), (s + 1) * (tb // nsplit))) for s in range(nsplit)]

    def mixed(q, rs):
        y = yof_ref[q, rs, 0:R] + yob_ref[q, rs, 0:R]
        mu = _head_sum(y) * inv_n
        yc = y - mu
        var = _head_sum(yc * yc) * inv_n
        yn = yc * lax.rsqrt(var + LNX_EPS) * lng_ref[...] + lnb_ref[...]
        parts = [(yn + pg_ref[q, rs, R:2 * R]) * pg_ref[q, rs, 0:R]]
        o = yof_ref[q, rs, R:] + yob_ref[q, rs, R:]
        for hh in range(GLA_HEADS):
            sl = slice(hh * GLA_VAL_DIM, (hh + 1) * GLA_VAL_DIM)
            oh = o[:, sl]
            oh = oh * lax.rsqrt(jnp.mean(oh * oh, axis=-1, keepdims=True) + GLA_NORM_EPS)
            parts.append(oh * gng * pg_ref[q, rs, 2 * R + hh * GLA_VAL_DIM:2 * R + (hh + 1) * GLA_VAL_DIM])
        return jnp.concatenate(parts, axis=-1)

    mix = [mixed(q, rs) for q, rs in rows]
    x1 = [x_ref[q, rs] + gt1 * _bdot(mx, wout_ref) for (q, rs), mx in zip(rows, mix)]
    h2 = [_rms(v) * n2g_ref[...] * (1.0 + sc2) + sh2 for v in x1]
    f = [jnp.maximum(_bdot(v, w1_ref), 0.0) for v in h2]
    x2 = [a + gt2 * _bdot(v * v, w2_ref) for a, v in zip(x1, f)]
    for (q, rs), v in zip(rows, x2):
        o_ref[q, rs] = _rms(v) * fng_ref[...]


def _post(x, mod3, mod_row, yo_f, yo_b, pg, vecs, mats, shared_mod):
    B, L, D = x.shape
    tb = min(POST_BLOCK, L)
    nblk = L // tb
    nseq = POST_BLOCK // L if (shared_mod and L < POST_BLOCK and B % (POST_BLOCK // L) == 0) else 1

    def tok(w):
        return pl.BlockSpec((nseq, tb, w), lambda b, i: (b, i, 0))

    in_specs = [tok(D), pl.BlockSpec((1, 1, N_MOD * D), lambda b, i: (mod_row(b * nseq), 0, 0))]
    in_specs += [tok(yo_f.shape[-1]), tok(yo_b.shape[-1]), tok(pg.shape[-1])]
    in_specs += [_const_spec(w.shape) for w in vecs]
    in_specs += [pl.BlockSpec(w.shape, lambda b, i: (0, 0), pipeline_mode=pl.Buffered(1)) for w in mats]
    return pl.pallas_call(
        _post_kernel,
        grid=(B // nseq, nblk),
        in_specs=in_specs,
        out_specs=tok(D),
        out_shape=jax.ShapeDtypeStruct((B, L, D), F32),
        compiler_params=pltpu.CompilerParams(dimension_semantics=("parallel", "arbitrary"),
                                             vmem_limit_bytes=VMEM_LIMIT),
        name="post",
    )(x, mod3, yo_f, yo_b, pg, *vecs, *mats)


def _blockdiag2(a, b):
    za = jnp.zeros((a.shape[0], b.shape[1]), a.dtype)
    zb = jnp.zeros((b.shape[0], a.shape[1]), a.dtype)
    return jnp.concatenate([jnp.concatenate([a, za], axis=1), jnp.concatenate([zb, b], axis=1)], axis=0)


def kernel(x_prompt, x_sample, c, state_rwkv_fwd, state_rwkv_bwd, state_gla_fwd, state_gla_bwd, c_ctx, ada_w, ada_b, norm1_g, norm2_g, w_in, rwkv_mu_rkv, rwkv_mu_wag, rwkv_w0, rwkv_w1, rwkv_w2, rwkv_a0, rwkv_a1, rwkv_a2, rwkv_g1, rwkv_g2, rwkv_k_k, rwkv_k_a, rwkv_r_k, rwkv_lnx_g, rwkv_lnx_b, gla_gk1, gla_gk2, gla_gk_b, gla_norm_g, w_out, mlp_w1, mlp_w2, final_norm_g):
    D = D_MODEL
    R = RWKV_WIDTH
    nb = x_prompt.shape[0]
    nd = x_sample.shape[0]
    assert ada_w.shape[0] == 1, "single-layer step"
    layer = 0

    craw = jnp.concatenate([c_ctx[None, :], c, jnp.zeros((8 - 1 - nd, D), F32)], axis=0)
    mod = _modulation(craw, ada_w[layer], ada_b[layer][None, :])
    mod3 = mod.reshape(8, 1, N_MOD * D)

    bf = lambda t: t.astype(BF16)
    row = lambda t: t.reshape(1, -1).astype(F32)
    w_in_bf = bf(w_in[layer])
    half = (D, 3 * R)
    pre_w = [
        row(norm1_g[layer]),
        (w_in_bf, pl.BlockSpec(half, lambda b, i: (0, 0))),
        (w_in_bf, pl.BlockSpec(half, lambda b, i: (0, 1))),
        bf(jnp.concatenate([gla_gk1[layer, 0], gla_gk1[layer, 1]], axis=1)),
        bf(jnp.concatenate([rwkv_w1[layer, 0], rwkv_w1[layer, 1]], axis=1)),
        bf(jnp.concatenate([rwkv_a1[layer, 0], rwkv_a1[layer, 1]], axis=1)),
        bf(rwkv_g1[layer]),
        bf(_blockdiag2(rwkv_w2[layer, 0], rwkv_w2[layer, 1])),
        bf(_blockdiag2(rwkv_a2[layer, 0], rwkv_a2[layer, 1])),
        bf(rwkv_g2[layer]),
        bf(_blockdiag2(gla_gk2[layer, 0], gla_gk2[layer, 1])),
        row(rwkv_mu_rkv[layer]),
        rwkv_mu_wag[layer],
        row(rwkv_w0[layer]),
        row(rwkv_a0[layer]),
        row(gla_gk_b[layer]),
        row(rwkv_k_k[layer]),
        row(rwkv_k_a[layer]),
        row(rwkv_r_k[layer]),
    ]
    post_vecs = [row(rwkv_lnx_g[layer]), row(rwkv_lnx_b[layer]), row(gla_norm_g[layer]),
                 row(norm2_g[layer]), row(final_norm_g)]
    post_mats = [bf(w_out[layer]), bf(mlp_w1[layer]), bf(mlp_w2[layer])]

    def run_group(x, mod_row, grid_shift, states, emit_states, shared_mod):
        ra, rd_f, rd_b, ga, gl, pg = _pre(x, mod3, mod_row, pre_w, grid_shift)
        yo_f, yo_b, *finals = _scan(ra, rd_f, rd_b, ga, gl, states, emit_states)
        y = _post(x, mod3, mod_row, yo_f, yo_b, pg, post_vecs, post_mats, shared_mod)
        return y, finals

    rpair = lambda s: s.reshape(nd, R // LANES, LANES, RWKV_HEAD_DIM)
    gpair = lambda s: s.reshape(nd, GLA_QK_WIDTH // LANES, LANES, GLA_VAL_DIM)
    y_prompt, (n_rf, n_rb, n_gf, n_gb) = run_group(x_prompt, lambda b: 0, False, None, True, True)
    y_sample, _ = run_group(x_sample, lambda b: b + 1, True,
                            (rpair(state_rwkv_fwd[:, layer]), rpair(state_rwkv_bwd[:, layer]),
                             gpair(state_gla_fwd[:, layer]), gpair(state_gla_bwd[:, layer])), False, False)
    rshape = (nb, 1, RWKV_HEADS, RWKV_HEAD_DIM, RWKV_HEAD_DIM)
    gshape = (nb, 1, GLA_HEADS, GLA_KEY_DIM, GLA_VAL_DIM)
    return (y_prompt, y_sample, n_rf.reshape(rshape), n_rb.reshape(rshape),
            n_gf.reshape(gshape), n_gb.reshape(gshape))
```

```python
import functools

import jax
import jax.numpy as jnp
from jax import lax
from jax.experimental import pallas as pl
from jax.experimental.pallas import tpu as pltpu

F32 = jnp.float32
BF16 = jnp.bfloat16

D_MODEL = 1024
GRID_W = 64
RWKV_WIDTH = 512
RWKV_HEAD_DIM = 64
RWKV_HEADS = 8
GLA_HEADS = 4
GLA_KEY_DIM = 64
GLA_VAL_DIM = 128
GLA_QK_WIDTH = 256
GLA_V_WIDTH = 512
GLA_GATE_NORMALIZER = 16.0
N_MOD = 6
RMS_EPS = 1e-6
LNX_EPS = 64e-5
GLA_NORM_EPS = 1e-5

LANES = 128
RWKV_CHUNK = 64
GLA_CHUNK = 16
GLA_WIDE_CHUNK = 64
GLA_SAFE_LOG = 60.0
SCAN_BLOCK = 256
SCAN_SUB = 128
PRE_BLOCK = 512
POST_BLOCK = 512
POST_SPLIT = 2
VMEM_LIMIT = 56 * 1024 * 1024
RWKV_TASK_LAG = 2
GLA_TASK_LAG = 14
DECAY_LOG_SCALE = 0.6065306597126334

_NN = (((1,), (0,)), ((), ()))
_NT = (((1,), (1,)), ((), ()))
_TN = (((0,), (0,)), ((), ()))


def _dot(a, b):
    return lax.dot_general(a, b, _NN, preferred_element_type=F32)


def _mm(a, b, dims):
    return lax.dot_general(a.astype(BF16), b.astype(BF16), dims, preferred_element_type=F32)


def _cumsum_mm(tri, x):
    hi = x.astype(BF16)
    r1 = x - hi.astype(F32)
    mid = r1.astype(BF16)
    lo = (r1 - mid.astype(F32)).astype(BF16)
    return _dot(tri, hi) + _dot(tri, mid) + _dot(tri, lo)


def _bdot(a, w_ref):
    return _dot(a.astype(BF16), w_ref[...])


def _sigmoid(x):
    return 1.0 / (1.0 + jnp.exp(-x))


def _softplus(x):
    return jnp.maximum(x, 0.0) + jnp.log(1.0 + jnp.exp(-jnp.abs(x)))


def _head_sum(x):
    first = lax.broadcasted_iota(jnp.int32, (1, LANES), 1) < RWKV_HEAD_DIM
    outs = []
    for j in range(x.shape[-1] // LANES):
        xb = x[:, j * LANES:(j + 1) * LANES]
        s0 = jnp.sum(jnp.where(first, xb, 0.0), axis=-1, keepdims=True)
        s1 = jnp.sum(jnp.where(first, 0.0, xb), axis=-1, keepdims=True)
        outs.append(jnp.where(first, s0, s1))
    return jnp.concatenate(outs, axis=-1)


def _rms(x):
    return x * lax.rsqrt(jnp.mean(x * x, axis=-1, keepdims=True) + RMS_EPS)


def _mod_kernel(c_ref, w_ref, b_ref, o_ref):
    c = c_ref[...]
    cond = c * _sigmoid(c)
    o_ref[...] = _dot(cond.astype(BF16), w_ref[...].astype(BF16)) + b_ref[...]


def _modulation(craw, ada_w, ada_b):
    n = ada_w.shape[1]
    bn = 1536
    return pl.pallas_call(
        _mod_kernel,
        grid=(n // bn,),
        in_specs=[
            pl.BlockSpec((8, D_MODEL), lambda j: (0, 0)),
            pl.BlockSpec((D_MODEL, bn), lambda j: (0, j)),
            pl.BlockSpec((1, bn), lambda j: (0, j)),
        ],
        out_specs=pl.BlockSpec((8, bn), lambda j: (0, j)),
        out_shape=jax.ShapeDtypeStruct((8, n), F32),
        compiler_params=pltpu.CompilerParams(dimension_semantics=("arbitrary",), vmem_limit_bytes=VMEM_LIMIT),
        name="modulation",
    )(craw, ada_w, ada_b)


def _pre_kernel(*refs, grid_shift, tb):
    if grid_shift:
        x_ref, xp_ref, xn_ref = refs[:3]
        refs = refs[3:]
    else:
        x_ref = refs[0]
        refs = refs[1:]
    (mod_ref, n1g_ref, wrkv_ref, wrest_ref, gk1_ref, w1_ref, a1_ref, g1_ref, w2_ref, a2_ref, g2_ref,
     gk2_ref, murkv_ref, muwag_ref, w0_ref, a0_ref, gkb_ref, kk_ref, ka_ref, rk_ref) = refs[:20]
    ra_o, rf_o, rb_o, ga_o, gl_o, pg_o = refs[20:]
    D = D_MODEL
    R = RWKV_WIDTH
    m = mod_ref[0]
    sh1 = m[:, 0:D]
    sc1 = m[:, D:2 * D]
    n1g = n1g_ref[...]

    def normmod(xx):
        return _rms(xx) * n1g * (1.0 + sc1) + sh1

    h = normmod(x_ref[0])
    row = lax.broadcasted_iota(jnp.int32, (tb, 1), 0)
    if grid_shift:
        i = pl.program_id(1)
        n = pl.num_programs(1)
        hp = normmod(xp_ref[0]) * (i > 0).astype(F32)
        hn = normmod(xn_ref[0]) * (i < n - 1).astype(F32)
        hext = jnp.concatenate([hp, h, hn], axis=0)
        col = row % GRID_W
        m_l = (col != 0).astype(F32)
        m_r = (col != GRID_W - 1).astype(F32)
        ne = tb + 2 * GRID_W

        def shift(ext):
            up = ext[0:tb]
            down = ext[2 * GRID_W:2 * GRID_W + tb]
            left = pltpu.roll(ext, 1, 0)[GRID_W:GRID_W + tb]
            right = pltpu.roll(ext, ne - 1, 0)[GRID_W:GRID_W + tb]
            return 0.25 * (up + down + m_l * left + m_r * right)

        halo = GRID_W
    else:
        hext = h
        m_l = (row != 0).astype(F32)
        m_r = (row != tb - 1).astype(F32)

        def shift(ext):
            return 0.5 * (m_l * pltpu.roll(ext, 1, 0) + m_r * pltpu.roll(ext, tb - 1, 0))

        halo = 0

    rest = _bdot(h, wrest_ref)
    dh = shift(hext) - h
    mu = muwag_ref[...]
    lora_w = _bdot(h + mu[0:1] * dh, w1_ref)
    lora_a = _bdot(h + mu[1:2] * dh, a1_ref)
    lora_g = _bdot(h + mu[2:3] * dh, g1_ref)
    lora_gk = _bdot(h, gk1_ref)
    z = w0_ref[...] + _bdot(jnp.tanh(lora_w), w2_ref)
    a = _sigmoid(a0_ref[...] + _bdot(lora_a, a2_ref))
    gate = _bdot(_sigmoid(lora_g), g2_ref)
    logits = _bdot(lora_gk, gk2_ref) + gkb_ref[...]
    rkv_ext = _bdot(hext, wrkv_ref)

    Q = GLA_QK_WIDTH
    ga_o[0, :, 0:Q] = rest[:, 0:Q] * (GLA_KEY_DIM ** -0.5)
    ga_o[0, :, Q:] = rest[:, Q:2 * Q + GLA_V_WIDTH]
    gg = rest[:, 2 * Q + GLA_V_WIDTH:]
    pg_o[0, :, 2 * R:] = gg * _sigmoid(gg)
    gl_o[0] = -_softplus(-logits) * (1.0 / GLA_GATE_NORMALIZER)

    rkv = rkv_ext[halo:halo + tb]
    rkv = rkv + murkv_ref[...] * (shift(rkv_ext) - rkv)
    r = rkv[:, 0:R]
    k = rkv[:, R:2 * R]
    v = rkv[:, 2 * R:3 * R]
    lw = -DECAY_LOG_SCALE * _sigmoid(z)

    kap = k * kk_ref[...]
    kap = kap * lax.rsqrt(jnp.maximum(_head_sum(kap * kap), 1e-12))
    ka = ka_ref[...]
    a_f = a[:, 0:R]
    a_b = a[:, R:2 * R]
    kd_f = k * (1.0 + (a_f - 1.0) * ka)
    kd_b = k * (1.0 + (a_b - 1.0) * ka)
    bonus = _head_sum(r * (kd_f + kd_b) * rk_ref[...]) * v

    ra_o[0, :, 0:R] = r
    ra_o[0, :, R:2 * R] = v
    ra_o[0, :, 2 * R:3 * R] = kap
    rf_o[0, :, 0:R] = kd_f
    rf_o[0, :, R:2 * R] = a_f * kap
    rf_o[0, :, 2 * R:3 * R] = lw[:, 0:R]
    rb_o[0, :, 0:R] = kd_b
    rb_o[0, :, R:2 * R] = a_b * kap
    rb_o[0, :, 2 * R:3 * R] = lw[:, R:2 * R]
    pg_o[0, :, 0:R] = gate
    pg_o[0, :, R:2 * R] = bonus


def _const_spec(shape):
    nd = len(shape)
    return pl.BlockSpec(shape, lambda b, i: (0,) * nd)


def _pre(x, mod3, mod_row, weights, grid_shift):
    B, L, D = x.shape
    tb = min(PRE_BLOCK, L)
    nblk = L // tb
    if not grid_shift:
        assert nblk == 1
    x_spec = pl.BlockSpec((1, tb, D), lambda b, i: (b, i, 0))
    in_specs = [x_spec]
    args = [x]
    if grid_shift:
        per = tb // GRID_W
        nrow = L // GRID_W
        in_specs += [
            pl.BlockSpec((1, GRID_W, D), lambda b, i: (b, jnp.maximum(i * per - 1, 0), 0)),
            pl.BlockSpec((1, GRID_W, D), lambda b, i: (b, jnp.minimum((i + 1) * per, nrow - 1), 0)),
        ]
        args += [x, x]
    in_specs.append(pl.BlockSpec((1, 1, N_MOD * D), lambda b, i: (mod_row(b), 0, 0)))
    args.append(mod3)
    for w in weights:
        w, spec = w if isinstance(w, tuple) else (w, _const_spec(w.shape))
        in_specs.append(spec)
        args.append(w)
    R = RWKV_WIDTH
    widths = [3 * R, 3 * R, 3 * R, 2 * GLA_QK_WIDTH + GLA_V_WIDTH, 2 * GLA_QK_WIDTH, 2 * R + GLA_V_WIDTH]
    out_specs = [pl.BlockSpec((1, tb, w), lambda b, i: (b, i, 0)) for w in widths]
    out_shape = [jax.ShapeDtypeStruct((B, L, w), F32) for w in widths]
    return pl.pallas_call(
        functools.partial(_pre_kernel, grid_shift=grid_shift, tb=tb),
        grid=(B, nblk),
        in_specs=in_specs,
        out_specs=out_specs,
        out_shape=out_shape,
        compiler_params=pltpu.CompilerParams(dimension_semantics=("parallel", "arbitrary"),
                                             vmem_limit_bytes=VMEM_LIMIT),
        name="pre_grid" if grid_shift else "pre_seq",
    )(*args)


def _interleave(tasks):
    live = list(tasks)
    rnd = 0
    while live:
        keep = []
        for first, gen in live:
            if rnd >= first:
                try:
                    next(gen)
                except StopIteration:
                    continue
            keep.append((first, gen))
        live = keep
        rnd += 1


def _inv_unit_triangular(lms, eye, blk16, blk32, blockdiag):
    def mm(a, b):
        return _mm(a, blockdiag(b), _NN)

    n = lms[0].shape[0]
    l0 = [jnp.where(blk16, lm, 0.0) for lm in lms]
    l2 = [mm(a, a) for a in l0]
    yield
    t = [eye - a for a in l0]
    s = [mm(jnp.concatenate([a, b], axis=0), b) for a, b in zip(t, l2)]
    yield
    t = [a + x[0:n] for a, x in zip(t, s)]
    l4 = [x[n:2 * n] for x in s]
    s = [mm(jnp.concatenate([a, b], axis=0), b) for a, b in zip(t, l4)]
    yield
    t = [a + x[0:n] for a, x in zip(t, s)]
    l8 = [x[n:2 * n] for x in s]
    t = [a + mm(a, b) for a, b in zip(t, l8)]
    yield
    off1_mask = jnp.logical_and(blk32, jnp.logical_not(blk16))
    x = [mm(a, jnp.where(off1_mask, lm, 0.0)) for a, lm in zip(t, lms)]
    yield
    t = [a - mm(b, a) for a, b in zip(t, x)]
    yield
    x = [mm(a, jnp.where(blk32, 0.0, lm)) for a, lm in zip(t, lms)]
    yield
    t = [a - mm(b, a) for a, b in zip(t, x)]
    yield
    return t


def _rwkv_prepare(units, cst):
    C = RWKV_CHUNK
    m0, cm0, eye, blk16, blk32, _ = cst[2]

    def stack(x):
        return jnp.concatenate([jnp.where(m0, x, 0.0), jnp.where(m0, 0.0, x)], axis=0)

    def blockdiag(side):
        return jnp.concatenate([jnp.where(cm0, side, 0.0), jnp.where(cm0, 0.0, side)], axis=0)

    cums = [_cumsum_mm(cst[rev][0], lw) for (_, _, _, _, _, lw, rev) in units]
    yield
    prep = []
    for (r, k, v, kap, b, lw, rev), cum in zip(units, cums):
        cumx = cum - lw
        if rev:
            mid = cum[C // 2:C // 2 + 1]
            tot = cum[0:1]
        else:
            mid = cum[C // 2 - 1:C // 2]
            tot = cum[C - 1:C]
        e_mid = jnp.exp(mid)
        rt = r * jnp.exp(cum - mid)
        kt = kap * jnp.exp(cumx - mid)
        es = jnp.exp(mid - cum)
        e_end = jnp.exp(tot - mid)
        prep.append(dict(rt=rt, kt=kt, bh=b * es, kh=k * es, r0=rt * e_mid, k0=stack(kt * e_mid),
                         be=b * es * e_end, ke=k * es * e_end, vs=stack(v), v=v,
                         e_tot=jnp.exp(tot), strict2=cst[rev][1], incl2=cst[rev][2]))
    gs = [_mm(jnp.concatenate([p["kt"], p["rt"]], axis=0),
              jnp.concatenate([stack(p["bh"]), stack(p["kh"])], axis=0), _NT) for p in prep]
    yield
    for p, g in zip(prep, gs):
        p["ab"] = jnp.where(p["strict2"], g[0:C, 0:2 * C], 0.0)
        p["ak"] = jnp.where(p["strict2"], g[0:C, 2 * C:4 * C], 0.0)
        p["rb"] = jnp.where(p["incl2"], g[C:2 * C, 0:2 * C], 0.0)
        p["rk"] = jnp.where(p["incl2"], g[C:2 * C, 2 * C:4 * C], 0.0)
    akv = [_mm(p["ak"], p["vs"], _NN) for p in prep]
    yield
    ts = yield from _inv_unit_triangular([p["ab"] for p in prep], eye, blk16, blk32, blockdiag)
    wu = [_mm(t, jnp.concatenate([p["k0"], stack(a)], axis=1), _NN) for t, p, a in zip(ts, prep, akv)]
    yield
    return [dict(wr=jnp.concatenate([x[:, :LANES], p["r0"]], axis=0), u0=x[:, LANES:],
                 rbk=jnp.concatenate([p["rb"], p["rk"]], axis=1), vs=p["vs"], v=p["v"],
                 bke=jnp.concatenate([p["be"], p["ke"]], axis=0), e_tot=p["e_tot"])
            for x, p in zip(wu, prep)]


def _rwkv_apply(prep, states, cst):
    C = RWKV_CHUNK
    m0 = cst[2][0]
    blk64 = cst[2][5]
    ws = [_mm(p["wr"], s, _NT) for p, s in zip(prep, states)]
    yield
    us = [-p["u0"] - w[0:C] for p, w in zip(prep, ws)]
    upd = [_mm(jnp.concatenate([u, p["v"]], axis=0), p["bke"], _TN) for u, p in zip(us, prep)]
    yield
    s_new = [s * p["e_tot"] + jnp.where(blk64, d, 0.0) for p, s, d in zip(prep, states, upd)]
    ys = [w[C:2 * C] + _mm(p["rbk"], jnp.concatenate([jnp.where(m0, u, 0.0), jnp.where(m0, 0.0, u), p["vs"]],
                                                     axis=0), _NN)
          for p, w, u in zip(prep, ws, us)]
    yield
    return ys, s_new


def _rwkv_consts():
    C = RWKV_CHUNK
    row = lax.broadcasted_iota(jnp.int32, (C, C), 0)
    col = lax.broadcasted_iota(jnp.int32, (C, C), 1)
    row2 = lax.broadcasted_iota(jnp.int32, (C, 2 * C), 0)
    col2 = lax.broadcasted_iota(jnp.int32, (C, 2 * C), 1)
    cs = col2 % C
    lane = lax.broadcasted_iota(jnp.int32, (1, LANES), 1)
    m0 = lane < RWKV_HEAD_DIM
    cm0 = lax.broadcasted_iota(jnp.int32, (1, 2 * C), 1) < C
    rr = lax.broadcasted_iota(jnp.int32, (2 * C, 2 * C), 0)
    cc = lax.broadcasted_iota(jnp.int32, (2 * C, 2 * C), 1)
    eye = (row2 == cs).astype(F32)
    blk16 = (row2 // 16) == (cs // 16)
    blk32 = (row2 // 32) == (cs // 32)
    blk64 = (rr // 64) == (cc // 64)
    fwd = ((col <= row).astype(BF16), cs < row2, cs <= row2)
    bwd = ((col >= row).astype(BF16), cs > row2, cs >= row2)
    return fwd, bwd, (m0, cm0, eye, blk16, blk32, blk64)


def _gla_intra(items, cst):
    G = GLA_CHUNK
    ind_v, sel, blk = cst[2]
    srow = lax.broadcasted_iota(jnp.int32, (G, 1), 0)
    cums = [_cumsum_mm(cst[rev], la) for (_, _, _, la, rev) in items]
    pmats = []
    for (q, k, v, la, rev), cum in zip(items, cums):
        ps = []
        for t in range(G):
            msk = (srow >= t) if rev else (srow <= t)
            e = jnp.exp(jnp.where(msk, cum[t:t + 1] - cum, 0.0))
            ps.append(jnp.where(msk, e * (q[t:t + 1] * k), 0.0))
        pmats.append(jnp.concatenate(ps, axis=0))
    atts = [_mm(pm, ind_v, _NN) for pm in pmats]
    o_intra = [_mm(sel, att * jnp.concatenate([it[2]] * G, axis=0), _NN)
               for att, it in zip(atts, items)]
    out = []
    for (q, k, v, la, rev), cum, oi in zip(items, cums, o_intra):
        tot = cum[0:1] if rev else cum[G - 1:G]
        upd = _mm(v, k * jnp.exp(tot - cum), _TN)
        out.append((oi, q * jnp.exp(cum), jnp.where(blk, upd, 0.0), jnp.exp(tot)))
    yield
    return out


def _gla_consts():
    G = GLA_CHUNK
    row = lax.broadcasted_iota(jnp.int32, (G, G), 0)
    col = lax.broadcasted_iota(jnp.int32, (G, G), 1)
    kc = lax.broadcasted_iota(jnp.int32, (LANES, 2 * LANES), 0)
    vc = lax.broadcasted_iota(jnp.int32, (LANES, 2 * LANES), 1)
    ind_v = ((kc // GLA_KEY_DIM) == (vc // GLA_VAL_DIM)).astype(F32)
    st = lax.broadcasted_iota(jnp.int32, (G, G * G), 0)
    sj = lax.broadcasted_iota(jnp.int32, (G, G * G), 1)
    sel = ((sj // G) == st).astype(F32)
    br = lax.broadcasted_iota(jnp.int32, (2 * LANES, LANES), 0)
    bc = lax.broadcasted_iota(jnp.int32, (2 * LANES, LANES), 1)
    blk = (br // GLA_VAL_DIM) == (bc // GLA_KEY_DIM)
    return (col <= row).astype(BF16), (col >= row).astype(BF16), (ind_v, sel, blk)


def _gla_factored(items, cst):
    F = GLA_WIDE_CHUNK
    m0, vm0, blk = cst[2]
    cums = [_cumsum_mm(cst[rev][0], la) for (_, _, _, la, rev) in items]
    yield
    scores = []
    for (q, k, v, la, rev), cum in zip(items, cums):
        mid = cum[F // 2:F // 2 + 1] if rev else cum[F // 2 - 1:F // 2]
        kt = k * jnp.exp(mid - cum)
        kstk = jnp.concatenate([jnp.where(m0, kt, 0.0), jnp.where(m0, 0.0, kt)], axis=0)
        scores.append(_mm(q * jnp.exp(cum - mid), kstk, _NT))
    yield
    o_intra = []
    for (q, k, v, la, rev), sc in zip(items, scores):
        vstk = jnp.concatenate([jnp.where(vm0, v, 0.0), jnp.where(vm0, 0.0, v)], axis=0)
        o_intra.append(_mm(jnp.where(cst[rev][1], sc, 0.0), vstk, _NN))
    yield
    out = []
    for (q, k, v, la, rev), cum, oi in zip(items, cums, o_intra):
        tot = cum[0:1] if rev else cum[F - 1:F]
        upd = _mm(v, k * jnp.exp(tot - cum), _TN)
        out.append((oi, q * jnp.exp(cum), jnp.where(blk, upd, 0.0), jnp.exp(tot)))
    yield
    return out


def _gla_wide_consts():
    F = GLA_WIDE_CHUNK
    row = lax.broadcasted_iota(jnp.int32, (F, F), 0)
    col = lax.broadcasted_iota(jnp.int32, (F, F), 1)
    row2 = lax.broadcasted_iota(jnp.int32, (F, 2 * F), 0)
    cs = lax.broadcasted_iota(jnp.int32, (F, 2 * F), 1) % F
    m0 = lax.broadcasted_iota(jnp.int32, (1, LANES), 1) < GLA_KEY_DIM
    vm0 = lax.broadcasted_iota(jnp.int32, (1, 2 * LANES), 1) < GLA_VAL_DIM
    br = lax.broadcasted_iota(jnp.int32, (2 * LANES, LANES), 0)
    bc = lax.broadcasted_iota(jnp.int32, (2 * LANES, LANES), 1)
    blk = (br // GLA_VAL_DIM) == (bc // GLA_KEY_DIM)
    return (((col <= row).astype(BF16), cs <= row2), ((col >= row).astype(BF16), cs >= row2), (m0, vm0, blk))


def _scan_kernel(raf_ref, rdf_ref, rab_ref, rdb_ref, gaf_ref, glf_ref, gab_ref, glb_ref, *refs,
                 has_init, emit_states):
    refs = list(refs)
    init_refs = [refs.pop(0) for _ in range(4)] if has_init else None
    yof_ref, yob_ref = refs.pop(0), refs.pop(0)
    final_refs = [refs.pop(0) for _ in range(4)] if emit_states else None
    srf_ref, srb_ref, sgf_ref, sgb_ref = refs
    i = pl.program_id(1)
    lane = lax.broadcasted_iota(jnp.int32, (1, LANES), 1)
    first_half = lane < LANES // 2
    rr = lax.broadcasted_iota(jnp.int32, (LANES, LANES), 0)
    cc = lax.broadcasted_iota(jnp.int32, (LANES, LANES), 1)
    diag_blocks = (rr // RWKV_HEAD_DIM) == (cc // RWKV_HEAD_DIM)

    @pl.when(i == 0)
    def _():
        if not has_init:
            for s_ref in (srf_ref, srb_ref, sgf_ref, sgb_ref):
                s_ref[...] = jnp.zeros(s_ref.shape, F32)
            return
        for s_ref, init in ((srf_ref, init_refs[0]), (srb_ref, init_refs[1])):
            for p in range(RWKV_WIDTH // LANES):
                x = init[0, p]
                s_ref[0, p] = jnp.where(diag_blocks, jnp.concatenate([x, x], axis=1), 0.0)
        for s_ref, init in ((sgf_ref, init_refs[2]), (sgb_ref, init_refs[3])):
            for p in range(GLA_QK_WIDTH // LANES):
                xt = init[0, p].T
                s_ref[0, p] = jnp.concatenate([jnp.where(first_half, xt, 0.0),
                                               jnp.where(first_half, 0.0, xt)], axis=0)

    T = SCAN_SUB
    C = RWKV_CHUNK
    nr = T // C
    nsub = SCAN_BLOCK // T
    R = RWKV_WIDTH
    Q = GLA_QK_WIDTH
    npair = Q // LANES

    def sub_block(h, carry):
        base = (0, 0) if nsub == 1 else (pl.multiple_of(h * T, T), pl.multiple_of((nsub - 1 - h) * T, T))

        def rows_of(d, start, size):
            return pl.ds(base[d] + start, size)

        rcst = _rwkv_consts()
        g_dirs = ((False, gaf_ref, glf_ref, 0, sgf_ref, yof_ref),
                  (True, gab_ref, glb_ref, Q, sgb_ref, yob_ref))
        la_tot = jnp.minimum(jnp.sum(glf_ref[0, rows_of(0, 0, T), 0:Q], axis=0, keepdims=True),
                             jnp.sum(glb_ref[0, rows_of(1, 0, T), Q:2 * Q], axis=0, keepdims=True))
        gla_split_ok = jnp.min(la_tot) >= -GLA_SAFE_LOG

        r_dirs = ((False, raf_ref, rdf_ref, srf_ref, yof_ref),
                  (True, rab_ref, rdb_ref, srb_ref, yob_ref))
        r_units = {}
        r_dests = {}
        for d, (rev, ra_ref, rd_ref, s_ref, y_ref) in enumerate(r_dirs):
            for j in range(nr):
                c = nr - 1 - j if rev else j
                rows = rows_of(d, c * C, C)
                r_units[d, j] = []
                r_dests[d, j] = []
                for p in range(R // LANES):
                    sl, sl1, sl2 = (slice(o + p * LANES, o + (p + 1) * LANES) for o in (0, R, 2 * R))
                    r_units[d, j].append((ra_ref[0, rows, sl], rd_ref[0, rows, sl], ra_ref[0, rows, sl1],
                                          ra_ref[0, rows, sl2], rd_ref[0, rows, sl1], rd_ref[0, rows, sl2], rev))
                    r_dests[d, j].append((y_ref, s_ref, p, rows, sl))
        r_states = {(d, 0): [s_ref[0, p] for (_, s_ref, p, _, _) in r_dests[d, 0]] for d in range(len(r_dirs))}

        def rwkv_task(d, j):
            prep = yield from _rwkv_prepare(r_units[d, j], rcst)
            while (d, j) not in r_states:
                yield
            ys, r_states[d, j + 1] = yield from _rwkv_apply(prep, r_states[d, j], rcst)
            for (y_ref, _, _, rows, sl), y in zip(r_dests[d, j], ys):
                y_ref[0, rows, sl] = y

        g_init = [[s_ref[0, p] for p in range(npair)] for (_, _, _, _, s_ref, _) in g_dirs]

        def gla_task(chunk, intra, cst):
            ng = T // chunk
            items = [[] for _ in range(ng)]
            dests = [[] for _ in range(ng)]
            for d, (rev, ga_ref, gl_ref, la_off, s_ref, o_ref) in enumerate(g_dirs):
                for p in range(npair):
                    qs, ks, las = (slice(o + p * LANES, o + (p + 1) * LANES) for o in (0, Q, la_off))
                    vs = slice(2 * Q + p * 2 * LANES, 2 * Q + (p + 1) * 2 * LANES)
                    os_ = slice(R + p * 2 * LANES, R + (p + 1) * 2 * LANES)
                    for j in range(ng):
                        c = ng - 1 - j if rev else j
                        rows = rows_of(d, c * chunk, chunk)
                        items[j].append((ga_ref[0, rows, qs], ga_ref[0, rows, ks], ga_ref[0, rows, vs],
                                         gl_ref[0, rows, las], rev))
                        dests[j].append((o_ref, s_ref, d, p, rows, os_))
            parts = yield from intra([it for its in items for it in its], cst)
            nu = len(items[0])
            sts = [g_init[d][p] for (_, _, d, p, _, _) in dests[0]]
            for j in range(ng):
                new = []
                for (o_ref, _, _, _, rows, vs), (o_intra, q_in, upd, e_tot), st in zip(
                        dests[j], parts[j * nu:(j + 1) * nu], sts):
                    o_ref[0, rows, vs] = o_intra + _mm(q_in, st, _NT)
                    new.append(st * e_tot + upd)
                sts = new
                yield
            for (_, s_ref, _, p, _, _), st in zip(dests[0], sts):
                s_ref[0, p] = st

        nd = len(r_dirs)
        _interleave([(RWKV_TASK_LAG * j, rwkv_task(d, j)) for j in range(nr) for d in range(nd)]
                    + [(GLA_TASK_LAG, gla_task(GLA_WIDE_CHUNK, _gla_factored, _gla_wide_consts()))])
        for d in range(nd):
            for (_, s_ref, p, _, _), s in zip(r_dests[d, 0], r_states[d, nr]):
                s_ref[0, p] = s

        @pl.when(jnp.logical_not(gla_split_ok))
        def _():
            _interleave([(0, gla_task(GLA_CHUNK, _gla_intra, _gla_consts()))])

        return carry

    if nsub == 1:
        sub_block(0, 0)
    else:
        lax.fori_loop(0, nsub, sub_block, 0)

    if emit_states:
        @pl.when(i == pl.num_programs(1) - 1)
        def _():
            for s_ref, out in ((srf_ref, final_refs[0]), (srb_ref, final_refs[1])):
                for p in range(RWKV_WIDTH // LANES):
                    s = s_ref[0, p]
                    out[0, p] = (s + pltpu.roll(s, LANES // 2, 1))[:, 0:LANES // 2]
            for s_ref, out in ((sgf_ref, final_refs[2]), (sgb_ref, final_refs[3])):
                for p in range(GLA_QK_WIDTH // LANES):
                    st = s_ref[0, p]
                    out[0, p] = st[0:LANES].T + st[LANES:2 * LANES].T


def _scan(ra, rd_f, rd_b, ga, gl, states, emit_states):
    B, L, _ = ra.shape
    R = RWKV_WIDTH
    Q = GLA_QK_WIDTH
    V = GLA_V_WIDTH
    T = SCAN_BLOCK
    n = L // T

    def fwd(w):
        return pl.BlockSpec((1, T, w), lambda b, i: (b, i, 0))

    def bwd(w):
        return pl.BlockSpec((1, T, w), lambda b, i: (b, n - 1 - i, 0))

    rshape = (R // LANES, LANES, LANES // 2)
    gshape = (Q // LANES, LANES, LANES)
    rst = pl.BlockSpec((1,) + rshape, lambda b, i: (b, 0, 0, 0))
    gst = pl.BlockSpec((1,) + gshape, lambda b, i: (b, 0, 0, 0))
    st_specs = [rst, rst, gst, gst]
    st_shapes = [jax.ShapeDtypeStruct((B,) + s, F32) for s in (rshape, rshape, gshape, gshape)]
    states = () if states is None else tuple(states)
    return pl.pallas_call(
        functools.partial(_scan_kernel, has_init=bool(states), emit_states=emit_states),
        grid=(B, n),
        in_specs=[fwd(3 * R), fwd(3 * R), bwd(3 * R), bwd(3 * R),
                  fwd(2 * Q + V), fwd(2 * Q), bwd(2 * Q + V), bwd(2 * Q)] + st_specs[:len(states)],
        out_specs=[fwd(R + V), bwd(R + V)] + (st_specs if emit_states else []),
        out_shape=[jax.ShapeDtypeStruct((B, L, R + V), F32)] * 2 + (st_shapes if emit_states else []),
        scratch_shapes=[pltpu.VMEM((1, R // LANES, LANES, LANES), F32)] * 2
        + [pltpu.VMEM((1, Q // LANES, 2 * LANES, LANES), F32)] * 2,
        compiler_params=pltpu.CompilerParams(dimension_semantics=("parallel", "arbitrary"),
                                             vmem_limit_bytes=VMEM_LIMIT),
        name="scan",
    )(ra, rd_f, ra, rd_b, ga, gl, ga, gl, *states)


def _post_kernel(x_ref, mod_ref, yof_ref, yob_ref, pg_ref,
                 lng_ref, lnb_ref, gng_ref, n2g_ref, fng_ref, wout_ref, w1_ref, w2_ref, o_ref):
    D = D_MODEL
    R = RWKV_WIDTH
    m = mod_ref[0]
    gt1 = m[:, 2 * D:3 * D]
    sh2 = m[:, 3 * D:4 * D]
    sc2 = m[:, 4 * D:5 * D]
    gt2 = m[:, 5 * D:6 * D]
    inv_n = 1.0 / RWKV_HEAD_DIM
    gng = gng_ref[...]
    nseq, tb = x_ref.shape[0], x_ref.shape[1]
    if nseq > 1:
        rows = [(s, slice(0, tb)) for s in range(nseq)]
    else:
        nsplit = POST_SPLIT if tb % (8 * POST_SPLIT) == 0 else 1
        rows = [(0, slice(s * (tb // nsplit), (s + 1) * (tb // nsplit))) for s in range(nsplit)]

    def mixed(q, rs):
        y = yof_ref[q, rs, 0:R] + yob_ref[q, rs, 0:R]
        mu = _head_sum(y) * inv_n
        yc = y - mu
        var = _head_sum(yc * yc) * inv_n
        yn = yc * lax.rsqrt(var + LNX_EPS) * lng_ref[...] + lnb_ref[...]
        parts = [(yn + pg_ref[q, rs, R:2 * R]) * pg_ref[q, rs, 0:R]]
        o = yof_ref[q, rs, R:] + yob_ref[q, rs, R:]
        for hh in range(GLA_HEADS):
            sl = slice(hh * GLA_VAL_DIM, (hh + 1) * GLA_VAL_DIM)
            oh = o[:, sl]
            oh = oh * lax.rsqrt(jnp.mean(oh * oh, axis=-1, keepdims=True) + GLA_NORM_EPS)
            parts.append(oh * gng * pg_ref[q, rs, 2 * R + hh * GLA_VAL_DIM:2 * R + (hh + 1) * GLA_VAL_DIM])
        return jnp.concatenate(parts, axis=-1)

    mix = [mixed(q, rs) for q, rs in rows]
    x1 = [x_ref[q, rs] + gt1 * _bdot(mx, wout_ref) for (q, rs), mx in zip(rows, mix)]
    h2 = [_rms(v) * n2g_ref[...] * (1.0 + sc2) + sh2 for v in x1]
    f = [jnp.maximum(_bdot(v, w1_ref), 0.0) for v in h2]
    x2 = [a + gt2 * _bdot(v * v, w2_ref) for a, v in zip(x1, f)]
    for (q, rs), v in zip(rows, x2):
        o_ref[q, rs] = _rms(v) * fng_ref[...]


def _post(x, mod3, mod_row, yo_f, yo_b, pg, vecs, mats, shared_mod):
    B, L, D = x.shape
    tb = min(POST_BLOCK, L)
    nblk = L // tb
    nseq = POST_BLOCK // L if (shared_mod and L < POST_BLOCK and B % (POST_BLOCK // L) == 0) else 1

    def tok(w):
        return pl.BlockSpec((nseq, tb, w), lambda b, i: (b, i, 0))

    in_specs = [tok(D), pl.BlockSpec((1, 1, N_MOD * D), lambda b, i: (mod_row(b * nseq), 0, 0))]
    in_specs += [tok(yo_f.shape[-1]), tok(yo_b.shape[-1]), tok(pg.shape[-1])]
    in_specs += [_const_spec(w.shape) for w in vecs]
    in_specs += [pl.BlockSpec(w.shape, lambda b, i: (0, 0), pipeline_mode=pl.Buffered(1)) for w in mats]
    return pl.pallas_call(
        _post_kernel,
        grid=(B // nseq, nblk),
        in_specs=in_specs,
        out_specs=tok(D),
        out_shape=jax.ShapeDtypeStruct((B, L, D), F32),
        compiler_params=pltpu.CompilerParams(dimension_semantics=("parallel", "arbitrary"),
                                             vmem_limit_bytes=VMEM_LIMIT),
        name="post",
    )(x, mod3, yo_f, yo_b, pg, *vecs, *mats)


def _blockdiag2(a, b):
    za = jnp.zeros((a.shape[0], b.shape[1]), a.dtype)
    zb = jnp.zeros((b.shape[0], a.shape[1]), a.dtype)
    return jnp.concatenate([jnp.concatenate([a, za], axis=1), jnp.concatenate([zb, b], axis=1)], axis=0)


def kernel(x_prompt, x_sample, c, state_rwkv_fwd, state_rwkv_bwd, state_gla_fwd, state_gla_bwd, c_ctx, ada_w, ada_b, norm1_g, norm2_g, w_in, rwkv_mu_rkv, rwkv_mu_wag, rwkv_w0, rwkv_w1, rwkv_w2, rwkv_a0, rwkv_a1, rwkv_a2, rwkv_g1, rwkv_g2, rwkv_k_k, rwkv_k_a, rwkv_r_k, rwkv_lnx_g, rwkv_lnx_b, gla_gk1, gla_gk2, gla_gk_b, gla_norm_g, w_out, mlp_w1, mlp_w2, final_norm_g):
    D = D_MODEL
    R = RWKV_WIDTH
    nb = x_prompt.shape[0]
    nd = x_sample.shape[0]
    assert ada_w.shape[0] == 1, "single-layer step"
    layer = 0

    craw = jnp.concatenate([c_ctx[None, :], c, jnp.zeros((8 - 1 - nd, D), F32)], axis=0)
    mod = _modulation(craw, ada_w[layer], ada_b[layer][None, :])
    mod3 = mod.reshape(8, 1, N_MOD * D)

    bf = lambda t: t.astype(BF16)
    row = lambda t: t.reshape(1, -1).astype(F32)
    w_in_bf = bf(w_in[layer])
    half = (D, 3 * R)
    pre_w = [
        row(norm1_g[layer]),
        (w_in_bf, pl.BlockSpec(half, lambda b, i: (0, 0))),
        (w_in_bf, pl.BlockSpec(half, lambda b, i: (0, 1))),
        bf(jnp.concatenate([gla_gk1[layer, 0], gla_gk1[layer, 1]], axis=1)),
        bf(jnp.concatenate([rwkv_w1[layer, 0], rwkv_w1[layer, 1]], axis=1)),
        bf(jnp.concatenate([rwkv_a1[layer, 0], rwkv_a1[layer, 1]], axis=1)),
        bf(rwkv_g1[layer]),
        bf(_blockdiag2(rwkv_w2[layer, 0], rwkv_w2[layer, 1])),
        bf(_blockdiag2(rwkv_a2[layer, 0], rwkv_a2[layer, 1])),
        bf(rwkv_g2[layer]),
        bf(_blockdiag2(gla_gk2[layer, 0], gla_gk2[layer, 1])),
        row(rwkv_mu_rkv[layer]),
        rwkv_mu_wag[layer],
        row(rwkv_w0[layer]),
        row(rwkv_a0[layer]),
        row(gla_gk_b[layer]),
        row(rwkv_k_k[layer]),
        row(rwkv_k_a[layer]),
        row(rwkv_r_k[layer]),
    ]
    post_vecs = [row(rwkv_lnx_g[layer]), row(rwkv_lnx_b[layer]), row(gla_norm_g[layer]),
                 row(norm2_g[layer]), row(final_norm_g)]
    post_mats = [bf(w_out[layer]), bf(mlp_w1[layer]), bf(mlp_w2[layer])]

    def run_group(x, mod_row, grid_shift, states, emit_states, shared_mod):
        ra, rd_f, rd_b, ga, gl, pg = _pre(x, mod3, mod_row, pre_w, grid_shift)
        yo_f, yo_b, *finals = _scan(ra, rd_f, rd_b, ga, gl, states, emit_states)
        y = _post(x, mod3, mod_row, yo_f, yo_b, pg, post_vecs, post_mats, shared_mod)
        return y, finals

    rpair = lambda s: s.reshape(nd, R // LANES, LANES, RWKV_HEAD_DIM)
    gpair = lambda s: s.reshape(nd, GLA_QK_WIDTH // LANES, LANES, GLA_VAL_DIM)
    y_prompt, (n_rf, n_rb, n_gf, n_gb) = run_group(x_prompt, lambda b: 0, False, None, True, True)
    y_sample, _ = run_group(x_sample, lambda b: b + 1, True,
                            (rpair(state_rwkv_fwd[:, layer]), rpair(state_rwkv_bwd[:, layer]),
                             gpair(state_gla_fwd[:, layer]), gpair(state_gla_bwd[:, layer])), False, False)
    rshape = (nb, 1, RWKV_HEADS, RWKV_HEAD_DIM, RWKV_HEAD_DIM)
    gshape = (nb, 1, GLA_HEADS, GLA_KEY_DIM, GLA_VAL_DIM)
    return (y_prompt, y_sample, n_rf.reshape(rshape), n_rb.reshape(rshape),
            n_gf.reshape(gshape), n_gb.reshape(gshape))
```

```python
import functools

import jax
import jax.numpy as jnp
from jax import lax
from jax.experimental import pallas as pl
from jax.experimental.pallas import tpu as pltpu

F32 = jnp.float32
BF16 = jnp.bfloat16

D_MODEL = 1024
GRID_W = 64
RWKV_WIDTH = 512
RWKV_HEAD_DIM = 64
RWKV_HEADS = 8
GLA_HEADS = 4
GLA_KEY_DIM = 64
GLA_VAL_DIM = 128
GLA_QK_WIDTH = 256
GLA_V_WIDTH = 512
GLA_GATE_NORMALIZER = 16.0
N_MOD = 6
RMS_EPS = 1e-6
LNX_EPS = 64e-5
GLA_NORM_EPS = 1e-5

LANES = 128
RWKV_CHUNK = 64
GLA_CHUNK = 16
GLA_WIDE_CHUNK = 128
GLA_SAFE_LOG = 60.0
SCAN_BLOCK = 256
SCAN_SUB = 128
PRE_BLOCK = 512
POST_BLOCK = 512
POST_SPLIT = 2
VMEM_LIMIT = 56 * 1024 * 1024
RWKV_TASK_LAG = 2
GLA_TASK_LAG = 14
DECAY_LOG_SCALE = 0.6065306597126334

_NN = (((1,), (0,)), ((), ()))
_NT = (((1,), (1,)), ((), ()))
_TN = (((0,), (0,)), ((), ()))


def _dot(a, b):
    return lax.dot_general(a, b, _NN, preferred_element_type=F32)


def _mm(a, b, dims):
    return lax.dot_general(a.astype(BF16), b.astype(BF16), dims, preferred_element_type=F32)


def _cumsum_mm(tri, x):
    hi = x.astype(BF16)
    r1 = x - hi.astype(F32)
    mid = r1.astype(BF16)
    lo = (r1 - mid.astype(F32)).astype(BF16)
    return _dot(tri, hi) + _dot(tri, mid) + _dot(tri, lo)


def _bdot(a, w_ref):
    return _dot(a.astype(BF16), w_ref[...])


def _sigmoid(x):
    return 1.0 / (1.0 + jnp.exp(-x))


def _softplus(x):
    return jnp.maximum(x, 0.0) + jnp.log(1.0 + jnp.exp(-jnp.abs(x)))


def _head_sum(x):
    first = lax.broadcasted_iota(jnp.int32, (1, LANES), 1) < RWKV_HEAD_DIM
    outs = []
    for j in range(x.shape[-1] // LANES):
        xb = x[:, j * LANES:(j + 1) * LANES]
        s0 = jnp.sum(jnp.where(first, xb, 0.0), axis=-1, keepdims=True)
        s1 = jnp.sum(jnp.where(first, 0.0, xb), axis=-1, keepdims=True)
        outs.append(jnp.where(first, s0, s1))
    return jnp.concatenate(outs, axis=-1)


def _rms(x):
    return x * lax.rsqrt(jnp.mean(x * x, axis=-1, keepdims=True) + RMS_EPS)


def _mod_kernel(c_ref, w_ref, b_ref, o_ref):
    c = c_ref[...]
    cond = c * _sigmoid(c)
    o_ref[...] = _dot(cond.astype(BF16), w_ref[...].astype(BF16)) + b_ref[...]


def _modulation(craw, ada_w, ada_b):
    n = ada_w.shape[1]
    bn = 1536
    return pl.pallas_call(
        _mod_kernel,
        grid=(n // bn,),
        in_specs=[
            pl.BlockSpec((8, D_MODEL), lambda j: (0, 0)),
            pl.BlockSpec((D_MODEL, bn), lambda j: (0, j)),
            pl.BlockSpec((1, bn), lambda j: (0, j)),
        ],
        out_specs=pl.BlockSpec((8, bn), lambda j: (0, j)),
        out_shape=jax.ShapeDtypeStruct((8, n), F32),
        compiler_params=pltpu.CompilerParams(dimension_semantics=("arbitrary",), vmem_limit_bytes=VMEM_LIMIT),
        name="modulation",
    )(craw, ada_w, ada_b)


def _pre_kernel(*refs, grid_shift, tb):
    if grid_shift:
        x_ref, xp_ref, xn_ref = refs[:3]
        refs = refs[3:]
    else:
        x_ref = refs[0]
        refs = refs[1:]
    (mod_ref, n1g_ref, wrkv_ref, wrest_ref, gk1_ref, w1_ref, a1_ref, g1_ref, w2_ref, a2_ref, g2_ref,
     gk2_ref, murkv_ref, muwag_ref, w0_ref, a0_ref, gkb_ref, kk_ref, ka_ref, rk_ref) = refs[:20]
    ra_o, rf_o, rb_o, ga_o, gl_o, pg_o = refs[20:]
    D = D_MODEL
    R = RWKV_WIDTH
    m = mod_ref[0]
    sh1 = m[:, 0:D]
    sc1 = m[:, D:2 * D]
    n1g = n1g_ref[...]

    def normmod(xx):
        return _rms(xx) * n1g * (1.0 + sc1) + sh1

    h = normmod(x_ref[0])
    row = lax.broadcasted_iota(jnp.int32, (tb, 1), 0)
    if grid_shift:
        i = pl.program_id(1)
        n = pl.num_programs(1)
        hp = normmod(xp_ref[0]) * (i > 0).astype(F32)
        hn = normmod(xn_ref[0]) * (i < n - 1).astype(F32)
        hext = jnp.concatenate([hp, h, hn], axis=0)
        col = row % GRID_W
        m_l = (col != 0).astype(F32)
        m_r = (col != GRID_W - 1).astype(F32)
        ne = tb + 2 * GRID_W

        def shift(ext):
            up = ext[0:tb]
            down = ext[2 * GRID_W:2 * GRID_W + tb]
            left = pltpu.roll(ext, 1, 0)[GRID_W:GRID_W + tb]
            right = pltpu.roll(ext, ne - 1, 0)[GRID_W:GRID_W + tb]
            return 0.25 * (up + down + m_l * left + m_r * right)

        halo = GRID_W
    else:
        hext = h
        m_l = (row != 0).astype(F32)
        m_r = (row != tb - 1).astype(F32)

        def shift(ext):
            return 0.5 * (m_l * pltpu.roll(ext, 1, 0) + m_r * pltpu.roll(ext, tb - 1, 0))

        halo = 0

    rest = _bdot(h, wrest_ref)
    dh = shift(hext) - h
    mu = muwag_ref[...]
    lora_w = _bdot(h + mu[0:1] * dh, w1_ref)
    lora_a = _bdot(h + mu[1:2] * dh, a1_ref)
    lora_g = _bdot(h + mu[2:3] * dh, g1_ref)
    lora_gk = _bdot(h, gk1_ref)
    z = w0_ref[...] + _bdot(jnp.tanh(lora_w), w2_ref)
    a = _sigmoid(a0_ref[...] + _bdot(lora_a, a2_ref))
    gate = _bdot(_sigmoid(lora_g), g2_ref)
    logits = _bdot(lora_gk, gk2_ref) + gkb_ref[...]
    rkv_ext = _bdot(hext, wrkv_ref)

    Q = GLA_QK_WIDTH
    ga_o[0, :, 0:Q] = rest[:, 0:Q] * (GLA_KEY_DIM ** -0.5)
    ga_o[0, :, Q:] = rest[:, Q:2 * Q + GLA_V_WIDTH]
    gg = rest[:, 2 * Q + GLA_V_WIDTH:]
    pg_o[0, :, 2 * R:] = gg * _sigmoid(gg)
    gl_o[0] = -_softplus(-logits) * (1.0 / GLA_GATE_NORMALIZER)

    rkv = rkv_ext[halo:halo + tb]
    rkv = rkv + murkv_ref[...] * (shift(rkv_ext) - rkv)
    r = rkv[:, 0:R]
    k = rkv[:, R:2 * R]
    v = rkv[:, 2 * R:3 * R]
    lw = -DECAY_LOG_SCALE * _sigmoid(z)

    kap = k * kk_ref[...]
    kap = kap * lax.rsqrt(jnp.maximum(_head_sum(kap * kap), 1e-12))
    ka = ka_ref[...]
    a_f = a[:, 0:R]
    a_b = a[:, R:2 * R]
    kd_f = k * (1.0 + (a_f - 1.0) * ka)
    kd_b = k * (1.0 + (a_b - 1.0) * ka)
    bonus = _head_sum(r * (kd_f + kd_b) * rk_ref[...]) * v

    ra_o[0, :, 0:R] = r
    ra_o[0, :, R:2 * R] = v
    ra_o[0, :, 2 * R:3 * R] = kap
    rf_o[0, :, 0:R] = kd_f
    rf_o[0, :, R:2 * R] = a_f * kap
    rf_o[0, :, 2 * R:3 * R] = lw[:, 0:R]
    rb_o[0, :, 0:R] = kd_b
    rb_o[0, :, R:2 * R] = a_b * kap
    rb_o[0, :, 2 * R:3 * R] = lw[:, R:2 * R]
    pg_o[0, :, 0:R] = gate
    pg_o[0, :, R:2 * R] = bonus


def _const_spec(shape):
    nd = len(shape)
    return pl.BlockSpec(shape, lambda b, i: (0,) * nd)


def _pre(x, mod3, mod_row, weights, grid_shift):
    B, L, D = x.shape
    tb = min(PRE_BLOCK, L)
    nblk = L // tb
    if not grid_shift:
        assert nblk == 1
    x_spec = pl.BlockSpec((1, tb, D), lambda b, i: (b, i, 0))
    in_specs = [x_spec]
    args = [x]
    if grid_shift:
        per = tb // GRID_W
        nrow = L // GRID_W
        in_specs += [
            pl.BlockSpec((1, GRID_W, D), lambda b, i: (b, jnp.maximum(i * per - 1, 0), 0)),
            pl.BlockSpec((1, GRID_W, D), lambda b, i: (b, jnp.minimum((i + 1) * per, nrow - 1), 0)),
        ]
        args += [x, x]
    in_specs.append(pl.BlockSpec((1, 1, N_MOD * D), lambda b, i: (mod_row(b), 0, 0)))
    args.append(mod3)
    for w in weights:
        w, spec = w if isinstance(w, tuple) else (w, _const_spec(w.shape))
        in_specs.append(spec)
        args.append(w)
    R = RWKV_WIDTH
    widths = [3 * R, 3 * R, 3 * R, 2 * GLA_QK_WIDTH + GLA_V_WIDTH, 2 * GLA_QK_WIDTH, 2 * R + GLA_V_WIDTH]
    out_specs = [pl.BlockSpec((1, tb, w), lambda b, i: (b, i, 0)) for w in widths]
    out_shape = [jax.ShapeDtypeStruct((B, L, w), F32) for w in widths]
    return pl.pallas_call(
        functools.partial(_pre_kernel, grid_shift=grid_shift, tb=tb),
        grid=(B, nblk),
        in_specs=in_specs,
        out_specs=out_specs,
        out_shape=out_shape,
        compiler_params=pltpu.CompilerParams(dimension_semantics=("parallel", "arbitrary"),
                                             vmem_limit_bytes=VMEM_LIMIT),
        name="pre_grid" if grid_shift else "pre_seq",
    )(*args)


def _interleave(tasks):
    live = list(tasks)
    rnd = 0
    while live:
        keep = []
        for first, gen in live:
            if rnd >= first:
                try:
                    next(gen)
                except StopIteration:
                    continue
            keep.append((first, gen))
        live = keep
        rnd += 1


def _inv_unit_triangular(lms, eye, blk16, blk32, blockdiag):
    def mm(a, b):
        return _mm(a, blockdiag(b), _NN)

    n = lms[0].shape[0]
    l0 = [jnp.where(blk16, lm, 0.0) for lm in lms]
    l2 = [mm(a, a) for a in l0]
    yield
    t = [eye - a for a in l0]
    s = [mm(jnp.concatenate([a, b], axis=0), b) for a, b in zip(t, l2)]
    yield
    t = [a + x[0:n] for a, x in zip(t, s)]
    l4 = [x[n:2 * n] for x in s]
    s = [mm(jnp.concatenate([a, b], axis=0), b) for a, b in zip(t, l4)]
    yield
    t = [a + x[0:n] for a, x in zip(t, s)]
    l8 = [x[n:2 * n] for x in s]
    t = [a + mm(a, b) for a, b in zip(t, l8)]
    yield
    off1_mask = jnp.logical_and(blk32, jnp.logical_not(blk16))
    x = [mm(a, jnp.where(off1_mask, lm, 0.0)) for a, lm in zip(t, lms)]
    yield
    t = [a - mm(b, a) for a, b in zip(t, x)]
    yield
    x = [mm(a, jnp.where(blk32, 0.0, lm)) for a, lm in zip(t, lms)]
    yield
    t = [a - mm(b, a) for a, b in zip(t, x)]
    yield
    return t


def _rwkv_prepare(units, cst):
    C = RWKV_CHUNK
    m0, cm0, eye, blk16, blk32, _ = cst[2]

    def stack(x):
        return jnp.concatenate([jnp.where(m0, x, 0.0), jnp.where(m0, 0.0, x)], axis=0)

    def blockdiag(side):
        return jnp.concatenate([jnp.where(cm0, side, 0.0), jnp.where(cm0, 0.0, side)], axis=0)

    cums = [_cumsum_mm(cst[rev][0], lw) for (_, _, _, _, _, lw, rev) in units]
    yield
    prep = []
    for (r, k, v, kap, b, lw, rev), cum in zip(units, cums):
        cumx = cum - lw
        if rev:
            mid = cum[C // 2:C // 2 + 1]
            tot = cum[0:1]
        else:
            mid = cum[C // 2 - 1:C // 2]
            tot = cum[C - 1:C]
        e_mid = jnp.exp(mid)
        rt = r * jnp.exp(cum - mid)
        kt = kap * jnp.exp(cumx - mid)
        es = jnp.exp(mid - cum)
        e_end = jnp.exp(tot - mid)
        prep.append(dict(rt=rt, kt=kt, bh=b * es, kh=k * es, r0=rt * e_mid, k0=stack(kt * e_mid),
                         be=b * es * e_end, ke=k * es * e_end, vs=stack(v), v=v,
                         e_tot=jnp.exp(tot), strict2=cst[rev][1], incl2=cst[rev][2]))
    gs = [_mm(jnp.concatenate([p["kt"], p["rt"]], axis=0),
              jnp.concatenate([stack(p["bh"]), stack(p["kh"])], axis=0), _NT) for p in prep]
    yield
    for p, g in zip(prep, gs):
        p["ab"] = jnp.where(p["strict2"], g[0:C, 0:2 * C], 0.0)
        p["ak"] = jnp.where(p["strict2"], g[0:C, 2 * C:4 * C], 0.0)
        p["rb"] = jnp.where(p["incl2"], g[C:2 * C, 0:2 * C], 0.0)
        p["rk"] = jnp.where(p["incl2"], g[C:2 * C, 2 * C:4 * C], 0.0)
    akv = [_mm(p["ak"], p["vs"], _NN) for p in prep]
    yield
    ts = yield from _inv_unit_triangular([p["ab"] for p in prep], eye, blk16, blk32, blockdiag)
    wu = [_mm(t, jnp.concatenate([p["k0"], stack(a)], axis=1), _NN) for t, p, a in zip(ts, prep, akv)]
    yield
    return [dict(wr=jnp.concatenate([x[:, :LANES], p["r0"]], axis=0), u0=x[:, LANES:],
                 rbk=jnp.concatenate([p["rb"], p["rk"]], axis=1), vs=p["vs"], v=p["v"],
                 bke=jnp.concatenate([p["be"], p["ke"]], axis=0), e_tot=p["e_tot"])
            for x, p in zip(wu, prep)]


def _rwkv_apply(prep, states, cst):
    C = RWKV_CHUNK
    m0 = cst[2][0]
    blk64 = cst[2][5]
    ws = [_mm(p["wr"], s, _NT) for p, s in zip(prep, states)]
    yield
    us = [-p["u0"] - w[0:C] for p, w in zip(prep, ws)]
    upd = [_mm(jnp.concatenate([u, p["v"]], axis=0), p["bke"], _TN) for u, p in zip(us, prep)]
    yield
    s_new = [s * p["e_tot"] + jnp.where(blk64, d, 0.0) for p, s, d in zip(prep, states, upd)]
    ys = [w[C:2 * C] + _mm(p["rbk"], jnp.concatenate([jnp.where(m0, u, 0.0), jnp.where(m0, 0.0, u), p["vs"]],
                                                     axis=0), _NN)
          for p, w, u in zip(prep, ws, us)]
    yield
    return ys, s_new


def _rwkv_consts():
    C = RWKV_CHUNK
    row = lax.broadcasted_iota(jnp.int32, (C, C), 0)
    col = lax.broadcasted_iota(jnp.int32, (C, C), 1)
    row2 = lax.broadcasted_iota(jnp.int32, (C, 2 * C), 0)
    col2 = lax.broadcasted_iota(jnp.int32, (C, 2 * C), 1)
    cs = col2 % C
    lane = lax.broadcasted_iota(jnp.int32, (1, LANES), 1)
    m0 = lane < RWKV_HEAD_DIM
    cm0 = lax.broadcasted_iota(jnp.int32, (1, 2 * C), 1) < C
    rr = lax.broadcasted_iota(jnp.int32, (2 * C, 2 * C), 0)
    cc = lax.broadcasted_iota(jnp.int32, (2 * C, 2 * C), 1)
    eye = (row2 == cs).astype(F32)
    blk16 = (row2 // 16) == (cs // 16)
    blk32 = (row2 // 32) == (cs // 32)
    blk64 = (rr // 64) == (cc // 64)
    fwd = ((col <= row).astype(BF16), cs < row2, cs <= row2)
    bwd = ((col >= row).astype(BF16), cs > row2, cs >= row2)
    return fwd, bwd, (m0, cm0, eye, blk16, blk32, blk64)


def _gla_intra(items, cst):
    G = GLA_CHUNK
    ind_v, sel, blk = cst[2]
    srow = lax.broadcasted_iota(jnp.int32, (G, 1), 0)
    cums = [_cumsum_mm(cst[rev], la) for (_, _, _, la, rev) in items]
    pmats = []
    for (q, k, v, la, rev), cum in zip(items, cums):
        ps = []
        for t in range(G):
            msk = (srow >= t) if rev else (srow <= t)
            e = jnp.exp(jnp.where(msk, cum[t:t + 1] - cum, 0.0))
            ps.append(jnp.where(msk, e * (q[t:t + 1] * k), 0.0))
        pmats.append(jnp.concatenate(ps, axis=0))
    atts = [_mm(pm, ind_v, _NN) for pm in pmats]
    o_intra = [_mm(sel, att * jnp.concatenate([it[2]] * G, axis=0), _NN)
               for att, it in zip(atts, items)]
    out = []
    for (q, k, v, la, rev), cum, oi in zip(items, cums, o_intra):
        tot = cum[0:1] if rev else cum[G - 1:G]
        upd = _mm(v, k * jnp.exp(tot - cum), _TN)
        out.append((oi, q * jnp.exp(cum), jnp.where(blk, upd, 0.0), jnp.exp(tot)))
    yield
    return out


def _gla_consts():
    G = GLA_CHUNK
    row = lax.broadcasted_iota(jnp.int32, (G, G), 0)
    col = lax.broadcasted_iota(jnp.int32, (G, G), 1)
    kc = lax.broadcasted_iota(jnp.int32, (LANES, 2 * LANES), 0)
    vc = lax.broadcasted_iota(jnp.int32, (LANES, 2 * LANES), 1)
    ind_v = ((kc // GLA_KEY_DIM) == (vc // GLA_VAL_DIM)).astype(F32)
    st = lax.broadcasted_iota(jnp.int32, (G, G * G), 0)
    sj = lax.broadcasted_iota(jnp.int32, (G, G * G), 1)
    sel = ((sj // G) == st).astype(F32)
    br = lax.broadcasted_iota(jnp.int32, (2 * LANES, LANES), 0)
    bc = lax.broadcasted_iota(jnp.int32, (2 * LANES, LANES), 1)
    blk = (br // GLA_VAL_DIM) == (bc // GLA_KEY_DIM)
    return (col <= row).astype(BF16), (col >= row).astype(BF16), (ind_v, sel, blk)


def _gla_factored(items, cst):
    F = GLA_WIDE_CHUNK
    m0, vm0, blk = cst[2]
    cums = [_cumsum_mm(cst[rev][0], la) for (_, _, _, la, rev) in items]
    yield
    scores = []
    for (q, k, v, la, rev), cum in zip(items, cums):
        mid = cum[F // 2:F // 2 + 1] if rev else cum[F // 2 - 1:F // 2]
        kt = k * jnp.exp(mid - cum)
        kstk = jnp.concatenate([jnp.where(m0, kt, 0.0), jnp.where(m0, 0.0, kt)], axis=0)
        scores.append(_mm(q * jnp.exp(cum - mid), kstk, _NT))
    yield
    o_intra = []
    for (q, k, v, la, rev), sc in zip(items, scores):
        vstk = jnp.concatenate([jnp.where(vm0, v, 0.0), jnp.where(vm0, 0.0, v)], axis=0)
        o_intra.append(_mm(jnp.where(cst[rev][1], sc, 0.0), vstk, _NN))
    yield
    out = []
    for (q, k, v, la, rev), cum, oi in zip(items, cums, o_intra):
        tot = cum[0:1] if rev else cum[F - 1:F]
        upd = _mm(v, k * jnp.exp(tot - cum), _TN)
        out.append((oi, q * jnp.exp(cum), jnp.where(blk, upd, 0.0), jnp.exp(tot)))
    yield
    return out


def _gla_wide_consts():
    F = GLA_WIDE_CHUNK
    row = lax.broadcasted_iota(jnp.int32, (F, F), 0)
    col = lax.broadcasted_iota(jnp.int32, (F, F), 1)
    row2 = lax.broadcasted_iota(jnp.int32, (F, 2 * F), 0)
    cs = lax.broadcasted_iota(jnp.int32, (F, 2 * F), 1) % F
    m0 = lax.broadcasted_iota(jnp.int32, (1, LANES), 1) < GLA_KEY_DIM
    vm0 = lax.broadcasted_iota(jnp.int32, (1, 2 * LANES), 1) < GLA_VAL_DIM
    br = lax.broadcasted_iota(jnp.int32, (2 * LANES, LANES), 0)
    bc = lax.broadcasted_iota(jnp.int32, (2 * LANES, LANES), 1)
    blk = (br // GLA_VAL_DIM) == (bc // GLA_KEY_DIM)
    return (((col <= row).astype(BF16), cs <= row2), ((col >= row).astype(BF16), cs >= row2), (m0, vm0, blk))


def _scan_kernel(raf_ref, rdf_ref, rab_ref, rdb_ref, gaf_ref, glf_ref, gab_ref, glb_ref, *refs,
                 has_init, emit_states):
    refs = list(refs)
    init_refs = [refs.pop(0) for _ in range(4)] if has_init else None
    yof_ref, yob_ref = refs.pop(0), refs.pop(0)
    final_refs = [refs.pop(0) for _ in range(4)] if emit_states else None
    srf_ref, srb_ref, sgf_ref, sgb_ref = refs
    i = pl.program_id(1)
    lane = lax.broadcasted_iota(jnp.int32, (1, LANES), 1)
    first_half = lane < LANES // 2
    rr = lax.broadcasted_iota(jnp.int32, (LANES, LANES), 0)
    cc = lax.broadcasted_iota(jnp.int32, (LANES, LANES), 1)
    diag_blocks = (rr // RWKV_HEAD_DIM) == (cc // RWKV_HEAD_DIM)

    @pl.when(i == 0)
    def _():
        if not has_init:
            for s_ref in (srf_ref, srb_ref, sgf_ref, sgb_ref):
                s_ref[...] = jnp.zeros(s_ref.shape, F32)
            return
        for s_ref, init in ((srf_ref, init_refs[0]), (srb_ref, init_refs[1])):
            for p in range(RWKV_WIDTH // LANES):
                x = init[0, p]
                s_ref[0, p] = jnp.where(diag_blocks, jnp.concatenate([x, x], axis=1), 0.0)
        for s_ref, init in ((sgf_ref, init_refs[2]), (sgb_ref, init_refs[3])):
            for p in range(GLA_QK_WIDTH // LANES):
                xt = init[0, p].T
                s_ref[0, p] = jnp.concatenate([jnp.where(first_half, xt, 0.0),
                                               jnp.where(first_half, 0.0, xt)], axis=0)

    T = SCAN_SUB
    C = RWKV_CHUNK
    nr = T // C
    nsub = SCAN_BLOCK // T
    R = RWKV_WIDTH
    Q = GLA_QK_WIDTH
    npair = Q // LANES

    def sub_block(h, carry):
        base = (0, 0) if nsub == 1 else (pl.multiple_of(h * T, T), pl.multiple_of((nsub - 1 - h) * T, T))

        def rows_of(d, start, size):
            return pl.ds(base[d] + start, size)

        rcst = _rwkv_consts()
        g_dirs = ((False, gaf_ref, glf_ref, 0, sgf_ref, yof_ref),
                  (True, gab_ref, glb_ref, Q, sgb_ref, yob_ref))
        la_tot = jnp.minimum(jnp.sum(glf_ref[0, rows_of(0, 0, T), 0:Q], axis=0, keepdims=True),
                             jnp.sum(glb_ref[0, rows_of(1, 0, T), Q:2 * Q], axis=0, keepdims=True))
        gla_split_ok = jnp.min(la_tot) >= -GLA_SAFE_LOG

        r_dirs = ((False, raf_ref, rdf_ref, srf_ref, yof_ref),
                  (True, rab_ref, rdb_ref, srb_ref, yob_ref))
        r_units = {}
        r_dests = {}
        for d, (rev, ra_ref, rd_ref, s_ref, y_ref) in enumerate(r_dirs):
            for j in range(nr):
                c = nr - 1 - j if rev else j
                rows = rows_of(d, c * C, C)
                r_units[d, j] = []
                r_dests[d, j] = []
                for p in range(R // LANES):
                    sl, sl1, sl2 = (slice(o + p * LANES, o + (p + 1) * LANES) for o in (0, R, 2 * R))
                    r_units[d, j].append((ra_ref[0, rows, sl], rd_ref[0, rows, sl], ra_ref[0, rows, sl1],
                                          ra_ref[0, rows, sl2], rd_ref[0, rows, sl1], rd_ref[0, rows, sl2], rev))
                    r_dests[d, j].append((y_ref, s_ref, p, rows, sl))
        r_states = {(d, 0): [s_ref[0, p] for (_, s_ref, p, _, _) in r_dests[d, 0]] for d in range(len(r_dirs))}

        def rwkv_task(d, j):
            prep = yield from _rwkv_prepare(r_units[d, j], rcst)
            while (d, j) not in r_states:
                yield
            ys, r_states[d, j + 1] = yield from _rwkv_apply(prep, r_states[d, j], rcst)
            for (y_ref, _, _, rows, sl), y in zip(r_dests[d, j], ys):
                y_ref[0, rows, sl] = y

        g_init = [[s_ref[0, p] for p in range(npair)] for (_, _, _, _, s_ref, _) in g_dirs]

        def gla_task(chunk, intra, cst):
            ng = T // chunk
            items = [[] for _ in range(ng)]
            dests = [[] for _ in range(ng)]
            for d, (rev, ga_ref, gl_ref, la_off, s_ref, o_ref) in enumerate(g_dirs):
                for p in range(npair):
                    qs, ks, las = (slice(o + p * LANES, o + (p + 1) * LANES) for o in (0, Q, la_off))
                    vs = slice(2 * Q + p * 2 * LANES, 2 * Q + (p + 1) * 2 * LANES)
                    os_ = slice(R + p * 2 * LANES, R + (p + 1) * 2 * LANES)
                    for j in range(ng):
                        c = ng - 1 - j if rev else j
                        rows = rows_of(d, c * chunk, chunk)
                        items[j].append((ga_ref[0, rows, qs], ga_ref[0, rows, ks], ga_ref[0, rows, vs],
                                         gl_ref[0, rows, las], rev))
                        dests[j].append((o_ref, s_ref, d, p, rows, os_))
            parts = yield from intra([it for its in items for it in its], cst)
            nu = len(items[0])
            sts = [g_init[d][p] for (_, _, d, p, _, _) in dests[0]]
            for j in range(ng):
                new = []
                for (o_ref, _, _, _, rows, vs), (o_intra, q_in, upd, e_tot), st in zip(
                        dests[j], parts[j * nu:(j + 1) * nu], sts):
                    o_ref[0, rows, vs] = o_intra + _mm(q_in, st, _NT)
                    new.append(st * e_tot + upd)
                sts = new
                yield
            for (_, s_ref, _, p, _, _), st in zip(dests[0], sts):
                s_ref[0, p] = st

        nd = len(r_dirs)
        _interleave([(RWKV_TASK_LAG * j, rwkv_task(d, j)) for j in range(nr) for d in range(nd)]
                    + [(GLA_TASK_LAG, gla_task(GLA_WIDE_CHUNK, _gla_factored, _gla_wide_consts()))])
        for d in range(nd):
            for (_, s_ref, p, _, _), s in zip(r_dests[d, 0], r_states[d, nr]):
                s_ref[0, p] = s

        @pl.when(jnp.logical_not(gla_split_ok))
        def _():
            _interleave([(0, gla_task(GLA_CHUNK, _gla_intra, _gla_consts()))])

        return carry

    if nsub == 1:
        sub_block(0, 0)
    else:
        lax.fori_loop(0, nsub, sub_block, 0)

    if emit_states:
        @pl.when(i == pl.num_programs(1) - 1)
        def _():
            for s_ref, out in ((srf_ref, final_refs[0]), (srb_ref, final_refs[1])):
                for p in range(RWKV_WIDTH // LANES):
                    s = s_ref[0, p]
                    out[0, p] = (s + pltpu.roll(s, LANES // 2, 1))[:, 0:LANES // 2]
            for s_ref, out in ((sgf_ref, final_refs[2]), (sgb_ref, final_refs[3])):
                for p in range(GLA_QK_WIDTH // LANES):
                    st = s_ref[0, p]
                    out[0, p] = st[0:LANES].T + st[LANES:2 * LANES].T


def _scan(ra, rd_f, rd_b, ga, gl, states, emit_states):
    B, L, _ = ra.shape
    R = RWKV_WIDTH
    Q = GLA_QK_WIDTH
    V = GLA_V_WIDTH
    T = SCAN_BLOCK
    n = L // T

    def fwd(w):
        return pl.BlockSpec((1, T, w), lambda b, i: (b, i, 0))

    def bwd(w):
        return pl.BlockSpec((1, T, w), lambda b, i: (b, n - 1 - i, 0))

    rshape = (R // LANES, LANES, LANES // 2)
    gshape = (Q // LANES, LANES, LANES)
    rst = pl.BlockSpec((1,) + rshape, lambda b, i: (b, 0, 0, 0))
    gst = pl.BlockSpec((1,) + gshape, lambda b, i: (b, 0, 0, 0))
    st_specs = [rst, rst, gst, gst]
    st_shapes = [jax.ShapeDtypeStruct((B,) + s, F32) for s in (rshape, rshape, gshape, gshape)]
    states = () if states is None else tuple(states)
    return pl.pallas_call(
        functools.partial(_scan_kernel, has_init=bool(states), emit_states=emit_states),
        grid=(B, n),
        in_specs=[fwd(3 * R), fwd(3 * R), bwd(3 * R), bwd(3 * R),
                  fwd(2 * Q + V), fwd(2 * Q), bwd(2 * Q + V), bwd(2 * Q)] + st_specs[:len(states)],
        out_specs=[fwd(R + V), bwd(R + V)] + (st_specs if emit_states else []),
        out_shape=[jax.ShapeDtypeStruct((B, L, R + V), F32)] * 2 + (st_shapes if emit_states else []),
        scratch_shapes=[pltpu.VMEM((1, R // LANES, LANES, LANES), F32)] * 2
        + [pltpu.VMEM((1, Q // LANES, 2 * LANES, LANES), F32)] * 2,
        compiler_params=pltpu.CompilerParams(dimension_semantics=("parallel", "arbitrary"),
                                             vmem_limit_bytes=VMEM_LIMIT),
        name="scan",
    )(ra, rd_f, ra, rd_b, ga, gl, ga, gl, *states)


def _post_kernel(x_ref, mod_ref, yof_ref, yob_ref, pg_ref,
                 lng_ref, lnb_ref, gng_ref, n2g_ref, fng_ref, wout_ref, w1_ref, w2_ref, o_ref):
    D = D_MODEL
    R = RWKV_WIDTH
    m = mod_ref[0]
    gt1 = m[:, 2 * D:3 * D]
    sh2 = m[:, 3 * D:4 * D]
    sc2 = m[:, 4 * D:5 * D]
    gt2 = m[:, 5 * D:6 * D]
    inv_n = 1.0 / RWKV_HEAD_DIM
    gng = gng_ref[...]
    nseq, tb = x_ref.shape[0], x_ref.shape[1]
    if nseq > 1:
        rows = [(s, slice(0, tb)) for s in range(nseq)]
    else:
        nsplit = POST_SPLIT if tb % (8 * POST_SPLIT) == 0 else 1
        rows = [(0, slice(s * (tb // nsplit), (s + 1) * (tb // nsplit))) for s in range(nsplit)]

    def mixed(q, rs):
        y = yof_ref[q, rs, 0:R] + yob_ref[q, rs, 0:R]
        mu = _head_sum(y) * inv_n
        yc = y - mu
        var = _head_sum(yc * yc) * inv_n
        yn = yc * lax.rsqrt(var + LNX_EPS) * lng_ref[...] + lnb_ref[...]
        parts = [(yn + pg_ref[q, rs, R:2 * R]) * pg_ref[q, rs, 0:R]]
        o = yof_ref[q, rs, R:] + yob_ref[q, rs, R:]
        for hh in range(GLA_HEADS):
            sl = slice(hh * GLA_VAL_DIM, (hh + 1) * GLA_VAL_DIM)
            oh = o[:, sl]
            oh = oh * lax.rsqrt(jnp.mean(oh * oh, axis=-1, keepdims=True) + GLA_NORM_EPS)
            parts.append(oh * gng * pg_ref[q, rs, 2 * R + hh * GLA_VAL_DIM:2 * R + (hh + 1) * GLA_VAL_DIM])
        return jnp.concatenate(parts, axis=-1)

    mix = [mixed(q, rs) for q, rs in rows]
    x1 = [x_ref[q, rs] + gt1 * _bdot(mx, wout_ref) for (q, rs), mx in zip(rows, mix)]
    h2 = [_rms(v) * n2g_ref[...] * (1.0 + sc2) + sh2 for v in x1]
    f = [jnp.maximum(_bdot(v, w1_ref), 0.0) for v in h2]
    x2 = [a + gt2 * _bdot(v * v, w2_ref) for a, v in zip(x1, f)]
    for (q, rs), v in zip(rows, x2):
        o_ref[q, rs] = _rms(v) * fng_ref[...]


def _post(x, mod3, mod_row, yo_f, yo_b, pg, vecs, mats, shared_mod):
    B, L, D = x.shape
    tb = min(POST_BLOCK, L)
    nblk = L // tb
    nseq = POST_BLOCK // L if (shared_mod and L < POST_BLOCK and B % (POST_BLOCK // L) == 0) else 1

    def tok(w):
        return pl.BlockSpec((nseq, tb, w), lambda b, i: (b, i, 0))

    in_specs = [tok(D), pl.BlockSpec((1, 1, N_MOD * D), lambda b, i: (mod_row(b * nseq), 0, 0))]
    in_specs += [tok(yo_f.shape[-1]), tok(yo_b.shape[-1]), tok(pg.shape[-1])]
    in_specs += [_const_spec(w.shape) for w in vecs]
    in_specs += [pl.BlockSpec(w.shape, lambda b, i: (0, 0), pipeline_mode=pl.Buffered(1)) for w in mats]
    return pl.pallas_call(
        _post_kernel,
        grid=(B // nseq, nblk),
        in_specs=in_specs,
        out_specs=tok(D),
        out_shape=jax.ShapeDtypeStruct((B, L, D), F32),
        compiler_params=pltpu.CompilerParams(dimension_semantics=("parallel", "arbitrary"),
                                             vmem_limit_bytes=VMEM_LIMIT),
        name="post",
    )(x, mod3, yo_f, yo_b, pg, *vecs, *mats)


def _blockdiag2(a, b):
    za = jnp.zeros((a.shape[0], b.shape[1]), a.dtype)
    zb = jnp.zeros((b.shape[0], a.shape[1]), a.dtype)
    return jnp.concatenate([jnp.concatenate([a, za], axis=1), jnp.concatenate([zb, b], axis=1)], axis=0)


def kernel(x_prompt, x_sample, c, state_rwkv_fwd, state_rwkv_bwd, state_gla_fwd, state_gla_bwd, c_ctx, ada_w, ada_b, norm1_g, norm2_g, w_in, rwkv_mu_rkv, rwkv_mu_wag, rwkv_w0, rwkv_w1, rwkv_w2, rwkv_a0, rwkv_a1, rwkv_a2, rwkv_g1, rwkv_g2, rwkv_k_k, rwkv_k_a, rwkv_r_k, rwkv_lnx_g, rwkv_lnx_b, gla_gk1, gla_gk2, gla_gk_b, gla_norm_g, w_out, mlp_w1, mlp_w2, final_norm_g):
    D = D_MODEL
    R = RWKV_WIDTH
    nb = x_prompt.shape[0]
    nd = x_sample.shape[0]
    assert ada_w.shape[0] == 1, "single-layer step"
    layer = 0

    craw = jnp.concatenate([c_ctx[None, :], c, jnp.zeros((8 - 1 - nd, D), F32)], axis=0)
    mod = _modulation(craw, ada_w[layer], ada_b[layer][None, :])
    mod3 = mod.reshape(8, 1, N_MOD * D)

    bf = lambda t: t.astype(BF16)
    row = lambda t: t.reshape(1, -1).astype(F32)
    w_in_bf = bf(w_in[layer])
    half = (D, 3 * R)
    pre_w = [
        row(norm1_g[layer]),
        (w_in_bf, pl.BlockSpec(half, lambda b, i: (0, 0))),
        (w_in_bf, pl.BlockSpec(half, lambda b, i: (0, 1))),
        bf(jnp.concatenate([gla_gk1[layer, 0], gla_gk1[layer, 1]], axis=1)),
        bf(jnp.concatenate([rwkv_w1[layer, 0], rwkv_w1[layer, 1]], axis=1)),
        bf(jnp.concatenate([rwkv_a1[layer, 0], rwkv_a1[layer, 1]], axis=1)),
        bf(rwkv_g1[layer]),
        bf(_blockdiag2(rwkv_w2[layer, 0], rwkv_w2[layer, 1])),
        bf(_blockdiag2(rwkv_a2[layer, 0], rwkv_a2[layer, 1])),
        bf(rwkv_g2[layer]),
        bf(_blockdiag2(gla_gk2[layer, 0], gla_gk2[layer, 1])),
        row(rwkv_mu_rkv[layer]),
        rwkv_mu_wag[layer],
        row(rwkv_w0[layer]),
        row(rwkv_a0[layer]),
        row(gla_gk_b[layer]),
        row(rwkv_k_k[layer]),
        row(rwkv_k_a[layer]),
        row(rwkv_r_k[layer]),
    ]
    post_vecs = [row(rwkv_lnx_g[layer]), row(rwkv_lnx_b[layer]), row(gla_norm_g[layer]),
                 row(norm2_g[layer]), row(final_norm_g)]
    post_mats = [bf(w_out[layer]), bf(mlp_w1[layer]), bf(mlp_w2[layer])]

    def run_group(x, mod_row, grid_shift, states, emit_states, shared_mod):
        ra, rd_f, rd_b, ga, gl, pg = _pre(x, mod3, mod_row, pre_w, grid_shift)
        yo_f, yo_b, *finals = _scan(ra, rd_f, rd_b, ga, gl, states, emit_states)
        y = _post(x, mod3, mod_row, yo_f, yo_b, pg, post_vecs, post_mats, shared_mod)
        return y, finals

    rpair = lambda s: s.reshape(nd, R // LANES, LANES, RWKV_HEAD_DIM)
    gpair = lambda s: s.reshape(nd, GLA_QK_WIDTH // LANES, LANES, GLA_VAL_DIM)
    y_prompt, (n_rf, n_rb, n_gf, n_gb) = run_group(x_prompt, lambda b: 0, False, None, True, True)
    y_sample, _ = run_group(x_sample, lambda b: b + 1, True,
                            (rpair(state_rwkv_fwd[:, layer]), rpair(state_rwkv_bwd[:, layer]),
                             gpair(state_gla_fwd[:, layer]), gpair(state_gla_bwd[:, layer])), False, False)
    rshape = (nb, 1, RWKV_HEADS, RWKV_HEAD_DIM, RWKV_HEAD_DIM)
    gshape = (nb, 1, GLA_HEADS, GLA_KEY_DIM, GLA_VAL_DIM)
    return (y_prompt, y_sample, n_rf.reshape(rshape), n_rb.reshape(rshape),
            n_gf.reshape(gshape), n_gb.reshape(gshape))
```
